```python
import math
import jax, jax.numpy as jnp
from jax import lax
import numpy as np

D_MODEL = 1024
BATCH = 2
SEQ = 8192
DEPTH = 2

N_BRANCH = 4
BRANCH_W = D_MODEL // 4
POOL_WINDOWS = (2, 4, 8, 16)
POOL_GROUPS = len(POOL_WINDOWS)
POOL_GW = BRANCH_W // POOL_GROUPS
SG_HEADS = 4
SG_CHUNK = 128
SG_HD = BRANCH_W // SG_HEADS
CONV_WIDTH = 31
ATT_HEADS = 4
ATT_HD = BRANCH_W // ATT_HEADS
ATT_BLOCK = 128
COL_SIZES = (
    BRANCH_W,
    2 * BRANCH_W,
    2 * BRANCH_W,
    BRANCH_W,
    BRANCH_W,
    BRANCH_W,
    ATT_HEADS,
    N_BRANCH * D_MODEL,
)
COL_TOTAL = int(sum(COL_SIZES))
COL_SPLITS = tuple(int(i) for i in np.cumsum(COL_SIZES)[:-1])
N_EXPERTS = 64
TOP_K = 6
N_GROUPS = 8
TOPK_GROUPS = 4
EXPERT_FF = 256
SHARED_FF = 256
ROUTE_SCALE = 2.5
MOE_CHUNK = 128
EPS = 1e-6
N_MOD = 6

kernel_name = "hybrid_pool_sgmlp_conformer_fox_moe_adaln"


def rms_norm(x, g):
    x32 = x.astype(jnp.float32)
    y = x32 * lax.rsqrt(jnp.mean(x32 * x32, axis=-1, keepdims=True) + EPS)
    return (y * g.astype(jnp.float32)).astype(x.dtype)


def layer_norm(x, g, b):
    x32 = x.astype(jnp.float32)
    mu = jnp.mean(x32, axis=-1, keepdims=True)
    var = jnp.mean(jnp.square(x32 - mu), axis=-1, keepdims=True)
    y = (x32 - mu) * lax.rsqrt(var + EPS)
    return (y * g.astype(jnp.float32) + b.astype(jnp.float32)).astype(x.dtype)


def multiscale_pool(u, w_pool, scale):
    B, S, _ = u.shape
    ug = u.reshape(B, S, POOL_GROUPS, POOL_GW).astype(jnp.float32)
    cs = jnp.cumsum(ug, axis=1)
    count = jnp.arange(1, S + 1, dtype=jnp.float32)
    outs = []
    for gi, w in enumerate(POOL_WINDOWS):
        csg = cs[:, :, gi]
        lag = jnp.pad(csg, ((0, 0), (w, 0), (0, 0)))[:, :S]
        mean = (csg - lag) / jnp.minimum(count, float(w))[None, :, None]
        outs.append(mean - ug[:, :, gi])
    pooled = jnp.stack(outs, axis=2)
    y = jnp.einsum('bsgc,gcd->bsgd', pooled, w_pool.astype(jnp.float32)).reshape(B, S, BRANCH_W)
    return (y * scale.astype(jnp.float32)).astype(u.dtype)


def spatial_gating(z, ln_g, ln_b, w_s, b_s):
    B, S, _ = z.shape
    z = jax.nn.gelu(z)
    u, v = jnp.split(z, 2, axis=-1)
    v = layer_norm(v, ln_g, ln_b)
    vb = v.reshape(B, S // SG_CHUNK, SG_CHUNK, SG_HEADS, SG_HD)
    causal = jnp.tril(jnp.ones((SG_CHUNK, SG_CHUNK), dtype=w_s.dtype))
    s = jnp.einsum('hts,bnshc->bnthc', w_s * causal, vb)
    s = s + jnp.transpose(b_s)[None, None, :, :, None]
    return u * s.reshape(B, S, BRANCH_W)


def conformer_conv(z, conv_w, conv_b, ln_g, ln_b):
    a, g = jnp.split(z, 2, axis=-1)
    y = a * jax.nn.sigmoid(g)
    y = lax.conv_general_dilated(
        y, conv_w[:, None, :].astype(y.dtype), window_strides=(1,),
        padding=[(CONV_WIDTH - 1, 0)], dimension_numbers=('NWC', 'WIO', 'NWC'),
        feature_group_count=BRANCH_W) + conv_b
    y = layer_norm(y, ln_g, ln_b)
    return jax.nn.silu(y)


def forgetting_attention(q, k, v, f_logit):
    B, S, H, dh = q.shape
    nb = S // ATT_BLOCK
    log_f = jax.nn.log_sigmoid(f_logit.astype(jnp.float32))
    cum = jnp.cumsum(log_f, axis=1)
    cum_k = jnp.transpose(cum, (0, 2, 1))
    k32 = k.astype(jnp.float32)
    v32 = v.astype(jnp.float32)
    qb = jnp.transpose(q.reshape(B, nb, ATT_BLOCK, H, dh), (1, 0, 2, 3, 4))
    cb = jnp.transpose(cum.reshape(B, nb, ATT_BLOCK, H), (1, 0, 3, 2))
    starts = jnp.arange(nb, dtype=jnp.int32) * ATT_BLOCK
    kpos = jnp.arange(S, dtype=jnp.int32)
    scale = 1.0 / math.sqrt(dh)

    def one_block(args):
        q_blk, c_blk, start = args
        logits = jnp.einsum('bqhd,bkhd->bhqk', q_blk.astype(jnp.float32), k32) * scale
        logits = logits + c_blk[..., None] - cum_k[:, :, None, :]
        qpos = start + jnp.arange(ATT_BLOCK, dtype=jnp.int32)
        mask = kpos[None, :] <= qpos[:, None]
        logits = jnp.where(mask[None, None], logits, -jnp.inf)
        p = jax.nn.softmax(logits, axis=-1)
        return jnp.einsum('bhqk,bkhd->bqhd', p, v32)

    out = lax.map(one_block, (qb, cb, starts))
    out = jnp.transpose(out, (1, 0, 2, 3, 4)).reshape(B, S, H * dh)
    return out.astype(q.dtype)


def mixer_sublayer(h, w_in, b_f, pool_w, pool_scale, sg_ln_g, sg_ln_b, sg_w, sg_b,
                   conv_w, conv_b, conv_ln_g, conv_ln_b, w_branch, w_out):
    B, S, D = h.shape
    proj = jnp.einsum('bsd,dc->bsc', h, w_in)
    z_pool, z_sg, z_conv, q, k, v, f_logit, gate_cols = jnp.split(proj, COL_SPLITS, axis=-1)
    y_a = multiscale_pool(z_pool, pool_w, pool_scale)
    y_b = spatial_gating(z_sg, sg_ln_g, sg_ln_b, sg_w, sg_b)
    y_c = conformer_conv(z_conv, conv_w, conv_b, conv_ln_g, conv_ln_b)
    y_d = forgetting_attention(q.reshape(B, S, ATT_HEADS, ATT_HD),
                               k.reshape(B, S, ATT_HEADS, ATT_HD),
                               v.reshape(B, S, ATT_HEADS, ATT_HD),
                               f_logit + b_f)
    ys = jnp.stack([y_a, y_b, y_c, y_d.astype(y_a.dtype)], axis=2)
    branch = jnp.einsum('bsnc,ncd->bsnd', ys, w_branch)
    gates = jax.nn.sigmoid(gate_cols).reshape(B, S, N_BRANCH, D)
    merged = jnp.sum(gates * branch, axis=2)
    return jnp.einsum('bsd,de->bse', merged, w_out)


def moe_ffn(h, router_w, router_bias, w1, w3, w2, sw1, sw3, sw2):
    B, S, D = h.shape
    T = B * S
    t = h.reshape(T, D)
    scores = jax.nn.sigmoid(jnp.einsum('td,de->te', t, router_w).astype(jnp.float32))
    sel = scores + router_bias.astype(jnp.float32)
    grouped = sel.reshape(T, N_GROUPS, N_EXPERTS // N_GROUPS)
    group_score = jnp.sum(lax.top_k(grouped, 2)[0], axis=-1)
    _, gidx = lax.top_k(group_score, TOPK_GROUPS)
    gmask = jnp.sum(jax.nn.one_hot(gidx, N_GROUPS, dtype=jnp.float32), axis=1) > 0
    emask = jnp.repeat(gmask, N_EXPERTS // N_GROUPS, axis=1)
    _, eidx = lax.top_k(jnp.where(emask, sel, -jnp.inf), TOP_K)
    w = jnp.take_along_axis(scores, eidx, axis=1)
    w = w / jnp.sum(w, axis=-1, keepdims=True) * ROUTE_SCALE
    dense_gates = jnp.sum(jax.nn.one_hot(eidx, N_EXPERTS, dtype=jnp.float32) * w[..., None], axis=1)
    nchunk = T // MOE_CHUNK

    def chunk_fn(args):
        tc, gc = args
        a = jnp.einsum('td,edf->tef', tc, w1)
        b = jnp.einsum('td,edf->tef', tc, w3)
        hid = jax.nn.silu(a) * b * gc[:, :, None].astype(tc.dtype)
        return jnp.einsum('tef,efd->td', hid, w2)

    routed = lax.map(chunk_fn, (t.reshape(nchunk, MOE_CHUNK, D),
                                dense_gates.reshape(nchunk, MOE_CHUNK, N_EXPERTS))).reshape(T, D)
    shared = jnp.einsum('tf,fd->td', jax.nn.silu(t @ sw1) * (t @ sw3), sw2)
    return (routed + shared).reshape(B, S, D)


def setup_inputs(seed: int = 0) -> dict:
    key = jax.random.key(seed)
    ks = jax.random.split(key, 32)
    f32 = jnp.float32
    n = lambda i, shape: jax.random.normal(ks[i], shape, dtype=f32)
    L, D, E = DEPTH, D_MODEL, N_EXPERTS
    return {
        "x": n(0, (BATCH, SEQ, D)),
        "c": n(1, (BATCH, D)),
        "w_in": n(2, (L, D, COL_TOTAL)) * D ** -0.5,
        "b_f": 3.0 + 0.5 * n(3, (L, ATT_HEADS)),
        "pool_w": n(4, (L, POOL_GROUPS, POOL_GW, POOL_GW)) * POOL_GW ** -0.5,
        "pool_scale": 1.0 + 0.1 * n(5, (L, BRANCH_W)),
        "sg_ln_g": 1.0 + 0.1 * n(6, (L, BRANCH_W)),
        "sg_ln_b": 0.02 * n(7, (L, BRANCH_W)),
        "sg_w": n(8, (L, SG_HEADS, SG_CHUNK, SG_CHUNK)) * (0.5 * SG_CHUNK ** -0.5),
        "sg_b": 1.0 + 0.1 * n(9, (L, SG_HEADS, SG_CHUNK)),
        "conv_w": n(10, (L, CONV_WIDTH, BRANCH_W)) * CONV_WIDTH ** -0.5,
        "conv_b": 0.02 * n(11, (L, BRANCH_W)),
        "conv_ln_g": 1.0 + 0.1 * n(12, (L, BRANCH_W)),
        "conv_ln_b": 0.02 * n(13, (L, BRANCH_W)),
        "w_branch": n(14, (L, N_BRANCH, BRANCH_W, D)) * BRANCH_W ** -0.5,
        "w_out": n(15, (L, D, D)) * D ** -0.5,
        "mix_norm_g": 1.0 + 0.1 * n(16, (L, D)),
        "ffn_norm_g": 1.0 + 0.1 * n(17, (L, D)),
        "ada_w": n(18, (L, D, N_MOD * D)) * (0.5 * D ** -0.5),
        "ada_b": 0.02 * n(19, (L, N_MOD * D)),
        "router_w": n(20, (L, D, E)) * D ** -0.5,
        "router_bias": 0.01 * n(21, (L, E)),
        "exp_w1": n(22, (L, E, D, EXPERT_FF)) * D ** -0.5,
        "exp_w3": n(23, (L, E, D, EXPERT_FF)) * D ** -0.5,
        "exp_w2": n(24, (L, E, EXPERT_FF, D)) * EXPERT_FF ** -0.5,
        "shared_w1": n(25, (L, D, SHARED_FF)) * D ** -0.5,
        "shared_w3": n(26, (L, D, SHARED_FF)) * D ** -0.5,
        "shared_w2": n(27, (L, SHARED_FF, D)) * SHARED_FF ** -0.5,
        "final_norm_g": 1.0 + 0.1 * n(28, (D,)),
    }


def reference(x, c, w_in, b_f, pool_w, pool_scale, sg_ln_g, sg_ln_b, sg_w, sg_b,
              conv_w, conv_b, conv_ln_g, conv_ln_b, w_branch, w_out, mix_norm_g, ffn_norm_g,
              ada_w, ada_b, router_w, router_bias, exp_w1, exp_w3, exp_w2,
              shared_w1, shared_w3, shared_w2, final_norm_g):
    B = x.shape[0]
    c_act = jax.nn.silu(c)
    for l in range(DEPTH):
        mod = (c_act @ ada_w[l] + ada_b[l]).reshape(B, N_MOD, D_MODEL)
        shift_m, scale_m, gate_m = mod[:, 0, None], mod[:, 1, None], mod[:, 2, None]
        shift_f, scale_f, gate_f = mod[:, 3, None], mod[:, 4, None], mod[:, 5, None]
        h = rms_norm(x, mix_norm_g[l]) * (1.0 + scale_m) + shift_m
        mix = mixer_sublayer(h, w_in[l], b_f[l], pool_w[l], pool_scale[l], sg_ln_g[l], sg_ln_b[l],
                             sg_w[l], sg_b[l], conv_w[l], conv_b[l], conv_ln_g[l], conv_ln_b[l],
                             w_branch[l], w_out[l])
        x = x + gate_m * mix
        h = rms_norm(x, ffn_norm_g[l]) * (1.0 + scale_f) + shift_f
        ffn = moe_ffn(h, router_w[l], router_bias[l], exp_w1[l], exp_w3[l], exp_w2[l],
                      shared_w1[l], shared_w3[l], shared_w2[l])
        x = x + gate_f * ffn
    return rms_norm(x, final_norm_g)
```

```python
import functools
import math

import jax
import jax.numpy as jnp
from jax import lax
from jax.experimental import pallas as pl
from jax.experimental.pallas import tpu as pltpu

F32 = jnp.float32
BF16 = jnp.bfloat16
HIGHEST = lax.Precision.HIGHEST

D_MODEL = 1024
BRANCH_W = 256
N_BRANCH = 4
POOL_WINDOWS = (2, 4, 8, 16)
POOL_GW = 64
POOL_TAIL = 16
SG_HEADS = 4
SG_CHUNK = 128
SG_HD = 64
CONV_WIDTH = 31
CONV_TAIL = 32
ATT_HEADS = 4
ATT_HD = 64
N_EXPERTS = 64
TOP_K = 6
N_GROUPS = 8
GROUP_SIZE = N_EXPERTS // N_GROUPS
TOPK_GROUPS = 4
EXPERT_FF = 256
SHARED_FF = 256
ROUTE_SCALE = 2.5
EPS = 1e-6
N_MOD = 6
MIX_COLS = 8 * BRANCH_W
LANES = 128
SUBLANES = 8
VMEM_LIMIT = 56 * 1024 * 1024


def _sigmoid(x):
    return 1.0 / (1.0 + jnp.exp(-x))


def _silu(x):
    return x * _sigmoid(x)


def _rms_mod(x, g, scale, shift):
    y = x * lax.rsqrt(jnp.mean(x * x, axis=-1, keepdims=True) + EPS)
    return (y * g) * (1.0 + scale) + shift


def _layer_norm(x, g, b):
    mu = jnp.mean(x, axis=-1, keepdims=True)
    xc = x - mu
    var = jnp.mean(xc * xc, axis=-1, keepdims=True)
    return xc * lax.rsqrt(var + EPS) * g + b


def _params(*sem):
    return pltpu.CompilerParams(dimension_semantics=sem, vmem_limit_bytes=VMEM_LIMIT)


def _mod_kernel(c_ref, w_ref, b_ref, o_ref):
    c = c_ref[...]
    o_ref[0] = jnp.dot(_silu(c), w_ref[0], precision=HIGHEST, preferred_element_type=F32) + b_ref[0]


def _modulation(c, ada_w, ada_b):
    L, D, N = ada_w.shape
    B = c.shape[0]
    tn = 1536
    out = pl.pallas_call(
        _mod_kernel,
        grid=(L, N // tn),
        in_specs=[pl.BlockSpec((B, D), lambda l, j: (0, 0)),
                  pl.BlockSpec((1, D, tn), lambda l, j: (l, 0, j)),
                  pl.BlockSpec((1, 1, tn), lambda l, j: (l, 0, j))],
        out_specs=pl.BlockSpec((1, B, tn), lambda l, j: (l, 0, j)),
        out_shape=jax.ShapeDtypeStruct((L, B, N), F32),
        compiler_params=_params("arbitrary", "arbitrary"),
        name="modulation",
    )(c, ada_w, ada_b.reshape(L, 1, N))
    return out.reshape(L, B, N_MOD, D)


def _mixer_in_kernel(x_ref, mod_ref, g_ref, wmix_ref, wfc_ref, wft_ref, bfc_ref, bft_ref,
                     poolw_ref, pools_ref, sglg_ref, sglb_ref, sgw_ref, sgb_ref,
                     cw_ref, cb_ref, clg_ref, clb_ref,
                     ya_ref, yb_ref, yc_ref, q_ref, k_ref, v_ref, cc_ref, ct_ref,
                     pool_ext, conv_ext, cum_c, cum_t, *, tb):
    j = pl.program_id(1)

    @pl.when(j == 0)
    def _():
        pool_ext[0:POOL_TAIL, :] = jnp.zeros((POOL_TAIL, BRANCH_W), F32)
        conv_ext[0:CONV_TAIL, :] = jnp.zeros((CONV_TAIL, BRANCH_W), F32)
        cum_c[...] = jnp.zeros_like(cum_c)
        cum_t[...] = jnp.zeros_like(cum_t)

    mod = mod_ref[0]
    h = _rms_mod(x_ref[0], g_ref[...], mod[1:2, :], mod[0:1, :])
    hb = h.astype(BF16)
    proj = jnp.dot(hb, wmix_ref[...], preferred_element_type=F32)

    lane = lax.broadcasted_iota(jnp.int32, (1, BRANCH_W), 1)
    row = lax.broadcasted_iota(jnp.int32, (tb, 1), 0)

    u = proj[:, 0:BRANCH_W]
    pool_ext[POOL_TAIL:POOL_TAIL + tb, :] = u
    ext = pool_ext[...]
    s2 = ext + pltpu.roll(ext, 1, 0)
    s4 = s2 + pltpu.roll(s2, 2, 0)
    s8 = s4 + pltpu.roll(s4, 4, 0)
    s16 = s8 + pltpu.roll(s8, 8, 0)
    grp = lane // POOL_GW
    wsum = jnp.where(grp == 0, s2, jnp.where(grp == 1, s4, jnp.where(grp == 2, s8, s16)))[POOL_TAIL:, :]
    win = jnp.where(grp == 0, 2.0, jnp.where(grp == 1, 4.0, jnp.where(grp == 2, 8.0, 16.0)))
    count = jnp.minimum((j * tb + row + 1).astype(F32), win)
    pooled = wsum / count - u
    ya = jnp.dot(pooled.astype(BF16), poolw_ref[...], preferred_element_type=F32) * pools_ref[...]
    ya_ref[0] = ya.astype(BF16)
    pool_ext[0:POOL_TAIL, :] = u[tb - POOL_TAIL:, :]

    z = proj[:, BRANCH_W:3 * BRANCH_W]
    z = 0.5 * z * (1.0 + jnp.tanh(math.sqrt(2.0 / math.pi) * (z + 0.044715 * (z * z * z))))
    su = z[:, 0:BRANCH_W]
    sv = _layer_norm(z[:, BRANCH_W:], sglg_ref[...], sglb_ref[...])
    r128 = lax.broadcasted_iota(jnp.int32, (SG_CHUNK, SG_CHUNK), 0)
    c128 = lax.broadcasted_iota(jnp.int32, (SG_CHUNK, SG_CHUNK), 1)
    wcat = jnp.concatenate(
        [jnp.where(r128 >= c128, sgw_ref[hh], 0.0) for hh in range(SG_HEADS)], axis=1).astype(BF16)
    head = lane // SG_HD
    for ci in range(tb // SG_CHUNK):
        rows = slice(ci * SG_CHUNK, (ci + 1) * SG_CHUNK)
        vch = sv[rows, :]
        vstack = jnp.concatenate(
            [jnp.where(head == hh, vch, 0.0) for hh in range(SG_HEADS)], axis=0).astype(BF16)
        s = jnp.dot(wcat, vstack, preferred_element_type=F32) + sgb_ref[...]
        yb_ref[0, rows, :] = (su[rows, :] * s).astype(BF16)

    glu = proj[:, 3 * BRANCH_W:4 * BRANCH_W] * _sigmoid(proj[:, 4 * BRANCH_W:5 * BRANCH_W])
    conv_ext[CONV_TAIL:CONV_TAIL + tb, :] = glu
    acc = jnp.zeros((tb, BRANCH_W), F32) + cb_ref[...]
    for kk in range(CONV_WIDTH):
        off = CONV_TAIL - (CONV_WIDTH - 1) + kk
        acc = acc + cw_ref[kk:kk + 1, :] * conv_ext[off:off + tb, :]
    yc_ref[0] = _silu(_layer_norm(acc, clg_ref[...], clb_ref[...])).astype(BF16)
    conv_ext[0:CONV_TAIL, :] = glu[tb - CONV_TAIL:, :]

    q_ref[0] = (proj[:, 5 * BRANCH_W:6 * BRANCH_W] * (1.0 / math.sqrt(ATT_HD))).astype(BF16)
    k_ref[0] = proj[:, 6 * BRANCH_W:7 * BRANCH_W].astype(BF16)
    v_ref[0] = proj[:, 7 * BRANCH_W:8 * BRANCH_W].astype(BF16)

    def log_sigmoid(t):
        return jnp.minimum(t, 0.0) - jnp.log(1.0 + jnp.exp(-jnp.abs(t)))

    rr = lax.broadcasted_iota(jnp.int32, (tb, tb), 0)
    cc = lax.broadcasted_iota(jnp.int32, (tb, tb), 1)
    lower = jnp.where(rr >= cc, 1.0, 0.0).astype(F32)
    upper = jnp.where(rr <= cc, 1.0, 0.0).astype(F32)
    lf_c = log_sigmoid(jnp.dot(hb, wfc_ref[...], preferred_element_type=F32) + bfc_ref[...])
    cs_c = jnp.dot(lower, lf_c, precision=HIGHEST, preferred_element_type=F32) + cum_c[...]
    cc_ref[0] = cs_c
    cum_c[...] = cs_c[tb - 1:tb, :]
    lf_t = log_sigmoid(
        lax.dot_general(wft_ref[...], hb, (((1,), (1,)), ((), ())), preferred_element_type=F32) + bft_ref[...])
    cs_t = jnp.dot(lf_t, upper, precision=HIGHEST, preferred_element_type=F32) + cum_t[...]
    ct_ref[0] = cs_t
    cum_t[...] = cs_t[:, tb - 1:tb]


def _mixer_in(x, mod_l, norm_g, wmix, wfc, wft, bfc, bft, poolw, pools, sglg, sglb, sgw, sgb,
              cw, cb, clg, clb, *, tb):
    B, S, D = x.shape
    full = lambda a: pl.BlockSpec(a.shape, lambda b, j: (0,) * a.ndim)
    tok = lambda w: pl.BlockSpec((1, tb, w), lambda b, j: (b, j, 0))
    consts = (norm_g, wmix, wfc, wft, bfc, bft, poolw, pools, sglg, sglb, sgw, sgb, cw, cb, clg, clb)
    act = jax.ShapeDtypeStruct((B, S, BRANCH_W), BF16)
    return pl.pallas_call(
        functools.partial(_mixer_in_kernel, tb=tb),
        grid=(B, S // tb),
        in_specs=[tok(D), pl.BlockSpec((1, N_MOD, D), lambda b, j: (b, 0, 0))] + [full(a) for a in consts],
        out_specs=[tok(BRANCH_W)] * 6 + [tok(LANES), pl.BlockSpec((1, SUBLANES, tb), lambda b, j: (b, 0, j))],
        out_shape=[act] * 6 + [jax.ShapeDtypeStruct((B, S, LANES), F32),
                               jax.ShapeDtypeStruct((B, SUBLANES, S), F32)],
        scratch_shapes=[pltpu.VMEM((POOL_TAIL + tb, BRANCH_W), F32),
                        pltpu.VMEM((CONV_TAIL + tb, BRANCH_W), F32),
                        pltpu.VMEM((1, LANES), F32),
                        pltpu.VMEM((SUBLANES, 1), F32)],
        compiler_params=_params("arbitrary", "arbitrary"),
        name="mixer_in",
    )(x, mod_l, *consts)


def _attn_kernel(q_ref, k_ref, v_ref, cq_ref, ck_ref, o_ref, *, tq):
    i = pl.program_id(1)
    q = q_ref[0]
    cq_all = cq_ref[0]
    lane = lax.broadcasted_iota(jnp.int32, (1, BRANCH_W), 1)
    rr = lax.broadcasted_iota(jnp.int32, (tq, tq), 0)
    cc = lax.broadcasted_iota(jnp.int32, (tq, tq), 1)
    out = jnp.zeros((tq, BRANCH_W), F32)
    for hh in range(ATT_HEADS):
        in_head = (lane // ATT_HD) == hh
        qh = jnp.where(in_head, q, jnp.zeros_like(q))
        cq = cq_all[:, hh:hh + 1]

        def step(kj, carry, diagonal):
            m, l, acc = carry
            ks = pl.multiple_of(kj * tq, tq)
            kb = k_ref[0, pl.ds(ks, tq), :]
            vb = v_ref[0, pl.ds(ks, tq), :]
            s = lax.dot_general(qh, kb, (((1,), (1,)), ((), ())), preferred_element_type=F32)
            s = s + (cq - ck_ref[0, hh:hh + 1, pl.ds(ks, tq)])
            if diagonal:
                s = jnp.where(cc <= rr, s, -jnp.inf)
            m_new = jnp.maximum(m, jnp.max(s, axis=-1, keepdims=True))
            p = jnp.exp(s - m_new)
            alpha = jnp.exp(m - m_new)
            l = alpha * l + jnp.sum(p, axis=-1, keepdims=True)
            acc = alpha * acc + jnp.dot(p.astype(BF16), vb, preferred_element_type=F32)
            return m_new, l, acc

        init = (jnp.full((tq, 1), -jnp.inf, F32), jnp.zeros((tq, 1), F32), jnp.zeros((tq, BRANCH_W), F32))
        carry = lax.fori_loop(0, i, functools.partial(step, diagonal=False), init)
        _, l, acc = step(i, carry, True)
        out = out + jnp.where(in_head, acc / l, 0.0)
    o_ref[0] = out.astype(BF16)


def _attention(q, k, v, cum_c, cum_t, *, tq):
    B, S, W = q.shape
    return pl.pallas_call(
        functools.partial(_attn_kernel, tq=tq),
        grid=(B, S // tq),
        in_specs=[pl.BlockSpec((1, tq, W), lambda b, i: (b, i, 0)),
                  pl.BlockSpec((1, S, W), lambda b, i: (b, 0, 0)),
                  pl.BlockSpec((1, S, W), lambda b, i: (b, 0, 0)),
                  pl.BlockSpec((1, tq, LANES), lambda b, i: (b, i, 0)),
                  pl.BlockSpec((1, SUBLANES, S), lambda b, i: (b, 0, 0))],
        out_specs=pl.BlockSpec((1, tq, W), lambda b, i: (b, i, 0)),
        out_shape=jax.ShapeDtypeStruct((B, S, W), BF16),
        compiler_params=_params("arbitrary", "arbitrary"),
        name="attention",
    )(q, k, v, cum_c, cum_t)


def _merge_kernel(x_ref, mod_ref, g_ref, wg_ref, wbr_ref, wout_ref, ya_ref, yb_ref, yc_ref, yd_ref, o_ref):
    x = x_ref[0]
    mod = mod_ref[0]
    hb = _rms_mod(x, g_ref[...], mod[1:2, :], mod[0:1, :]).astype(BF16)
    merged = jnp.zeros(x.shape, F32)
    for n, y_ref in enumerate((ya_ref, yb_ref, yc_ref, yd_ref)):
        gate = _sigmoid(jnp.dot(hb, wg_ref[:, n * D_MODEL:(n + 1) * D_MODEL], preferred_element_type=F32))
        merged = merged + gate * jnp.dot(y_ref[0], wbr_ref[n], preferred_element_type=F32)
    out = jnp.dot(merged.astype(BF16), wout_ref[...], preferred_element_type=F32)
    o_ref[0] = x + mod[2:3, :] * out


def _merge(x, mod_l, norm_g, wgate, wbranch, wout, ya, yb, yc, yd, *, tb):
    B, S, D = x.shape
    full = lambda a: pl.BlockSpec(a.shape, lambda b, j: (0,) * a.ndim)
    tok = lambda w: pl.BlockSpec((1, tb, w), lambda b, j: (b, j, 0))
    return pl.pallas_call(
        _merge_kernel,
        grid=(B, S // tb),
        in_specs=[tok(D), pl.BlockSpec((1, N_MOD, D), lambda b, j: (b, 0, 0)),
                  full(norm_g), full(wgate), full(wbranch), full(wout)] + [tok(BRANCH_W)] * 4,
        out_specs=tok(D),
        out_shape=jax.ShapeDtypeStruct((B, S, D), F32),
        compiler_params=_params("arbitrary", "arbitrary"),
        name="merge",
    )(x, mod_l, norm_g, wgate, wbranch, wout, ya, yb, yc, yd)


def _route(scores_t, bias_t):
    E, n = scores_t.shape
    sel = scores_t + bias_t
    eidx = lax.broadcasted_iota(jnp.int32, (E, n), 0)
    neg = jnp.full((E, n), -jnp.inf, F32)
    gscore = []
    sub = lax.broadcasted_iota(jnp.int32, (GROUP_SIZE, n), 0)
    for g in range(N_GROUPS):
        blk = sel[g * GROUP_SIZE:(g + 1) * GROUP_SIZE, :]
        m1 = jnp.max(blk, axis=0, keepdims=True)
        first = jnp.min(jnp.where(blk == m1, sub, GROUP_SIZE), axis=0, keepdims=True)
        m2 = jnp.max(jnp.where(sub == first, -jnp.inf, blk), axis=0, keepdims=True)
        gscore.append(m1 + m2)
    emask = []
    for g in range(N_GROUPS):
        beaten = jnp.zeros((1, n), jnp.int32)
        for g2 in range(N_GROUPS):
            if g2 == g:
                continue
            wins = (gscore[g2] > gscore[g]) | ((gscore[g2] == gscore[g]) & (g2 < g))
            beaten = beaten + wins.astype(jnp.int32)
        emask.append(jnp.broadcast_to(beaten < TOPK_GROUPS, (GROUP_SIZE, n)))
    cur = jnp.where(jnp.concatenate(emask, axis=0), sel, neg)
    chosen = jnp.zeros((E, n), jnp.bool_)
    for _ in range(TOP_K):
        m = jnp.max(cur, axis=0, keepdims=True)
        first = jnp.min(jnp.where(cur == m, eidx, E), axis=0, keepdims=True)
        hit = eidx == first
        chosen = chosen | hit
        cur = jnp.where(hit, neg, cur)
    w = jnp.where(chosen, scores_t, 0.0)
    return w / jnp.sum(w, axis=0, keepdims=True) * ROUTE_SCALE


def _ffn_pre_kernel(x_ref, mod_ref, g_ref, rwt_ref, rb_ref, h_ref, gates_ref):
    mod = mod_ref[0]
    h = _rms_mod(x_ref[0], g_ref[...], mod[4:5, :], mod[3:4, :])
    h_ref[0] = h.astype(BF16)
    logits_t = lax.dot_general(rwt_ref[...], h, (((1,), (1,)), ((), ())),
                               precision=HIGHEST, preferred_element_type=F32)
    gates_t = _route(_sigmoid(logits_t), rb_ref[...])
    tb = gates_t.shape[1]
    padded = jnp.concatenate([gates_t, jnp.zeros((LANES - N_EXPERTS, tb), F32)], axis=0)
    gates_ref[0] = padded.T


def _ffn_pre(x, mod_l, norm_g, router_wt, router_b, *, tb):
    B, S, D = x.shape
    full = lambda a: pl.BlockSpec(a.shape, lambda b, j: (0,) * a.ndim)
    tok = lambda w: pl.BlockSpec((1, tb, w), lambda b, j: (b, j, 0))
    return pl.pallas_call(
        _ffn_pre_kernel,
        grid=(B, S // tb),
        in_specs=[tok(D), pl.BlockSpec((1, N_MOD, D), lambda b, j: (b, 0, 0)),
                  full(norm_g), full(router_wt), full(router_b)],
        out_specs=[tok(D), tok(LANES)],
        out_shape=[jax.ShapeDtypeStruct((B, S, D), BF16), jax.ShapeDtypeStruct((B, S, LANES), F32)],
        compiler_params=_params("arbitrary", "arbitrary"),
        name="ffn_pre",
    )(x, mod_l, norm_g, router_wt, router_b)


def _moe_kernel(x_ref, mod_ref, h_ref, gates_ref, w1_ref, w3_ref, w2_ref, sw1_ref, sw3_ref, sw2_ref,
                fg_ref, o_ref, acc_ref, *, final_norm):
    e = pl.program_id(2)
    hb = h_ref[0]

    @pl.when(e == 0)
    def _():
        a = jnp.dot(hb, sw1_ref[...], preferred_element_type=F32)
        b = jnp.dot(hb, sw3_ref[...], preferred_element_type=F32)
        acc_ref[...] = jnp.dot((_silu(a) * b).astype(BF16), sw2_ref[...], preferred_element_type=F32)

    lane = lax.broadcasted_iota(jnp.int32, (1, LANES), 1)
    gcol = jnp.sum(jnp.where(lane == e, gates_ref[0], 0.0), axis=1, keepdims=True)
    a = jnp.dot(hb, w1_ref[0], preferred_element_type=F32)
    b = jnp.dot(hb, w3_ref[0], preferred_element_type=F32)
    hid = _silu(a) * b * gcol
    acc_ref[...] += jnp.dot(hid.astype(BF16), w2_ref[0], preferred_element_type=F32)

    @pl.when(e == N_EXPERTS - 1)
    def _():
        y = x_ref[0] + mod_ref[0][5:6, :] * acc_ref[...]
        if final_norm:
            y = y * lax.rsqrt(jnp.mean(y * y, axis=-1, keepdims=True) + EPS) * fg_ref[...]
        o_ref[0] = y


def _moe(x, mod_l, hb, gates, w1, w3, w2, sw1, sw3, sw2, final_g, *, tm, final_norm):
    B, S, D = x.shape
    full = lambda a: pl.BlockSpec(a.shape, lambda b, j, e: (0,) * a.ndim)
    tok = lambda w: pl.BlockSpec((1, tm, w), lambda b, j, e: (b, j, 0))
    exp = lambda a: pl.BlockSpec((1,) + a.shape[1:], lambda b, j, e: (e, 0, 0))
    return pl.pallas_call(
        functools.partial(_moe_kernel, final_norm=final_norm),
        grid=(B, S // tm, N_EXPERTS),
        in_specs=[tok(D), pl.BlockSpec((1, N_MOD, D), lambda b, j, e: (b, 0, 0)), tok(D), tok(LANES),
                  exp(w1), exp(w3), exp(w2), full(sw1), full(sw3), full(sw2), full(final_g)],
        out_specs=tok(D),
        out_shape=jax.ShapeDtypeStruct((B, S, D), F32),
        scratch_shapes=[pltpu.VMEM((tm, D), F32)],
        compiler_params=_params("arbitrary", "arbitrary", "arbitrary"),
        name="moe",
    )(x, mod_l, hb, gates, w1, w3, w2, sw1, sw3, sw2, final_g)


def _block_diag(w):
    G, a, b = w.shape
    out = jnp.zeros((G * a, G * b), w.dtype)
    for g in range(G):
        out = out.at[g * a:(g + 1) * a, g * b:(g + 1) * b].set(w[g])
    return out


def kernel(x, c, w_in, b_f, pool_w, pool_scale, sg_ln_g, sg_ln_b, sg_w, sg_b, conv_w, conv_b, conv_ln_g,
           conv_ln_b, w_branch, w_out, mix_norm_g, ffn_norm_g, ada_w, ada_b, router_w, router_bias,
           exp_w1, exp_w3, exp_w2, shared_w1, shared_w3, shared_w2, final_norm_g):
    B, S, D = x.shape
    L = w_in.shape[0]
    tb = min(512, S)
    tq = min(256, S)
    tm = min(1024, S)
    row = lambda a: a.reshape(1, -1)

    mod = _modulation(c, ada_w, ada_b)
    for l in range(L):
        wmix = w_in[l, :, :MIX_COLS].astype(BF16)
        wf = w_in[l, :, MIX_COLS:MIX_COLS + ATT_HEADS]
        wfc = jnp.pad(wf, ((0, 0), (0, LANES - ATT_HEADS))).astype(BF16)
        wft = jnp.pad(wf.T, ((0, SUBLANES - ATT_HEADS), (0, 0))).astype(BF16)
        bfc = jnp.pad(b_f[l], (0, LANES - ATT_HEADS)).reshape(1, LANES)
        bft = jnp.pad(b_f[l], (0, SUBLANES - ATT_HEADS)).reshape(SUBLANES, 1)
        wgate = w_in[l, :, MIX_COLS + ATT_HEADS:].astype(BF16)
        poolw = _block_diag(pool_w[l]).astype(BF16)
        sgb = jnp.repeat(sg_b[l].T, SG_HD, axis=1)

        ya, yb, yc, q, k, v, cum_c, cum_t = _mixer_in(
            x, mod[l], row(mix_norm_g[l]), wmix, wfc, wft, bfc, bft, poolw, row(pool_scale[l]),
            row(sg_ln_g[l]), row(sg_ln_b[l]), sg_w[l], sgb, conv_w[l], row(conv_b[l]),
            row(conv_ln_g[l]), row(conv_ln_b[l]), tb=tb)
        yd = _attention(q, k, v, cum_c, cum_t, tq=tq)
        x = _merge(x, mod[l], row(mix_norm_g[l]), wgate, w_branch[l].astype(BF16), w_out[l].astype(BF16),
                   ya, yb, yc, yd, tb=tb)
        hb, gates = _ffn_pre(x, mod[l], row(ffn_norm_g[l]), router_w[l].T, router_bias[l].reshape(-1, 1), tb=tb)
        x = _moe(x, mod[l], hb, gates, exp_w1[l].astype(BF16), exp_w3[l].astype(BF16), exp_w2[l].astype(BF16),
                 shared_w1[l].astype(BF16), shared_w3[l].astype(BF16), shared_w2[l].astype(BF16),
                 row(final_norm_g), tm=tm, final_norm=(l == L - 1))
    return x
```

```python
import functools
import math

import jax
import jax.numpy as jnp
from jax import lax
from jax.experimental import pallas as pl
from jax.experimental.pallas import tpu as pltpu

F32 = jnp.float32
BF16 = jnp.bfloat16
HIGHEST = lax.Precision.HIGHEST

D_MODEL = 1024
BRANCH_W = 256
N_BRANCH = 4
POOL_WINDOWS = (2, 4, 8, 16)
POOL_GW = 64
POOL_TAIL = 16
SG_HEADS = 4
SG_CHUNK = 128
SG_HD = 64
CONV_WIDTH = 31
CONV_TAIL = 32
ATT_HEADS = 4
ATT_HD = 64
N_EXPERTS = 64
TOP_K = 6
N_GROUPS = 8
GROUP_SIZE = N_EXPERTS // N_GROUPS
TOPK_GROUPS = 4
EXPERT_FF = 256
SHARED_FF = 256
ROUTE_SCALE = 2.5
EPS = 1e-6
N_MOD = 6
MIX_COLS = 8 * BRANCH_W
LANES = 128
SUBLANES = 8
LOG2E = 1.4426950408889634
VMEM_LIMIT = 56 * 1024 * 1024


def _sigmoid(x):
    return 1.0 / (1.0 + jnp.exp(-x))


def _silu(x):
    return x * _sigmoid(x)


def _rms_mod(x, g, scale, shift):
    y = x * lax.rsqrt(jnp.mean(x * x, axis=-1, keepdims=True) + EPS)
    return (y * g) * (1.0 + scale) + shift


def _layer_norm(x, g, b):
    mu = jnp.mean(x, axis=-1, keepdims=True)
    xc = x - mu
    var = jnp.mean(xc * xc, axis=-1, keepdims=True)
    return xc * lax.rsqrt(var + EPS) * g + b


def _params(*sem):
    return pltpu.CompilerParams(dimension_semantics=sem, vmem_limit_bytes=VMEM_LIMIT)


def _mod_kernel(c_ref, w_ref, b_ref, o_ref):
    c = c_ref[...]
    o_ref[0] = jnp.dot(_silu(c), w_ref[0], precision=HIGHEST, preferred_element_type=F32) + b_ref[0]


def _modulation(c, ada_w, ada_b):
    L, D, N = ada_w.shape
    B = c.shape[0]
    tn = 1536
    out = pl.pallas_call(
        _mod_kernel,
        grid=(L, N // tn),
        in_specs=[pl.BlockSpec((B, D), lambda l, j: (0, 0)),
                  pl.BlockSpec((1, D, tn), lambda l, j: (l, 0, j)),
                  pl.BlockSpec((1, 1, tn), lambda l, j: (l, 0, j))],
        out_specs=pl.BlockSpec((1, B, tn), lambda l, j: (l, 0, j)),
        out_shape=jax.ShapeDtypeStruct((L, B, N), F32),
        compiler_params=_params("arbitrary", "arbitrary"),
        name="modulation",
    )(c, ada_w, ada_b.reshape(L, 1, N))
    return out.reshape(L, B, N_MOD, D)


def _mixer_in_kernel(x_ref, mod_ref, g_ref, wmix_ref, wfc_ref, bfc_ref,
                     poolw_ref, pools_ref, sglg_ref, sglb_ref, sgw_ref, sgb_ref,
                     cw_ref, cb_ref, clg_ref, clb_ref,
                     ya_ref, yb_ref, yc_ref, q_ref, k_ref, v_ref,
                     pool_ext, conv_ext, cum_c, *, tb):
    j = pl.program_id(1)

    @pl.when(j == 0)
    def _():
        pool_ext[0:POOL_TAIL, :] = jnp.zeros((POOL_TAIL, BRANCH_W), F32)
        conv_ext[0:CONV_TAIL, :] = jnp.zeros((CONV_TAIL, BRANCH_W), F32)
        cum_c[...] = jnp.zeros_like(cum_c)

    mod = mod_ref[0]
    h = _rms_mod(x_ref[0], g_ref[...], mod[1:2, :], mod[0:1, :])
    hb = h.astype(BF16)
    proj = jnp.dot(hb, wmix_ref[...], preferred_element_type=F32)

    lane = lax.broadcasted_iota(jnp.int32, (1, BRANCH_W), 1)
    row = lax.broadcasted_iota(jnp.int32, (tb, 1), 0)

    u = proj[:, 0:BRANCH_W]
    pool_ext[POOL_TAIL:POOL_TAIL + tb, :] = u
    ext = pool_ext[...]
    s2 = ext + pltpu.roll(ext, 1, 0)
    s4 = s2 + pltpu.roll(s2, 2, 0)
    s8 = s4 + pltpu.roll(s4, 4, 0)
    s16 = s8 + pltpu.roll(s8, 8, 0)
    grp = lane // POOL_GW
    wsum = jnp.where(grp == 0, s2, jnp.where(grp == 1, s4, jnp.where(grp == 2, s8, s16)))[POOL_TAIL:, :]
    win = jnp.where(grp == 0, 2.0, jnp.where(grp == 1, 4.0, jnp.where(grp == 2, 8.0, 16.0)))
    count = jnp.minimum((j * tb + row + 1).astype(F32), win)
    pooled = wsum / count - u
    ya = jnp.dot(pooled.astype(BF16), poolw_ref[...], preferred_element_type=F32) * pools_ref[...]
    ya_ref[0] = ya.astype(BF16)
    pool_ext[0:POOL_TAIL, :] = u[tb - POOL_TAIL:, :]

    z = proj[:, BRANCH_W:3 * BRANCH_W]
    z = 0.5 * z * (1.0 + jnp.tanh(math.sqrt(2.0 / math.pi) * (z + 0.044715 * (z * z * z))))
    su = z[:, 0:BRANCH_W]
    sv = _layer_norm(z[:, BRANCH_W:], sglg_ref[...], sglb_ref[...])
    r128 = lax.broadcasted_iota(jnp.int32, (SG_CHUNK, SG_CHUNK), 0)
    c128 = lax.broadcasted_iota(jnp.int32, (SG_CHUNK, SG_CHUNK), 1)
    wcat = jnp.concatenate(
        [jnp.where(r128 >= c128, sgw_ref[hh], 0.0) for hh in range(SG_HEADS)], axis=1).astype(BF16)
    head = lane // SG_HD
    for ci in range(tb // SG_CHUNK):
        rows = slice(ci * SG_CHUNK, (ci + 1) * SG_CHUNK)
        vch = sv[rows, :]
        vstack = jnp.concatenate(
            [jnp.where(head == hh, vch, 0.0) for hh in range(SG_HEADS)], axis=0).astype(BF16)
        s = jnp.dot(wcat, vstack, preferred_element_type=F32) + sgb_ref[...]
        yb_ref[0, rows, :] = (su[rows, :] * s).astype(BF16)

    glu = proj[:, 3 * BRANCH_W:4 * BRANCH_W] * _sigmoid(proj[:, 4 * BRANCH_W:5 * BRANCH_W])
    conv_ext[CONV_TAIL:CONV_TAIL + tb, :] = glu
    acc = jnp.zeros((tb, BRANCH_W), F32) + cb_ref[...]
    for kk in range(CONV_WIDTH):
        off = CONV_TAIL - (CONV_WIDTH - 1) + kk
        acc = acc + cw_ref[kk:kk + 1, :] * conv_ext[off:off + tb, :]
    yc_ref[0] = _silu(_layer_norm(acc, clg_ref[...], clb_ref[...])).astype(BF16)
    conv_ext[0:CONV_TAIL, :] = glu[tb - CONV_TAIL:, :]

    def log_sigmoid(t):
        return jnp.minimum(t, 0.0) - jnp.log(1.0 + jnp.exp(-jnp.abs(t)))

    rr = lax.broadcasted_iota(jnp.int32, (tb, tb), 0)
    cc = lax.broadcasted_iota(jnp.int32, (tb, tb), 1)
    lower = jnp.where(rr >= cc, 1.0, 0.0).astype(F32)
    lf_c = log_sigmoid(jnp.dot(hb, wfc_ref[...], preferred_element_type=F32) + bfc_ref[...])
    cs_c = jnp.dot(lower, lf_c, precision=HIGHEST, preferred_element_type=F32) + cum_c[...]
    cum_c[...] = cs_c[tb - 1:tb, :]
    cl2 = cs_c * LOG2E
    ln = lax.broadcasted_iota(jnp.int32, (1, LANES), 1)
    ones_q = jnp.where((ln >= ATT_HD + 3) & (ln < ATT_HD + 6), 1.0, 0.0)
    ones_k = jnp.where((ln >= ATT_HD) & (ln < ATT_HD + 3), 1.0, 0.0)
    ones_v = jnp.where(ln == ATT_HD, 1.0, 0.0)
    for hh in range(ATT_HEADS):
        pair = (hh // 2) * LANES
        qs = proj[:, 5 * BRANCH_W + pair:5 * BRANCH_W + pair + LANES] * (LOG2E / math.sqrt(ATT_HD))
        ks = proj[:, 6 * BRANCH_W + pair:6 * BRANCH_W + pair + LANES]
        vs = proj[:, 7 * BRANCH_W + pair:7 * BRANCH_W + pair + LANES]
        if hh % 2:
            qs, ks, vs = (pltpu.roll(a, ATT_HD, 1) for a in (qs, ks, vs))
        c = cl2[:, hh:hh + 1]
        hi = c.astype(BF16).astype(F32)
        mid = (c - hi).astype(BF16).astype(F32)
        lo = (c - hi) - mid
        q_extra = jnp.where(ln == ATT_HD, hi, jnp.where(ln == ATT_HD + 1, mid, jnp.where(ln == ATT_HD + 2, lo, ones_q)))
        k_extra = jnp.where(ln == ATT_HD + 3, -hi,
                            jnp.where(ln == ATT_HD + 4, -mid, jnp.where(ln == ATT_HD + 5, -lo, ones_k)))
        q_ref[0, hh, 0] = jnp.where(ln < ATT_HD, qs, q_extra).T.astype(BF16)
        k_ref[0, hh] = jnp.where(ln < ATT_HD, ks, k_extra).astype(BF16)
        v_ref[0, hh, 0] = jnp.where(ln < ATT_HD, vs, ones_v).T.astype(BF16)


def _mixer_in(x, mod_l, norm_g, wmix, wfc, bfc, poolw, pools, sglg, sglb, sgw, sgb, cw, cb, clg, clb, *, tb):
    B, S, D = x.shape
    full = lambda a: pl.BlockSpec(a.shape, lambda b, j: (0,) * a.ndim)
    tok = lambda w: pl.BlockSpec((1, tb, w), lambda b, j: (b, j, 0))
    head = pl.BlockSpec((1, ATT_HEADS, tb, LANES), lambda b, j: (b, 0, j, 0))
    head_t = pl.BlockSpec((1, ATT_HEADS, 1, LANES, tb), lambda b, j: (b, 0, j, 0, 0))
    consts = (norm_g, wmix, wfc, bfc, poolw, pools, sglg, sglb, sgw, sgb, cw, cb, clg, clb)
    act = jax.ShapeDtypeStruct((B, S, BRANCH_W), BF16)
    att = jax.ShapeDtypeStruct((B, ATT_HEADS, S, LANES), BF16)
    att_t = jax.ShapeDtypeStruct((B, ATT_HEADS, S // tb, LANES, tb), BF16)
    return pl.pallas_call(
        functools.partial(_mixer_in_kernel, tb=tb),
        grid=(B, S // tb),
        in_specs=[tok(D), pl.BlockSpec((1, N_MOD, D), lambda b, j: (b, 0, 0))] + [full(a) for a in consts],
        out_specs=[tok(BRANCH_W)] * 3 + [head_t, head, head_t],
        out_shape=[act] * 3 + [att_t, att, att_t],
        scratch_shapes=[pltpu.VMEM((POOL_TAIL + tb, BRANCH_W), F32),
                        pltpu.VMEM((CONV_TAIL + tb, BRANCH_W), F32),
                        pltpu.VMEM((1, LANES), F32)],
        compiler_params=_params("arbitrary", "arbitrary"),
        name="mixer_in",
    )(x, mod_l, *consts)


def _attn_kernel(q_ref, k_ref, v_ref, o_ref, m_ref, acc_ref, *, tq):
    i = pl.program_id(1)
    key = lax.broadcasted_iota(jnp.int32, (tq, tq), 0)
    qry = lax.broadcasted_iota(jnp.int32, (tq, tq), 1)
    m_ref[...] = jnp.full(m_ref.shape, -jnp.inf, F32)
    acc_ref[...] = jnp.zeros(acc_ref.shape, F32)

    def block(kj, diagonal):
        ks = pl.multiple_of(kj * tq, tq)
        logits = [jnp.dot(k_ref[0, hh, pl.ds(ks, tq), :], q_ref[0, hh, 0], preferred_element_type=F32)
                  for hh in range(ATT_HEADS)]
        for hh in range(ATT_HEADS):
            s = logits[hh]
            if diagonal:
                s = jnp.where(key <= qry, s, -jnp.inf)
            m_old = m_ref[hh]
            m_new = jnp.maximum(m_old, jnp.max(s, axis=0, keepdims=True))
            p = jnp.exp2(s - m_new)
            pv = jnp.dot(v_ref[0, hh, kj], p.astype(BF16), preferred_element_type=F32)
            acc_ref[hh] = jnp.exp2(m_old - m_new) * acc_ref[hh] + pv
            m_ref[hh] = m_new

    def body(kj, carry):
        block(kj, False)
        return carry

    lax.fori_loop(0, i, body, 0)
    block(i, True)

    ln = lax.broadcasted_iota(jnp.int32, (1, LANES), 1)
    for pr in range(ATT_HEADS // 2):
        o = []
        for hh in (2 * pr, 2 * pr + 1):
            acc = acc_ref[hh]
            o.append((acc / acc[ATT_HD:ATT_HD + 1, :]).T)
        o_ref[0, :, pr * LANES:(pr + 1) * LANES] = jnp.where(ln < ATT_HD, o[0], pltpu.roll(o[1], ATT_HD, 1)).astype(BF16)


def _attention(q_t, k, v_t, *, tq):
    B, H, S, W = k.shape
    nblk = S // tq
    assert q_t.shape == (B, H, nblk, W, tq) and v_t.shape == q_t.shape
    return pl.pallas_call(
        functools.partial(_attn_kernel, tq=tq),
        grid=(B, nblk),
        in_specs=[pl.BlockSpec((1, H, 1, W, tq), lambda b, i: (b, 0, i, 0, 0)),
                  pl.BlockSpec((1, H, S, W), lambda b, i: (b, 0, 0, 0), pipeline_mode=pl.Buffered(1)),
                  pl.BlockSpec((1, H, nblk, W, tq), lambda b, i: (b, 0, 0, 0, 0), pipeline_mode=pl.Buffered(1))],
        out_specs=pl.BlockSpec((1, tq, BRANCH_W), lambda b, i: (b, i, 0)),
        out_shape=jax.ShapeDtypeStruct((B, S, BRANCH_W), BF16),
        scratch_shapes=[pltpu.VMEM((H, 1, tq), F32), pltpu.VMEM((H, W, tq), F32)],
        compiler_params=_params("arbitrary", "arbitrary"),
        name="attention",
    )(q_t, k, v_t)


def _merge_kernel(x_ref, mod_ref, g_ref, wg_ref, wbr_ref, wout_ref, ya_ref, yb_ref, yc_ref, yd_ref, o_ref):
    x = x_ref[0]
    mod = mod_ref[0]
    hb = _rms_mod(x, g_ref[...], mod[1:2, :], mod[0:1, :]).astype(BF16)
    merged = jnp.zeros(x.shape, F32)
    for n, y_ref in enumerate((ya_ref, yb_ref, yc_ref, yd_ref)):
        gate = _sigmoid(jnp.dot(hb, wg_ref[:, n * D_MODEL:(n + 1) * D_MODEL], preferred_element_type=F32))
        merged = merged + gate * jnp.dot(y_ref[0], wbr_ref[n], preferred_element_type=F32)
    out = jnp.dot(merged.astype(BF16), wout_ref[...], preferred_element_type=F32)
    o_ref[0] = x + mod[2:3, :] * out


def _merge(x, mod_l, norm_g, wgate, wbranch, wout, ya, yb, yc, yd, *, tb):
    B, S, D = x.shape
    full = lambda a: pl.BlockSpec(a.shape, lambda b, j: (0,) * a.ndim)
    tok = lambda w: pl.BlockSpec((1, tb, w), lambda b, j: (b, j, 0))
    return pl.pallas_call(
        _merge_kernel,
        grid=(B, S // tb),
        in_specs=[tok(D), pl.BlockSpec((1, N_MOD, D), lambda b, j: (b, 0, 0)),
                  full(norm_g), full(wgate), full(wbranch), full(wout)] + [tok(BRANCH_W)] * 4,
        out_specs=tok(D),
        out_shape=jax.ShapeDtypeStruct((B, S, D), F32),
        compiler_params=_params("arbitrary", "arbitrary"),
        name="merge",
    )(x, mod_l, norm_g, wgate, wbranch, wout, ya, yb, yc, yd)


def _route(scores_t, bias_t):
    E, n = scores_t.shape
    sel = scores_t + bias_t
    eidx = lax.broadcasted_iota(jnp.int32, (E, n), 0)
    neg = jnp.full((E, n), -jnp.inf, F32)
    gscore = []
    sub = lax.broadcasted_iota(jnp.int32, (GROUP_SIZE, n), 0)
    for g in range(N_GROUPS):
        blk = sel[g * GROUP_SIZE:(g + 1) * GROUP_SIZE, :]
        m1 = jnp.max(blk, axis=0, keepdims=True)
        first = jnp.min(jnp.where(blk == m1, sub, GROUP_SIZE), axis=0, keepdims=True)
        m2 = jnp.max(jnp.where(sub == first, -jnp.inf, blk), axis=0, keepdims=True)
        gscore.append(m1 + m2)
    emask = []
    for g in range(N_GROUPS):
        beaten = jnp.zeros((1, n), jnp.int32)
        for g2 in range(N_GROUPS):
            if g2 == g:
                continue
            wins = (gscore[g2] > gscore[g]) | ((gscore[g2] == gscore[g]) & (g2 < g))
            beaten = beaten + wins.astype(jnp.int32)
        emask.append(jnp.broadcast_to(beaten < TOPK_GROUPS, (GROUP_SIZE, n)))
    cur = jnp.where(jnp.concatenate(emask, axis=0), sel, neg)
    chosen = jnp.zeros((E, n), jnp.bool_)
    for _ in range(TOP_K):
        m = jnp.max(cur, axis=0, keepdims=True)
        first = jnp.min(jnp.where(cur == m, eidx, E), axis=0, keepdims=True)
        hit = eidx == first
        chosen = chosen | hit
        cur = jnp.where(hit, neg, cur)
    w = jnp.where(chosen, scores_t, 0.0)
    return w / jnp.sum(w, axis=0, keepdims=True) * ROUTE_SCALE


def _ffn_pre_kernel(x_ref, mod_ref, g_ref, rwt_ref, rb_ref, h_ref, gates_ref):
    mod = mod_ref[0]
    h = _rms_mod(x_ref[0], g_ref[...], mod[4:5, :], mod[3:4, :])
    h_ref[0] = h.astype(BF16)
    logits_t = lax.dot_general(rwt_ref[...], h, (((1,), (1,)), ((), ())),
                               precision=HIGHEST, preferred_element_type=F32)
    gates_t = _route(_sigmoid(logits_t), rb_ref[...])
    tb = gates_t.shape[1]
    padded = jnp.concatenate([gates_t, jnp.zeros((LANES - N_EXPERTS, tb), F32)], axis=0)
    gates_ref[0] = padded.T


def _ffn_pre(x, mod_l, norm_g, router_wt, router_b, *, tb):
    B, S, D = x.shape
    full = lambda a: pl.BlockSpec(a.shape, lambda b, j: (0,) * a.ndim)
    tok = lambda w: pl.BlockSpec((1, tb, w), lambda b, j: (b, j, 0))
    return pl.pallas_call(
        _ffn_pre_kernel,
        grid=(B, S // tb),
        in_specs=[tok(D), pl.BlockSpec((1, N_MOD, D), lambda b, j: (b, 0, 0)),
                  full(norm_g), full(router_wt), full(router_b)],
        out_specs=[tok(D), tok(LANES)],
        out_shape=[jax.ShapeDtypeStruct((B, S, D), BF16), jax.ShapeDtypeStruct((B, S, LANES), F32)],
        compiler_params=_params("arbitrary", "arbitrary"),
        name="ffn_pre",
    )(x, mod_l, norm_g, router_wt, router_b)


def _moe_kernel(x_ref, mod_ref, h_ref, gates_ref, w1_ref, w3_ref, w2_ref, sw1_ref, sw3_ref, sw2_ref,
                fg_ref, o_ref, acc_ref, *, final_norm):
    e = pl.program_id(2)
    hb = h_ref[0]

    @pl.when(e == 0)
    def _():
        a = jnp.dot(hb, sw1_ref[...], preferred_element_type=F32)
        b = jnp.dot(hb, sw3_ref[...], preferred_element_type=F32)
        acc_ref[...] = jnp.dot((_silu(a) * b).astype(BF16), sw2_ref[...], preferred_element_type=F32)

    lane = lax.broadcasted_iota(jnp.int32, (1, LANES), 1)
    gcol = jnp.sum(jnp.where(lane == e, gates_ref[0], 0.0), axis=1, keepdims=True)
    a = jnp.dot(hb, w1_ref[0], preferred_element_type=F32)
    b = jnp.dot(hb, w3_ref[0], preferred_element_type=F32)
    hid = _silu(a) * b * gcol
    acc_ref[...] += jnp.dot(hid.astype(BF16), w2_ref[0], preferred_element_type=F32)

    @pl.when(e == N_EXPERTS - 1)
    def _():
        y = x_ref[0] + mod_ref[0][5:6, :] * acc_ref[...]
        if final_norm:
            y = y * lax.rsqrt(jnp.mean(y * y, axis=-1, keepdims=True) + EPS) * fg_ref[...]
        o_ref[0] = y


def _moe(x, mod_l, hb, gates, w1, w3, w2, sw1, sw3, sw2, final_g, *, tm, final_norm):
    B, S, D = x.shape
    full = lambda a: pl.BlockSpec(a.shape, lambda b, j, e: (0,) * a.ndim)
    tok = lambda w: pl.BlockSpec((1, tm, w), lambda b, j, e: (b, j, 0))
    exp = lambda a: pl.BlockSpec((1,) + a.shape[1:], lambda b, j, e: (e, 0, 0))
    return pl.pallas_call(
        functools.partial(_moe_kernel, final_norm=final_norm),
        grid=(B, S // tm, N_EXPERTS),
        in_specs=[tok(D), pl.BlockSpec((1, N_MOD, D), lambda b, j, e: (b, 0, 0)), tok(D), tok(LANES),
                  exp(w1), exp(w3), exp(w2), full(sw1), full(sw3), full(sw2), full(final_g)],
        out_specs=tok(D),
        out_shape=jax.ShapeDtypeStruct((B, S, D), F32),
        scratch_shapes=[pltpu.VMEM((tm, D), F32)],
        compiler_params=_params("arbitrary", "arbitrary", "arbitrary"),
        name="moe",
    )(x, mod_l, hb, gates, w1, w3, w2, sw1, sw3, sw2, final_g)


def _block_diag(w):
    G, a, b = w.shape
    out = jnp.zeros((G * a, G * b), w.dtype)
    for g in range(G):
        out = out.at[g * a:(g + 1) * a, g * b:(g + 1) * b].set(w[g])
    return out


def kernel(x, c, w_in, b_f, pool_w, pool_scale, sg_ln_g, sg_ln_b, sg_w, sg_b, conv_w, conv_b, conv_ln_g,
           conv_ln_b, w_branch, w_out, mix_norm_g, ffn_norm_g, ada_w, ada_b, router_w, router_bias,
           exp_w1, exp_w3, exp_w2, shared_w1, shared_w3, shared_w2, final_norm_g):
    B, S, D = x.shape
    L = w_in.shape[0]
    tb = min(512, S)
    tq = tb
    tm = min(1024, S)
    row = lambda a: a.reshape(1, -1)

    mod = _modulation(c, ada_w, ada_b)
    for l in range(L):
        wmix = w_in[l, :, :MIX_COLS].astype(BF16)
        wf = w_in[l, :, MIX_COLS:MIX_COLS + ATT_HEADS]
        wfc = jnp.pad(wf, ((0, 0), (0, LANES - ATT_HEADS))).astype(BF16)
        bfc = jnp.pad(b_f[l], (0, LANES - ATT_HEADS)).reshape(1, LANES)
        wgate = w_in[l, :, MIX_COLS + ATT_HEADS:].astype(BF16)
        poolw = _block_diag(pool_w[l]).astype(BF16)
        sgb = jnp.repeat(sg_b[l].T, SG_HD, axis=1)

        ya, yb, yc, q, k, v = _mixer_in(
            x, mod[l], row(mix_norm_g[l]), wmix, wfc, bfc, poolw, row(pool_scale[l]),
            row(sg_ln_g[l]), row(sg_ln_b[l]), sg_w[l], sgb, conv_w[l], row(conv_b[l]),
            row(conv_ln_g[l]), row(conv_ln_b[l]), tb=tb)
        yd = _attention(q, k, v, tq=tq)
        x = _merge(x, mod[l], row(mix_norm_g[l]), wgate, w_branch[l].astype(BF16), w_out[l].astype(BF16),
                   ya, yb, yc, yd, tb=tb)
        hb, gates = _ffn_pre(x, mod[l], row(ffn_norm_g[l]), router_w[l].T, router_bias[l].reshape(-1, 1), tb=tb)
        x = _moe(x, mod[l], hb, gates, exp_w1[l].astype(BF16), exp_w3[l].astype(BF16), exp_w2[l].astype(BF16),
                 shared_w1[l].astype(BF16), shared_w3[l].astype(BF16), shared_w2[l].astype(BF16),
                 row(final_norm_g), tm=tm, final_norm=(l == L - 1))
    return x
```

```python
import functools
import math

import jax
import jax.numpy as jnp
from jax import lax
from jax.experimental import pallas as pl
from jax.experimental.pallas import tpu as pltpu

F32 = jnp.float32
BF16 = jnp.bfloat16
HIGHEST = lax.Precision.HIGHEST

D_MODEL = 1024
BRANCH_W = 256
N_BRANCH = 4
POOL_WINDOWS = (2, 4, 8, 16)
POOL_GW = 64
POOL_TAIL = 16
SG_HEADS = 4
SG_CHUNK = 128
SG_HD = 64
CONV_WIDTH = 31
CONV_TAIL = 32
ATT_HEADS = 4
ATT_HD = 64
N_EXPERTS = 64
TOP_K = 6
N_GROUPS = 8
GROUP_SIZE = N_EXPERTS // N_GROUPS
TOPK_GROUPS = 4
EXPERT_FF = 256
SHARED_FF = 256
ROUTE_SCALE = 2.5
EPS = 1e-6
N_MOD = 6
MIX_COLS = 8 * BRANCH_W
LANES = 128
SUBLANES = 8
LOG2E = 1.4426950408889634
EXPERT_TILE = 256
VMEM_LIMIT = 56 * 1024 * 1024


def _sigmoid(x):
    return 1.0 / (1.0 + jnp.exp(-x))


def _silu(x):
    return x * _sigmoid(x)


def _rms_mod(x, g, scale, shift):
    y = x * lax.rsqrt(jnp.mean(x * x, axis=-1, keepdims=True) + EPS)
    return (y * g) * (1.0 + scale) + shift


def _layer_norm(x, g, b):
    mu = jnp.mean(x, axis=-1, keepdims=True)
    xc = x - mu
    var = jnp.mean(xc * xc, axis=-1, keepdims=True)
    return xc * lax.rsqrt(var + EPS) * g + b


def _params(*sem):
    return pltpu.CompilerParams(dimension_semantics=sem, vmem_limit_bytes=VMEM_LIMIT)


def _mod_kernel(c_ref, w_ref, b_ref, o_ref):
    c = c_ref[...]
    o_ref[0] = jnp.dot(_silu(c), w_ref[0], precision=HIGHEST, preferred_element_type=F32) + b_ref[0]


def _modulation(c, ada_w, ada_b):
    L, D, N = ada_w.shape
    B = c.shape[0]
    tn = 1536
    out = pl.pallas_call(
        _mod_kernel,
        grid=(L, N // tn),
        in_specs=[pl.BlockSpec((B, D), lambda l, j: (0, 0)),
                  pl.BlockSpec((1, D, tn), lambda l, j: (l, 0, j)),
                  pl.BlockSpec((1, 1, tn), lambda l, j: (l, 0, j))],
        out_specs=pl.BlockSpec((1, B, tn), lambda l, j: (l, 0, j)),
        out_shape=jax.ShapeDtypeStruct((L, B, N), F32),
        compiler_params=_params("arbitrary", "arbitrary"),
        name="modulation",
    )(c, ada_w, ada_b.reshape(L, 1, N))
    return out.reshape(L, B, N_MOD, D)


def _mixer_in_kernel(x_ref, mod_ref, g_ref, wmix_ref, wfc_ref, bfc_ref,
                     poolw_ref, pools_ref, sglg_ref, sglb_ref, sgw_ref, sgb_ref,
                     cw_ref, cb_ref, clg_ref, clb_ref,
                     ya_ref, yb_ref, yc_ref, q_ref, k_ref, v_ref,
                     pool_ext, conv_ext, cum_c, *, tb):
    j = pl.program_id(1)

    @pl.when(j == 0)
    def _():
        pool_ext[0:POOL_TAIL, :] = jnp.zeros((POOL_TAIL, BRANCH_W), F32)
        conv_ext[0:CONV_TAIL, :] = jnp.zeros((CONV_TAIL, BRANCH_W), F32)
        cum_c[...] = jnp.zeros_like(cum_c)

    mod = mod_ref[0]
    h = _rms_mod(x_ref[0], g_ref[...], mod[1:2, :], mod[0:1, :])
    hb = h.astype(BF16)
    proj = jnp.dot(hb, wmix_ref[...], preferred_element_type=F32)

    lane = lax.broadcasted_iota(jnp.int32, (1, BRANCH_W), 1)
    row = lax.broadcasted_iota(jnp.int32, (tb, 1), 0)

    u = proj[:, 0:BRANCH_W]
    pool_ext[POOL_TAIL:POOL_TAIL + tb, :] = u
    ext = pool_ext[...]
    s2 = ext + pltpu.roll(ext, 1, 0)
    s4 = s2 + pltpu.roll(s2, 2, 0)
    s8 = s4 + pltpu.roll(s4, 4, 0)
    s16 = s8 + pltpu.roll(s8, 8, 0)
    grp = lane // POOL_GW
    wsum = jnp.where(grp == 0, s2, jnp.where(grp == 1, s4, jnp.where(grp == 2, s8, s16)))[POOL_TAIL:, :]
    win = jnp.where(grp == 0, 2.0, jnp.where(grp == 1, 4.0, jnp.where(grp == 2, 8.0, 16.0)))
    count = jnp.minimum((j * tb + row + 1).astype(F32), win)
    pooled = wsum / count - u
    ya = jnp.dot(pooled.astype(BF16), poolw_ref[...], preferred_element_type=F32) * pools_ref[...]
    ya_ref[0] = ya.astype(BF16)
    pool_ext[0:POOL_TAIL, :] = u[tb - POOL_TAIL:, :]

    z = proj[:, BRANCH_W:3 * BRANCH_W]
    z = 0.5 * z * (1.0 + jnp.tanh(math.sqrt(2.0 / math.pi) * (z + 0.044715 * (z * z * z))))
    su = z[:, 0:BRANCH_W]
    sv = _layer_norm(z[:, BRANCH_W:], sglg_ref[...], sglb_ref[...])
    r128 = lax.broadcasted_iota(jnp.int32, (SG_CHUNK, SG_CHUNK), 0)
    c128 = lax.broadcasted_iota(jnp.int32, (SG_CHUNK, SG_CHUNK), 1)
    wcat = jnp.concatenate(
        [jnp.where(r128 >= c128, sgw_ref[hh], 0.0) for hh in range(SG_HEADS)], axis=1).astype(BF16)
    head = lane // SG_HD
    for ci in range(tb // SG_CHUNK):
        rows = slice(ci * SG_CHUNK, (ci + 1) * SG_CHUNK)
        vch = sv[rows, :]
        vstack = jnp.concatenate(
            [jnp.where(head == hh, vch, 0.0) for hh in range(SG_HEADS)], axis=0).astype(BF16)
        s = jnp.dot(wcat, vstack, preferred_element_type=F32) + sgb_ref[...]
        yb_ref[0, rows, :] = (su[rows, :] * s).astype(BF16)

    glu = proj[:, 3 * BRANCH_W:4 * BRANCH_W] * _sigmoid(proj[:, 4 * BRANCH_W:5 * BRANCH_W])
    conv_ext[CONV_TAIL:CONV_TAIL + tb, :] = glu
    acc = jnp.zeros((tb, BRANCH_W), F32) + cb_ref[...]
    for kk in range(CONV_WIDTH):
        off = CONV_TAIL - (CONV_WIDTH - 1) + kk
        acc = acc + cw_ref[kk:kk + 1, :] * conv_ext[off:off + tb, :]
    yc_ref[0] = _silu(_layer_norm(acc, clg_ref[...], clb_ref[...])).astype(BF16)
    conv_ext[0:CONV_TAIL, :] = glu[tb - CONV_TAIL:, :]

    def log_sigmoid(t):
        return jnp.minimum(t, 0.0) - jnp.log(1.0 + jnp.exp(-jnp.abs(t)))

    rr = lax.broadcasted_iota(jnp.int32, (tb, tb), 0)
    cc = lax.broadcasted_iota(jnp.int32, (tb, tb), 1)
    lower = jnp.where(rr >= cc, 1.0, 0.0).astype(F32)
    lf_c = log_sigmoid(jnp.dot(hb, wfc_ref[...], preferred_element_type=F32) + bfc_ref[...])
    cs_c = jnp.dot(lower, lf_c, precision=HIGHEST, preferred_element_type=F32) + cum_c[...]
    cum_c[...] = cs_c[tb - 1:tb, :]
    cl2 = cs_c * LOG2E
    ln = lax.broadcasted_iota(jnp.int32, (1, LANES), 1)
    ones_q = jnp.where((ln >= ATT_HD + 3) & (ln < ATT_HD + 6), 1.0, 0.0)
    ones_k = jnp.where((ln >= ATT_HD) & (ln < ATT_HD + 3), 1.0, 0.0)
    ones_v = jnp.where(ln == ATT_HD, 1.0, 0.0)
    for hh in range(ATT_HEADS):
        pair = (hh // 2) * LANES
        qs = proj[:, 5 * BRANCH_W + pair:5 * BRANCH_W + pair + LANES] * (LOG2E / math.sqrt(ATT_HD))
        ks = proj[:, 6 * BRANCH_W + pair:6 * BRANCH_W + pair + LANES]
        vs = proj[:, 7 * BRANCH_W + pair:7 * BRANCH_W + pair + LANES]
        if hh % 2:
            qs, ks, vs = (pltpu.roll(a, ATT_HD, 1) for a in (qs, ks, vs))
        c = cl2[:, hh:hh + 1]
        hi = c.astype(BF16).astype(F32)
        mid = (c - hi).astype(BF16).astype(F32)
        lo = (c - hi) - mid
        q_extra = jnp.where(ln == ATT_HD, hi, jnp.where(ln == ATT_HD + 1, mid, jnp.where(ln == ATT_HD + 2, lo, ones_q)))
        k_extra = jnp.where(ln == ATT_HD + 3, -hi,
                            jnp.where(ln == ATT_HD + 4, -mid, jnp.where(ln == ATT_HD + 5, -lo, ones_k)))
        q_ref[0, hh, 0] = jnp.where(ln < ATT_HD, qs, q_extra).T.astype(BF16)
        k_ref[0, hh] = jnp.where(ln < ATT_HD, ks, k_extra).astype(BF16)
        v_ref[0, hh, 0] = jnp.where(ln < ATT_HD, vs, ones_v).T.astype(BF16)


def _mixer_in(x, mod_l, norm_g, wmix, wfc, bfc, poolw, pools, sglg, sglb, sgw, sgb, cw, cb, clg, clb, *, tb):
    B, S, D = x.shape
    full = lambda a: pl.BlockSpec(a.shape, lambda b, j: (0,) * a.ndim)
    tok = lambda w: pl.BlockSpec((1, tb, w), lambda b, j: (b, j, 0))
    head = pl.BlockSpec((1, ATT_HEADS, tb, LANES), lambda b, j: (b, 0, j, 0))
    head_t = pl.BlockSpec((1, ATT_HEADS, 1, LANES, tb), lambda b, j: (b, 0, j, 0, 0))
    consts = (norm_g, wmix, wfc, bfc, poolw, pools, sglg, sglb, sgw, sgb, cw, cb, clg, clb)
    act = jax.ShapeDtypeStruct((B, S, BRANCH_W), BF16)
    att = jax.ShapeDtypeStruct((B, ATT_HEADS, S, LANES), BF16)
    att_t = jax.ShapeDtypeStruct((B, ATT_HEADS, S // tb, LANES, tb), BF16)
    return pl.pallas_call(
        functools.partial(_mixer_in_kernel, tb=tb),
        grid=(B, S // tb),
        in_specs=[tok(D), pl.BlockSpec((1, N_MOD, D), lambda b, j: (b, 0, 0))] + [full(a) for a in consts],
        out_specs=[tok(BRANCH_W)] * 3 + [head_t, head, head_t],
        out_shape=[act] * 3 + [att_t, att, att_t],
        scratch_shapes=[pltpu.VMEM((POOL_TAIL + tb, BRANCH_W), F32),
                        pltpu.VMEM((CONV_TAIL + tb, BRANCH_W), F32),
                        pltpu.VMEM((1, LANES), F32)],
        compiler_params=_params("arbitrary", "arbitrary"),
        name="mixer_in",
    )(x, mod_l, *consts)


def _attn_kernel(q_ref, k_ref, v_ref, o_ref, m_ref, acc_ref, *, tq):
    i = pl.program_id(1)
    key = lax.broadcasted_iota(jnp.int32, (tq, tq), 0)
    qry = lax.broadcasted_iota(jnp.int32, (tq, tq), 1)
    m_ref[...] = jnp.full(m_ref.shape, -jnp.inf, F32)
    acc_ref[...] = jnp.zeros(acc_ref.shape, F32)

    def block(kj, diagonal):
        ks = pl.multiple_of(kj * tq, tq)
        logits = [jnp.dot(k_ref[0, hh, pl.ds(ks, tq), :], q_ref[0, hh, 0], preferred_element_type=F32)
                  for hh in range(ATT_HEADS)]
        for hh in range(ATT_HEADS):
            s = logits[hh]
            if diagonal:
                s = jnp.where(key <= qry, s, -jnp.inf)
            m_old = m_ref[hh]
            m_new = jnp.maximum(m_old, jnp.max(s, axis=0, keepdims=True))
            p = jnp.exp2(s - m_new)
            pv = jnp.dot(v_ref[0, hh, kj], p.astype(BF16), preferred_element_type=F32)
            acc_ref[hh] = jnp.exp2(m_old - m_new) * acc_ref[hh] + pv
            m_ref[hh] = m_new

    def body(kj, carry):
        block(kj, False)
        return carry

    lax.fori_loop(0, i, body, 0)
    block(i, True)

    ln = lax.broadcasted_iota(jnp.int32, (1, LANES), 1)
    for pr in range(ATT_HEADS // 2):
        o = []
        for hh in (2 * pr, 2 * pr + 1):
            acc = acc_ref[hh]
            o.append((acc / acc[ATT_HD:ATT_HD + 1, :]).T)
        o_ref[0, :, pr * LANES:(pr + 1) * LANES] = jnp.where(ln < ATT_HD, o[0], pltpu.roll(o[1], ATT_HD, 1)).astype(BF16)


def _attention(q_t, k, v_t, *, tq):
    B, H, S, W = k.shape
    nblk = S // tq
    assert q_t.shape == (B, H, nblk, W, tq) and v_t.shape == q_t.shape
    return pl.pallas_call(
        functools.partial(_attn_kernel, tq=tq),
        grid=(B, nblk),
        in_specs=[pl.BlockSpec((1, H, 1, W, tq), lambda b, i: (b, 0, i, 0, 0)),
                  pl.BlockSpec((1, H, S, W), lambda b, i: (b, 0, 0, 0), pipeline_mode=pl.Buffered(1)),
                  pl.BlockSpec((1, H, nblk, W, tq), lambda b, i: (b, 0, 0, 0, 0), pipeline_mode=pl.Buffered(1))],
        out_specs=pl.BlockSpec((1, tq, BRANCH_W), lambda b, i: (b, i, 0)),
        out_shape=jax.ShapeDtypeStruct((B, S, BRANCH_W), BF16),
        scratch_shapes=[pltpu.VMEM((H, 1, tq), F32), pltpu.VMEM((H, W, tq), F32)],
        compiler_params=_params("arbitrary", "arbitrary"),
        name="attention",
    )(q_t, k, v_t)


def _merge_kernel(x_ref, mod_ref, g_ref, wg_ref, wbr_ref, wout_ref, ya_ref, yb_ref, yc_ref, yd_ref, o_ref):
    x = x_ref[0]
    mod = mod_ref[0]
    hb = _rms_mod(x, g_ref[...], mod[1:2, :], mod[0:1, :]).astype(BF16)
    merged = jnp.zeros(x.shape, F32)
    for n, y_ref in enumerate((ya_ref, yb_ref, yc_ref, yd_ref)):
        gate = _sigmoid(jnp.dot(hb, wg_ref[:, n * D_MODEL:(n + 1) * D_MODEL], preferred_element_type=F32))
        merged = merged + gate * jnp.dot(y_ref[0], wbr_ref[n], preferred_element_type=F32)
    out = jnp.dot(merged.astype(BF16), wout_ref[...], preferred_element_type=F32)
    o_ref[0] = x + mod[2:3, :] * out


def _merge(x, mod_l, norm_g, wgate, wbranch, wout, ya, yb, yc, yd, *, tb):
    B, S, D = x.shape
    full = lambda a: pl.BlockSpec(a.shape, lambda b, j: (0,) * a.ndim)
    tok = lambda w: pl.BlockSpec((1, tb, w), lambda b, j: (b, j, 0))
    return pl.pallas_call(
        _merge_kernel,
        grid=(B, S // tb),
        in_specs=[tok(D), pl.BlockSpec((1, N_MOD, D), lambda b, j: (b, 0, 0)),
                  full(norm_g), full(wgate), full(wbranch), full(wout)] + [tok(BRANCH_W)] * 4,
        out_specs=tok(D),
        out_shape=jax.ShapeDtypeStruct((B, S, D), F32),
        compiler_params=_params("arbitrary", "arbitrary"),
        name="merge",
    )(x, mod_l, norm_g, wgate, wbranch, wout, ya, yb, yc, yd)


def _route(scores_t, bias_t):
    E, n = scores_t.shape
    sel = scores_t + bias_t
    eidx = lax.broadcasted_iota(jnp.int32, (E, n), 0)
    neg = jnp.full((E, n), -jnp.inf, F32)
    gscore = []
    sub = lax.broadcasted_iota(jnp.int32, (GROUP_SIZE, n), 0)
    for g in range(N_GROUPS):
        blk = sel[g * GROUP_SIZE:(g + 1) * GROUP_SIZE, :]
        m1 = jnp.max(blk, axis=0, keepdims=True)
        first = jnp.min(jnp.where(blk == m1, sub, GROUP_SIZE), axis=0, keepdims=True)
        m2 = jnp.max(jnp.where(sub == first, -jnp.inf, blk), axis=0, keepdims=True)
        gscore.append(m1 + m2)
    emask = []
    for g in range(N_GROUPS):
        beaten = jnp.zeros((1, n), jnp.int32)
        for g2 in range(N_GROUPS):
            if g2 == g:
                continue
            wins = (gscore[g2] > gscore[g]) | ((gscore[g2] == gscore[g]) & (g2 < g))
            beaten = beaten + wins.astype(jnp.int32)
        emask.append(jnp.broadcast_to(beaten < TOPK_GROUPS, (GROUP_SIZE, n)))
    cur = jnp.where(jnp.concatenate(emask, axis=0), sel, neg)
    chosen = jnp.zeros((E, n), jnp.bool_)
    firsts = []
    for _ in range(TOP_K):
        m = jnp.max(cur, axis=0, keepdims=True)
        first = jnp.min(jnp.where(cur == m, eidx, E), axis=0, keepdims=True)
        hit = eidx == first
        chosen = chosen | hit
        cur = jnp.where(hit, neg, cur)
        firsts.append(first)
    w = jnp.where(chosen, scores_t, 0.0)
    return w / jnp.sum(w, axis=0, keepdims=True) * ROUTE_SCALE, chosen, firsts


def _route_kernel(x_ref, mod_ref, g_ref, rwt_ref, rb_ref, e_ref, p_ref, wt_ref, cnt_ref, count):
    first_step = (pl.program_id(0) == 0) & (pl.program_id(1) == 0)

    @pl.when(first_step)
    def _():
        count[...] = jnp.zeros_like(count)

    mod = mod_ref[0]
    h = _rms_mod(x_ref[0], g_ref[...], mod[4:5, :], mod[3:4, :])
    logits_t = lax.dot_general(rwt_ref[...], h, (((1,), (1,)), ((), ())),
                               precision=HIGHEST, preferred_element_type=F32)
    gates_t, chosen, firsts = _route(_sigmoid(logits_t), rb_ref[...])
    E, tb = gates_t.shape
    ones = jnp.where(chosen, 1.0, 0.0)
    rr = lax.broadcasted_iota(jnp.int32, (tb, tb), 0)
    cc = lax.broadcasted_iota(jnp.int32, (tb, tb), 1)
    upper = jnp.where(rr <= cc, 1.0, 0.0).astype(BF16)
    incl = jnp.dot(ones.astype(BF16), upper, preferred_element_type=F32)
    pos_all = count[...] + (incl - ones)
    eidx = lax.broadcasted_iota(jnp.int32, (E, tb), 0)
    pad = SUBLANES - TOP_K
    pos_rows = [jnp.sum(jnp.where(eidx == f, pos_all, 0.0), axis=0, keepdims=True) for f in firsts]
    w_rows = [jnp.sum(jnp.where(eidx == f, gates_t, 0.0), axis=0, keepdims=True) for f in firsts]
    e_ref[...] = jnp.concatenate(firsts + [jnp.zeros((pad, tb), jnp.int32)], axis=0)
    p_ref[...] = jnp.concatenate(pos_rows + [jnp.zeros((pad, tb), F32)], axis=0).astype(jnp.int32)
    w_pad = jnp.concatenate(w_rows + [jnp.zeros((LANES - TOP_K, tb), F32)], axis=0)
    wt_ref[0] = w_pad.T
    count[...] = count[...] + jnp.sum(ones, axis=1, keepdims=True)
    cnt_ref[...] = jnp.broadcast_to(count[...], cnt_ref.shape).astype(jnp.int32)


def _route_call(x, mod_l, norm_g, router_wt, router_b, *, tb):
    B, S, D = x.shape
    nj = S // tb
    full = lambda a: pl.BlockSpec(a.shape, lambda b, j: (0,) * a.ndim)
    tok = lambda w: pl.BlockSpec((1, tb, w), lambda b, j: (b, j, 0))
    kt = pl.BlockSpec((SUBLANES, tb), lambda b, j: (0, b * nj + j))
    return pl.pallas_call(
        _route_kernel,
        grid=(B, nj),
        in_specs=[tok(D), pl.BlockSpec((1, N_MOD, D), lambda b, j: (b, 0, 0)),
                  full(norm_g), full(router_wt), full(router_b)],
        out_specs=[kt, kt, tok(LANES), pl.BlockSpec((N_EXPERTS, LANES), lambda b, j: (0, 0))],
        out_shape=[jax.ShapeDtypeStruct((SUBLANES, B * S), jnp.int32),
                   jax.ShapeDtypeStruct((SUBLANES, B * S), jnp.int32),
                   jax.ShapeDtypeStruct((B, S, LANES), F32),
                   jax.ShapeDtypeStruct((N_EXPERTS, LANES), jnp.int32)],
        scratch_shapes=[pltpu.VMEM((N_EXPERTS, 1), F32)],
        compiler_params=_params("arbitrary", "arbitrary"),
        name="route",
    )(x, mod_l, norm_g, router_wt, router_b)


def _route_fin_kernel(e_ref, p_ref, cnt_ref, slot_ref, meta_ref, base_ref, *, tm):
    cnt = cnt_ref[...]
    ntile = lax.shift_right_logical(cnt + (tm - 1), int(math.log2(tm)))
    er = lax.broadcasted_iota(jnp.int32, (N_EXPERTS, N_EXPERTS), 0)
    ec = lax.broadcasted_iota(jnp.int32, (N_EXPERTS, N_EXPERTS), 1)
    before = jnp.where(ec < er, 1.0, 0.0).astype(F32)
    start = jnp.dot(before, ntile.astype(F32), precision=HIGHEST, preferred_element_type=F32).astype(jnp.int32)
    base = start * tm
    base_ref[...] = base
    e = e_ref[...]
    slot = p_ref[...]
    for ex in range(N_EXPERTS):
        slot = slot + jnp.where(e == ex, base[ex:ex + 1, 0:1], 0)
    slot_ref[...] = slot
    nlane = meta_ref.shape[1]
    lane = lax.broadcasted_iota(jnp.int32, (1, nlane), 1)
    end = (start + ntile)[:, 0:1]
    tile_expert = jnp.sum(jnp.where(end <= lane, 1, 0), axis=0, keepdims=True)
    tile_expert = jnp.minimum(tile_expert, N_EXPERTS - 1)
    meta_ref[...] = jnp.where(lane == nlane - 1, end[N_EXPERTS - 1:N_EXPERTS, :], tile_expert)


def _route_fin(e_idx, pos, cnt, *, tm, meta_lanes):
    full = lambda a: pl.BlockSpec(a.shape, lambda i: (0,) * a.ndim)
    outs = [jax.ShapeDtypeStruct(e_idx.shape, jnp.int32), jax.ShapeDtypeStruct((1, meta_lanes), jnp.int32),
            jax.ShapeDtypeStruct(cnt.shape, jnp.int32)]
    return pl.pallas_call(
        functools.partial(_route_fin_kernel, tm=tm),
        grid=(1,),
        in_specs=[full(e_idx), full(pos), full(cnt)],
        out_specs=[full(o) for o in outs],
        out_shape=outs,
        compiler_params=_params("arbitrary"),
        name="route_fin",
    )(e_idx, pos, cnt)


def _dispatch_kernel(cnt_sm, base_sm, slot_sm, x_ref, mod_ref, g_ref, xs_hbm, hs, zrow, sem, zsem, *, tb, tm):
    first_step = (pl.program_id(0) == 0) & (pl.program_id(1) == 0)
    mod = mod_ref[0]
    hs[...] = _rms_mod(x_ref[0], g_ref[...], mod[4:5, :], mod[3:4, :])

    def row_copy(t, k):
        return pltpu.make_async_copy(hs.at[pl.ds(t, 1)], xs_hbm.at[pl.ds(slot_sm[k, t], 1)], sem)

    def for_all_copies(fn):
        def body(t, carry):
            for k in range(TOP_K):
                fn(row_copy(t, k))
            return carry
        lax.fori_loop(0, tb, body, 0)

    for_all_copies(lambda cp: cp.start())

    @pl.when(first_step)
    def _():
        zrow[...] = jnp.zeros_like(zrow)

        def per_expert(e, carry):
            n = cnt_sm[e]
            first_pad = base_sm[e] + n
            npad = lax.rem(tm - lax.rem(n, tm), tm)
            zcopy = lambda r: pltpu.make_async_copy(zrow, xs_hbm.at[pl.ds(first_pad + r, 1)], zsem)
            lax.fori_loop(0, npad, lambda r, c: (zcopy(r).start(), c)[1], 0)
            lax.fori_loop(0, npad, lambda r, c: (zcopy(r).wait(), c)[1], 0)
            return carry

        lax.fori_loop(0, N_EXPERTS, per_expert, 0)

    for_all_copies(lambda cp: cp.wait())


def _dispatch(cnt, base, slot, x, mod_l, norm_g, *, tb, tm, n_slots):
    B, S, D = x.shape
    nj = S // tb
    grid_spec = pltpu.PrefetchScalarGridSpec(
        num_scalar_prefetch=2,
        grid=(B, nj),
        in_specs=[pl.BlockSpec((SUBLANES, tb), lambda b, j, *_: (0, b * nj + j), memory_space=pltpu.SMEM),
                  pl.BlockSpec((1, tb, D), lambda b, j, *_: (b, j, 0)),
                  pl.BlockSpec((1, N_MOD, D), lambda b, j, *_: (b, 0, 0)),
                  pl.BlockSpec(norm_g.shape, lambda b, j, *_: (0, 0))],
        out_specs=pl.BlockSpec(memory_space=pltpu.HBM),
        scratch_shapes=[pltpu.VMEM((tb, D), F32), pltpu.VMEM((1, D), F32),
                        pltpu.SemaphoreType.DMA, pltpu.SemaphoreType.DMA])
    return pl.pallas_call(
        functools.partial(_dispatch_kernel, tb=tb, tm=tm),
        grid_spec=grid_spec,
        out_shape=jax.ShapeDtypeStruct((n_slots, D), F32),
        compiler_params=_params("arbitrary", "arbitrary"),
        name="dispatch",
    )(cnt, base, slot, x, mod_l, norm_g)


def _experts_kernel(te_sm, nt_sm, x_ref, w1_ref, w3_ref, w2_ref, y_ref, w1b, w3b, w2b):
    i = pl.program_id(0)
    in_use = i < nt_sm[0]

    @pl.when(in_use)
    def _():
        @pl.when((i == 0) | (te_sm[i] != te_sm[jnp.maximum(i - 1, 0)]))
        def _():
            w1b[...] = w1_ref[0].astype(BF16)
            w3b[...] = w3_ref[0].astype(BF16)
            w2b[...] = w2_ref[0].astype(BF16)

        xb = x_ref[...].astype(BF16)
        a = jnp.dot(xb, w1b[...], preferred_element_type=F32)
        b = jnp.dot(xb, w3b[...], preferred_element_type=F32)
        y_ref[...] = jnp.dot((_silu(a) * b).astype(BF16), w2b[...], preferred_element_type=F32)

    @pl.when(jnp.logical_not(in_use))
    def _():
        y_ref[...] = jnp.zeros_like(y_ref)


def _experts(tile_expert, n_tiles, xs, w1, w3, w2, *, tm):
    n_slots, D = xs.shape
    E, _, F = w1.shape
    last = lambda i, te, nt: jnp.minimum(i, nt[0] - 1)
    grid_spec = pltpu.PrefetchScalarGridSpec(
        num_scalar_prefetch=2,
        grid=(n_slots // tm,),
        in_specs=[pl.BlockSpec((tm, D), lambda i, te, nt: (last(i, te, nt), 0)),
                  pl.BlockSpec((1, D, F), lambda i, te, nt: (te[last(i, te, nt)], 0, 0)),
                  pl.BlockSpec((1, D, F), lambda i, te, nt: (te[last(i, te, nt)], 0, 0)),
                  pl.BlockSpec((1, F, D), lambda i, te, nt: (te[last(i, te, nt)], 0, 0))],
        out_specs=pl.BlockSpec((tm, D), lambda i, te, nt: (i, 0)),
        scratch_shapes=[pltpu.VMEM((D, F), BF16), pltpu.VMEM((D, F), BF16), pltpu.VMEM((F, D), BF16)])
    return pl.pallas_call(
        _experts_kernel,
        grid_spec=grid_spec,
        out_shape=jax.ShapeDtypeStruct((n_slots, D), F32),
        compiler_params=_params("arbitrary"),
        name="experts",
    )(tile_expert, n_tiles, xs, w1, w3, w2)


def _combine_kernel(slot_sm, x_ref, mod_ref, g_ref, wt_ref, sw1_ref, sw3_ref, sw2_ref, fg_ref, ys_hbm, o_ref,
                    gbuf, sem, *, tb, final_norm):
    def row_copy(t, k):
        return pltpu.make_async_copy(ys_hbm.at[pl.ds(slot_sm[k, t], 1)], gbuf.at[k, pl.ds(t, 1)], sem)

    def for_all_copies(fn):
        def body(t, carry):
            for k in range(TOP_K):
                fn(row_copy(t, k))
            return carry
        lax.fori_loop(0, tb, body, 0)

    for_all_copies(lambda cp: cp.start())

    x = x_ref[0]
    mod = mod_ref[0]
    hb = _rms_mod(x, g_ref[...], mod[4:5, :], mod[3:4, :]).astype(BF16)
    a = jnp.dot(hb, sw1_ref[...], preferred_element_type=F32)
    b = jnp.dot(hb, sw3_ref[...], preferred_element_type=F32)
    acc = jnp.dot((_silu(a) * b).astype(BF16), sw2_ref[...], preferred_element_type=F32)
    wt = wt_ref[0]
    for_all_copies(lambda cp: cp.wait())
    for k in range(TOP_K):
        acc = acc + wt[:, k:k + 1] * gbuf[k]
    y = x + mod[5:6, :] * acc
    if final_norm:
        y = y * lax.rsqrt(jnp.mean(y * y, axis=-1, keepdims=True) + EPS) * fg_ref[...]
    o_ref[0] = y


def _combine(slot, x, mod_l, norm_g, wt, sw1, sw3, sw2, final_g, ys, *, tb, final_norm):
    B, S, D = x.shape
    nj = S // tb
    full = lambda a: pl.BlockSpec(a.shape, lambda b, j: (0,) * a.ndim)
    tok = lambda w: pl.BlockSpec((1, tb, w), lambda b, j: (b, j, 0))
    return pl.pallas_call(
        functools.partial(_combine_kernel, tb=tb, final_norm=final_norm),
        grid=(B, nj),
        in_specs=[pl.BlockSpec((SUBLANES, tb), lambda b, j: (0, b * nj + j), memory_space=pltpu.SMEM),
                  tok(D), pl.BlockSpec((1, N_MOD, D), lambda b, j: (b, 0, 0)), full(norm_g), tok(LANES),
                  full(sw1), full(sw3), full(sw2), full(final_g), pl.BlockSpec(memory_space=pltpu.HBM)],
        out_specs=tok(D),
        out_shape=jax.ShapeDtypeStruct((B, S, D), F32),
        scratch_shapes=[pltpu.VMEM((TOP_K, tb, D), F32), pltpu.SemaphoreType.DMA],
        compiler_params=_params("arbitrary", "arbitrary"),
        name="combine",
    )(slot, x, mod_l, norm_g, wt, sw1, sw3, sw2, final_g, ys)


def _block_diag(w):
    G, a, b = w.shape
    out = jnp.zeros((G * a, G * b), w.dtype)
    for g in range(G):
        out = out.at[g * a:(g + 1) * a, g * b:(g + 1) * b].set(w[g])
    return out


def kernel(x, c, w_in, b_f, pool_w, pool_scale, sg_ln_g, sg_ln_b, sg_w, sg_b, conv_w, conv_b, conv_ln_g,
           conv_ln_b, w_branch, w_out, mix_norm_g, ffn_norm_g, ada_w, ada_b, router_w, router_bias,
           exp_w1, exp_w3, exp_w2, shared_w1, shared_w3, shared_w2, final_norm_g):
    B, S, D = x.shape
    L = w_in.shape[0]
    tb = min(512, S)
    tq = tb
    tm = EXPERT_TILE
    n_tiles = (B * S * TOP_K) // tm + N_EXPERTS
    meta_lanes = -(-(n_tiles + 1) // LANES) * LANES
    row = lambda a: a.reshape(1, -1)

    mod = _modulation(c, ada_w, ada_b)
    for l in range(L):
        wmix = w_in[l, :, :MIX_COLS].astype(BF16)
        wf = w_in[l, :, MIX_COLS:MIX_COLS + ATT_HEADS]
        wfc = jnp.pad(wf, ((0, 0), (0, LANES - ATT_HEADS))).astype(BF16)
        bfc = jnp.pad(b_f[l], (0, LANES - ATT_HEADS)).reshape(1, LANES)
        wgate = w_in[l, :, MIX_COLS + ATT_HEADS:].astype(BF16)
        poolw = _block_diag(pool_w[l]).astype(BF16)
        sgb = jnp.repeat(sg_b[l].T, SG_HD, axis=1)

        ya, yb, yc, q, k, v = _mixer_in(
            x, mod[l], row(mix_norm_g[l]), wmix, wfc, bfc, poolw, row(pool_scale[l]),
            row(sg_ln_g[l]), row(sg_ln_b[l]), sg_w[l], sgb, conv_w[l], row(conv_b[l]),
            row(conv_ln_g[l]), row(conv_ln_b[l]), tb=tb)
        yd = _attention(q, k, v, tq=tq)
        x = _merge(x, mod[l], row(mix_norm_g[l]), wgate, w_branch[l].astype(BF16), w_out[l].astype(BF16),
                   ya, yb, yc, yd, tb=tb)
        fnorm = row(ffn_norm_g[l])
        e_idx, pos, wt, cnt = _route_call(x, mod[l], fnorm, router_w[l].T, router_bias[l].reshape(-1, 1), tb=tb)
        slot, meta, base = _route_fin(e_idx, pos, cnt, tm=tm, meta_lanes=meta_lanes)
        xs = _dispatch(cnt[:, 0], base[:, 0], slot, x, mod[l], fnorm, tb=tb, tm=tm, n_slots=n_tiles * tm)
        ys = _experts(meta[0, :n_tiles], meta[0, meta_lanes - 1:], xs, exp_w1[l], exp_w3[l], exp_w2[l], tm=tm)
        x = _combine(slot, x, mod[l], fnorm, wt, shared_w1[l].astype(BF16), shared_w3[l].astype(BF16),
                     shared_w2[l].astype(BF16), row(final_norm_g), ys, tb=tb, final_norm=(l == L - 1))
    return x
```

```python
import functools
import math

import jax
import jax.numpy as jnp
from jax import lax
from jax.experimental import pallas as pl
from jax.experimental.pallas import tpu as pltpu

F32 = jnp.float32
BF16 = jnp.bfloat16
HIGHEST = lax.Precision.HIGHEST

D_MODEL = 1024
BRANCH_W = 256
N_BRANCH = 4
POOL_WINDOWS = (2, 4, 8, 16)
POOL_GW = 64
POOL_TAIL = 16
SG_HEADS = 4
SG_CHUNK = 128
SG_HD = 64
CONV_WIDTH = 31
CONV_TAIL = 32
ATT_HEADS = 4
ATT_HD = 64
N_EXPERTS = 64
TOP_K = 6
N_GROUPS = 8
GROUP_SIZE = N_EXPERTS // N_GROUPS
TOPK_GROUPS = 4
EXPERT_FF = 256
SHARED_FF = 256
ROUTE_SCALE = 2.5
EPS = 1e-6
N_MOD = 6
MIX_COLS = 8 * BRANCH_W
LANES = 128
SUBLANES = 8
LOG2E = 1.4426950408889634
EXPERT_TILE = 256
VMEM_LIMIT = 56 * 1024 * 1024


def _sigmoid(x):
    return 1.0 / (1.0 + jnp.exp(-x))


def _silu(x):
    return x * _sigmoid(x)


def _rms_mod(x, g, scale, shift):
    y = x * lax.rsqrt(jnp.mean(x * x, axis=-1, keepdims=True) + EPS)
    return (y * g) * (1.0 + scale) + shift


def _layer_norm(x, g, b):
    mu = jnp.mean(x, axis=-1, keepdims=True)
    xc = x - mu
    var = jnp.mean(xc * xc, axis=-1, keepdims=True)
    return xc * lax.rsqrt(var + EPS) * g + b


def _params(*sem):
    return pltpu.CompilerParams(dimension_semantics=sem, vmem_limit_bytes=VMEM_LIMIT)


def _mod_kernel(c_ref, w_ref, b_ref, o_ref):
    c = c_ref[...]
    o_ref[0] = jnp.dot(_silu(c), w_ref[0], precision=HIGHEST, preferred_element_type=F32) + b_ref[0]


def _modulation(c, ada_w, ada_b):
    L, D, N = ada_w.shape
    B = c.shape[0]
    tn = 1536
    out = pl.pallas_call(
        _mod_kernel,
        grid=(L, N // tn),
        in_specs=[pl.BlockSpec((B, D), lambda l, j: (0, 0)),
                  pl.BlockSpec((1, D, tn), lambda l, j: (l, 0, j)),
                  pl.BlockSpec((1, 1, tn), lambda l, j: (l, 0, j))],
        out_specs=pl.BlockSpec((1, B, tn), lambda l, j: (l, 0, j)),
        out_shape=jax.ShapeDtypeStruct((L, B, N), F32),
        compiler_params=_params("arbitrary", "arbitrary"),
        name="modulation",
    )(c, ada_w, ada_b.reshape(L, 1, N))
    return out.reshape(L, B, N_MOD, D)


def _mixer_in_kernel(x_ref, mod_ref, g_ref, wmix_ref, wfc_ref, bfc_ref,
                     poolw_ref, pools_ref, sglg_ref, sglb_ref, sgw_ref, sgb_ref,
                     cw_ref, cb_ref, clg_ref, clb_ref,
                     ya_ref, yb_ref, yc_ref, q_ref, k_ref, v_ref,
                     pool_ext, conv_ext, cum_c, *, tb):
    j = pl.program_id(1)

    @pl.when(j == 0)
    def _():
        pool_ext[0:POOL_TAIL, :] = jnp.zeros((POOL_TAIL, BRANCH_W), F32)
        conv_ext[0:CONV_TAIL, :] = jnp.zeros((CONV_TAIL, BRANCH_W), F32)
        cum_c[...] = jnp.zeros_like(cum_c)

    mod = mod_ref[0]
    h = _rms_mod(x_ref[0], g_ref[...], mod[1:2, :], mod[0:1, :])
    hb = h.astype(BF16)
    proj = jnp.dot(hb, wmix_ref[...], preferred_element_type=F32)

    lane = lax.broadcasted_iota(jnp.int32, (1, BRANCH_W), 1)
    row = lax.broadcasted_iota(jnp.int32, (tb, 1), 0)

    u = proj[:, 0:BRANCH_W]
    pool_ext[POOL_TAIL:POOL_TAIL + tb, :] = u
    ext = pool_ext[...]
    s2 = ext + pltpu.roll(ext, 1, 0)
    s4 = s2 + pltpu.roll(s2, 2, 0)
    s8 = s4 + pltpu.roll(s4, 4, 0)
    s16 = s8 + pltpu.roll(s8, 8, 0)
    grp = lane // POOL_GW
    wsum = jnp.where(grp == 0, s2, jnp.where(grp == 1, s4, jnp.where(grp == 2, s8, s16)))[POOL_TAIL:, :]
    win = jnp.where(grp == 0, 2.0, jnp.where(grp == 1, 4.0, jnp.where(grp == 2, 8.0, 16.0)))
    count = jnp.minimum((j * tb + row + 1).astype(F32), win)
    pooled = wsum / count - u
    ya = jnp.dot(pooled.astype(BF16), poolw_ref[...], preferred_element_type=F32) * pools_ref[...]
    ya_ref[0] = ya.astype(BF16)
    pool_ext[0:POOL_TAIL, :] = u[tb - POOL_TAIL:, :]

    z = proj[:, BRANCH_W:3 * BRANCH_W]
    z = 0.5 * z * (1.0 + jnp.tanh(math.sqrt(2.0 / math.pi) * (z + 0.044715 * (z * z * z))))
    su = z[:, 0:BRANCH_W]
    sv = _layer_norm(z[:, BRANCH_W:], sglg_ref[...], sglb_ref[...])
    r128 = lax.broadcasted_iota(jnp.int32, (SG_CHUNK, SG_CHUNK), 0)
    c128 = lax.broadcasted_iota(jnp.int32, (SG_CHUNK, SG_CHUNK), 1)
    wcat = jnp.concatenate(
        [jnp.where(r128 >= c128, sgw_ref[hh], 0.0) for hh in range(SG_HEADS)], axis=1).astype(BF16)
    head = lane // SG_HD
    for ci in range(tb // SG_CHUNK):
        rows = slice(ci * SG_CHUNK, (ci + 1) * SG_CHUNK)
        vch = sv[rows, :]
        vstack = jnp.concatenate(
            [jnp.where(head == hh, vch, 0.0) for hh in range(SG_HEADS)], axis=0).astype(BF16)
        s = jnp.dot(wcat, vstack, preferred_element_type=F32) + sgb_ref[...]
        yb_ref[0, rows, :] = (su[rows, :] * s).astype(BF16)

    glu = proj[:, 3 * BRANCH_W:4 * BRANCH_W] * _sigmoid(proj[:, 4 * BRANCH_W:5 * BRANCH_W])
    conv_ext[CONV_TAIL:CONV_TAIL + tb, :] = glu
    acc = jnp.zeros((tb, BRANCH_W), F32) + cb_ref[...]
    for kk in range(CONV_WIDTH):
        off = CONV_TAIL - (CONV_WIDTH - 1) + kk
        acc = acc + cw_ref[kk:kk + 1, :] * conv_ext[off:off + tb, :]
    yc_ref[0] = _silu(_layer_norm(acc, clg_ref[...], clb_ref[...])).astype(BF16)
    conv_ext[0:CONV_TAIL, :] = glu[tb - CONV_TAIL:, :]

    def log_sigmoid(t):
        return jnp.minimum(t, 0.0) - jnp.log(1.0 + jnp.exp(-jnp.abs(t)))

    rr = lax.broadcasted_iota(jnp.int32, (tb, tb), 0)
    cc = lax.broadcasted_iota(jnp.int32, (tb, tb), 1)
    lower = jnp.where(rr >= cc, 1.0, 0.0).astype(F32)
    lf_c = log_sigmoid(jnp.dot(hb, wfc_ref[...], preferred_element_type=F32) + bfc_ref[...])
    cs_c = jnp.dot(lower, lf_c, precision=HIGHEST, preferred_element_type=F32) + cum_c[...]
    cum_c[...] = cs_c[tb - 1:tb, :]
    cl2 = cs_c * LOG2E
    ln = lax.broadcasted_iota(jnp.int32, (1, LANES), 1)
    ones_q = jnp.where((ln >= ATT_HD + 3) & (ln < ATT_HD + 6), 1.0, 0.0)
    ones_k = jnp.where((ln >= ATT_HD) & (ln < ATT_HD + 3), 1.0, 0.0)
    ones_v = jnp.where(ln == ATT_HD, 1.0, 0.0)
    for hh in range(ATT_HEADS):
        pair = (hh // 2) * LANES
        qs = proj[:, 5 * BRANCH_W + pair:5 * BRANCH_W + pair + LANES] * (LOG2E / math.sqrt(ATT_HD))
        ks = proj[:, 6 * BRANCH_W + pair:6 * BRANCH_W + pair + LANES]
        vs = proj[:, 7 * BRANCH_W + pair:7 * BRANCH_W + pair + LANES]
        if hh % 2:
            qs, ks, vs = (pltpu.roll(a, ATT_HD, 1) for a in (qs, ks, vs))
        c = cl2[:, hh:hh + 1]
        hi = c.astype(BF16).astype(F32)
        mid = (c - hi).astype(BF16).astype(F32)
        lo = (c - hi) - mid
        q_extra = jnp.where(ln == ATT_HD, hi, jnp.where(ln == ATT_HD + 1, mid, jnp.where(ln == ATT_HD + 2, lo, ones_q)))
        k_extra = jnp.where(ln == ATT_HD + 3, -hi,
                            jnp.where(ln == ATT_HD + 4, -mid, jnp.where(ln == ATT_HD + 5, -lo, ones_k)))
        q_ref[0, hh, 0] = jnp.where(ln < ATT_HD, qs, q_extra).T.astype(BF16)
        k_ref[0, hh] = jnp.where(ln < ATT_HD, ks, k_extra).astype(BF16)
        v_ref[0, hh, 0] = jnp.where(ln < ATT_HD, vs, ones_v).T.astype(BF16)


def _mixer_in(x, mod_l, norm_g, wmix, wfc, bfc, poolw, pools, sglg, sglb, sgw, sgb, cw, cb, clg, clb, *, tb):
    B, S, D = x.shape
    full = lambda a: pl.BlockSpec(a.shape, lambda b, j: (0,) * a.ndim)
    tok = lambda w: pl.BlockSpec((1, tb, w), lambda b, j: (b, j, 0))
    head = pl.BlockSpec((1, ATT_HEADS, tb, LANES), lambda b, j: (b, 0, j, 0))
    head_t = pl.BlockSpec((1, ATT_HEADS, 1, LANES, tb), lambda b, j: (b, 0, j, 0, 0))
    consts = (norm_g, wmix, wfc, bfc, poolw, pools, sglg, sglb, sgw, sgb, cw, cb, clg, clb)
    act = jax.ShapeDtypeStruct((B, S, BRANCH_W), BF16)
    att = jax.ShapeDtypeStruct((B, ATT_HEADS, S, LANES), BF16)
    att_t = jax.ShapeDtypeStruct((B, ATT_HEADS, S // tb, LANES, tb), BF16)
    return pl.pallas_call(
        functools.partial(_mixer_in_kernel, tb=tb),
        grid=(B, S // tb),
        in_specs=[tok(D), pl.BlockSpec((1, N_MOD, D), lambda b, j: (b, 0, 0))] + [full(a) for a in consts],
        out_specs=[tok(BRANCH_W)] * 3 + [head_t, head, head_t],
        out_shape=[act] * 3 + [att_t, att, att_t],
        scratch_shapes=[pltpu.VMEM((POOL_TAIL + tb, BRANCH_W), F32),
                        pltpu.VMEM((CONV_TAIL + tb, BRANCH_W), F32),
                        pltpu.VMEM((1, LANES), F32)],
        compiler_params=_params("arbitrary", "arbitrary"),
        name="mixer_in",
    )(x, mod_l, *consts)


def _attn_kernel(q_ref, k_ref, v_ref, o_ref, m_ref, acc_ref, *, tq):
    i = pl.program_id(1)
    key = lax.broadcasted_iota(jnp.int32, (tq, tq), 0)
    qry = lax.broadcasted_iota(jnp.int32, (tq, tq), 1)
    m_ref[...] = jnp.full(m_ref.shape, -jnp.inf, F32)
    acc_ref[...] = jnp.zeros(acc_ref.shape, F32)

    def block(kj, diagonal):
        ks = pl.multiple_of(kj * tq, tq)
        logits = [jnp.dot(k_ref[0, hh, pl.ds(ks, tq), :], q_ref[0, hh, 0], preferred_element_type=F32)
                  for hh in range(ATT_HEADS)]
        for hh in range(ATT_HEADS):
            s = logits[hh]
            if diagonal:
                s = jnp.where(key <= qry, s, -jnp.inf)
            m_old = m_ref[hh]
            m_new = jnp.maximum(m_old, jnp.max(s, axis=0, keepdims=True))
            p = jnp.exp2(s - m_new)
            pv = jnp.dot(v_ref[0, hh, kj], p.astype(BF16), preferred_element_type=F32)
            acc_ref[hh] = jnp.exp2(m_old - m_new) * acc_ref[hh] + pv
            m_ref[hh] = m_new

    def body(kj, carry):
        block(kj, False)
        return carry

    lax.fori_loop(0, i, body, 0)
    block(i, True)

    ln = lax.broadcasted_iota(jnp.int32, (1, LANES), 1)
    for pr in range(ATT_HEADS // 2):
        o = []
        for hh in (2 * pr, 2 * pr + 1):
            acc = acc_ref[hh]
            o.append((acc / acc[ATT_HD:ATT_HD + 1, :]).T)
        o_ref[0, :, pr * LANES:(pr + 1) * LANES] = jnp.where(ln < ATT_HD, o[0], pltpu.roll(o[1], ATT_HD, 1)).astype(BF16)


def _attention(q_t, k, v_t, *, tq):
    B, H, S, W = k.shape
    nblk = S // tq
    assert q_t.shape == (B, H, nblk, W, tq) and v_t.shape == q_t.shape
    return pl.pallas_call(
        functools.partial(_attn_kernel, tq=tq),
        grid=(B, nblk),
        in_specs=[pl.BlockSpec((1, H, 1, W, tq), lambda b, i: (b, 0, i, 0, 0)),
                  pl.BlockSpec((1, H, S, W), lambda b, i: (b, 0, 0, 0), pipeline_mode=pl.Buffered(1)),
                  pl.BlockSpec((1, H, nblk, W, tq), lambda b, i: (b, 0, 0, 0, 0), pipeline_mode=pl.Buffered(1))],
        out_specs=pl.BlockSpec((1, tq, BRANCH_W), lambda b, i: (b, i, 0)),
        out_shape=jax.ShapeDtypeStruct((B, S, BRANCH_W), BF16),
        scratch_shapes=[pltpu.VMEM((H, 1, tq), F32), pltpu.VMEM((H, W, tq), F32)],
        compiler_params=_params("arbitrary", "arbitrary"),
        name="attention",
    )(q_t, k, v_t)


def _merge_kernel(x_ref, mod_ref, g_ref, wg_ref, wbr_ref, wout_ref, ya_ref, yb_ref, yc_ref, yd_ref, o_ref):
    x = x_ref[0]
    mod = mod_ref[0]
    hb = _rms_mod(x, g_ref[...], mod[1:2, :], mod[0:1, :]).astype(BF16)
    merged = jnp.zeros(x.shape, F32)
    for n, y_ref in enumerate((ya_ref, yb_ref, yc_ref, yd_ref)):
        gate = _sigmoid(jnp.dot(hb, wg_ref[:, n * D_MODEL:(n + 1) * D_MODEL], preferred_element_type=F32))
        merged = merged + gate * jnp.dot(y_ref[0], wbr_ref[n], preferred_element_type=F32)
    out = jnp.dot(merged.astype(BF16), wout_ref[...], preferred_element_type=F32)
    o_ref[0] = x + mod[2:3, :] * out


def _merge(x, mod_l, norm_g, wgate, wbranch, wout, ya, yb, yc, yd, *, tb):
    B, S, D = x.shape
    full = lambda a: pl.BlockSpec(a.shape, lambda b, j: (0,) * a.ndim)
    tok = lambda w: pl.BlockSpec((1, tb, w), lambda b, j: (b, j, 0))
    return pl.pallas_call(
        _merge_kernel,
        grid=(B, S // tb),
        in_specs=[tok(D), pl.BlockSpec((1, N_MOD, D), lambda b, j: (b, 0, 0)),
                  full(norm_g), full(wgate), full(wbranch), full(wout)] + [tok(BRANCH_W)] * 4,
        out_specs=tok(D),
        out_shape=jax.ShapeDtypeStruct((B, S, D), F32),
        compiler_params=_params("arbitrary", "arbitrary"),
        name="merge",
    )(x, mod_l, norm_g, wgate, wbranch, wout, ya, yb, yc, yd)


def _route(scores_t, bias_t):
    E, n = scores_t.shape
    sel = scores_t + bias_t
    eidx = lax.broadcasted_iota(jnp.int32, (E, n), 0)
    neg = jnp.full((E, n), -jnp.inf, F32)
    gscore = []
    sub = lax.broadcasted_iota(jnp.int32, (GROUP_SIZE, n), 0)
    for g in range(N_GROUPS):
        blk = sel[g * GROUP_SIZE:(g + 1) * GROUP_SIZE, :]
        m1 = jnp.max(blk, axis=0, keepdims=True)
        first = jnp.min(jnp.where(blk == m1, sub, GROUP_SIZE), axis=0, keepdims=True)
        m2 = jnp.max(jnp.where(sub == first, -jnp.inf, blk), axis=0, keepdims=True)
        gscore.append(m1 + m2)
    emask = []
    for g in range(N_GROUPS):
        beaten = jnp.zeros((1, n), jnp.int32)
        for g2 in range(N_GROUPS):
            if g2 == g:
                continue
            wins = (gscore[g2] > gscore[g]) | ((gscore[g2] == gscore[g]) & (g2 < g))
            beaten = beaten + wins.astype(jnp.int32)
        emask.append(jnp.broadcast_to(beaten < TOPK_GROUPS, (GROUP_SIZE, n)))
    cur = jnp.where(jnp.concatenate(emask, axis=0), sel, neg)
    chosen = jnp.zeros((E, n), jnp.bool_)
    firsts = []
    for _ in range(TOP_K):
        m = jnp.max(cur, axis=0, keepdims=True)
        first = jnp.min(jnp.where(cur == m, eidx, E), axis=0, keepdims=True)
        hit = eidx == first
        chosen = chosen | hit
        cur = jnp.where(hit, neg, cur)
        firsts.append(first)
    w = jnp.where(chosen, scores_t, 0.0)
    return w / jnp.sum(w, axis=0, keepdims=True) * ROUTE_SCALE, chosen, firsts


def _route_kernel(x_ref, mod_ref, g_ref, rwt_ref, rb_ref, e_ref, p_ref, wt_ref, cnt_ref, count):
    first_step = (pl.program_id(0) == 0) & (pl.program_id(1) == 0)

    @pl.when(first_step)
    def _():
        count[...] = jnp.zeros_like(count)

    mod = mod_ref[0]
    h = _rms_mod(x_ref[0], g_ref[...], mod[4:5, :], mod[3:4, :])
    logits_t = lax.dot_general(rwt_ref[...], h, (((1,), (1,)), ((), ())),
                               precision=HIGHEST, preferred_element_type=F32)
    gates_t, chosen, firsts = _route(_sigmoid(logits_t), rb_ref[...])
    E, tb = gates_t.shape
    ones = jnp.where(chosen, 1.0, 0.0)
    rr = lax.broadcasted_iota(jnp.int32, (tb, tb), 0)
    cc = lax.broadcasted_iota(jnp.int32, (tb, tb), 1)
    upper = jnp.where(rr <= cc, 1.0, 0.0).astype(BF16)
    incl = jnp.dot(ones.astype(BF16), upper, preferred_element_type=F32)
    pos_all = count[...] + (incl - ones)
    eidx = lax.broadcasted_iota(jnp.int32, (E, tb), 0)
    pad = SUBLANES - TOP_K
    pos_rows = [jnp.sum(jnp.where(eidx == f, pos_all, 0.0), axis=0, keepdims=True) for f in firsts]
    w_rows = [jnp.sum(jnp.where(eidx == f, gates_t, 0.0), axis=0, keepdims=True) for f in firsts]
    e_ref[...] = jnp.concatenate(firsts + [jnp.zeros((pad, tb), jnp.int32)], axis=0)
    p_ref[...] = jnp.concatenate(pos_rows + [jnp.zeros((pad, tb), F32)], axis=0).astype(jnp.int32)
    w_pad = jnp.concatenate(w_rows + [jnp.zeros((LANES - TOP_K, tb), F32)], axis=0)
    wt_ref[0] = w_pad.T
    count[...] = count[...] + jnp.sum(ones, axis=1, keepdims=True)
    cnt_ref[...] = jnp.broadcast_to(count[...], cnt_ref.shape).astype(jnp.int32)


def _route_call(x, mod_l, norm_g, router_wt, router_b, *, tb):
    B, S, D = x.shape
    nj = S // tb
    full = lambda a: pl.BlockSpec(a.shape, lambda b, j: (0,) * a.ndim)
    tok = lambda w: pl.BlockSpec((1, tb, w), lambda b, j: (b, j, 0))
    kt = pl.BlockSpec((SUBLANES, tb), lambda b, j: (0, b * nj + j))
    return pl.pallas_call(
        _route_kernel,
        grid=(B, nj),
        in_specs=[tok(D), pl.BlockSpec((1, N_MOD, D), lambda b, j: (b, 0, 0)),
                  full(norm_g), full(router_wt), full(router_b)],
        out_specs=[kt, kt, tok(LANES), pl.BlockSpec((N_EXPERTS, LANES), lambda b, j: (0, 0))],
        out_shape=[jax.ShapeDtypeStruct((SUBLANES, B * S), jnp.int32),
                   jax.ShapeDtypeStruct((SUBLANES, B * S), jnp.int32),
                   jax.ShapeDtypeStruct((B, S, LANES), F32),
                   jax.ShapeDtypeStruct((N_EXPERTS, LANES), jnp.int32)],
        scratch_shapes=[pltpu.VMEM((N_EXPERTS, 1), F32)],
        compiler_params=_params("arbitrary", "arbitrary"),
        name="route",
    )(x, mod_l, norm_g, router_wt, router_b)


def _route_fin_kernel(e_ref, p_ref, cnt_ref, slot_ref, meta_ref, base_ref, *, tm):
    cnt = cnt_ref[...]
    ntile = lax.shift_right_logical(cnt + (tm - 1), int(math.log2(tm)))
    er = lax.broadcasted_iota(jnp.int32, (N_EXPERTS, N_EXPERTS), 0)
    ec = lax.broadcasted_iota(jnp.int32, (N_EXPERTS, N_EXPERTS), 1)
    before = jnp.where(ec < er, 1.0, 0.0).astype(F32)
    start = jnp.dot(before, ntile.astype(F32), precision=HIGHEST, preferred_element_type=F32).astype(jnp.int32)
    base = start * tm
    base_ref[...] = base
    e = e_ref[...]
    slot = p_ref[...]
    for ex in range(N_EXPERTS):
        slot = slot + jnp.where(e == ex, base[ex:ex + 1, 0:1], 0)
    slot_ref[...] = slot
    nlane = meta_ref.shape[1]
    lane = lax.broadcasted_iota(jnp.int32, (1, nlane), 1)
    end = (start + ntile)[:, 0:1]
    tile_expert = jnp.sum(jnp.where(end <= lane, 1, 0), axis=0, keepdims=True)
    tile_expert = jnp.minimum(tile_expert, N_EXPERTS - 1)
    meta_ref[...] = jnp.where(lane == nlane - 1, end[N_EXPERTS - 1:N_EXPERTS, :], tile_expert)


def _route_fin(e_idx, pos, cnt, *, tm, meta_lanes):
    full = lambda a: pl.BlockSpec(a.shape, lambda i: (0,) * a.ndim)
    outs = [jax.ShapeDtypeStruct(e_idx.shape, jnp.int32), jax.ShapeDtypeStruct((1, meta_lanes), jnp.int32),
            jax.ShapeDtypeStruct(cnt.shape, jnp.int32)]
    return pl.pallas_call(
        functools.partial(_route_fin_kernel, tm=tm),
        grid=(1,),
        in_specs=[full(e_idx), full(pos), full(cnt)],
        out_specs=[full(o) for o in outs],
        out_shape=outs,
        compiler_params=_params("arbitrary"),
        name="route_fin",
    )(e_idx, pos, cnt)


SLAB = D_MODEL // LANES


def _to_slabs(ref, value):
    n = value.shape[0]
    for s in range(SLAB):
        ref[pl.ds(s, n, stride=SLAB), :] = value[:, s * LANES:(s + 1) * LANES]


def _from_slabs(ref, n):
    return jnp.concatenate([ref[pl.ds(s, n, stride=SLAB), :] for s in range(SLAB)], axis=1)


def _dispatch_kernel(cnt_sm, base_sm, slot_sm, x_ref, mod_ref, g_ref, xs_hbm, hs, zrow, sem, zsem, *, tb, tm):
    first_step = (pl.program_id(0) == 0) & (pl.program_id(1) == 0)
    mod = mod_ref[0]
    _to_slabs(hs, _rms_mod(x_ref[0], g_ref[...], mod[4:5, :], mod[3:4, :]))

    def row_copy(t, k):
        src = hs.at[pl.ds(pl.multiple_of(t * SLAB, SLAB), SLAB)]
        dst = xs_hbm.at[pl.ds(pl.multiple_of(slot_sm[k, t] * SLAB, SLAB), SLAB)]
        return pltpu.make_async_copy(src, dst, sem)

    def for_all_copies(fn):
        def body(t, carry):
            for k in range(TOP_K):
                fn(row_copy(t, k), k)
            return carry
        lax.fori_loop(0, tb, body, 0)

    for_all_copies(lambda cp, k: cp.start(priority=k % 2))

    @pl.when(first_step)
    def _():
        zrow[...] = jnp.zeros_like(zrow)

        def per_expert(e, carry):
            n = cnt_sm[e]
            first_pad = base_sm[e] + n
            npad = lax.rem(tm - lax.rem(n, tm), tm)
            zcopy = lambda r: pltpu.make_async_copy(
                zrow, xs_hbm.at[pl.ds(pl.multiple_of((first_pad + r) * SLAB, SLAB), SLAB)], zsem)
            lax.fori_loop(0, npad, lambda r, c: (zcopy(r).start(), c)[1], 0)
            lax.fori_loop(0, npad, lambda r, c: (zcopy(r).wait(), c)[1], 0)
            return carry

        lax.fori_loop(0, N_EXPERTS, per_expert, 0)

    for_all_copies(lambda cp, k: cp.wait())


def _dispatch(cnt, base, slot, x, mod_l, norm_g, *, tb, tm, n_slots):
    B, S, D = x.shape
    nj = S // tb
    grid_spec = pltpu.PrefetchScalarGridSpec(
        num_scalar_prefetch=2,
        grid=(B, nj),
        in_specs=[pl.BlockSpec((SUBLANES, tb), lambda b, j, *_: (0, b * nj + j), memory_space=pltpu.SMEM),
                  pl.BlockSpec((1, tb, D), lambda b, j, *_: (b, j, 0)),
                  pl.BlockSpec((1, N_MOD, D), lambda b, j, *_: (b, 0, 0)),
                  pl.BlockSpec(norm_g.shape, lambda b, j, *_: (0, 0))],
        out_specs=pl.BlockSpec(memory_space=pltpu.HBM),
        scratch_shapes=[pltpu.VMEM((tb * SLAB, LANES), F32), pltpu.VMEM((SLAB, LANES), F32),
                        pltpu.SemaphoreType.DMA, pltpu.SemaphoreType.DMA])
    return pl.pallas_call(
        functools.partial(_dispatch_kernel, tb=tb, tm=tm),
        grid_spec=grid_spec,
        out_shape=jax.ShapeDtypeStruct((n_slots * SLAB, LANES), F32),
        compiler_params=_params("arbitrary", "arbitrary"),
        name="dispatch",
    )(cnt, base, slot, x, mod_l, norm_g)


def _experts_kernel(te_sm, nt_sm, x_ref, w1_ref, w3_ref, w2_ref, y_ref, w1b, w3b, w2b, *, tm):
    i = pl.program_id(0)
    in_use = i < nt_sm[0]

    @pl.when(in_use)
    def _():
        @pl.when((i == 0) | (te_sm[i] != te_sm[jnp.maximum(i - 1, 0)]))
        def _():
            w1b[...] = w1_ref[0].astype(BF16)
            w3b[...] = w3_ref[0].astype(BF16)
            w2b[...] = w2_ref[0].astype(BF16)

        xb = _from_slabs(x_ref, tm).astype(BF16)
        a = jnp.dot(xb, w1b[...], preferred_element_type=F32)
        b = jnp.dot(xb, w3b[...], preferred_element_type=F32)
        _to_slabs(y_ref, jnp.dot((_silu(a) * b).astype(BF16), w2b[...], preferred_element_type=F32))

    @pl.when(jnp.logical_not(in_use))
    def _():
        y_ref[...] = jnp.zeros_like(y_ref)


def _experts(tile_expert, n_tiles, xs, w1, w3, w2, *, tm):
    E, D, F = w1.shape
    n_slots = xs.shape[0] // SLAB
    last = lambda i, te, nt: jnp.minimum(i, nt[0] - 1)
    grid_spec = pltpu.PrefetchScalarGridSpec(
        num_scalar_prefetch=2,
        grid=(n_slots // tm,),
        in_specs=[pl.BlockSpec((tm * SLAB, LANES), lambda i, te, nt: (last(i, te, nt), 0)),
                  pl.BlockSpec((1, D, F), lambda i, te, nt: (te[last(i, te, nt)], 0, 0)),
                  pl.BlockSpec((1, D, F), lambda i, te, nt: (te[last(i, te, nt)], 0, 0)),
                  pl.BlockSpec((1, F, D), lambda i, te, nt: (te[last(i, te, nt)], 0, 0))],
        out_specs=pl.BlockSpec((tm * SLAB, LANES), lambda i, te, nt: (i, 0)),
        scratch_shapes=[pltpu.VMEM((D, F), BF16), pltpu.VMEM((D, F), BF16), pltpu.VMEM((F, D), BF16)])
    return pl.pallas_call(
        functools.partial(_experts_kernel, tm=tm),
        grid_spec=grid_spec,
        out_shape=jax.ShapeDtypeStruct(xs.shape, F32),
        compiler_params=_params("arbitrary"),
        name="experts",
    )(tile_expert, n_tiles, xs, w1, w3, w2)


def _combine_kernel(slot_sm, x_ref, mod_ref, g_ref, wt_ref, sw1_ref, sw3_ref, sw2_ref, fg_ref, ys_hbm, o_ref,
                    gbuf, sem, *, tb, final_norm):
    def row_copy(t, k):
        src = ys_hbm.at[pl.ds(pl.multiple_of(slot_sm[k, t] * SLAB, SLAB), SLAB)]
        return pltpu.make_async_copy(src, gbuf.at[k, pl.ds(pl.multiple_of(t * SLAB, SLAB), SLAB)], sem)

    def for_all_copies(fn):
        def body(t, carry):
            for k in range(TOP_K):
                fn(row_copy(t, k), k)
            return carry
        lax.fori_loop(0, tb, body, 0)

    for_all_copies(lambda cp, k: cp.start(priority=k % 2))

    x = x_ref[0]
    mod = mod_ref[0]
    hb = _rms_mod(x, g_ref[...], mod[4:5, :], mod[3:4, :]).astype(BF16)
    a = jnp.dot(hb, sw1_ref[...], preferred_element_type=F32)
    b = jnp.dot(hb, sw3_ref[...], preferred_element_type=F32)
    acc = jnp.dot((_silu(a) * b).astype(BF16), sw2_ref[...], preferred_element_type=F32)
    wt = wt_ref[0]
    for_all_copies(lambda cp, k: cp.wait())
    for k in range(TOP_K):
        acc = acc + wt[:, k:k + 1] * _from_slabs(gbuf.at[k], tb)
    y = x + mod[5:6, :] * acc
    if final_norm:
        y = y * lax.rsqrt(jnp.mean(y * y, axis=-1, keepdims=True) + EPS) * fg_ref[...]
    o_ref[0] = y


def _combine(slot, x, mod_l, norm_g, wt, sw1, sw3, sw2, final_g, ys, *, tb, final_norm):
    B, S, D = x.shape
    nj = S // tb
    full = lambda a: pl.BlockSpec(a.shape, lambda b, j: (0,) * a.ndim)
    tok = lambda w: pl.BlockSpec((1, tb, w), lambda b, j: (b, j, 0))
    return pl.pallas_call(
        functools.partial(_combine_kernel, tb=tb, final_norm=final_norm),
        grid=(B, nj),
        in_specs=[pl.BlockSpec((SUBLANES, tb), lambda b, j: (0, b * nj + j), memory_space=pltpu.SMEM),
                  tok(D), pl.BlockSpec((1, N_MOD, D), lambda b, j: (b, 0, 0)), full(norm_g), tok(LANES),
                  full(sw1), full(sw3), full(sw2), full(final_g), pl.BlockSpec(memory_space=pltpu.HBM)],
        out_specs=tok(D),
        out_shape=jax.ShapeDtypeStruct((B, S, D), F32),
        scratch_shapes=[pltpu.VMEM((TOP_K, tb * SLAB, LANES), F32), pltpu.SemaphoreType.DMA],
        compiler_params=_params("arbitrary", "arbitrary"),
        name="combine",
    )(slot, x, mod_l, norm_g, wt, sw1, sw3, sw2, final_g, ys)


def _block_diag(w):
    G, a, b = w.shape
    out = jnp.zeros((G * a, G * b), w.dtype)
    for g in range(G):
        out = out.at[g * a:(g + 1) * a, g * b:(g + 1) * b].set(w[g])
    return out


def kernel(x, c, w_in, b_f, pool_w, pool_scale, sg_ln_g, sg_ln_b, sg_w, sg_b, conv_w, conv_b, conv_ln_g,
           conv_ln_b, w_branch, w_out, mix_norm_g, ffn_norm_g, ada_w, ada_b, router_w, router_bias,
           exp_w1, exp_w3, exp_w2, shared_w1, shared_w3, shared_w2, final_norm_g):
    B, S, D = x.shape
    L = w_in.shape[0]
    tb = min(512, S)
    tq = tb
    tm = EXPERT_TILE
    n_tiles = (B * S * TOP_K) // tm + N_EXPERTS
    meta_lanes = -(-(n_tiles + 1) // LANES) * LANES
    row = lambda a: a.reshape(1, -1)

    mod = _modulation(c, ada_w, ada_b)
    for l in range(L):
        wmix = w_in[l, :, :MIX_COLS].astype(BF16)
        wf = w_in[l, :, MIX_COLS:MIX_COLS + ATT_HEADS]
        wfc = jnp.pad(wf, ((0, 0), (0, LANES - ATT_HEADS))).astype(BF16)
        bfc = jnp.pad(b_f[l], (0, LANES - ATT_HEADS)).reshape(1, LANES)
        wgate = w_in[l, :, MIX_COLS + ATT_HEADS:].astype(BF16)
        poolw = _block_diag(pool_w[l]).astype(BF16)
        sgb = jnp.repeat(sg_b[l].T, SG_HD, axis=1)

        ya, yb, yc, q, k, v = _mixer_in(
            x, mod[l], row(mix_norm_g[l]), wmix, wfc, bfc, poolw, row(pool_scale[l]),
            row(sg_ln_g[l]), row(sg_ln_b[l]), sg_w[l], sgb, conv_w[l], row(conv_b[l]),
            row(conv_ln_g[l]), row(conv_ln_b[l]), tb=tb)
        yd = _attention(q, k, v, tq=tq)
        x = _merge(x, mod[l], row(mix_norm_g[l]), wgate, w_branch[l].astype(BF16), w_out[l].astype(BF16),
                   ya, yb, yc, yd, tb=tb)
        fnorm = row(ffn_norm_g[l])
        e_idx, pos, wt, cnt = _route_call(x, mod[l], fnorm, router_w[l].T, router_bias[l].reshape(-1, 1), tb=tb)
        slot, meta, base = _route_fin(e_idx, pos, cnt, tm=tm, meta_lanes=meta_lanes)
        xs = _dispatch(cnt[:, 0], base[:, 0], slot, x, mod[l], fnorm, tb=tb, tm=tm, n_slots=n_tiles * tm)
        ys = _experts(meta[0, :n_tiles], meta[0, meta_lanes - 1:], xs, exp_w1[l], exp_w3[l], exp_w2[l], tm=tm)
        x = _combine(slot, x, mod[l], fnorm, wt, shared_w1[l].astype(BF16), shared_w3[l].astype(BF16),
                     shared_w2[l].astype(BF16), row(final_norm_g), ys, tb=tb, final_norm=(l == L - 1))
    return x
```

```python
import functools
import math

import jax
import jax.numpy as jnp
from jax import lax
from jax.experimental import pallas as pl
from jax.experimental.pallas import tpu as pltpu

F32 = jnp.float32
BF16 = jnp.bfloat16
HIGHEST = lax.Precision.HIGHEST

D_MODEL = 1024
BRANCH_W = 256
N_BRANCH = 4
POOL_WINDOWS = (2, 4, 8, 16)
POOL_GW = 64
POOL_TAIL = 16
SG_HEADS = 4
SG_CHUNK = 128
SG_HD = 64
CONV_WIDTH = 31
CONV_TAIL = 32
ATT_HEADS = 4
ATT_HD = 64
N_EXPERTS = 64
TOP_K = 6
N_GROUPS = 8
GROUP_SIZE = N_EXPERTS // N_GROUPS
TOPK_GROUPS = 4
EXPERT_FF = 256
SHARED_FF = 256
ROUTE_SCALE = 2.5
EPS = 1e-6
N_MOD = 6
MIX_COLS = 8 * BRANCH_W
LANES = 128
SUBLANES = 8
LOG2E = 1.4426950408889634
EXPERT_TILE = 256
VMEM_LIMIT = 56 * 1024 * 1024


def _sigmoid(x):
    return 1.0 / (1.0 + jnp.exp(-x))


def _silu(x):
    return x * _sigmoid(x)


def _rms_mod(x, g, scale, shift):
    y = x * lax.rsqrt(jnp.mean(x * x, axis=-1, keepdims=True) + EPS)
    return (y * g) * (1.0 + scale) + shift


def _layer_norm(x, g, b):
    mu = jnp.mean(x, axis=-1, keepdims=True)
    xc = x - mu
    var = jnp.mean(xc * xc, axis=-1, keepdims=True)
    return xc * lax.rsqrt(var + EPS) * g + b


def _params(*sem):
    return pltpu.CompilerParams(dimension_semantics=sem, vmem_limit_bytes=VMEM_LIMIT)


def _mod_kernel(c_ref, w_ref, b_ref, o_ref):
    c = c_ref[...]
    o_ref[0] = jnp.dot(_silu(c), w_ref[0], precision=HIGHEST, preferred_element_type=F32) + b_ref[0]


def _modulation(c, ada_w, ada_b):
    L, D, N = ada_w.shape
    B = c.shape[0]
    tn = 1536
    out = pl.pallas_call(
        _mod_kernel,
        grid=(L, N // tn),
        in_specs=[pl.BlockSpec((B, D), lambda l, j: (0, 0)),
                  pl.BlockSpec((1, D, tn), lambda l, j: (l, 0, j)),
                  pl.BlockSpec((1, 1, tn), lambda l, j: (l, 0, j))],
        out_specs=pl.BlockSpec((1, B, tn), lambda l, j: (l, 0, j)),
        out_shape=jax.ShapeDtypeStruct((L, B, N), F32),
        compiler_params=_params("arbitrary", "arbitrary"),
        name="modulation",
    )(c, ada_w, ada_b.reshape(L, 1, N))
    return out.reshape(L, B, N_MOD, D)


def _mixer_in_kernel(x_ref, mod_ref, g_ref, wmix_ref, wfc_ref, bfc_ref,
                     poolw_ref, pools_ref, sglg_ref, sglb_ref, sgw_ref, sgb_ref,
                     cw_ref, cb_ref, clg_ref, clb_ref,
                     ya_ref, yb_ref, yc_ref, q_ref, k_ref, v_ref,
                     pool_ext, conv_ext, cum_c, *, tb):
    j = pl.program_id(1)

    @pl.when(j == 0)
    def _():
        pool_ext[0:POOL_TAIL, :] = jnp.zeros((POOL_TAIL, BRANCH_W), F32)
        conv_ext[0:CONV_TAIL, :] = jnp.zeros((CONV_TAIL, BRANCH_W), F32)
        cum_c[...] = jnp.zeros_like(cum_c)

    mod = mod_ref[0]
    h = _rms_mod(x_ref[0], g_ref[...], mod[1:2, :], mod[0:1, :])
    hb = h.astype(BF16)
    proj = jnp.dot(hb, wmix_ref[...], preferred_element_type=F32)

    lane = lax.broadcasted_iota(jnp.int32, (1, BRANCH_W), 1)
    row = lax.broadcasted_iota(jnp.int32, (tb, 1), 0)

    u = proj[:, 0:BRANCH_W]
    pool_ext[POOL_TAIL:POOL_TAIL + tb, :] = u
    ext = pool_ext[...]
    s2 = ext + pltpu.roll(ext, 1, 0)
    s4 = s2 + pltpu.roll(s2, 2, 0)
    s8 = s4 + pltpu.roll(s4, 4, 0)
    s16 = s8 + pltpu.roll(s8, 8, 0)
    grp = lane // POOL_GW
    wsum = jnp.where(grp == 0, s2, jnp.where(grp == 1, s4, jnp.where(grp == 2, s8, s16)))[POOL_TAIL:, :]
    win = jnp.where(grp == 0, 2.0, jnp.where(grp == 1, 4.0, jnp.where(grp == 2, 8.0, 16.0)))
    count = jnp.minimum((j * tb + row + 1).astype(F32), win)
    pooled = wsum / count - u
    ya = jnp.dot(pooled.astype(BF16), poolw_ref[...], preferred_element_type=F32) * pools_ref[...]
    ya_ref[0] = ya.astype(BF16)
    pool_ext[0:POOL_TAIL, :] = u[tb - POOL_TAIL:, :]

    z = proj[:, BRANCH_W:3 * BRANCH_W]
    z = 0.5 * z * (1.0 + jnp.tanh(math.sqrt(2.0 / math.pi) * (z + 0.044715 * (z * z * z))))
    su = z[:, 0:BRANCH_W]
    sv = _layer_norm(z[:, BRANCH_W:], sglg_ref[...], sglb_ref[...])
    r128 = lax.broadcasted_iota(jnp.int32, (SG_CHUNK, SG_CHUNK), 0)
    c128 = lax.broadcasted_iota(jnp.int32, (SG_CHUNK, SG_CHUNK), 1)
    wcat = jnp.concatenate(
        [jnp.where(r128 >= c128, sgw_ref[hh], 0.0) for hh in range(SG_HEADS)], axis=1).astype(BF16)
    head = lane // SG_HD
    for ci in range(tb // SG_CHUNK):
        rows = slice(ci * SG_CHUNK, (ci + 1) * SG_CHUNK)
        vch = sv[rows, :]
        vstack = jnp.concatenate(
            [jnp.where(head == hh, vch, 0.0) for hh in range(SG_HEADS)], axis=0).astype(BF16)
        s = jnp.dot(wcat, vstack, preferred_element_type=F32) + sgb_ref[...]
        yb_ref[0, rows, :] = (su[rows, :] * s).astype(BF16)

    glu = proj[:, 3 * BRANCH_W:4 * BRANCH_W] * _sigmoid(proj[:, 4 * BRANCH_W:5 * BRANCH_W])
    conv_ext[CONV_TAIL:CONV_TAIL + tb, :] = glu
    acc = jnp.zeros((tb, BRANCH_W), F32) + cb_ref[...]
    for kk in range(CONV_WIDTH):
        off = CONV_TAIL - (CONV_WIDTH - 1) + kk
        acc = acc + cw_ref[kk:kk + 1, :] * conv_ext[off:off + tb, :]
    yc_ref[0] = _silu(_layer_norm(acc, clg_ref[...], clb_ref[...])).astype(BF16)
    conv_ext[0:CONV_TAIL, :] = glu[tb - CONV_TAIL:, :]

    def log_sigmoid(t):
        return jnp.minimum(t, 0.0) - jnp.log(1.0 + jnp.exp(-jnp.abs(t)))

    rr = lax.broadcasted_iota(jnp.int32, (tb, tb), 0)
    cc = lax.broadcasted_iota(jnp.int32, (tb, tb), 1)
    lower = jnp.where(rr >= cc, 1.0, 0.0).astype(F32)
    lf_c = log_sigmoid(jnp.dot(hb, wfc_ref[...], preferred_element_type=F32) + bfc_ref[...])
    cs_c = jnp.dot(lower, lf_c, precision=HIGHEST, preferred_element_type=F32) + cum_c[...]
    cum_c[...] = cs_c[tb - 1:tb, :]
    cl2 = cs_c * LOG2E
    ln = lax.broadcasted_iota(jnp.int32, (1, LANES), 1)
    ones_q = jnp.where((ln >= ATT_HD + 3) & (ln < ATT_HD + 6), 1.0, 0.0)
    ones_k = jnp.where((ln >= ATT_HD) & (ln < ATT_HD + 3), 1.0, 0.0)
    ones_v = jnp.where(ln == ATT_HD, 1.0, 0.0)
    for hh in range(ATT_HEADS):
        pair = (hh // 2) * LANES
        qs = proj[:, 5 * BRANCH_W + pair:5 * BRANCH_W + pair + LANES] * (LOG2E / math.sqrt(ATT_HD))
        ks = proj[:, 6 * BRANCH_W + pair:6 * BRANCH_W + pair + LANES]
        vs = proj[:, 7 * BRANCH_W + pair:7 * BRANCH_W + pair + LANES]
        if hh % 2:
            qs, ks, vs = (pltpu.roll(a, ATT_HD, 1) for a in (qs, ks, vs))
        c = cl2[:, hh:hh + 1]
        hi = c.astype(BF16).astype(F32)
        mid = (c - hi).astype(BF16).astype(F32)
        lo = (c - hi) - mid
        q_extra = jnp.where(ln == ATT_HD, hi, jnp.where(ln == ATT_HD + 1, mid, jnp.where(ln == ATT_HD + 2, lo, ones_q)))
        k_extra = jnp.where(ln == ATT_HD + 3, -hi,
                            jnp.where(ln == ATT_HD + 4, -mid, jnp.where(ln == ATT_HD + 5, -lo, ones_k)))
        q_ref[0, hh, 0] = jnp.where(ln < ATT_HD, qs, q_extra).T.astype(BF16)
        k_ref[0, hh] = jnp.where(ln < ATT_HD, ks, k_extra).astype(BF16)
        v_ref[0, hh, 0] = jnp.where(ln < ATT_HD, vs, ones_v).T.astype(BF16)


def _mixer_in(x, mod_l, norm_g, wmix, wfc, bfc, poolw, pools, sglg, sglb, sgw, sgb, cw, cb, clg, clb, *, tb):
    B, S, D = x.shape
    full = lambda a: pl.BlockSpec(a.shape, lambda b, j: (0,) * a.ndim)
    tok = lambda w: pl.BlockSpec((1, tb, w), lambda b, j: (b, j, 0))
    head = pl.BlockSpec((1, ATT_HEADS, tb, LANES), lambda b, j: (b, 0, j, 0))
    head_t = pl.BlockSpec((1, ATT_HEADS, 1, LANES, tb), lambda b, j: (b, 0, j, 0, 0))
    consts = (norm_g, wmix, wfc, bfc, poolw, pools, sglg, sglb, sgw, sgb, cw, cb, clg, clb)
    act = jax.ShapeDtypeStruct((B, S, BRANCH_W), BF16)
    att = jax.ShapeDtypeStruct((B, ATT_HEADS, S, LANES), BF16)
    att_t = jax.ShapeDtypeStruct((B, ATT_HEADS, S // tb, LANES, tb), BF16)
    return pl.pallas_call(
        functools.partial(_mixer_in_kernel, tb=tb),
        grid=(B, S // tb),
        in_specs=[tok(D), pl.BlockSpec((1, N_MOD, D), lambda b, j: (b, 0, 0))] + [full(a) for a in consts],
        out_specs=[tok(BRANCH_W)] * 3 + [head_t, head, head_t],
        out_shape=[act] * 3 + [att_t, att, att_t],
        scratch_shapes=[pltpu.VMEM((POOL_TAIL + tb, BRANCH_W), F32),
                        pltpu.VMEM((CONV_TAIL + tb, BRANCH_W), F32),
                        pltpu.VMEM((1, LANES), F32)],
        compiler_params=_params("arbitrary", "arbitrary"),
        name="mixer_in",
    )(x, mod_l, *consts)


def _attn_kernel(q_ref, k_ref, v_ref, o_ref, m_ref, acc_ref, *, tq):
    i = pl.program_id(1)
    key = lax.broadcasted_iota(jnp.int32, (tq, tq), 0)
    qry = lax.broadcasted_iota(jnp.int32, (tq, tq), 1)
    m_ref[...] = jnp.full(m_ref.shape, -jnp.inf, F32)
    acc_ref[...] = jnp.zeros(acc_ref.shape, F32)

    def block(kj, diagonal):
        ks = pl.multiple_of(kj * tq, tq)
        logits = [jnp.dot(k_ref[0, hh, pl.ds(ks, tq), :], q_ref[0, hh, 0], preferred_element_type=F32)
                  for hh in range(ATT_HEADS)]
        for hh in range(ATT_HEADS):
            s = logits[hh]
            if diagonal:
                s = jnp.where(key <= qry, s, -jnp.inf)
            m_old = m_ref[hh]
            m_new = jnp.maximum(m_old, jnp.max(s, axis=0, keepdims=True))
            p = jnp.exp2(s - m_new)
            pv = jnp.dot(v_ref[0, hh, kj], p.astype(BF16), preferred_element_type=F32)
            acc_ref[hh] = jnp.exp2(m_old - m_new) * acc_ref[hh] + pv
            m_ref[hh] = m_new

    def body(kj, carry):
        block(kj, False)
        return carry

    lax.fori_loop(0, i, body, 0)
    block(i, True)

    ln = lax.broadcasted_iota(jnp.int32, (1, LANES), 1)
    for pr in range(ATT_HEADS // 2):
        o = []
        for hh in (2 * pr, 2 * pr + 1):
            acc = acc_ref[hh]
            o.append((acc / acc[ATT_HD:ATT_HD + 1, :]).T)
        o_ref[0, :, pr * LANES:(pr + 1) * LANES] = jnp.where(ln < ATT_HD, o[0], pltpu.roll(o[1], ATT_HD, 1)).astype(BF16)


def _attention(q_t, k, v_t, *, tq):
    B, H, S, W = k.shape
    nblk = S // tq
    assert q_t.shape == (B, H, nblk, W, tq) and v_t.shape == q_t.shape
    return pl.pallas_call(
        functools.partial(_attn_kernel, tq=tq),
        grid=(B, nblk),
        in_specs=[pl.BlockSpec((1, H, 1, W, tq), lambda b, i: (b, 0, i, 0, 0)),
                  pl.BlockSpec((1, H, S, W), lambda b, i: (b, 0, 0, 0), pipeline_mode=pl.Buffered(1)),
                  pl.BlockSpec((1, H, nblk, W, tq), lambda b, i: (b, 0, 0, 0, 0), pipeline_mode=pl.Buffered(1))],
        out_specs=pl.BlockSpec((1, tq, BRANCH_W), lambda b, i: (b, i, 0)),
        out_shape=jax.ShapeDtypeStruct((B, S, BRANCH_W), BF16),
        scratch_shapes=[pltpu.VMEM((H, 1, tq), F32), pltpu.VMEM((H, W, tq), F32)],
        compiler_params=_params("arbitrary", "arbitrary"),
        name="attention",
    )(q_t, k, v_t)


def _merge_kernel(x_ref, mod_ref, g_ref, wg_ref, wbr_ref, wout_ref, ya_ref, yb_ref, yc_ref, yd_ref, o_ref):
    x = x_ref[0]
    mod = mod_ref[0]
    hb = _rms_mod(x, g_ref[...], mod[1:2, :], mod[0:1, :]).astype(BF16)
    merged = jnp.zeros(x.shape, F32)
    for n, y_ref in enumerate((ya_ref, yb_ref, yc_ref, yd_ref)):
        gate = _sigmoid(jnp.dot(hb, wg_ref[:, n * D_MODEL:(n + 1) * D_MODEL], preferred_element_type=F32))
        merged = merged + gate * jnp.dot(y_ref[0], wbr_ref[n], preferred_element_type=F32)
    out = jnp.dot(merged.astype(BF16), wout_ref[...], preferred_element_type=F32)
    o_ref[0] = x + mod[2:3, :] * out


def _merge(x, mod_l, norm_g, wgate, wbranch, wout, ya, yb, yc, yd, *, tb):
    B, S, D = x.shape
    full = lambda a: pl.BlockSpec(a.shape, lambda b, j: (0,) * a.ndim)
    tok = lambda w: pl.BlockSpec((1, tb, w), lambda b, j: (b, j, 0))
    return pl.pallas_call(
        _merge_kernel,
        grid=(B, S // tb),
        in_specs=[tok(D), pl.BlockSpec((1, N_MOD, D), lambda b, j: (b, 0, 0)),
                  full(norm_g), full(wgate), full(wbranch), full(wout)] + [tok(BRANCH_W)] * 4,
        out_specs=tok(D),
        out_shape=jax.ShapeDtypeStruct((B, S, D), F32),
        compiler_params=_params("arbitrary", "arbitrary"),
        name="merge",
    )(x, mod_l, norm_g, wgate, wbranch, wout, ya, yb, yc, yd)


def _route(scores_t, bias_t):
    E, n = scores_t.shape
    sel = scores_t + bias_t
    eidx = lax.broadcasted_iota(jnp.int32, (E, n), 0)
    neg = jnp.full((E, n), -jnp.inf, F32)
    gscore = []
    sub = lax.broadcasted_iota(jnp.int32, (GROUP_SIZE, n), 0)
    for g in range(N_GROUPS):
        blk = sel[g * GROUP_SIZE:(g + 1) * GROUP_SIZE, :]
        m1 = jnp.max(blk, axis=0, keepdims=True)
        first = jnp.min(jnp.where(blk == m1, sub, GROUP_SIZE), axis=0, keepdims=True)
        m2 = jnp.max(jnp.where(sub == first, -jnp.inf, blk), axis=0, keepdims=True)
        gscore.append(m1 + m2)
    emask = []
    for g in range(N_GROUPS):
        beaten = jnp.zeros((1, n), jnp.int32)
        for g2 in range(N_GROUPS):
            if g2 == g:
                continue
            wins = (gscore[g2] > gscore[g]) | ((gscore[g2] == gscore[g]) & (g2 < g))
            beaten = beaten + wins.astype(jnp.int32)
        emask.append(jnp.broadcast_to(beaten < TOPK_GROUPS, (GROUP_SIZE, n)))
    cur = jnp.where(jnp.concatenate(emask, axis=0), sel, neg)
    chosen = jnp.zeros((E, n), jnp.bool_)
    firsts = []
    for _ in range(TOP_K):
        m = jnp.max(cur, axis=0, keepdims=True)
        first = jnp.min(jnp.where(cur == m, eidx, E), axis=0, keepdims=True)
        hit = eidx == first
        chosen = chosen | hit
        cur = jnp.where(hit, neg, cur)
        firsts.append(first)
    w = jnp.where(chosen, scores_t, 0.0)
    return w / jnp.sum(w, axis=0, keepdims=True) * ROUTE_SCALE, chosen, firsts


def _route_kernel(x_ref, mod_ref, g_ref, rwt_ref, rb_ref, ls_ref, lst_ref, wt_ref, bc_ref, cr_ref, cnt_ref, count):
    first_step = (pl.program_id(0) == 0) & (pl.program_id(1) == 0)

    @pl.when(first_step)
    def _():
        count[...] = jnp.zeros_like(count)

    mod = mod_ref[0]
    h = _rms_mod(x_ref[0], g_ref[...], mod[4:5, :], mod[3:4, :])
    logits_t = lax.dot_general(rwt_ref[...], h, (((1,), (1,)), ((), ())),
                               precision=HIGHEST, preferred_element_type=F32)
    gates_t, chosen, firsts = _route(_sigmoid(logits_t), rb_ref[...])
    E, tb = gates_t.shape
    ones = jnp.where(chosen, 1.0, 0.0)
    rr = lax.broadcasted_iota(jnp.int32, (tb, tb), 0)
    cc = lax.broadcasted_iota(jnp.int32, (tb, tb), 1)
    upper = jnp.where(rr <= cc, 1.0, 0.0).astype(BF16)
    incl = jnp.dot(ones.astype(BF16), upper, preferred_element_type=F32)
    block_count = jnp.sum(ones, axis=1, keepdims=True)
    er = lax.broadcasted_iota(jnp.int32, (E, E), 0)
    ec = lax.broadcasted_iota(jnp.int32, (E, E), 1)
    before = jnp.where(ec < er, 1.0, 0.0).astype(F32)
    local_base = jnp.dot(before, jnp.broadcast_to(block_count, (E, LANES)), precision=HIGHEST,
                         preferred_element_type=F32)[:, 0:1]
    row_all = local_base + (incl - ones)
    eidx = lax.broadcasted_iota(jnp.int32, (E, tb), 0)
    ls_rows = [jnp.sum(jnp.where(eidx == f, row_all, 0.0), axis=0, keepdims=True) for f in firsts]
    w_rows = [jnp.sum(jnp.where(eidx == f, gates_t, 0.0), axis=0, keepdims=True) for f in firsts]
    ls_ref[...] = jnp.concatenate(ls_rows + [jnp.zeros((SUBLANES - TOP_K, tb), F32)], axis=0).astype(jnp.int32)
    zpad = jnp.zeros((LANES - TOP_K, tb), F32)
    lst_ref[0] = jnp.concatenate(ls_rows + [zpad], axis=0).T.astype(jnp.int32)
    wt_ref[0] = jnp.concatenate(w_rows + [zpad], axis=0).T

    def as_row(col):
        sel = lax.broadcasted_iota(jnp.int32, (E, LANES), 0) == lax.broadcasted_iota(jnp.int32, (E, LANES), 1)
        return jnp.sum(jnp.where(sel, col, 0.0), axis=0, keepdims=True).astype(jnp.int32)

    bc_ref[0] = as_row(block_count)
    cr_ref[0] = as_row(count[...])
    count[...] = count[...] + block_count
    cnt_ref[...] = jnp.broadcast_to(count[...], cnt_ref.shape).astype(jnp.int32)


def _route_call(x, mod_l, norm_g, router_wt, router_b, *, tb):
    B, S, D = x.shape
    nj = S // tb
    full = lambda a: pl.BlockSpec(a.shape, lambda b, j: (0,) * a.ndim)
    tok = lambda w: pl.BlockSpec((1, tb, w), lambda b, j: (b, j, 0))
    kt = pl.BlockSpec((SUBLANES, tb), lambda b, j: (0, b * nj + j))
    per_block = pl.BlockSpec((1, 1, LANES), lambda b, j: (b * nj + j, 0, 0))
    return pl.pallas_call(
        _route_kernel,
        grid=(B, nj),
        in_specs=[tok(D), pl.BlockSpec((1, N_MOD, D), lambda b, j: (b, 0, 0)),
                  full(norm_g), full(router_wt), full(router_b)],
        out_specs=[kt, tok(LANES), tok(LANES), per_block, per_block,
                   pl.BlockSpec((N_EXPERTS, LANES), lambda b, j: (0, 0))],
        out_shape=[jax.ShapeDtypeStruct((SUBLANES, B * S), jnp.int32),
                   jax.ShapeDtypeStruct((B, S, LANES), jnp.int32),
                   jax.ShapeDtypeStruct((B, S, LANES), F32),
                   jax.ShapeDtypeStruct((B * nj, 1, LANES), jnp.int32),
                   jax.ShapeDtypeStruct((B * nj, 1, LANES), jnp.int32),
                   jax.ShapeDtypeStruct((N_EXPERTS, LANES), jnp.int32)],
        scratch_shapes=[pltpu.VMEM((N_EXPERTS, 1), F32)],
        compiler_params=_params("arbitrary", "arbitrary"),
        name="route",
    )(x, mod_l, norm_g, router_wt, router_b)


def _route_fin_kernel(cnt_ref, meta_ref, base_ref, *, tm):
    cnt = cnt_ref[...]
    ntile = lax.shift_right_logical(cnt + (tm - 1), int(math.log2(tm)))
    er = lax.broadcasted_iota(jnp.int32, (N_EXPERTS, N_EXPERTS), 0)
    ec = lax.broadcasted_iota(jnp.int32, (N_EXPERTS, N_EXPERTS), 1)
    before = jnp.where(ec < er, 1.0, 0.0).astype(F32)
    start = jnp.dot(before, ntile.astype(F32), precision=HIGHEST, preferred_element_type=F32).astype(jnp.int32)
    base = start * tm
    base_ref[...] = base
    nlane = meta_ref.shape[1]
    lane = lax.broadcasted_iota(jnp.int32, (1, nlane), 1)
    end = (start + ntile)[:, 0:1]
    tile_expert = jnp.sum(jnp.where(end <= lane, 1, 0), axis=0, keepdims=True)
    tile_expert = jnp.minimum(tile_expert, N_EXPERTS - 1)
    meta_ref[...] = jnp.where(lane == nlane - 1, end[N_EXPERTS - 1:N_EXPERTS, :], tile_expert)


def _route_fin(cnt, *, tm, meta_lanes):
    full = lambda a: pl.BlockSpec(a.shape, lambda i: (0,) * a.ndim)
    outs = [jax.ShapeDtypeStruct((1, meta_lanes), jnp.int32), jax.ShapeDtypeStruct(cnt.shape, jnp.int32)]
    return pl.pallas_call(
        functools.partial(_route_fin_kernel, tm=tm),
        grid=(1,),
        in_specs=[full(cnt)],
        out_specs=[full(o) for o in outs],
        out_shape=outs,
        compiler_params=_params("arbitrary"),
        name="route_fin",
    )(cnt)


SLAB = D_MODEL // LANES


def _to_slabs(ref, value):
    n = value.shape[0]
    for s in range(SLAB):
        ref[pl.ds(s, n, stride=SLAB), :] = value[:, s * LANES:(s + 1) * LANES]


def _from_slabs(ref, n):
    return jnp.concatenate([ref[pl.ds(s, n, stride=SLAB), :] for s in range(SLAB)], axis=1)


def _for_expert_runs(base_sm, bc_sm, cr_sm, max_run, fn):
    def per_expert(e, local_row):
        n = bc_sm[0, 0, e]
        sorted_row = base_sm[e] + cr_sm[0, 0, e]
        done = 0
        bit = max_run
        while bit:
            take = n & bit

            @pl.when(take != 0)
            def _(done=done, bit=bit):
                fn(local_row + done, sorted_row + done, bit)

            done = done + take
            bit //= 2
        return local_row + n

    lax.fori_loop(0, N_EXPERTS, per_expert, 0)


def _slab_rows(ref, row, nrows):
    return ref.at[pl.ds(pl.multiple_of(row * SLAB, SLAB), nrows * SLAB)]


def _dispatch_kernel(cnt_sm, base_sm, bc_sm, cr_sm, ls_ref, x_ref, mod_ref, g_ref, xs_hbm, stage, zrow, sem, zsem,
                     *, tb, tm):
    first_step = (pl.program_id(0) == 0) & (pl.program_id(1) == 0)
    mod = mod_ref[0]
    hb = _rms_mod(x_ref[0], g_ref[...], mod[4:5, :], mod[3:4, :]).astype(BF16)
    ls = ls_ref[...]
    for grp in range(TOP_K):
        row = grp * tb + lax.broadcasted_iota(jnp.int32, (tb, tb), 0)
        hit = ls[0:1, :] == row
        for k in range(1, TOP_K):
            hit = hit | (ls[k:k + 1, :] == row)
        perm = jnp.where(hit, 1.0, 0.0).astype(BF16)
        _to_slabs(stage.at[pl.ds(grp * tb * SLAB, tb * SLAB)], jnp.dot(perm, hb, preferred_element_type=F32))

    def run_copy(local_row, sorted_row, nrows):
        return pltpu.make_async_copy(_slab_rows(stage, local_row, nrows), _slab_rows(xs_hbm, sorted_row, nrows), sem)

    _for_expert_runs(base_sm, bc_sm, cr_sm, tb, lambda *a: run_copy(*a).start())

    @pl.when(first_step)
    def _():
        zrow[...] = jnp.zeros_like(zrow)

        def per_expert(e, carry):
            n = cnt_sm[e]
            first_pad = base_sm[e] + n
            npad = lax.rem(tm - lax.rem(n, tm), tm)
            zcopy = lambda r: pltpu.make_async_copy(
                zrow, xs_hbm.at[pl.ds(pl.multiple_of((first_pad + r) * SLAB, SLAB), SLAB)], zsem)
            lax.fori_loop(0, npad, lambda r, c: (zcopy(r).start(), c)[1], 0)
            lax.fori_loop(0, npad, lambda r, c: (zcopy(r).wait(), c)[1], 0)
            return carry

        lax.fori_loop(0, N_EXPERTS, per_expert, 0)

    _for_expert_runs(base_sm, bc_sm, cr_sm, tb, lambda *a: run_copy(*a).wait())


def _dispatch(cnt, base, blk_cnt, blk_carry, ls, x, mod_l, norm_g, *, tb, tm, n_slots):
    B, S, D = x.shape
    nj = S // tb
    per_block = pl.BlockSpec((1, 1, LANES), lambda b, j, *_: (b * nj + j, 0, 0), memory_space=pltpu.SMEM)
    grid_spec = pltpu.PrefetchScalarGridSpec(
        num_scalar_prefetch=2,
        grid=(B, nj),
        in_specs=[per_block, per_block,
                  pl.BlockSpec((SUBLANES, tb), lambda b, j, *_: (0, b * nj + j)),
                  pl.BlockSpec((1, tb, D), lambda b, j, *_: (b, j, 0)),
                  pl.BlockSpec((1, N_MOD, D), lambda b, j, *_: (b, 0, 0)),
                  pl.BlockSpec(norm_g.shape, lambda b, j, *_: (0, 0))],
        out_specs=pl.BlockSpec(memory_space=pltpu.HBM),
        scratch_shapes=[pltpu.VMEM((TOP_K * tb * SLAB, LANES), F32), pltpu.VMEM((SLAB, LANES), F32),
                        pltpu.SemaphoreType.DMA, pltpu.SemaphoreType.DMA])
    return pl.pallas_call(
        functools.partial(_dispatch_kernel, tb=tb, tm=tm),
        grid_spec=grid_spec,
        out_shape=jax.ShapeDtypeStruct((n_slots * SLAB, LANES), F32),
        compiler_params=_params("arbitrary", "arbitrary"),
        name="dispatch",
    )(cnt, base, blk_cnt, blk_carry, ls, x, mod_l, norm_g)


def _experts_kernel(te_sm, nt_sm, x_ref, w1_ref, w3_ref, w2_ref, y_ref, w1b, w3b, w2b, *, tm):
    i = pl.program_id(0)
    in_use = i < nt_sm[0]

    @pl.when(in_use)
    def _():
        @pl.when((i == 0) | (te_sm[i] != te_sm[jnp.maximum(i - 1, 0)]))
        def _():
            w1b[...] = w1_ref[0].astype(BF16)
            w3b[...] = w3_ref[0].astype(BF16)
            w2b[...] = w2_ref[0].astype(BF16)

        xb = _from_slabs(x_ref, tm).astype(BF16)
        a = jnp.dot(xb, w1b[...], preferred_element_type=F32)
        b = jnp.dot(xb, w3b[...], preferred_element_type=F32)
        _to_slabs(y_ref, jnp.dot((_silu(a) * b).astype(BF16), w2b[...], preferred_element_type=F32))

    @pl.when(jnp.logical_not(in_use))
    def _():
        y_ref[...] = jnp.zeros_like(y_ref)


def _experts(tile_expert, n_tiles, xs, w1, w3, w2, *, tm):
    E, D, F = w1.shape
    n_slots = xs.shape[0] // SLAB
    last = lambda i, te, nt: jnp.minimum(i, nt[0] - 1)
    grid_spec = pltpu.PrefetchScalarGridSpec(
        num_scalar_prefetch=2,
        grid=(n_slots // tm,),
        in_specs=[pl.BlockSpec((tm * SLAB, LANES), lambda i, te, nt: (last(i, te, nt), 0)),
                  pl.BlockSpec((1, D, F), lambda i, te, nt: (te[last(i, te, nt)], 0, 0)),
                  pl.BlockSpec((1, D, F), lambda i, te, nt: (te[last(i, te, nt)], 0, 0)),
                  pl.BlockSpec((1, F, D), lambda i, te, nt: (te[last(i, te, nt)], 0, 0))],
        out_specs=pl.BlockSpec((tm * SLAB, LANES), lambda i, te, nt: (i, 0)),
        scratch_shapes=[pltpu.VMEM((D, F), BF16), pltpu.VMEM((D, F), BF16), pltpu.VMEM((F, D), BF16)])
    return pl.pallas_call(
        functools.partial(_experts_kernel, tm=tm),
        grid_spec=grid_spec,
        out_shape=jax.ShapeDtypeStruct(xs.shape, F32),
        compiler_params=_params("arbitrary"),
        name="experts",
    )(tile_expert, n_tiles, xs, w1, w3, w2)


def _combine_kernel(base_sm, bc_sm, cr_sm, x_ref, mod_ref, g_ref, lst_ref, wt_ref, sw1_ref, sw3_ref, sw2_ref, fg_ref,
                    ys_hbm, o_ref, stage, sem, *, tb, final_norm):
    def run_copy(local_row, sorted_row, nrows):
        return pltpu.make_async_copy(_slab_rows(ys_hbm, sorted_row, nrows), _slab_rows(stage, local_row, nrows), sem)

    _for_expert_runs(base_sm, bc_sm, cr_sm, tb, lambda *a: run_copy(*a).start())

    x = x_ref[0]
    mod = mod_ref[0]
    hb = _rms_mod(x, g_ref[...], mod[4:5, :], mod[3:4, :]).astype(BF16)
    a = jnp.dot(hb, sw1_ref[...], preferred_element_type=F32)
    b = jnp.dot(hb, sw3_ref[...], preferred_element_type=F32)
    acc = jnp.dot((_silu(a) * b).astype(BF16), sw2_ref[...], preferred_element_type=F32)
    wt = wt_ref[0]
    lst = lst_ref[0]
    _for_expert_runs(base_sm, bc_sm, cr_sm, tb, lambda *a: run_copy(*a).wait())
    for grp in range(TOP_K):
        yg = _from_slabs(stage.at[pl.ds(grp * tb * SLAB, tb * SLAB)], tb).astype(BF16)
        row = grp * tb + lax.broadcasted_iota(jnp.int32, (1, tb), 1)
        gate = jnp.zeros((tb, tb), F32)
        for k in range(TOP_K):
            gate = gate + jnp.where(lst[:, k:k + 1] == row, wt[:, k:k + 1], 0.0)
        hi = gate.astype(BF16)
        lo = (gate - hi.astype(F32)).astype(BF16)
        acc = acc + jnp.dot(hi, yg, preferred_element_type=F32) + jnp.dot(lo, yg, preferred_element_type=F32)
    y = x + mod[5:6, :] * acc
    if final_norm:
        y = y * lax.rsqrt(jnp.mean(y * y, axis=-1, keepdims=True) + EPS) * fg_ref[...]
    o_ref[0] = y


def _combine(base, blk_cnt, blk_carry, x, mod_l, norm_g, lst, wt, sw1, sw3, sw2, final_g, ys, *, tb, final_norm):
    B, S, D = x.shape
    nj = S // tb
    full = lambda a: pl.BlockSpec(a.shape, lambda b, j, *_: (0,) * a.ndim)
    tok = lambda w: pl.BlockSpec((1, tb, w), lambda b, j, *_: (b, j, 0))
    per_block = pl.BlockSpec((1, 1, LANES), lambda b, j, *_: (b * nj + j, 0, 0), memory_space=pltpu.SMEM)
    grid_spec = pltpu.PrefetchScalarGridSpec(
        num_scalar_prefetch=1,
        grid=(B, nj),
        in_specs=[per_block, per_block, tok(D), pl.BlockSpec((1, N_MOD, D), lambda b, j, *_: (b, 0, 0)),
                  full(norm_g), tok(LANES), tok(LANES), full(sw1), full(sw3), full(sw2), full(final_g),
                  pl.BlockSpec(memory_space=pltpu.HBM)],
        out_specs=tok(D),
        scratch_shapes=[pltpu.VMEM((TOP_K * tb * SLAB, LANES), F32), pltpu.SemaphoreType.DMA])
    return pl.pallas_call(
        functools.partial(_combine_kernel, tb=tb, final_norm=final_norm),
        grid_spec=grid_spec,
        out_shape=jax.ShapeDtypeStruct((B, S, D), F32),
        compiler_params=_params("arbitrary", "arbitrary"),
        name="combine",
    )(base, blk_cnt, blk_carry, x, mod_l, norm_g, lst, wt, sw1, sw3, sw2, final_g, ys)


def _block_diag(w):
    G, a, b = w.shape
    out = jnp.zeros((G * a, G * b), w.dtype)
    for g in range(G):
        out = out.at[g * a:(g + 1) * a, g * b:(g + 1) * b].set(w[g])
    return out


def kernel(x, c, w_in, b_f, pool_w, pool_scale, sg_ln_g, sg_ln_b, sg_w, sg_b, conv_w, conv_b, conv_ln_g,
           conv_ln_b, w_branch, w_out, mix_norm_g, ffn_norm_g, ada_w, ada_b, router_w, router_bias,
           exp_w1, exp_w3, exp_w2, shared_w1, shared_w3, shared_w2, final_norm_g):
    B, S, D = x.shape
    L = w_in.shape[0]
    tb = min(512, S)
    tq = tb
    tm = EXPERT_TILE
    n_tiles = (B * S * TOP_K) // tm + N_EXPERTS
    meta_lanes = -(-(n_tiles + 1) // LANES) * LANES
    row = lambda a: a.reshape(1, -1)

    mod = _modulation(c, ada_w, ada_b)
    for l in range(L):
        wmix = w_in[l, :, :MIX_COLS].astype(BF16)
        wf = w_in[l, :, MIX_COLS:MIX_COLS + ATT_HEADS]
        wfc = jnp.pad(wf, ((0, 0), (0, LANES - ATT_HEADS))).astype(BF16)
        bfc = jnp.pad(b_f[l], (0, LANES - ATT_HEADS)).reshape(1, LANES)
        wgate = w_in[l, :, MIX_COLS + ATT_HEADS:].astype(BF16)
        poolw = _block_diag(pool_w[l]).astype(BF16)
        sgb = jnp.repeat(sg_b[l].T, SG_HD, axis=1)

        ya, yb, yc, q, k, v = _mixer_in(
            x, mod[l], row(mix_norm_g[l]), wmix, wfc, bfc, poolw, row(pool_scale[l]),
            row(sg_ln_g[l]), row(sg_ln_b[l]), sg_w[l], sgb, conv_w[l], row(conv_b[l]),
            row(conv_ln_g[l]), row(conv_ln_b[l]), tb=tb)
        yd = _attention(q, k, v, tq=tq)
        x = _merge(x, mod[l], row(mix_norm_g[l]), wgate, w_branch[l].astype(BF16), w_out[l].astype(BF16),
                   ya, yb, yc, yd, tb=tb)
        fnorm = row(ffn_norm_g[l])
        ls, lst, wt, blk_cnt, blk_carry, cnt = _route_call(
            x, mod[l], fnorm, router_w[l].T, router_bias[l].reshape(-1, 1), tb=tb)
        meta, base = _route_fin(cnt, tm=tm, meta_lanes=meta_lanes)
        xs = _dispatch(cnt[:, 0], base[:, 0], blk_cnt, blk_carry, ls, x, mod[l], fnorm,
                       tb=tb, tm=tm, n_slots=n_tiles * tm)
        ys = _experts(meta[0, :n_tiles], meta[0, meta_lanes - 1:], xs, exp_w1[l], exp_w3[l], exp_w2[l], tm=tm)
        x = _combine(base[:, 0], blk_cnt, blk_carry, x, mod[l], fnorm, lst, wt, shared_w1[l].astype(BF16),
                     shared_w3[l].astype(BF16), shared_w2[l].astype(BF16), row(final_norm_g), ys,
                     tb=tb, final_norm=(l == L - 1))
    return x
```

```python
import functools
import math

import jax
import jax.numpy as jnp
from jax import lax
from jax.experimental import pallas as pl
from jax.experimental.pallas import tpu as pltpu

F32 = jnp.float32
BF16 = jnp.bfloat16
HIGHEST = lax.Precision.HIGHEST

D_MODEL = 1024
BRANCH_W = 256
N_BRANCH = 4
POOL_WINDOWS = (2, 4, 8, 16)
POOL_GW = 64
POOL_TAIL = 16
SG_HEADS = 4
SG_CHUNK = 128
SG_HD = 64
CONV_WIDTH = 31
CONV_TAIL = 32
ATT_HEADS = 4
ATT_HD = 64
N_EXPERTS = 64
TOP_K = 6
N_GROUPS = 8
GROUP_SIZE = N_EXPERTS // N_GROUPS
TOPK_GROUPS = 4
EXPERT_FF = 256
SHARED_FF = 256
ROUTE_SCALE = 2.5
EPS = 1e-6
N_MOD = 6
MIX_COLS = 8 * BRANCH_W
LANES = 128
SUBLANES = 8
LOG2E = 1.4426950408889634
EXPERT_TILE = 256
VMEM_LIMIT = 56 * 1024 * 1024


def _sigmoid(x):
    return 1.0 / (1.0 + jnp.exp(-x))


def _silu(x):
    return x * _sigmoid(x)


def _rms_mod(x, g, scale, shift):
    y = x * lax.rsqrt(jnp.mean(x * x, axis=-1, keepdims=True) + EPS)
    return (y * g) * (1.0 + scale) + shift


def _layer_norm(x, g, b):
    mu = jnp.mean(x, axis=-1, keepdims=True)
    xc = x - mu
    var = jnp.mean(xc * xc, axis=-1, keepdims=True)
    return xc * lax.rsqrt(var + EPS) * g + b


def _params(*sem):
    return pltpu.CompilerParams(dimension_semantics=sem, vmem_limit_bytes=VMEM_LIMIT)


def _mod_kernel(c_ref, w_ref, b_ref, o_ref):
    c = c_ref[...]
    o_ref[0] = jnp.dot(_silu(c), w_ref[0], precision=HIGHEST, preferred_element_type=F32) + b_ref[0]


def _modulation(c, ada_w, ada_b):
    L, D, N = ada_w.shape
    B = c.shape[0]
    tn = 1536
    out = pl.pallas_call(
        _mod_kernel,
        grid=(L, N // tn),
        in_specs=[pl.BlockSpec((B, D), lambda l, j: (0, 0)),
                  pl.BlockSpec((1, D, tn), lambda l, j: (l, 0, j)),
                  pl.BlockSpec((1, 1, tn), lambda l, j: (l, 0, j))],
        out_specs=pl.BlockSpec((1, B, tn), lambda l, j: (l, 0, j)),
        out_shape=jax.ShapeDtypeStruct((L, B, N), F32),
        compiler_params=_params("arbitrary", "arbitrary"),
        name="modulation",
    )(c, ada_w, ada_b.reshape(L, 1, N))
    return out.reshape(L, B, N_MOD, D)


def _mixer_in_kernel(x_ref, mod_ref, g_ref, wmix_ref, wfc_ref, bfc_ref,
                     poolw_ref, pools_ref, sglg_ref, sglb_ref, sgw_ref, sgb_ref,
                     cw_ref, cb_ref, clg_ref, clb_ref,
                     ya_ref, yb_ref, yc_ref, q_ref, k_ref, v_ref,
                     pool_ext, conv_ext, cum_c, *, tb):
    j = pl.program_id(1)

    @pl.when(j == 0)
    def _():
        pool_ext[0:POOL_TAIL, :] = jnp.zeros((POOL_TAIL, BRANCH_W), F32)
        conv_ext[0:CONV_TAIL, :] = jnp.zeros((CONV_TAIL, BRANCH_W), F32)
        cum_c[...] = jnp.zeros_like(cum_c)

    mod = mod_ref[0]
    h = _rms_mod(x_ref[0], g_ref[...], mod[1:2, :], mod[0:1, :])
    hb = h.astype(BF16)
    proj = jnp.dot(hb, wmix_ref[...], preferred_element_type=F32)

    lane = lax.broadcasted_iota(jnp.int32, (1, BRANCH_W), 1)
    row = lax.broadcasted_iota(jnp.int32, (tb, 1), 0)

    u = proj[:, 0:BRANCH_W]
    pool_ext[POOL_TAIL:POOL_TAIL + tb, :] = u
    ext = pool_ext[...]
    s2 = ext + pltpu.roll(ext, 1, 0)
    s4 = s2 + pltpu.roll(s2, 2, 0)
    s8 = s4 + pltpu.roll(s4, 4, 0)
    s16 = s8 + pltpu.roll(s8, 8, 0)
    grp = lane // POOL_GW
    wsum = jnp.where(grp == 0, s2, jnp.where(grp == 1, s4, jnp.where(grp == 2, s8, s16)))[POOL_TAIL:, :]
    win = jnp.where(grp == 0, 2.0, jnp.where(grp == 1, 4.0, jnp.where(grp == 2, 8.0, 16.0)))
    count = jnp.minimum((j * tb + row + 1).astype(F32), win)
    pooled = wsum / count - u
    ya = jnp.dot(pooled.astype(BF16), poolw_ref[...], preferred_element_type=F32) * pools_ref[...]
    ya_ref[0] = ya.astype(BF16)
    pool_ext[0:POOL_TAIL, :] = u[tb - POOL_TAIL:, :]

    z = proj[:, BRANCH_W:3 * BRANCH_W]
    z = 0.5 * z * (1.0 + jnp.tanh(math.sqrt(2.0 / math.pi) * (z + 0.044715 * (z * z * z))))
    su = z[:, 0:BRANCH_W]
    sv = _layer_norm(z[:, BRANCH_W:], sglg_ref[...], sglb_ref[...])
    r128 = lax.broadcasted_iota(jnp.int32, (SG_CHUNK, SG_CHUNK), 0)
    c128 = lax.broadcasted_iota(jnp.int32, (SG_CHUNK, SG_CHUNK), 1)
    wcat = jnp.concatenate(
        [jnp.where(r128 >= c128, sgw_ref[hh], 0.0) for hh in range(SG_HEADS)], axis=1).astype(BF16)
    head = lane // SG_HD
    for ci in range(tb // SG_CHUNK):
        rows = slice(ci * SG_CHUNK, (ci + 1) * SG_CHUNK)
        vch = sv[rows, :]
        vstack = jnp.concatenate(
            [jnp.where(head == hh, vch, 0.0) for hh in range(SG_HEADS)], axis=0).astype(BF16)
        s = jnp.dot(wcat, vstack, preferred_element_type=F32) + sgb_ref[...]
        yb_ref[0, rows, :] = (su[rows, :] * s).astype(BF16)

    glu = proj[:, 3 * BRANCH_W:4 * BRANCH_W] * _sigmoid(proj[:, 4 * BRANCH_W:5 * BRANCH_W])
    conv_ext[CONV_TAIL:CONV_TAIL + tb, :] = glu
    acc = jnp.zeros((tb, BRANCH_W), F32) + cb_ref[...]
    for kk in range(CONV_WIDTH):
        off = CONV_TAIL - (CONV_WIDTH - 1) + kk
        acc = acc + cw_ref[kk:kk + 1, :] * conv_ext[off:off + tb, :]
    yc_ref[0] = _silu(_layer_norm(acc, clg_ref[...], clb_ref[...])).astype(BF16)
    conv_ext[0:CONV_TAIL, :] = glu[tb - CONV_TAIL:, :]

    def log_sigmoid(t):
        return jnp.minimum(t, 0.0) - jnp.log(1.0 + jnp.exp(-jnp.abs(t)))

    rr = lax.broadcasted_iota(jnp.int32, (tb, tb), 0)
    cc = lax.broadcasted_iota(jnp.int32, (tb, tb), 1)
    lower = jnp.where(rr >= cc, 1.0, 0.0).astype(F32)
    lf_c = log_sigmoid(jnp.dot(hb, wfc_ref[...], preferred_element_type=F32) + bfc_ref[...])
    cs_c = jnp.dot(lower, lf_c, precision=HIGHEST, preferred_element_type=F32) + cum_c[...]
    cum_c[...] = cs_c[tb - 1:tb, :]
    cl2 = cs_c * LOG2E
    ln = lax.broadcasted_iota(jnp.int32, (1, LANES), 1)
    ones_q = jnp.where((ln >= ATT_HD + 3) & (ln < ATT_HD + 6), 1.0, 0.0)
    ones_k = jnp.where((ln >= ATT_HD) & (ln < ATT_HD + 3), 1.0, 0.0)
    ones_v = jnp.where(ln == ATT_HD, 1.0, 0.0)
    for hh in range(ATT_HEADS):
        pair = (hh // 2) * LANES
        qs = proj[:, 5 * BRANCH_W + pair:5 * BRANCH_W + pair + LANES] * (LOG2E / math.sqrt(ATT_HD))
        ks = proj[:, 6 * BRANCH_W + pair:6 * BRANCH_W + pair + LANES]
        vs = proj[:, 7 * BRANCH_W + pair:7 * BRANCH_W + pair + LANES]
        if hh % 2:
            qs, ks, vs = (pltpu.roll(a, ATT_HD, 1) for a in (qs, ks, vs))
        c = cl2[:, hh:hh + 1]
        hi = c.astype(BF16).astype(F32)
        mid = (c - hi).astype(BF16).astype(F32)
        lo = (c - hi) - mid
        q_extra = jnp.where(ln == ATT_HD, hi, jnp.where(ln == ATT_HD + 1, mid, jnp.where(ln == ATT_HD + 2, lo, ones_q)))
        k_extra = jnp.where(ln == ATT_HD + 3, -hi,
                            jnp.where(ln == ATT_HD + 4, -mid, jnp.where(ln == ATT_HD + 5, -lo, ones_k)))
        q_ref[0, hh, 0] = jnp.where(ln < ATT_HD, qs, q_extra).T.astype(BF16)
        k_ref[0, hh] = jnp.where(ln < ATT_HD, ks, k_extra).astype(BF16)
        v_ref[0, hh, 0] = jnp.where(ln < ATT_HD, vs, ones_v).T.astype(BF16)


def _mixer_in(x, mod_l, norm_g, wmix, wfc, bfc, poolw, pools, sglg, sglb, sgw, sgb, cw, cb, clg, clb, *, tb):
    B, S, D = x.shape
    full = lambda a: pl.BlockSpec(a.shape, lambda b, j: (0,) * a.ndim)
    tok = lambda w: pl.BlockSpec((1, tb, w), lambda b, j: (b, j, 0))
    head = pl.BlockSpec((1, ATT_HEADS, tb, LANES), lambda b, j: (b, 0, j, 0))
    head_t = pl.BlockSpec((1, ATT_HEADS, 1, LANES, tb), lambda b, j: (b, 0, j, 0, 0))
    consts = (norm_g, wmix, wfc, bfc, poolw, pools, sglg, sglb, sgw, sgb, cw, cb, clg, clb)
    act = jax.ShapeDtypeStruct((B, S, BRANCH_W), BF16)
    att = jax.ShapeDtypeStruct((B, ATT_HEADS, S, LANES), BF16)
    att_t = jax.ShapeDtypeStruct((B, ATT_HEADS, S // tb, LANES, tb), BF16)
    return pl.pallas_call(
        functools.partial(_mixer_in_kernel, tb=tb),
        grid=(B, S // tb),
        in_specs=[tok(D), pl.BlockSpec((1, N_MOD, D), lambda b, j: (b, 0, 0))] + [full(a) for a in consts],
        out_specs=[tok(BRANCH_W)] * 3 + [head_t, head, head_t],
        out_shape=[act] * 3 + [att_t, att, att_t],
        scratch_shapes=[pltpu.VMEM((POOL_TAIL + tb, BRANCH_W), F32),
                        pltpu.VMEM((CONV_TAIL + tb, BRANCH_W), F32),
                        pltpu.VMEM((1, LANES), F32)],
        compiler_params=_params("arbitrary", "arbitrary"),
        name="mixer_in",
    )(x, mod_l, *consts)


def _attn_kernel(q_ref, k_ref, v_ref, o_ref, m_ref, acc_ref, *, tq):
    i = pl.program_id(1)
    key = lax.broadcasted_iota(jnp.int32, (tq, tq), 0)
    qry = lax.broadcasted_iota(jnp.int32, (tq, tq), 1)
    m_ref[...] = jnp.full(m_ref.shape, -jnp.inf, F32)
    acc_ref[...] = jnp.zeros(acc_ref.shape, F32)

    def block(kj, diagonal):
        ks = pl.multiple_of(kj * tq, tq)
        logits = [jnp.dot(k_ref[0, hh, pl.ds(ks, tq), :], q_ref[0, hh, 0], preferred_element_type=F32)
                  for hh in range(ATT_HEADS)]
        for hh in range(ATT_HEADS):
            s = logits[hh]
            if diagonal:
                s = jnp.where(key <= qry, s, -jnp.inf)
            m_old = m_ref[hh]
            m_new = jnp.maximum(m_old, jnp.max(s, axis=0, keepdims=True))
            p = jnp.exp2(s - m_new)
            pv = jnp.dot(v_ref[0, hh, kj], p.astype(BF16), preferred_element_type=F32)
            acc_ref[hh] = jnp.exp2(m_old - m_new) * acc_ref[hh] + pv
            m_ref[hh] = m_new

    def body(kj, carry):
        block(kj, False)
        return carry

    lax.fori_loop(0, i, body, 0)
    block(i, True)

    ln = lax.broadcasted_iota(jnp.int32, (1, LANES), 1)
    for pr in range(ATT_HEADS // 2):
        o = []
        for hh in (2 * pr, 2 * pr + 1):
            acc = acc_ref[hh]
            o.append((acc / acc[ATT_HD:ATT_HD + 1, :]).T)
        o_ref[0, :, pr * LANES:(pr + 1) * LANES] = jnp.where(ln < ATT_HD, o[0], pltpu.roll(o[1], ATT_HD, 1)).astype(BF16)


def _attention(q_t, k, v_t, *, tq):
    B, H, S, W = k.shape
    nblk = S // tq
    assert q_t.shape == (B, H, nblk, W, tq) and v_t.shape == q_t.shape
    return pl.pallas_call(
        functools.partial(_attn_kernel, tq=tq),
        grid=(B, nblk),
        in_specs=[pl.BlockSpec((1, H, 1, W, tq), lambda b, i: (b, 0, i, 0, 0)),
                  pl.BlockSpec((1, H, S, W), lambda b, i: (b, 0, 0, 0), pipeline_mode=pl.Buffered(1)),
                  pl.BlockSpec((1, H, nblk, W, tq), lambda b, i: (b, 0, 0, 0, 0), pipeline_mode=pl.Buffered(1))],
        out_specs=pl.BlockSpec((1, tq, BRANCH_W), lambda b, i: (b, i, 0)),
        out_shape=jax.ShapeDtypeStruct((B, S, BRANCH_W), BF16),
        scratch_shapes=[pltpu.VMEM((H, 1, tq), F32), pltpu.VMEM((H, W, tq), F32)],
        compiler_params=_params("arbitrary", "arbitrary"),
        name="attention",
    )(q_t, k, v_t)


def _merge_kernel(x_ref, mod_ref, g_ref, wg_ref, wbr_ref, wout_ref, ya_ref, yb_ref, yc_ref, yd_ref, o_ref):
    x = x_ref[0]
    mod = mod_ref[0]
    hb = _rms_mod(x, g_ref[...], mod[1:2, :], mod[0:1, :]).astype(BF16)
    merged = jnp.zeros(x.shape, F32)
    for n, y_ref in enumerate((ya_ref, yb_ref, yc_ref, yd_ref)):
        gate = _sigmoid(jnp.dot(hb, wg_ref[:, n * D_MODEL:(n + 1) * D_MODEL], preferred_element_type=F32))
        merged = merged + gate * jnp.dot(y_ref[0], wbr_ref[n], preferred_element_type=F32)
    out = jnp.dot(merged.astype(BF16), wout_ref[...], preferred_element_type=F32)
    o_ref[0] = x + mod[2:3, :] * out


def _merge(x, mod_l, norm_g, wgate, wbranch, wout, ya, yb, yc, yd, *, tb):
    B, S, D = x.shape
    full = lambda a: pl.BlockSpec(a.shape, lambda b, j: (0,) * a.ndim)
    tok = lambda w: pl.BlockSpec((1, tb, w), lambda b, j: (b, j, 0))
    return pl.pallas_call(
        _merge_kernel,
        grid=(B, S // tb),
        in_specs=[tok(D), pl.BlockSpec((1, N_MOD, D), lambda b, j: (b, 0, 0)),
                  full(norm_g), full(wgate), full(wbranch), full(wout)] + [tok(BRANCH_W)] * 4,
        out_specs=tok(D),
        out_shape=jax.ShapeDtypeStruct((B, S, D), F32),
        compiler_params=_params("arbitrary", "arbitrary"),
        name="merge",
    )(x, mod_l, norm_g, wgate, wbranch, wout, ya, yb, yc, yd)


def _route(scores_t, bias_t):
    E, n = scores_t.shape
    sel = scores_t + bias_t
    eidx = lax.broadcasted_iota(jnp.int32, (E, n), 0)
    neg = jnp.full((E, n), -jnp.inf, F32)
    gscore = []
    sub = lax.broadcasted_iota(jnp.int32, (GROUP_SIZE, n), 0)
    for g in range(N_GROUPS):
        blk = sel[g * GROUP_SIZE:(g + 1) * GROUP_SIZE, :]
        m1 = jnp.max(blk, axis=0, keepdims=True)
        first = jnp.min(jnp.where(blk == m1, sub, GROUP_SIZE), axis=0, keepdims=True)
        m2 = jnp.max(jnp.where(sub == first, -jnp.inf, blk), axis=0, keepdims=True)
        gscore.append(m1 + m2)
    emask = []
    for g in range(N_GROUPS):
        beaten = jnp.zeros((1, n), jnp.int32)
        for g2 in range(N_GROUPS):
            if g2 == g:
                continue
            wins = (gscore[g2] > gscore[g]) | ((gscore[g2] == gscore[g]) & (g2 < g))
            beaten = beaten + wins.astype(jnp.int32)
        emask.append(jnp.broadcast_to(beaten < TOPK_GROUPS, (GROUP_SIZE, n)))
    cur = jnp.where(jnp.concatenate(emask, axis=0), sel, neg)
    chosen = jnp.zeros((E, n), jnp.bool_)
    firsts = []
    for _ in range(TOP_K):
        m = jnp.max(cur, axis=0, keepdims=True)
        first = jnp.min(jnp.where(cur == m, eidx, E), axis=0, keepdims=True)
        hit = eidx == first
        chosen = chosen | hit
        cur = jnp.where(hit, neg, cur)
        firsts.append(first)
    w = jnp.where(chosen, scores_t, 0.0)
    return w / jnp.sum(w, axis=0, keepdims=True) * ROUTE_SCALE, chosen, firsts


def _route_kernel(x_ref, mod_ref, g_ref, rwt_ref, rb_ref, ls_ref, lst_ref, wt_ref, bc_ref, cr_ref, cnt_ref, count):
    first_step = (pl.program_id(0) == 0) & (pl.program_id(1) == 0)

    @pl.when(first_step)
    def _():
        count[...] = jnp.zeros_like(count)

    mod = mod_ref[0]
    h = _rms_mod(x_ref[0], g_ref[...], mod[4:5, :], mod[3:4, :])
    logits_t = lax.dot_general(rwt_ref[...], h, (((1,), (1,)), ((), ())),
                               precision=HIGHEST, preferred_element_type=F32)
    gates_t, chosen, firsts = _route(_sigmoid(logits_t), rb_ref[...])
    E, tb = gates_t.shape
    ones = jnp.where(chosen, 1.0, 0.0)
    rr = lax.broadcasted_iota(jnp.int32, (tb, tb), 0)
    cc = lax.broadcasted_iota(jnp.int32, (tb, tb), 1)
    upper = jnp.where(rr <= cc, 1.0, 0.0).astype(BF16)
    incl = jnp.dot(ones.astype(BF16), upper, preferred_element_type=F32)
    block_count = jnp.sum(ones, axis=1, keepdims=True)
    er = lax.broadcasted_iota(jnp.int32, (E, E), 0)
    ec = lax.broadcasted_iota(jnp.int32, (E, E), 1)
    before = jnp.where(ec < er, 1.0, 0.0).astype(F32)
    local_base = jnp.dot(before, jnp.broadcast_to(block_count, (E, LANES)), precision=HIGHEST,
                         preferred_element_type=F32)[:, 0:1]
    row_all = local_base + (incl - ones)
    eidx = lax.broadcasted_iota(jnp.int32, (E, tb), 0)
    ls_rows = [jnp.sum(jnp.where(eidx == f, row_all, 0.0), axis=0, keepdims=True) for f in firsts]
    w_rows = [jnp.sum(jnp.where(eidx == f, gates_t, 0.0), axis=0, keepdims=True) for f in firsts]
    ls_ref[...] = jnp.concatenate(ls_rows + [jnp.zeros((SUBLANES - TOP_K, tb), F32)], axis=0).astype(jnp.int32)
    zpad = jnp.zeros((LANES - TOP_K, tb), F32)
    lst_ref[0] = jnp.concatenate(ls_rows + [zpad], axis=0).T.astype(jnp.int32)
    wt_ref[0] = jnp.concatenate(w_rows + [zpad], axis=0).T

    def as_row(col):
        sel = lax.broadcasted_iota(jnp.int32, (E, LANES), 0) == lax.broadcasted_iota(jnp.int32, (E, LANES), 1)
        return jnp.sum(jnp.where(sel, col, 0.0), axis=0, keepdims=True).astype(jnp.int32)

    bc_ref[0] = as_row(block_count)
    cr_ref[0] = as_row(count[...])
    count[...] = count[...] + block_count
    cnt_ref[...] = jnp.broadcast_to(count[...], cnt_ref.shape).astype(jnp.int32)


def _route_call(x, mod_l, norm_g, router_wt, router_b, *, tb):
    B, S, D = x.shape
    nj = S // tb
    full = lambda a: pl.BlockSpec(a.shape, lambda b, j: (0,) * a.ndim)
    tok = lambda w: pl.BlockSpec((1, tb, w), lambda b, j: (b, j, 0))
    kt = pl.BlockSpec((SUBLANES, tb), lambda b, j: (0, b * nj + j))
    per_block = pl.BlockSpec((1, 1, LANES), lambda b, j: (b * nj + j, 0, 0))
    return pl.pallas_call(
        _route_kernel,
        grid=(B, nj),
        in_specs=[tok(D), pl.BlockSpec((1, N_MOD, D), lambda b, j: (b, 0, 0)),
                  full(norm_g), full(router_wt), full(router_b)],
        out_specs=[kt, tok(LANES), tok(LANES), per_block, per_block,
                   pl.BlockSpec((N_EXPERTS, LANES), lambda b, j: (0, 0))],
        out_shape=[jax.ShapeDtypeStruct((SUBLANES, B * S), jnp.int32),
                   jax.ShapeDtypeStruct((B, S, LANES), jnp.int32),
                   jax.ShapeDtypeStruct((B, S, LANES), F32),
                   jax.ShapeDtypeStruct((B * nj, 1, LANES), jnp.int32),
                   jax.ShapeDtypeStruct((B * nj, 1, LANES), jnp.int32),
                   jax.ShapeDtypeStruct((N_EXPERTS, LANES), jnp.int32)],
        scratch_shapes=[pltpu.VMEM((N_EXPERTS, 1), F32)],
        compiler_params=_params("arbitrary", "arbitrary"),
        name="route",
    )(x, mod_l, norm_g, router_wt, router_b)


def _route_fin_kernel(cnt_ref, meta_ref, base_ref, *, tm):
    cnt = cnt_ref[...]
    ntile = lax.shift_right_logical(cnt + (tm - 1), int(math.log2(tm)))
    er = lax.broadcasted_iota(jnp.int32, (N_EXPERTS, N_EXPERTS), 0)
    ec = lax.broadcasted_iota(jnp.int32, (N_EXPERTS, N_EXPERTS), 1)
    before = jnp.where(ec < er, 1.0, 0.0).astype(F32)
    start = jnp.dot(before, ntile.astype(F32), precision=HIGHEST, preferred_element_type=F32).astype(jnp.int32)
    base = start * tm
    base_ref[...] = base
    nlane = meta_ref.shape[1]
    lane = lax.broadcasted_iota(jnp.int32, (1, nlane), 1)
    end = (start + ntile)[:, 0:1]
    tile_expert = jnp.sum(jnp.where(end <= lane, 1, 0), axis=0, keepdims=True)
    tile_expert = jnp.minimum(tile_expert, N_EXPERTS - 1)
    meta_ref[...] = jnp.where(lane == nlane - 1, end[N_EXPERTS - 1:N_EXPERTS, :], tile_expert)


def _route_fin(cnt, *, tm, meta_lanes):
    full = lambda a: pl.BlockSpec(a.shape, lambda i: (0,) * a.ndim)
    outs = [jax.ShapeDtypeStruct((1, meta_lanes), jnp.int32), jax.ShapeDtypeStruct(cnt.shape, jnp.int32)]
    return pl.pallas_call(
        functools.partial(_route_fin_kernel, tm=tm),
        grid=(1,),
        in_specs=[full(cnt)],
        out_specs=[full(o) for o in outs],
        out_shape=outs,
        compiler_params=_params("arbitrary"),
        name="route_fin",
    )(cnt)


SLAB = D_MODEL // LANES


def _to_slabs(ref, value):
    n = value.shape[0]
    for s in range(SLAB):
        ref[pl.ds(s, n, stride=SLAB), :] = value[:, s * LANES:(s + 1) * LANES]


def _from_slabs(ref, n):
    return jnp.concatenate([ref[pl.ds(s, n, stride=SLAB), :] for s in range(SLAB)], axis=1)


def _for_expert_runs(base_sm, bc_sm, cr_sm, max_run, fn):
    def per_expert(e, local_row):
        n = bc_sm[0, 0, e]
        _for_pieces(n, max_run, lambda done, size: fn(local_row + done, base_sm[e] + cr_sm[0, 0, e] + done, size))
        return local_row + n

    lax.fori_loop(0, N_EXPERTS, per_expert, 0)


def _for_pieces(n, max_piece, fn):
    done = 0
    bit = max_piece
    while bit:
        take = n & bit

        @pl.when(take != 0)
        def _(done=done, bit=bit):
            fn(done, bit)

        done = done + take
        bit //= 2


def _slab_rows(ref, row, nrows):
    return ref.at[pl.ds(pl.multiple_of(row * SLAB, SLAB), nrows * SLAB)]


def _dispatch_kernel(cnt_sm, base_sm, bc_sm, cr_sm, bc_prev, cr_prev, ls_ref, x_ref, mod_ref, g_ref, xs_hbm,
                     stage, zeros, sem, zsem, *, tb, tm):
    step = pl.program_id(0) * pl.num_programs(1) + pl.program_id(1)
    last_step = pl.num_programs(0) * pl.num_programs(1) - 1
    cur = lax.rem(step, 2)
    mod = mod_ref[0]
    hb = _rms_mod(x_ref[0], g_ref[...], mod[4:5, :], mod[3:4, :]).astype(BF16)
    ls = ls_ref[...]
    for grp in range(TOP_K):
        row = grp * tb + lax.broadcasted_iota(jnp.int32, (tb, tb), 0)
        hit = ls[0:1, :] == row
        for k in range(1, TOP_K):
            hit = hit | (ls[k:k + 1, :] == row)
        perm = jnp.where(hit, 1.0, 0.0).astype(BF16)
        _to_slabs(stage.at[cur, pl.ds(grp * tb * SLAB, tb * SLAB)], jnp.dot(perm, hb, preferred_element_type=F32))

    def run_copy(buf):
        return lambda local_row, sorted_row, nrows: pltpu.make_async_copy(
            _slab_rows(stage.at[buf], local_row, nrows), _slab_rows(xs_hbm, sorted_row, nrows), sem.at[buf])

    _for_expert_runs(base_sm, bc_sm, cr_sm, tb, lambda *a: run_copy(cur)(*a).start())

    @pl.when(step == 0)
    def _():
        zeros[...] = jnp.zeros_like(zeros)

        def per_expert(e, carry):
            n = cnt_sm[e]
            npad = lax.rem(tm - lax.rem(n, tm), tm)
            zcopy = lambda done, size: pltpu.make_async_copy(
                _slab_rows(zeros, 0, size), _slab_rows(xs_hbm, base_sm[e] + n + done, size), zsem)
            _for_pieces(npad, tm // 2, lambda *a: zcopy(*a).start())
            _for_pieces(npad, tm // 2, lambda *a: zcopy(*a).wait())
            return carry

        lax.fori_loop(0, N_EXPERTS, per_expert, 0)

    @pl.when(step > 0)
    def _():
        _for_expert_runs(base_sm, bc_prev, cr_prev, tb, lambda *a: run_copy(1 - cur)(*a).wait())

    @pl.when(step == last_step)
    def _():
        _for_expert_runs(base_sm, bc_sm, cr_sm, tb, lambda *a: run_copy(cur)(*a).wait())


def _dispatch(cnt, base, blk_cnt, blk_carry, ls, x, mod_l, norm_g, *, tb, tm, n_slots):
    B, S, D = x.shape
    nj = S // tb
    smem_block = lambda shift: pl.BlockSpec(
        (1, 1, LANES), lambda b, j, *_: (jnp.maximum(b * nj + j + shift, 0), 0, 0), memory_space=pltpu.SMEM)
    grid_spec = pltpu.PrefetchScalarGridSpec(
        num_scalar_prefetch=2,
        grid=(B, nj),
        in_specs=[smem_block(0), smem_block(0), smem_block(-1), smem_block(-1),
                  pl.BlockSpec((SUBLANES, tb), lambda b, j, *_: (0, b * nj + j)),
                  pl.BlockSpec((1, tb, D), lambda b, j, *_: (b, j, 0)),
                  pl.BlockSpec((1, N_MOD, D), lambda b, j, *_: (b, 0, 0)),
                  pl.BlockSpec(norm_g.shape, lambda b, j, *_: (0, 0))],
        out_specs=pl.BlockSpec(memory_space=pltpu.HBM),
        scratch_shapes=[pltpu.VMEM((2, TOP_K * tb * SLAB, LANES), F32), pltpu.VMEM((tm // 2 * SLAB, LANES), F32),
                        pltpu.SemaphoreType.DMA((2,)), pltpu.SemaphoreType.DMA])
    return pl.pallas_call(
        functools.partial(_dispatch_kernel, tb=tb, tm=tm),
        grid_spec=grid_spec,
        out_shape=jax.ShapeDtypeStruct((n_slots * SLAB, LANES), F32),
        compiler_params=_params("arbitrary", "arbitrary"),
        name="dispatch",
    )(cnt, base, blk_cnt, blk_carry, blk_cnt, blk_carry, ls, x, mod_l, norm_g)


def _experts_kernel(te_sm, nt_sm, x_ref, w1_ref, w3_ref, w2_ref, y_ref, w1b, w3b, w2b, *, tm):
    i = pl.program_id(0)
    in_use = i < nt_sm[0]

    @pl.when(in_use)
    def _():
        @pl.when((i == 0) | (te_sm[i] != te_sm[jnp.maximum(i - 1, 0)]))
        def _():
            w1b[...] = w1_ref[0].astype(BF16)
            w3b[...] = w3_ref[0].astype(BF16)
            w2b[...] = w2_ref[0].astype(BF16)

        xb = _from_slabs(x_ref, tm).astype(BF16)
        a = jnp.dot(xb, w1b[...], preferred_element_type=F32)
        b = jnp.dot(xb, w3b[...], preferred_element_type=F32)
        _to_slabs(y_ref, jnp.dot((_silu(a) * b).astype(BF16), w2b[...], preferred_element_type=F32))

    @pl.when(jnp.logical_not(in_use))
    def _():
        y_ref[...] = jnp.zeros_like(y_ref)


def _experts(tile_expert, n_tiles, xs, w1, w3, w2, *, tm):
    E, D, F = w1.shape
    n_slots = xs.shape[0] // SLAB
    last = lambda i, te, nt: jnp.minimum(i, nt[0] - 1)
    grid_spec = pltpu.PrefetchScalarGridSpec(
        num_scalar_prefetch=2,
        grid=(n_slots // tm,),
        in_specs=[pl.BlockSpec((tm * SLAB, LANES), lambda i, te, nt: (last(i, te, nt), 0)),
                  pl.BlockSpec((1, D, F), lambda i, te, nt: (te[last(i, te, nt)], 0, 0)),
                  pl.BlockSpec((1, D, F), lambda i, te, nt: (te[last(i, te, nt)], 0, 0)),
                  pl.BlockSpec((1, F, D), lambda i, te, nt: (te[last(i, te, nt)], 0, 0))],
        out_specs=pl.BlockSpec((tm * SLAB, LANES), lambda i, te, nt: (i, 0)),
        scratch_shapes=[pltpu.VMEM((D, F), BF16), pltpu.VMEM((D, F), BF16), pltpu.VMEM((F, D), BF16)])
    return pl.pallas_call(
        functools.partial(_experts_kernel, tm=tm),
        grid_spec=grid_spec,
        out_shape=jax.ShapeDtypeStruct(xs.shape, F32),
        compiler_params=_params("arbitrary"),
        name="experts",
    )(tile_expert, n_tiles, xs, w1, w3, w2)


def _combine_kernel(base_sm, bc_sm, cr_sm, bc_next, cr_next, x_ref, mod_ref, g_ref, lst_ref, wt_ref,
                    sw1_ref, sw3_ref, sw2_ref, fg_ref, ys_hbm, o_ref, stage, sem, *, tb, final_norm):
    step = pl.program_id(0) * pl.num_programs(1) + pl.program_id(1)
    last_step = pl.num_programs(0) * pl.num_programs(1) - 1
    cur = lax.rem(step, 2)

    def run_copy(buf):
        return lambda local_row, sorted_row, nrows: pltpu.make_async_copy(
            _slab_rows(ys_hbm, sorted_row, nrows), _slab_rows(stage.at[buf], local_row, nrows), sem.at[buf])

    @pl.when(step == 0)
    def _():
        _for_expert_runs(base_sm, bc_sm, cr_sm, tb, lambda *a: run_copy(cur)(*a).start())

    @pl.when(step < last_step)
    def _():
        _for_expert_runs(base_sm, bc_next, cr_next, tb, lambda *a: run_copy(1 - cur)(*a).start())

    x = x_ref[0]
    mod = mod_ref[0]
    hb = _rms_mod(x, g_ref[...], mod[4:5, :], mod[3:4, :]).astype(BF16)
    a = jnp.dot(hb, sw1_ref[...], preferred_element_type=F32)
    b = jnp.dot(hb, sw3_ref[...], preferred_element_type=F32)
    acc = jnp.dot((_silu(a) * b).astype(BF16), sw2_ref[...], preferred_element_type=F32)
    wt = wt_ref[0]
    lst = lst_ref[0]
    _for_expert_runs(base_sm, bc_sm, cr_sm, tb, lambda *a: run_copy(cur)(*a).wait())
    for grp in range(TOP_K):
        yg = _from_slabs(stage.at[cur, pl.ds(grp * tb * SLAB, tb * SLAB)], tb).astype(BF16)
        row = grp * tb + lax.broadcasted_iota(jnp.int32, (1, tb), 1)
        gate = jnp.zeros((tb, tb), F32)
        for k in range(TOP_K):
            gate = gate + jnp.where(lst[:, k:k + 1] == row, wt[:, k:k + 1], 0.0)
        hi = gate.astype(BF16)
        lo = (gate - hi.astype(F32)).astype(BF16)
        acc = acc + jnp.dot(hi, yg, preferred_element_type=F32) + jnp.dot(lo, yg, preferred_element_type=F32)
    y = x + mod[5:6, :] * acc
    if final_norm:
        y = y * lax.rsqrt(jnp.mean(y * y, axis=-1, keepdims=True) + EPS) * fg_ref[...]
    o_ref[0] = y


def _combine(base, blk_cnt, blk_carry, x, mod_l, norm_g, lst, wt, sw1, sw3, sw2, final_g, ys, *, tb, final_norm):
    B, S, D = x.shape
    nj = S // tb
    full = lambda a: pl.BlockSpec(a.shape, lambda b, j, *_: (0,) * a.ndim)
    tok = lambda w: pl.BlockSpec((1, tb, w), lambda b, j, *_: (b, j, 0))
    smem_block = lambda shift: pl.BlockSpec(
        (1, 1, LANES), lambda b, j, *_: (jnp.minimum(b * nj + j + shift, B * nj - 1), 0, 0), memory_space=pltpu.SMEM)
    grid_spec = pltpu.PrefetchScalarGridSpec(
        num_scalar_prefetch=1,
        grid=(B, nj),
        in_specs=[smem_block(0), smem_block(0), smem_block(1), smem_block(1), tok(D),
                  pl.BlockSpec((1, N_MOD, D), lambda b, j, *_: (b, 0, 0)),
                  full(norm_g), tok(LANES), tok(LANES), full(sw1), full(sw3), full(sw2), full(final_g),
                  pl.BlockSpec(memory_space=pltpu.HBM)],
        out_specs=tok(D),
        scratch_shapes=[pltpu.VMEM((2, TOP_K * tb * SLAB, LANES), F32), pltpu.SemaphoreType.DMA((2,))])
    return pl.pallas_call(
        functools.partial(_combine_kernel, tb=tb, final_norm=final_norm),
        grid_spec=grid_spec,
        out_shape=jax.ShapeDtypeStruct((B, S, D), F32),
        compiler_params=_params("arbitrary", "arbitrary"),
        name="combine",
    )(base, blk_cnt, blk_carry, blk_cnt, blk_carry, x, mod_l, norm_g, lst, wt, sw1, sw3, sw2, final_g, ys)


def _block_diag(w):
    G, a, b = w.shape
    out = jnp.zeros((G * a, G * b), w.dtype)
    for g in range(G):
        out = out.at[g * a:(g + 1) * a, g * b:(g + 1) * b].set(w[g])
    return out


def kernel(x, c, w_in, b_f, pool_w, pool_scale, sg_ln_g, sg_ln_b, sg_w, sg_b, conv_w, conv_b, conv_ln_g,
           conv_ln_b, w_branch, w_out, mix_norm_g, ffn_norm_g, ada_w, ada_b, router_w, router_bias,
           exp_w1, exp_w3, exp_w2, shared_w1, shared_w3, shared_w2, final_norm_g):
    B, S, D = x.shape
    L = w_in.shape[0]
    tb = min(512, S)
    tq = tb
    tm = EXPERT_TILE
    n_tiles = (B * S * TOP_K) // tm + N_EXPERTS
    meta_lanes = -(-(n_tiles + 1) // LANES) * LANES
    row = lambda a: a.reshape(1, -1)

    mod = _modulation(c, ada_w, ada_b)
    for l in range(L):
        wmix = w_in[l, :, :MIX_COLS].astype(BF16)
        wf = w_in[l, :, MIX_COLS:MIX_COLS + ATT_HEADS]
        wfc = jnp.pad(wf, ((0, 0), (0, LANES - ATT_HEADS))).astype(BF16)
        bfc = jnp.pad(b_f[l], (0, LANES - ATT_HEADS)).reshape(1, LANES)
        wgate = w_in[l, :, MIX_COLS + ATT_HEADS:].astype(BF16)
        poolw = _block_diag(pool_w[l]).astype(BF16)
        sgb = jnp.repeat(sg_b[l].T, SG_HD, axis=1)

        ya, yb, yc, q, k, v = _mixer_in(
            x, mod[l], row(mix_norm_g[l]), wmix, wfc, bfc, poolw, row(pool_scale[l]),
            row(sg_ln_g[l]), row(sg_ln_b[l]), sg_w[l], sgb, conv_w[l], row(conv_b[l]),
            row(conv_ln_g[l]), row(conv_ln_b[l]), tb=tb)
        yd = _attention(q, k, v, tq=tq)
        x = _merge(x, mod[l], row(mix_norm_g[l]), wgate, w_branch[l].astype(BF16), w_out[l].astype(BF16),
                   ya, yb, yc, yd, tb=tb)
        fnorm = row(ffn_norm_g[l])
        ls, lst, wt, blk_cnt, blk_carry, cnt = _route_call(
            x, mod[l], fnorm, router_w[l].T, router_bias[l].reshape(-1, 1), tb=tb)
        meta, base = _route_fin(cnt, tm=tm, meta_lanes=meta_lanes)
        xs = _dispatch(cnt[:, 0], base[:, 0], blk_cnt, blk_carry, ls, x, mod[l], fnorm,
                       tb=tb, tm=tm, n_slots=n_tiles * tm)
        ys = _experts(meta[0, :n_tiles], meta[0, meta_lanes - 1:], xs, exp_w1[l], exp_w3[l], exp_w2[l], tm=tm)
        x = _combine(base[:, 0], blk_cnt, blk_carry, x, mod[l], fnorm, lst, wt, shared_w1[l].astype(BF16),
                     shared_w3[l].astype(BF16), shared_w2[l].astype(BF16), row(final_norm_g), ys,
                     tb=tb, final_norm=(l == L - 1))
    return x
```

```python
import functools
import math

import jax
import jax.numpy as jnp
from jax import lax
from jax.experimental import pallas as pl
from jax.experimental.pallas import tpu as pltpu

F32 = jnp.float32
BF16 = jnp.bfloat16
HIGHEST = lax.Precision.HIGHEST

D_MODEL = 1024
BRANCH_W = 256
N_BRANCH = 4
POOL_WINDOWS = (2, 4, 8, 16)
POOL_GW = 64
POOL_TAIL = 16
SG_HEADS = 4
SG_CHUNK = 128
SG_HD = 64
CONV_WIDTH = 31
CONV_TAIL = 32
ATT_HEADS = 4
ATT_HD = 64
N_EXPERTS = 64
TOP_K = 6
N_GROUPS = 8
GROUP_SIZE = N_EXPERTS // N_GROUPS
TOPK_GROUPS = 4
EXPERT_FF = 256
SHARED_FF = 256
ROUTE_SCALE = 2.5
EPS = 1e-6
N_MOD = 6
MIX_COLS = 8 * BRANCH_W
LANES = 128
SUBLANES = 8
LOG2E = 1.4426950408889634
EXPERT_TILE = 256
VMEM_LIMIT = 56 * 1024 * 1024


def _sigmoid(x):
    return 1.0 / (1.0 + jnp.exp(-x))


def _silu(x):
    return x * _sigmoid(x)


def _rms_mod(x, g, scale, shift):
    y = x * lax.rsqrt(jnp.mean(x * x, axis=-1, keepdims=True) + EPS)
    return (y * g) * (1.0 + scale) + shift


def _layer_norm(x, g, b):
    mu = jnp.mean(x, axis=-1, keepdims=True)
    xc = x - mu
    var = jnp.mean(xc * xc, axis=-1, keepdims=True)
    return xc * lax.rsqrt(var + EPS) * g + b


def _params(*sem):
    return pltpu.CompilerParams(dimension_semantics=sem, vmem_limit_bytes=VMEM_LIMIT)


def _mod_kernel(c_ref, w_ref, b_ref, o_ref):
    c = c_ref[...]
    o_ref[0] = jnp.dot(_silu(c), w_ref[0], precision=HIGHEST, preferred_element_type=F32) + b_ref[0]


def _modulation(c, ada_w, ada_b):
    L, D, N = ada_w.shape
    B = c.shape[0]
    tn = 1536
    out = pl.pallas_call(
        _mod_kernel,
        grid=(L, N // tn),
        in_specs=[pl.BlockSpec((B, D), lambda l, j: (0, 0)),
                  pl.BlockSpec((1, D, tn), lambda l, j: (l, 0, j)),
                  pl.BlockSpec((1, 1, tn), lambda l, j: (l, 0, j))],
        out_specs=pl.BlockSpec((1, B, tn), lambda l, j: (l, 0, j)),
        out_shape=jax.ShapeDtypeStruct((L, B, N), F32),
        compiler_params=_params("arbitrary", "arbitrary"),
        name="modulation",
    )(c, ada_w, ada_b.reshape(L, 1, N))
    return out.reshape(L, B, N_MOD, D)


def _mixer_in_kernel(x_ref, mod_ref, g_ref, wmix_ref, wfc_ref, bfc_ref,
                     poolw_ref, pools_ref, sglg_ref, sglb_ref, sgw_ref, sgb_ref,
                     cw_ref, cb_ref, clg_ref, clb_ref,
                     ya_ref, yb_ref, yc_ref, q_ref, k_ref, v_ref,
                     pool_ext, conv_ext, cum_c, *, tb):
    j = pl.program_id(1)

    @pl.when(j == 0)
    def _():
        pool_ext[0:POOL_TAIL, :] = jnp.zeros((POOL_TAIL, BRANCH_W), F32)
        conv_ext[0:CONV_TAIL, :] = jnp.zeros((CONV_TAIL, BRANCH_W), F32)
        cum_c[...] = jnp.zeros_like(cum_c)

    mod = mod_ref[0]
    h = _rms_mod(x_ref[0], g_ref[...], mod[1:2, :], mod[0:1, :])
    hb = h.astype(BF16)
    proj = jnp.dot(hb, wmix_ref[...], preferred_element_type=F32)

    lane = lax.broadcasted_iota(jnp.int32, (1, BRANCH_W), 1)
    row = lax.broadcasted_iota(jnp.int32, (tb, 1), 0)

    u = proj[:, 0:BRANCH_W]
    pool_ext[POOL_TAIL:POOL_TAIL + tb, :] = u
    ext = pool_ext[...]
    s2 = ext + pltpu.roll(ext, 1, 0)
    s4 = s2 + pltpu.roll(s2, 2, 0)
    s8 = s4 + pltpu.roll(s4, 4, 0)
    s16 = s8 + pltpu.roll(s8, 8, 0)
    grp = lane // POOL_GW
    wsum = jnp.where(grp == 0, s2, jnp.where(grp == 1, s4, jnp.where(grp == 2, s8, s16)))[POOL_TAIL:, :]
    win = jnp.where(grp == 0, 2.0, jnp.where(grp == 1, 4.0, jnp.where(grp == 2, 8.0, 16.0)))
    count = jnp.minimum((j * tb + row + 1).astype(F32), win)
    pooled = wsum / count - u
    ya = jnp.dot(pooled.astype(BF16), poolw_ref[...], preferred_element_type=F32) * pools_ref[...]
    ya_ref[0] = ya.astype(BF16)
    pool_ext[0:POOL_TAIL, :] = u[tb - POOL_TAIL:, :]

    z = proj[:, BRANCH_W:3 * BRANCH_W]
    z = 0.5 * z * (1.0 + jnp.tanh(math.sqrt(2.0 / math.pi) * (z + 0.044715 * (z * z * z))))
    su = z[:, 0:BRANCH_W]
    sv = _layer_norm(z[:, BRANCH_W:], sglg_ref[...], sglb_ref[...])
    r128 = lax.broadcasted_iota(jnp.int32, (SG_CHUNK, SG_CHUNK), 0)
    c128 = lax.broadcasted_iota(jnp.int32, (SG_CHUNK, SG_CHUNK), 1)
    wcat = jnp.concatenate(
        [jnp.where(r128 >= c128, sgw_ref[hh], 0.0) for hh in range(SG_HEADS)], axis=1).astype(BF16)
    head = lane // SG_HD
    for ci in range(tb // SG_CHUNK):
        rows = slice(ci * SG_CHUNK, (ci + 1) * SG_CHUNK)
        vch = sv[rows, :]
        vstack = jnp.concatenate(
            [jnp.where(head == hh, vch, 0.0) for hh in range(SG_HEADS)], axis=0).astype(BF16)
        s = jnp.dot(wcat, vstack, preferred_element_type=F32) + sgb_ref[...]
        yb_ref[0, rows, :] = (su[rows, :] * s).astype(BF16)

    glu = proj[:, 3 * BRANCH_W:4 * BRANCH_W] * _sigmoid(proj[:, 4 * BRANCH_W:5 * BRANCH_W])
    conv_ext[CONV_TAIL:CONV_TAIL + tb, :] = glu
    acc = jnp.zeros((tb, BRANCH_W), F32) + cb_ref[...]
    for kk in range(CONV_WIDTH):
        off = CONV_TAIL - (CONV_WIDTH - 1) + kk
        acc = acc + cw_ref[kk:kk + 1, :] * conv_ext[off:off + tb, :]
    yc_ref[0] = _silu(_layer_norm(acc, clg_ref[...], clb_ref[...])).astype(BF16)
    conv_ext[0:CONV_TAIL, :] = glu[tb - CONV_TAIL:, :]

    def log_sigmoid(t):
        return jnp.minimum(t, 0.0) - jnp.log(1.0 + jnp.exp(-jnp.abs(t)))

    rr = lax.broadcasted_iota(jnp.int32, (tb, tb), 0)
    cc = lax.broadcasted_iota(jnp.int32, (tb, tb), 1)
    lower = jnp.where(rr >= cc, 1.0, 0.0).astype(F32)
    lf_c = log_sigmoid(jnp.dot(hb, wfc_ref[...], preferred_element_type=F32) + bfc_ref[...])
    cs_c = jnp.dot(lower, lf_c, precision=HIGHEST, preferred_element_type=F32) + cum_c[...]
    cum_c[...] = cs_c[tb - 1:tb, :]
    cl2 = cs_c * LOG2E
    ln = lax.broadcasted_iota(jnp.int32, (1, LANES), 1)
    ones_q = jnp.where((ln >= ATT_HD + 3) & (ln < ATT_HD + 6), 1.0, 0.0)
    ones_k = jnp.where((ln >= ATT_HD) & (ln < ATT_HD + 3), 1.0, 0.0)
    ones_v = jnp.where(ln == ATT_HD, 1.0, 0.0)
    for hh in range(ATT_HEADS):
        pair = (hh // 2) * LANES
        qs = proj[:, 5 * BRANCH_W + pair:5 * BRANCH_W + pair + LANES] * (LOG2E / math.sqrt(ATT_HD))
        ks = proj[:, 6 * BRANCH_W + pair:6 * BRANCH_W + pair + LANES]
        vs = proj[:, 7 * BRANCH_W + pair:7 * BRANCH_W + pair + LANES]
        if hh % 2:
            qs, ks, vs = (pltpu.roll(a, ATT_HD, 1) for a in (qs, ks, vs))
        c = cl2[:, hh:hh + 1]
        hi = c.astype(BF16).astype(F32)
        mid = (c - hi).astype(BF16).astype(F32)
        lo = (c - hi) - mid
        q_extra = jnp.where(ln == ATT_HD, hi, jnp.where(ln == ATT_HD + 1, mid, jnp.where(ln == ATT_HD + 2, lo, ones_q)))
        k_extra = jnp.where(ln == ATT_HD + 3, -hi,
                            jnp.where(ln == ATT_HD + 4, -mid, jnp.where(ln == ATT_HD + 5, -lo, ones_k)))
        q_ref[0, hh, 0] = jnp.where(ln < ATT_HD, qs, q_extra).T.astype(BF16)
        k_ref[0, hh] = jnp.where(ln < ATT_HD, ks, k_extra).astype(BF16)
        v_ref[0, hh, 0] = jnp.where(ln < ATT_HD, vs, ones_v).T.astype(BF16)


def _mixer_in(x, mod_l, norm_g, wmix, wfc, bfc, poolw, pools, sglg, sglb, sgw, sgb, cw, cb, clg, clb, *, tb):
    B, S, D = x.shape
    full = lambda a: pl.BlockSpec(a.shape, lambda b, j: (0,) * a.ndim)
    tok = lambda w: pl.BlockSpec((1, tb, w), lambda b, j: (b, j, 0))
    head = pl.BlockSpec((1, ATT_HEADS, tb, LANES), lambda b, j: (b, 0, j, 0))
    head_t = pl.BlockSpec((1, ATT_HEADS, 1, LANES, tb), lambda b, j: (b, 0, j, 0, 0))
    consts = (norm_g, wmix, wfc, bfc, poolw, pools, sglg, sglb, sgw, sgb, cw, cb, clg, clb)
    act = jax.ShapeDtypeStruct((B, S, BRANCH_W), BF16)
    att = jax.ShapeDtypeStruct((B, ATT_HEADS, S, LANES), BF16)
    att_t = jax.ShapeDtypeStruct((B, ATT_HEADS, S // tb, LANES, tb), BF16)
    return pl.pallas_call(
        functools.partial(_mixer_in_kernel, tb=tb),
        grid=(B, S // tb),
        in_specs=[tok(D), pl.BlockSpec((1, N_MOD, D), lambda b, j: (b, 0, 0))] + [full(a) for a in consts],
        out_specs=[tok(BRANCH_W)] * 3 + [head_t, head, head_t],
        out_shape=[act] * 3 + [att_t, att, att_t],
        scratch_shapes=[pltpu.VMEM((POOL_TAIL + tb, BRANCH_W), F32),
                        pltpu.VMEM((CONV_TAIL + tb, BRANCH_W), F32),
                        pltpu.VMEM((1, LANES), F32)],
        compiler_params=_params("arbitrary", "arbitrary"),
        name="mixer_in",
    )(x, mod_l, *consts)


def _attn_kernel(q_ref, k_ref, v_ref, o_ref, m_ref, acc_ref, *, tq):
    i = pl.program_id(1)
    key = lax.broadcasted_iota(jnp.int32, (tq, tq), 0)
    qry = lax.broadcasted_iota(jnp.int32, (tq, tq), 1)
    m_ref[...] = jnp.full(m_ref.shape, -jnp.inf, F32)
    acc_ref[...] = jnp.zeros(acc_ref.shape, F32)

    def block(kj, diagonal):
        ks = pl.multiple_of(kj * tq, tq)
        logits = [jnp.dot(k_ref[0, hh, pl.ds(ks, tq), :], q_ref[0, hh, 0], preferred_element_type=F32)
                  for hh in range(ATT_HEADS)]
        for hh in range(ATT_HEADS):
            s = logits[hh]
            if diagonal:
                s = jnp.where(key <= qry, s, -jnp.inf)
            m_old = m_ref[hh]
            m_new = jnp.maximum(m_old, jnp.max(s, axis=0, keepdims=True))
            p = jnp.exp2(s - m_new)
            pv = jnp.dot(v_ref[0, hh, kj], p.astype(BF16), preferred_element_type=F32)
            acc_ref[hh] = jnp.exp2(m_old - m_new) * acc_ref[hh] + pv
            m_ref[hh] = m_new

    def body(kj, carry):
        block(kj, False)
        return carry

    lax.fori_loop(0, i, body, 0)
    block(i, True)

    ln = lax.broadcasted_iota(jnp.int32, (1, LANES), 1)
    for pr in range(ATT_HEADS // 2):
        o = []
        for hh in (2 * pr, 2 * pr + 1):
            acc = acc_ref[hh]
            o.append((acc / acc[ATT_HD:ATT_HD + 1, :]).T)
        o_ref[0, :, pr * LANES:(pr + 1) * LANES] = jnp.where(ln < ATT_HD, o[0], pltpu.roll(o[1], ATT_HD, 1)).astype(BF16)


def _attention(q_t, k, v_t, *, tq):
    B, H, S, W = k.shape
    nblk = S // tq
    assert q_t.shape == (B, H, nblk, W, tq) and v_t.shape == q_t.shape
    return pl.pallas_call(
        functools.partial(_attn_kernel, tq=tq),
        grid=(B, nblk),
        in_specs=[pl.BlockSpec((1, H, 1, W, tq), lambda b, i: (b, 0, i, 0, 0)),
                  pl.BlockSpec((1, H, S, W), lambda b, i: (b, 0, 0, 0), pipeline_mode=pl.Buffered(1)),
                  pl.BlockSpec((1, H, nblk, W, tq), lambda b, i: (b, 0, 0, 0, 0), pipeline_mode=pl.Buffered(1))],
        out_specs=pl.BlockSpec((1, tq, BRANCH_W), lambda b, i: (b, i, 0)),
        out_shape=jax.ShapeDtypeStruct((B, S, BRANCH_W), BF16),
        scratch_shapes=[pltpu.VMEM((H, 1, tq), F32), pltpu.VMEM((H, W, tq), F32)],
        compiler_params=_params("arbitrary", "arbitrary"),
        name="attention",
    )(q_t, k, v_t)


def _merge_kernel(x_ref, mod_ref, g_ref, wg_ref, wbr_ref, wout_ref, ya_ref, yb_ref, yc_ref, yd_ref, o_ref):
    x = x_ref[0]
    mod = mod_ref[0]
    hb = _rms_mod(x, g_ref[...], mod[1:2, :], mod[0:1, :]).astype(BF16)
    merged = jnp.zeros(x.shape, F32)
    for n, y_ref in enumerate((ya_ref, yb_ref, yc_ref, yd_ref)):
        gate = _sigmoid(jnp.dot(hb, wg_ref[:, n * D_MODEL:(n + 1) * D_MODEL], preferred_element_type=F32))
        merged = merged + gate * jnp.dot(y_ref[0], wbr_ref[n], preferred_element_type=F32)
    out = jnp.dot(merged.astype(BF16), wout_ref[...], preferred_element_type=F32)
    o_ref[0] = x + mod[2:3, :] * out


def _merge(x, mod_l, norm_g, wgate, wbranch, wout, ya, yb, yc, yd, *, tb):
    B, S, D = x.shape
    full = lambda a: pl.BlockSpec(a.shape, lambda b, j: (0,) * a.ndim)
    tok = lambda w: pl.BlockSpec((1, tb, w), lambda b, j: (b, j, 0))
    return pl.pallas_call(
        _merge_kernel,
        grid=(B, S // tb),
        in_specs=[tok(D), pl.BlockSpec((1, N_MOD, D), lambda b, j: (b, 0, 0)),
                  full(norm_g), full(wgate), full(wbranch), full(wout)] + [tok(BRANCH_W)] * 4,
        out_specs=tok(D),
        out_shape=jax.ShapeDtypeStruct((B, S, D), F32),
        compiler_params=_params("arbitrary", "arbitrary"),
        name="merge",
    )(x, mod_l, norm_g, wgate, wbranch, wout, ya, yb, yc, yd)


def _route(scores_t, bias_t):
    E, n = scores_t.shape
    sel = scores_t + bias_t
    eidx = lax.broadcasted_iota(jnp.int32, (E, n), 0)
    neg = jnp.full((E, n), -jnp.inf, F32)
    gscore = []
    sub = lax.broadcasted_iota(jnp.int32, (GROUP_SIZE, n), 0)
    for g in range(N_GROUPS):
        blk = sel[g * GROUP_SIZE:(g + 1) * GROUP_SIZE, :]
        m1 = jnp.max(blk, axis=0, keepdims=True)
        first = jnp.min(jnp.where(blk == m1, sub, GROUP_SIZE), axis=0, keepdims=True)
        m2 = jnp.max(jnp.where(sub == first, -jnp.inf, blk), axis=0, keepdims=True)
        gscore.append(m1 + m2)
    emask = []
    for g in range(N_GROUPS):
        beaten = jnp.zeros((1, n), jnp.int32)
        for g2 in range(N_GROUPS):
            if g2 == g:
                continue
            wins = (gscore[g2] > gscore[g]) | ((gscore[g2] == gscore[g]) & (g2 < g))
            beaten = beaten + wins.astype(jnp.int32)
        emask.append(jnp.broadcast_to(beaten < TOPK_GROUPS, (GROUP_SIZE, n)))
    cur = jnp.where(jnp.concatenate(emask, axis=0), sel, neg)
    chosen = jnp.zeros((E, n), jnp.bool_)
    firsts = []
    for _ in range(TOP_K):
        m = jnp.max(cur, axis=0, keepdims=True)
        first = jnp.min(jnp.where(cur == m, eidx, E), axis=0, keepdims=True)
        hit = eidx == first
        chosen = chosen | hit
        cur = jnp.where(hit, neg, cur)
        firsts.append(first)
    w = jnp.where(chosen, scores_t, 0.0)
    return w / jnp.sum(w, axis=0, keepdims=True) * ROUTE_SCALE, chosen, firsts


def _route_kernel(x_ref, mod_ref, g_ref, rwt_ref, rb_ref, ls_ref, lst_ref, wt_ref, bc_ref, cr_ref, cnt_ref, count):
    first_step = (pl.program_id(0) == 0) & (pl.program_id(1) == 0)

    @pl.when(first_step)
    def _():
        count[...] = jnp.zeros_like(count)

    mod = mod_ref[0]
    h = _rms_mod(x_ref[0], g_ref[...], mod[4:5, :], mod[3:4, :])
    logits_t = lax.dot_general(rwt_ref[...], h, (((1,), (1,)), ((), ())),
                               precision=HIGHEST, preferred_element_type=F32)
    gates_t, chosen, firsts = _route(_sigmoid(logits_t), rb_ref[...])
    E, tb = gates_t.shape
    ones = jnp.where(chosen, 1.0, 0.0)
    rr = lax.broadcasted_iota(jnp.int32, (tb, tb), 0)
    cc = lax.broadcasted_iota(jnp.int32, (tb, tb), 1)
    upper = jnp.where(rr <= cc, 1.0, 0.0).astype(BF16)
    incl = jnp.dot(ones.astype(BF16), upper, preferred_element_type=F32)
    block_count = jnp.sum(ones, axis=1, keepdims=True)
    er = lax.broadcasted_iota(jnp.int32, (E, E), 0)
    ec = lax.broadcasted_iota(jnp.int32, (E, E), 1)
    before = jnp.where(ec < er, 1.0, 0.0).astype(F32)
    local_base = jnp.dot(before, jnp.broadcast_to(block_count, (E, LANES)), precision=HIGHEST,
                         preferred_element_type=F32)[:, 0:1]
    row_all = local_base + (incl - ones)
    eidx = lax.broadcasted_iota(jnp.int32, (E, tb), 0)
    ls_rows = [jnp.sum(jnp.where(eidx == f, row_all, 0.0), axis=0, keepdims=True) for f in firsts]
    w_rows = [jnp.sum(jnp.where(eidx == f, gates_t, 0.0), axis=0, keepdims=True) for f in firsts]
    ls_ref[...] = jnp.concatenate(ls_rows + [jnp.zeros((SUBLANES - TOP_K, tb), F32)], axis=0).astype(jnp.int32)
    zpad = jnp.zeros((LANES - TOP_K, tb), F32)
    lst_ref[0] = jnp.concatenate(ls_rows + [zpad], axis=0).T.astype(jnp.int32)
    wt_ref[0] = jnp.concatenate(w_rows + [zpad], axis=0).T

    def as_row(col):
        sel = lax.broadcasted_iota(jnp.int32, (E, LANES), 0) == lax.broadcasted_iota(jnp.int32, (E, LANES), 1)
        return jnp.sum(jnp.where(sel, col, 0.0), axis=0, keepdims=True).astype(jnp.int32)

    bc_ref[0] = as_row(block_count)
    cr_ref[0] = as_row(count[...])
    count[...] = count[...] + block_count
    cnt_ref[...] = jnp.broadcast_to(count[...], cnt_ref.shape).astype(jnp.int32)


def _route_call(x, mod_l, norm_g, router_wt, router_b, *, tb):
    B, S, D = x.shape
    nj = S // tb
    full = lambda a: pl.BlockSpec(a.shape, lambda b, j: (0,) * a.ndim)
    tok = lambda w: pl.BlockSpec((1, tb, w), lambda b, j: (b, j, 0))
    kt = pl.BlockSpec((SUBLANES, tb), lambda b, j: (0, b * nj + j))
    per_block = pl.BlockSpec((1, 1, LANES), lambda b, j: (b * nj + j, 0, 0))
    return pl.pallas_call(
        _route_kernel,
        grid=(B, nj),
        in_specs=[tok(D), pl.BlockSpec((1, N_MOD, D), lambda b, j: (b, 0, 0)),
                  full(norm_g), full(router_wt), full(router_b)],
        out_specs=[kt, tok(LANES), tok(LANES), per_block, per_block,
                   pl.BlockSpec((N_EXPERTS, LANES), lambda b, j: (0, 0))],
        out_shape=[jax.ShapeDtypeStruct((SUBLANES, B * S), jnp.int32),
                   jax.ShapeDtypeStruct((B, S, LANES), jnp.int32),
                   jax.ShapeDtypeStruct((B, S, LANES), F32),
                   jax.ShapeDtypeStruct((B * nj, 1, LANES), jnp.int32),
                   jax.ShapeDtypeStruct((B * nj, 1, LANES), jnp.int32),
                   jax.ShapeDtypeStruct((N_EXPERTS, LANES), jnp.int32)],
        scratch_shapes=[pltpu.VMEM((N_EXPERTS, 1), F32)],
        compiler_params=_params("arbitrary", "arbitrary"),
        name="route",
    )(x, mod_l, norm_g, router_wt, router_b)


def _route_fin_kernel(cnt_ref, meta_ref, base_ref, *, tm):
    cnt = cnt_ref[...]
    ntile = lax.shift_right_logical(cnt + (tm - 1), int(math.log2(tm)))
    er = lax.broadcasted_iota(jnp.int32, (N_EXPERTS, N_EXPERTS), 0)
    ec = lax.broadcasted_iota(jnp.int32, (N_EXPERTS, N_EXPERTS), 1)
    before = jnp.where(ec < er, 1.0, 0.0).astype(F32)
    start = jnp.dot(before, ntile.astype(F32), precision=HIGHEST, preferred_element_type=F32).astype(jnp.int32)
    base = start * tm
    base_ref[...] = base
    nlane = meta_ref.shape[1]
    lane = lax.broadcasted_iota(jnp.int32, (1, nlane), 1)
    end = (start + ntile)[:, 0:1]
    tile_expert = jnp.sum(jnp.where(end <= lane, 1, 0), axis=0, keepdims=True)
    tile_expert = jnp.minimum(tile_expert, N_EXPERTS - 1)
    meta_ref[...] = jnp.where(lane == nlane - 1, end[N_EXPERTS - 1:N_EXPERTS, :], tile_expert)


def _route_fin(cnt, *, tm, meta_lanes):
    full = lambda a: pl.BlockSpec(a.shape, lambda i: (0,) * a.ndim)
    outs = [jax.ShapeDtypeStruct((1, meta_lanes), jnp.int32), jax.ShapeDtypeStruct(cnt.shape, jnp.int32)]
    return pl.pallas_call(
        functools.partial(_route_fin_kernel, tm=tm),
        grid=(1,),
        in_specs=[full(cnt)],
        out_specs=[full(o) for o in outs],
        out_shape=outs,
        compiler_params=_params("arbitrary"),
        name="route_fin",
    )(cnt)


SLAB = D_MODEL // LANES


def _to_slabs(ref, value):
    n = value.shape[0]
    for s in range(SLAB):
        ref[pl.ds(s, n, stride=SLAB), :] = value[:, s * LANES:(s + 1) * LANES]


def _from_slabs(ref, n):
    return jnp.concatenate([ref[pl.ds(s, n, stride=SLAB), :] for s in range(SLAB)], axis=1)


def _for_expert_runs(base_sm, bc_sm, cr_sm, max_run, fn):
    def per_expert(e, local_row):
        n = bc_sm[0, 0, e]
        _for_pieces(n, max_run, lambda done, size: fn(local_row + done, base_sm[e] + cr_sm[0, 0, e] + done, size))
        return local_row + n

    lax.fori_loop(0, N_EXPERTS, per_expert, 0)


def _for_pieces(n, max_piece, fn):
    done = 0
    bit = max_piece
    while bit:
        take = n & bit

        @pl.when(take != 0)
        def _(done=done, bit=bit):
            fn(done, bit)

        done = done + take
        bit //= 2


def _queue(nrows):
    return int(math.log2(nrows)) % 2


def _slab_rows(ref, row, nrows):
    return ref.at[pl.ds(pl.multiple_of(row * SLAB, SLAB), nrows * SLAB)]


def _dispatch_kernel(cnt_sm, base_sm, bc_sm, cr_sm, bc_prev, cr_prev, ls_ref, x_ref, mod_ref, g_ref, xs_hbm,
                     stage, zeros, sem, zsem, *, tb, tm):
    step = pl.program_id(0) * pl.num_programs(1) + pl.program_id(1)
    last_step = pl.num_programs(0) * pl.num_programs(1) - 1
    cur = lax.rem(step, 2)
    mod = mod_ref[0]
    hb = _rms_mod(x_ref[0], g_ref[...], mod[4:5, :], mod[3:4, :]).astype(BF16)
    ls = ls_ref[...]
    for grp in range(TOP_K):
        row = grp * tb + lax.broadcasted_iota(jnp.int32, (tb, tb), 0)
        hit = ls[0:1, :] == row
        for k in range(1, TOP_K):
            hit = hit | (ls[k:k + 1, :] == row)
        perm = jnp.where(hit, 1.0, 0.0).astype(BF16)
        _to_slabs(stage.at[cur, pl.ds(grp * tb * SLAB, tb * SLAB)], jnp.dot(perm, hb, preferred_element_type=F32))

    def run_copy(buf):
        return lambda local_row, sorted_row, nrows: pltpu.make_async_copy(
            _slab_rows(stage.at[buf], local_row, nrows), _slab_rows(xs_hbm, sorted_row, nrows), sem.at[buf])

    _for_expert_runs(base_sm, bc_sm, cr_sm, tb, lambda *a: run_copy(cur)(*a).start(priority=_queue(a[2])))

    @pl.when(step == 0)
    def _():
        zeros[...] = jnp.zeros_like(zeros)

        def per_expert(e, carry):
            n = cnt_sm[e]
            npad = lax.rem(tm - lax.rem(n, tm), tm)
            zcopy = lambda done, size: pltpu.make_async_copy(
                _slab_rows(zeros, 0, size), _slab_rows(xs_hbm, base_sm[e] + n + done, size), zsem)
            _for_pieces(npad, tm // 2, lambda *a: zcopy(*a).start())
            _for_pieces(npad, tm // 2, lambda *a: zcopy(*a).wait())
            return carry

        lax.fori_loop(0, N_EXPERTS, per_expert, 0)

    @pl.when(step > 0)
    def _():
        _for_expert_runs(base_sm, bc_prev, cr_prev, tb, lambda *a: run_copy(1 - cur)(*a).wait())

    @pl.when(step == last_step)
    def _():
        _for_expert_runs(base_sm, bc_sm, cr_sm, tb, lambda *a: run_copy(cur)(*a).wait())


def _dispatch(cnt, base, blk_cnt, blk_carry, ls, x, mod_l, norm_g, *, tb, tm, n_slots):
    B, S, D = x.shape
    nj = S // tb
    smem_block = lambda shift: pl.BlockSpec(
        (1, 1, LANES), lambda b, j, *_: (jnp.maximum(b * nj + j + shift, 0), 0, 0), memory_space=pltpu.SMEM)
    grid_spec = pltpu.PrefetchScalarGridSpec(
        num_scalar_prefetch=2,
        grid=(B, nj),
        in_specs=[smem_block(0), smem_block(0), smem_block(-1), smem_block(-1),
                  pl.BlockSpec((SUBLANES, tb), lambda b, j, *_: (0, b * nj + j)),
                  pl.BlockSpec((1, tb, D), lambda b, j, *_: (b, j, 0)),
                  pl.BlockSpec((1, N_MOD, D), lambda b, j, *_: (b, 0, 0)),
                  pl.BlockSpec(norm_g.shape, lambda b, j, *_: (0, 0))],
        out_specs=pl.BlockSpec(memory_space=pltpu.HBM),
        scratch_shapes=[pltpu.VMEM((2, TOP_K * tb * SLAB, LANES), F32), pltpu.VMEM((tm // 2 * SLAB, LANES), F32),
                        pltpu.SemaphoreType.DMA((2,)), pltpu.SemaphoreType.DMA])
    return pl.pallas_call(
        functools.partial(_dispatch_kernel, tb=tb, tm=tm),
        grid_spec=grid_spec,
        out_shape=jax.ShapeDtypeStruct((n_slots * SLAB, LANES), F32),
        compiler_params=_params("arbitrary", "arbitrary"),
        name="dispatch",
    )(cnt, base, blk_cnt, blk_carry, blk_cnt, blk_carry, ls, x, mod_l, norm_g)


def _experts_kernel(te_sm, nt_sm, x_ref, w1_ref, w3_ref, w2_ref, y_ref, w1b, w3b, w2b, *, tm):
    i = pl.program_id(0)
    in_use = i < nt_sm[0]

    @pl.when(in_use)
    def _():
        @pl.when((i == 0) | (te_sm[i] != te_sm[jnp.maximum(i - 1, 0)]))
        def _():
            w1b[...] = w1_ref[0].astype(BF16)
            w3b[...] = w3_ref[0].astype(BF16)
            w2b[...] = w2_ref[0].astype(BF16)

        xb = _from_slabs(x_ref, tm).astype(BF16)
        a = jnp.dot(xb, w1b[...], preferred_element_type=F32)
        b = jnp.dot(xb, w3b[...], preferred_element_type=F32)
        _to_slabs(y_ref, jnp.dot((_silu(a) * b).astype(BF16), w2b[...], preferred_element_type=F32))

    @pl.when(jnp.logical_not(in_use))
    def _():
        y_ref[...] = jnp.zeros_like(y_ref)


def _experts(tile_expert, n_tiles, xs, w1, w3, w2, *, tm):
    E, D, F = w1.shape
    n_slots = xs.shape[0] // SLAB
    last = lambda i, te, nt: jnp.minimum(i, nt[0] - 1)
    grid_spec = pltpu.PrefetchScalarGridSpec(
        num_scalar_prefetch=2,
        grid=(n_slots // tm,),
        in_specs=[pl.BlockSpec((tm * SLAB, LANES), lambda i, te, nt: (last(i, te, nt), 0)),
                  pl.BlockSpec((1, D, F), lambda i, te, nt: (te[last(i, te, nt)], 0, 0)),
                  pl.BlockSpec((1, D, F), lambda i, te, nt: (te[last(i, te, nt)], 0, 0)),
                  pl.BlockSpec((1, F, D), lambda i, te, nt: (te[last(i, te, nt)], 0, 0))],
        out_specs=pl.BlockSpec((tm * SLAB, LANES), lambda i, te, nt: (i, 0)),
        scratch_shapes=[pltpu.VMEM((D, F), BF16), pltpu.VMEM((D, F), BF16), pltpu.VMEM((F, D), BF16)])
    return pl.pallas_call(
        functools.partial(_experts_kernel, tm=tm),
        grid_spec=grid_spec,
        out_shape=jax.ShapeDtypeStruct(xs.shape, F32),
        compiler_params=_params("arbitrary"),
        name="experts",
    )(tile_expert, n_tiles, xs, w1, w3, w2)


def _combine_kernel(base_sm, bc_sm, cr_sm, bc_next, cr_next, x_ref, mod_ref, g_ref, lst_ref, wt_ref,
                    sw1_ref, sw3_ref, sw2_ref, fg_ref, ys_hbm, o_ref, stage, sem, *, tb, final_norm):
    step = pl.program_id(0) * pl.num_programs(1) + pl.program_id(1)
    last_step = pl.num_programs(0) * pl.num_programs(1) - 1
    cur = lax.rem(step, 2)

    def run_copy(buf):
        return lambda local_row, sorted_row, nrows: pltpu.make_async_copy(
            _slab_rows(ys_hbm, sorted_row, nrows), _slab_rows(stage.at[buf], local_row, nrows), sem.at[buf])

    @pl.when(step == 0)
    def _():
        _for_expert_runs(base_sm, bc_sm, cr_sm, tb, lambda *a: run_copy(cur)(*a).start(priority=_queue(a[2])))

    @pl.when(step < last_step)
    def _():
        _for_expert_runs(base_sm, bc_next, cr_next, tb, lambda *a: run_copy(1 - cur)(*a).start(priority=_queue(a[2])))

    x = x_ref[0]
    mod = mod_ref[0]
    hb = _rms_mod(x, g_ref[...], mod[4:5, :], mod[3:4, :]).astype(BF16)
    a = jnp.dot(hb, sw1_ref[...], preferred_element_type=F32)
    b = jnp.dot(hb, sw3_ref[...], preferred_element_type=F32)
    acc = jnp.dot((_silu(a) * b).astype(BF16), sw2_ref[...], preferred_element_type=F32)
    wt = wt_ref[0]
    lst = lst_ref[0]
    _for_expert_runs(base_sm, bc_sm, cr_sm, tb, lambda *a: run_copy(cur)(*a).wait())
    for grp in range(TOP_K):
        yg = _from_slabs(stage.at[cur, pl.ds(grp * tb * SLAB, tb * SLAB)], tb).astype(BF16)
        row = grp * tb + lax.broadcasted_iota(jnp.int32, (1, tb), 1)
        gate = jnp.zeros((tb, tb), F32)
        for k in range(TOP_K):
            gate = gate + jnp.where(lst[:, k:k + 1] == row, wt[:, k:k + 1], 0.0)
        hi = gate.astype(BF16)
        lo = (gate - hi.astype(F32)).astype(BF16)
        acc = acc + jnp.dot(hi, yg, preferred_element_type=F32) + jnp.dot(lo, yg, preferred_element_type=F32)
    y = x + mod[5:6, :] * acc
    if final_norm:
        y = y * lax.rsqrt(jnp.mean(y * y, axis=-1, keepdims=True) + EPS) * fg_ref[...]
    o_ref[0] = y


def _combine(base, blk_cnt, blk_carry, x, mod_l, norm_g, lst, wt, sw1, sw3, sw2, final_g, ys, *, tb, final_norm):
    B, S, D = x.shape
    nj = S // tb
    full = lambda a: pl.BlockSpec(a.shape, lambda b, j, *_: (0,) * a.ndim)
    tok = lambda w: pl.BlockSpec((1, tb, w), lambda b, j, *_: (b, j, 0))
    smem_block = lambda shift: pl.BlockSpec(
        (1, 1, LANES), lambda b, j, *_: (jnp.minimum(b * nj + j + shift, B * nj - 1), 0, 0), memory_space=pltpu.SMEM)
    grid_spec = pltpu.PrefetchScalarGridSpec(
        num_scalar_prefetch=1,
        grid=(B, nj),
        in_specs=[smem_block(0), smem_block(0), smem_block(1), smem_block(1), tok(D),
                  pl.BlockSpec((1, N_MOD, D), lambda b, j, *_: (b, 0, 0)),
                  full(norm_g), tok(LANES), tok(LANES), full(sw1), full(sw3), full(sw2), full(final_g),
                  pl.BlockSpec(memory_space=pltpu.HBM)],
        out_specs=tok(D),
        scratch_shapes=[pltpu.VMEM((2, TOP_K * tb * SLAB, LANES), F32), pltpu.SemaphoreType.DMA((2,))])
    return pl.pallas_call(
        functools.partial(_combine_kernel, tb=tb, final_norm=final_norm),
        grid_spec=grid_spec,
        out_shape=jax.ShapeDtypeStruct((B, S, D), F32),
        compiler_params=_params("arbitrary", "arbitrary"),
        name="combine",
    )(base, blk_cnt, blk_carry, blk_cnt, blk_carry, x, mod_l, norm_g, lst, wt, sw1, sw3, sw2, final_g, ys)


def _block_diag(w):
    G, a, b = w.shape
    out = jnp.zeros((G * a, G * b), w.dtype)
    for g in range(G):
        out = out.at[g * a:(g + 1) * a, g * b:(g + 1) * b].set(w[g])
    return out


def kernel(x, c, w_in, b_f, pool_w, pool_scale, sg_ln_g, sg_ln_b, sg_w, sg_b, conv_w, conv_b, conv_ln_g,
           conv_ln_b, w_branch, w_out, mix_norm_g, ffn_norm_g, ada_w, ada_b, router_w, router_bias,
           exp_w1, exp_w3, exp_w2, shared_w1, shared_w3, shared_w2, final_norm_g):
    B, S, D = x.shape
    L = w_in.shape[0]
    tb = min(512, S)
    tq = tb
    tm = EXPERT_TILE
    n_tiles = (B * S * TOP_K) // tm + N_EXPERTS
    meta_lanes = -(-(n_tiles + 1) // LANES) * LANES
    row = lambda a: a.reshape(1, -1)

    mod = _modulation(c, ada_w, ada_b)
    for l in range(L):
        wmix = w_in[l, :, :MIX_COLS].astype(BF16)
        wf = w_in[l, :, MIX_COLS:MIX_COLS + ATT_HEADS]
        wfc = jnp.pad(wf, ((0, 0), (0, LANES - ATT_HEADS))).astype(BF16)
        bfc = jnp.pad(b_f[l], (0, LANES - ATT_HEADS)).reshape(1, LANES)
        wgate = w_in[l, :, MIX_COLS + ATT_HEADS:].astype(BF16)
        poolw = _block_diag(pool_w[l]).astype(BF16)
        sgb = jnp.repeat(sg_b[l].T, SG_HD, axis=1)

        ya, yb, yc, q, k, v = _mixer_in(
            x, mod[l], row(mix_norm_g[l]), wmix, wfc, bfc, poolw, row(pool_scale[l]),
            row(sg_ln_g[l]), row(sg_ln_b[l]), sg_w[l], sgb, conv_w[l], row(conv_b[l]),
            row(conv_ln_g[l]), row(conv_ln_b[l]), tb=tb)
        yd = _attention(q, k, v, tq=tq)
        x = _merge(x, mod[l], row(mix_norm_g[l]), wgate, w_branch[l].astype(BF16), w_out[l].astype(BF16),
                   ya, yb, yc, yd, tb=tb)
        fnorm = row(ffn_norm_g[l])
        ls, lst, wt, blk_cnt, blk_carry, cnt = _route_call(
            x, mod[l], fnorm, router_w[l].T, router_bias[l].reshape(-1, 1), tb=tb)
        meta, base = _route_fin(cnt, tm=tm, meta_lanes=meta_lanes)
        xs = _dispatch(cnt[:, 0], base[:, 0], blk_cnt, blk_carry, ls, x, mod[l], fnorm,
                       tb=tb, tm=tm, n_slots=n_tiles * tm)
        ys = _experts(meta[0, :n_tiles], meta[0, meta_lanes - 1:], xs, exp_w1[l], exp_w3[l], exp_w2[l], tm=tm)
        x = _combine(base[:, 0], blk_cnt, blk_carry, x, mod[l], fnorm, lst, wt, shared_w1[l].astype(BF16),
                     shared_w3[l].astype(BF16), shared_w2[l].astype(BF16), row(final_norm_g), ys,
                     tb=tb, final_norm=(l == L - 1))
    return x
```

```python
import functools
import math

import jax
import jax.numpy as jnp
from jax import lax
from jax.experimental import pallas as pl
from jax.experimental.pallas import tpu as pltpu

F32 = jnp.float32
BF16 = jnp.bfloat16
HIGHEST = lax.Precision.HIGHEST

D_MODEL = 1024
BRANCH_W = 256
N_BRANCH = 4
POOL_WINDOWS = (2, 4, 8, 16)
POOL_GW = 64
POOL_TAIL = 16
SG_HEADS = 4
SG_CHUNK = 128
SG_HD = 64
CONV_WIDTH = 31
CONV_TAIL = 32
ATT_HEADS = 4
ATT_HD = 64
N_EXPERTS = 64
TOP_K = 6
N_GROUPS = 8
GROUP_SIZE = N_EXPERTS // N_GROUPS
TOPK_GROUPS = 4
EXPERT_FF = 256
SHARED_FF = 256
ROUTE_SCALE = 2.5
EPS = 1e-6
N_MOD = 6
MIX_COLS = 8 * BRANCH_W
LANES = 128
SUBLANES = 8
LOG2E = 1.4426950408889634
EXPERT_TILE = 256
VMEM_LIMIT = 56 * 1024 * 1024


def _sigmoid(x):
    return 1.0 / (1.0 + jnp.exp(-x))


def _silu(x):
    return x * _sigmoid(x)


def _rms_mod(x, g, scale, shift):
    y = x * lax.rsqrt(jnp.mean(x * x, axis=-1, keepdims=True) + EPS)
    return (y * g) * (1.0 + scale) + shift


def _layer_norm(x, g, b):
    mu = jnp.mean(x, axis=-1, keepdims=True)
    xc = x - mu
    var = jnp.mean(xc * xc, axis=-1, keepdims=True)
    return xc * lax.rsqrt(var + EPS) * g + b


def _params(*sem):
    return pltpu.CompilerParams(dimension_semantics=sem, vmem_limit_bytes=VMEM_LIMIT)


def _mod_kernel(c_ref, w_ref, b_ref, o_ref):
    c = c_ref[...]
    o_ref[0] = jnp.dot(_silu(c), w_ref[0], precision=HIGHEST, preferred_element_type=F32) + b_ref[0]


def _modulation(c, ada_w, ada_b):
    L, D, N = ada_w.shape
    B = c.shape[0]
    tn = 1536
    out = pl.pallas_call(
        _mod_kernel,
        grid=(L, N // tn),
        in_specs=[pl.BlockSpec((B, D), lambda l, j: (0, 0)),
                  pl.BlockSpec((1, D, tn), lambda l, j: (l, 0, j)),
                  pl.BlockSpec((1, 1, tn), lambda l, j: (l, 0, j))],
        out_specs=pl.BlockSpec((1, B, tn), lambda l, j: (l, 0, j)),
        out_shape=jax.ShapeDtypeStruct((L, B, N), F32),
        compiler_params=_params("arbitrary", "arbitrary"),
        name="modulation",
    )(c, ada_w, ada_b.reshape(L, 1, N))
    return out.reshape(L, B, N_MOD, D)


def _mixer_in_kernel(x_ref, mod_ref, g_ref, wmix_ref, wfc_ref, bfc_ref,
                     poolw_ref, pools_ref, sglg_ref, sglb_ref, sgw_ref, sgb_ref,
                     cw_ref, cb_ref, clg_ref, clb_ref,
                     ya_ref, yb_ref, yc_ref, q_ref, k_ref, v_ref,
                     pool_ext, conv_ext, cum_c, *, tb):
    j = pl.program_id(1)

    @pl.when(j == 0)
    def _():
        pool_ext[0:POOL_TAIL, :] = jnp.zeros((POOL_TAIL, BRANCH_W), F32)
        conv_ext[0:CONV_TAIL, :] = jnp.zeros((CONV_TAIL, BRANCH_W), F32)
        cum_c[...] = jnp.zeros_like(cum_c)

    mod = mod_ref[0]
    h = _rms_mod(x_ref[0], g_ref[...], mod[1:2, :], mod[0:1, :])
    hb = h.astype(BF16)
    proj = jnp.dot(hb, wmix_ref[...], preferred_element_type=F32)

    lane = lax.broadcasted_iota(jnp.int32, (1, BRANCH_W), 1)
    row = lax.broadcasted_iota(jnp.int32, (tb, 1), 0)

    u = proj[:, 0:BRANCH_W]
    pool_ext[POOL_TAIL:POOL_TAIL + tb, :] = u
    ext = pool_ext[...]
    s2 = ext + pltpu.roll(ext, 1, 0)
    s4 = s2 + pltpu.roll(s2, 2, 0)
    s8 = s4 + pltpu.roll(s4, 4, 0)
    s16 = s8 + pltpu.roll(s8, 8, 0)
    grp = lane // POOL_GW
    wsum = jnp.where(grp == 0, s2, jnp.where(grp == 1, s4, jnp.where(grp == 2, s8, s16)))[POOL_TAIL:, :]
    win = jnp.where(grp == 0, 2.0, jnp.where(grp == 1, 4.0, jnp.where(grp == 2, 8.0, 16.0)))
    count = jnp.minimum((j * tb + row + 1).astype(F32), win)
    pooled = wsum / count - u
    ya = jnp.dot(pooled.astype(BF16), poolw_ref[...], preferred_element_type=F32) * pools_ref[...]
    ya_ref[0] = ya.astype(BF16)
    pool_ext[0:POOL_TAIL, :] = u[tb - POOL_TAIL:, :]

    z = proj[:, BRANCH_W:3 * BRANCH_W]
    z = 0.5 * z * (1.0 + jnp.tanh(math.sqrt(2.0 / math.pi) * (z + 0.044715 * (z * z * z))))
    su = z[:, 0:BRANCH_W]
    sv = _layer_norm(z[:, BRANCH_W:], sglg_ref[...], sglb_ref[...])
    r128 = lax.broadcasted_iota(jnp.int32, (SG_CHUNK, SG_CHUNK), 0)
    c128 = lax.broadcasted_iota(jnp.int32, (SG_CHUNK, SG_CHUNK), 1)
    wcat = jnp.concatenate(
        [jnp.where(r128 >= c128, sgw_ref[hh], 0.0) for hh in range(SG_HEADS)], axis=1).astype(BF16)
    head = lane // SG_HD
    for ci in range(tb // SG_CHUNK):
        rows = slice(ci * SG_CHUNK, (ci + 1) * SG_CHUNK)
        vch = sv[rows, :]
        vstack = jnp.concatenate(
            [jnp.where(head == hh, vch, 0.0) for hh in range(SG_HEADS)], axis=0).astype(BF16)
        s = jnp.dot(wcat, vstack, preferred_element_type=F32) + sgb_ref[...]
        yb_ref[0, rows, :] = (su[rows, :] * s).astype(BF16)

    glu = proj[:, 3 * BRANCH_W:4 * BRANCH_W] * _sigmoid(proj[:, 4 * BRANCH_W:5 * BRANCH_W])
    conv_ext[CONV_TAIL:CONV_TAIL + tb, :] = glu
    acc = jnp.zeros((tb, BRANCH_W), F32) + cb_ref[...]
    for kk in range(CONV_WIDTH):
        off = CONV_TAIL - (CONV_WIDTH - 1) + kk
        acc = acc + cw_ref[kk:kk + 1, :] * conv_ext[off:off + tb, :]
    yc_ref[0] = _silu(_layer_norm(acc, clg_ref[...], clb_ref[...])).astype(BF16)
    conv_ext[0:CONV_TAIL, :] = glu[tb - CONV_TAIL:, :]

    def log_sigmoid(t):
        return jnp.minimum(t, 0.0) - jnp.log(1.0 + jnp.exp(-jnp.abs(t)))

    rr = lax.broadcasted_iota(jnp.int32, (tb, tb), 0)
    cc = lax.broadcasted_iota(jnp.int32, (tb, tb), 1)
    lower = jnp.where(rr >= cc, 1.0, 0.0).astype(F32)
    lf_c = log_sigmoid(jnp.dot(hb, wfc_ref[...], preferred_element_type=F32) + bfc_ref[...])
    cs_c = jnp.dot(lower, lf_c, precision=HIGHEST, preferred_element_type=F32) + cum_c[...]
    cum_c[...] = cs_c[tb - 1:tb, :]
    cl2 = cs_c * LOG2E
    ln = lax.broadcasted_iota(jnp.int32, (1, LANES), 1)
    ones_q = jnp.where((ln >= ATT_HD + 3) & (ln < ATT_HD + 6), 1.0, 0.0)
    ones_k = jnp.where((ln >= ATT_HD) & (ln < ATT_HD + 3), 1.0, 0.0)
    ones_v = jnp.where(ln == ATT_HD, 1.0, 0.0)
    for hh in range(ATT_HEADS):
        pair = (hh // 2) * LANES
        qs = proj[:, 5 * BRANCH_W + pair:5 * BRANCH_W + pair + LANES] * (LOG2E / math.sqrt(ATT_HD))
        ks = proj[:, 6 * BRANCH_W + pair:6 * BRANCH_W + pair + LANES]
        vs = proj[:, 7 * BRANCH_W + pair:7 * BRANCH_W + pair + LANES]
        if hh % 2:
            qs, ks, vs = (pltpu.roll(a, ATT_HD, 1) for a in (qs, ks, vs))
        c = cl2[:, hh:hh + 1]
        hi = c.astype(BF16).astype(F32)
        mid = (c - hi).astype(BF16).astype(F32)
        lo = (c - hi) - mid
        q_extra = jnp.where(ln == ATT_HD, hi, jnp.where(ln == ATT_HD + 1, mid, jnp.where(ln == ATT_HD + 2, lo, ones_q)))
        k_extra = jnp.where(ln == ATT_HD + 3, -hi,
                            jnp.where(ln == ATT_HD + 4, -mid, jnp.where(ln == ATT_HD + 5, -lo, ones_k)))
        q_ref[0, hh, 0] = jnp.where(ln < ATT_HD, qs, q_extra).T.astype(BF16)
        k_ref[0, hh] = jnp.where(ln < ATT_HD, ks, k_extra).astype(BF16)
        v_ref[0, hh, 0] = jnp.where(ln < ATT_HD, vs, ones_v).T.astype(BF16)


def _mixer_in(x, mod_l, norm_g, wmix, wfc, bfc, poolw, pools, sglg, sglb, sgw, sgb, cw, cb, clg, clb, *, tb):
    B, S, D = x.shape
    full = lambda a: pl.BlockSpec(a.shape, lambda b, j: (0,) * a.ndim)
    tok = lambda w: pl.BlockSpec((1, tb, w), lambda b, j: (b, j, 0))
    head = pl.BlockSpec((1, ATT_HEADS, tb, LANES), lambda b, j: (b, 0, j, 0))
    head_t = pl.BlockSpec((1, ATT_HEADS, 1, LANES, tb), lambda b, j: (b, 0, j, 0, 0))
    consts = (norm_g, wmix, wfc, bfc, poolw, pools, sglg, sglb, sgw, sgb, cw, cb, clg, clb)
    act = jax.ShapeDtypeStruct((B, S, BRANCH_W), BF16)
    att = jax.ShapeDtypeStruct((B, ATT_HEADS, S, LANES), BF16)
    att_t = jax.ShapeDtypeStruct((B, ATT_HEADS, S // tb, LANES, tb), BF16)
    return pl.pallas_call(
        functools.partial(_mixer_in_kernel, tb=tb),
        grid=(B, S // tb),
        in_specs=[tok(D), pl.BlockSpec((1, N_MOD, D), lambda b, j: (b, 0, 0))] + [full(a) for a in consts],
        out_specs=[tok(BRANCH_W)] * 3 + [head_t, head, head_t],
        out_shape=[act] * 3 + [att_t, att, att_t],
        scratch_shapes=[pltpu.VMEM((POOL_TAIL + tb, BRANCH_W), F32),
                        pltpu.VMEM((CONV_TAIL + tb, BRANCH_W), F32),
                        pltpu.VMEM((1, LANES), F32)],
        compiler_params=_params("arbitrary", "arbitrary"),
        name="mixer_in",
    )(x, mod_l, *consts)


def _attn_kernel(q_ref, k_ref, v_ref, o_ref, m_ref, acc_ref, *, tq):
    i = pl.program_id(1)
    key = lax.broadcasted_iota(jnp.int32, (tq, tq), 0)
    qry = lax.broadcasted_iota(jnp.int32, (tq, tq), 1)
    m_ref[...] = jnp.full(m_ref.shape, -jnp.inf, F32)
    acc_ref[...] = jnp.zeros(acc_ref.shape, F32)

    def block(kj, diagonal):
        ks = pl.multiple_of(kj * tq, tq)
        logits = [jnp.dot(k_ref[0, hh, pl.ds(ks, tq), :], q_ref[0, hh, 0], preferred_element_type=F32)
                  for hh in range(ATT_HEADS)]
        for hh in range(ATT_HEADS):
            s = logits[hh]
            if diagonal:
                s = jnp.where(key <= qry, s, -jnp.inf)
            m_old = m_ref[hh]
            m_new = jnp.maximum(m_old, jnp.max(s, axis=0, keepdims=True))
            p = jnp.exp2(s - m_new)
            pv = jnp.dot(v_ref[0, hh, kj], p.astype(BF16), preferred_element_type=F32)
            acc_ref[hh] = jnp.exp2(m_old - m_new) * acc_ref[hh] + pv
            m_ref[hh] = m_new

    def body(kj, carry):
        block(kj, False)
        return carry

    lax.fori_loop(0, i, body, 0)
    block(i, True)

    ln = lax.broadcasted_iota(jnp.int32, (1, LANES), 1)
    for pr in range(ATT_HEADS // 2):
        o = []
        for hh in (2 * pr, 2 * pr + 1):
            acc = acc_ref[hh]
            o.append((acc / acc[ATT_HD:ATT_HD + 1, :]).T)
        o_ref[0, :, pr * LANES:(pr + 1) * LANES] = jnp.where(ln < ATT_HD, o[0], pltpu.roll(o[1], ATT_HD, 1)).astype(BF16)


def _attention(q_t, k, v_t, *, tq):
    B, H, S, W = k.shape
    nblk = S // tq
    assert q_t.shape == (B, H, nblk, W, tq) and v_t.shape == q_t.shape
    return pl.pallas_call(
        functools.partial(_attn_kernel, tq=tq),
        grid=(B, nblk),
        in_specs=[pl.BlockSpec((1, H, 1, W, tq), lambda b, i: (b, 0, i, 0, 0)),
                  pl.BlockSpec((1, H, S, W), lambda b, i: (b, 0, 0, 0), pipeline_mode=pl.Buffered(1)),
                  pl.BlockSpec((1, H, nblk, W, tq), lambda b, i: (b, 0, 0, 0, 0), pipeline_mode=pl.Buffered(1))],
        out_specs=pl.BlockSpec((1, tq, BRANCH_W), lambda b, i: (b, i, 0)),
        out_shape=jax.ShapeDtypeStruct((B, S, BRANCH_W), BF16),
        scratch_shapes=[pltpu.VMEM((H, 1, tq), F32), pltpu.VMEM((H, W, tq), F32)],
        compiler_params=_params("arbitrary", "arbitrary"),
        name="attention",
    )(q_t, k, v_t)


def _merge_kernel(x_ref, mod_ref, g_ref, wg_ref, wbr_ref, wout_ref, ya_ref, yb_ref, yc_ref, yd_ref, o_ref):
    x = x_ref[0]
    mod = mod_ref[0]
    hb = _rms_mod(x, g_ref[...], mod[1:2, :], mod[0:1, :]).astype(BF16)
    merged = jnp.zeros(x.shape, F32)
    for n, y_ref in enumerate((ya_ref, yb_ref, yc_ref, yd_ref)):
        gate = _sigmoid(jnp.dot(hb, wg_ref[:, n * D_MODEL:(n + 1) * D_MODEL], preferred_element_type=F32))
        merged = merged + gate * jnp.dot(y_ref[0], wbr_ref[n], preferred_element_type=F32)
    out = jnp.dot(merged.astype(BF16), wout_ref[...], preferred_element_type=F32)
    o_ref[0] = x + mod[2:3, :] * out


def _merge(x, mod_l, norm_g, wgate, wbranch, wout, ya, yb, yc, yd, *, tb):
    B, S, D = x.shape
    full = lambda a: pl.BlockSpec(a.shape, lambda b, j: (0,) * a.ndim)
    tok = lambda w: pl.BlockSpec((1, tb, w), lambda b, j: (b, j, 0))
    return pl.pallas_call(
        _merge_kernel,
        grid=(B, S // tb),
        in_specs=[tok(D), pl.BlockSpec((1, N_MOD, D), lambda b, j: (b, 0, 0)),
                  full(norm_g), full(wgate), full(wbranch), full(wout)] + [tok(BRANCH_W)] * 4,
        out_specs=tok(D),
        out_shape=jax.ShapeDtypeStruct((B, S, D), F32),
        compiler_params=_params("arbitrary", "arbitrary"),
        name="merge",
    )(x, mod_l, norm_g, wgate, wbranch, wout, ya, yb, yc, yd)


def _route(scores_t, bias_t):
    E, n = scores_t.shape
    sel = scores_t + bias_t
    eidx = lax.broadcasted_iota(jnp.int32, (E, n), 0)
    neg = jnp.full((E, n), -jnp.inf, F32)
    gscore = []
    sub = lax.broadcasted_iota(jnp.int32, (GROUP_SIZE, n), 0)
    for g in range(N_GROUPS):
        blk = sel[g * GROUP_SIZE:(g + 1) * GROUP_SIZE, :]
        m1 = jnp.max(blk, axis=0, keepdims=True)
        first = jnp.min(jnp.where(blk == m1, sub, GROUP_SIZE), axis=0, keepdims=True)
        m2 = jnp.max(jnp.where(sub == first, -jnp.inf, blk), axis=0, keepdims=True)
        gscore.append(m1 + m2)
    emask = []
    for g in range(N_GROUPS):
        beaten = jnp.zeros((1, n), jnp.int32)
        for g2 in range(N_GROUPS):
            if g2 == g:
                continue
            wins = (gscore[g2] > gscore[g]) | ((gscore[g2] == gscore[g]) & (g2 < g))
            beaten = beaten + wins.astype(jnp.int32)
        emask.append(jnp.broadcast_to(beaten < TOPK_GROUPS, (GROUP_SIZE, n)))
    cur = jnp.where(jnp.concatenate(emask, axis=0), sel, neg)
    chosen = jnp.zeros((E, n), jnp.bool_)
    firsts = []
    for _ in range(TOP_K):
        m = jnp.max(cur, axis=0, keepdims=True)
        first = jnp.min(jnp.where(cur == m, eidx, E), axis=0, keepdims=True)
        hit = eidx == first
        chosen = chosen | hit
        cur = jnp.where(hit, neg, cur)
        firsts.append(first)
    w = jnp.where(chosen, scores_t, 0.0)
    return w / jnp.sum(w, axis=0, keepdims=True) * ROUTE_SCALE, chosen, firsts


def _route_kernel(x_ref, mod_ref, g_ref, rwt_ref, rb_ref, ls_ref, lst_ref, wt_ref, bc_ref, cr_ref, cnt_ref, count):
    first_step = (pl.program_id(0) == 0) & (pl.program_id(1) == 0)

    @pl.when(first_step)
    def _():
        count[...] = jnp.zeros_like(count)

    mod = mod_ref[0]
    h = _rms_mod(x_ref[0], g_ref[...], mod[4:5, :], mod[3:4, :])
    logits_t = lax.dot_general(rwt_ref[...], h, (((1,), (1,)), ((), ())),
                               precision=HIGHEST, preferred_element_type=F32)
    gates_t, chosen, firsts = _route(_sigmoid(logits_t), rb_ref[...])
    E, tb = gates_t.shape
    ones = jnp.where(chosen, 1.0, 0.0)
    rr = lax.broadcasted_iota(jnp.int32, (tb, tb), 0)
    cc = lax.broadcasted_iota(jnp.int32, (tb, tb), 1)
    upper = jnp.where(rr <= cc, 1.0, 0.0).astype(BF16)
    incl = jnp.dot(ones.astype(BF16), upper, preferred_element_type=F32)
    block_count = jnp.sum(ones, axis=1, keepdims=True)
    block_count = block_count + (block_count.astype(jnp.int32) & 1).astype(F32)
    er = lax.broadcasted_iota(jnp.int32, (E, E), 0)
    ec = lax.broadcasted_iota(jnp.int32, (E, E), 1)
    before = jnp.where(ec < er, 1.0, 0.0).astype(F32)
    local_base = jnp.dot(before, jnp.broadcast_to(block_count, (E, LANES)), precision=HIGHEST,
                         preferred_element_type=F32)[:, 0:1]
    row_all = local_base + (incl - ones)
    eidx = lax.broadcasted_iota(jnp.int32, (E, tb), 0)
    ls_rows = [jnp.sum(jnp.where(eidx == f, row_all, 0.0), axis=0, keepdims=True) for f in firsts]
    w_rows = [jnp.sum(jnp.where(eidx == f, gates_t, 0.0), axis=0, keepdims=True) for f in firsts]
    ls_ref[...] = jnp.concatenate(ls_rows + [jnp.zeros((SUBLANES - TOP_K, tb), F32)], axis=0).astype(jnp.int32)
    zpad = jnp.zeros((LANES - TOP_K, tb), F32)
    lst_ref[0] = jnp.concatenate(ls_rows + [zpad], axis=0).T.astype(jnp.int32)
    wt_ref[0] = jnp.concatenate(w_rows + [zpad], axis=0).T

    def as_row(col):
        sel = lax.broadcasted_iota(jnp.int32, (E, LANES), 0) == lax.broadcasted_iota(jnp.int32, (E, LANES), 1)
        return jnp.sum(jnp.where(sel, col, 0.0), axis=0, keepdims=True).astype(jnp.int32)

    bc_ref[0] = as_row(block_count)
    cr_ref[0] = as_row(count[...])
    count[...] = count[...] + block_count
    cnt_ref[...] = jnp.broadcast_to(count[...], cnt_ref.shape).astype(jnp.int32)


def _route_call(x, mod_l, norm_g, router_wt, router_b, *, tb):
    B, S, D = x.shape
    nj = S // tb
    full = lambda a: pl.BlockSpec(a.shape, lambda b, j: (0,) * a.ndim)
    tok = lambda w: pl.BlockSpec((1, tb, w), lambda b, j: (b, j, 0))
    kt = pl.BlockSpec((SUBLANES, tb), lambda b, j: (0, b * nj + j))
    per_block = pl.BlockSpec((1, 1, LANES), lambda b, j: (b * nj + j, 0, 0))
    return pl.pallas_call(
        _route_kernel,
        grid=(B, nj),
        in_specs=[tok(D), pl.BlockSpec((1, N_MOD, D), lambda b, j: (b, 0, 0)),
                  full(norm_g), full(router_wt), full(router_b)],
        out_specs=[kt, tok(LANES), tok(LANES), per_block, per_block,
                   pl.BlockSpec((N_EXPERTS, LANES), lambda b, j: (0, 0))],
        out_shape=[jax.ShapeDtypeStruct((SUBLANES, B * S), jnp.int32),
                   jax.ShapeDtypeStruct((B, S, LANES), jnp.int32),
                   jax.ShapeDtypeStruct((B, S, LANES), F32),
                   jax.ShapeDtypeStruct((B * nj, 1, LANES), jnp.int32),
                   jax.ShapeDtypeStruct((B * nj, 1, LANES), jnp.int32),
                   jax.ShapeDtypeStruct((N_EXPERTS, LANES), jnp.int32)],
        scratch_shapes=[pltpu.VMEM((N_EXPERTS, 1), F32)],
        compiler_params=_params("arbitrary", "arbitrary"),
        name="route",
    )(x, mod_l, norm_g, router_wt, router_b)


def _route_fin_kernel(cnt_ref, meta_ref, base_ref, *, tm):
    cnt = cnt_ref[...]
    ntile = lax.shift_right_logical(cnt + (tm - 1), int(math.log2(tm)))
    er = lax.broadcasted_iota(jnp.int32, (N_EXPERTS, N_EXPERTS), 0)
    ec = lax.broadcasted_iota(jnp.int32, (N_EXPERTS, N_EXPERTS), 1)
    before = jnp.where(ec < er, 1.0, 0.0).astype(F32)
    start = jnp.dot(before, ntile.astype(F32), precision=HIGHEST, preferred_element_type=F32).astype(jnp.int32)
    base = start * tm
    base_ref[...] = base
    nlane = meta_ref.shape[1]
    lane = lax.broadcasted_iota(jnp.int32, (1, nlane), 1)
    end = (start + ntile)[:, 0:1]
    tile_expert = jnp.sum(jnp.where(end <= lane, 1, 0), axis=0, keepdims=True)
    tile_expert = jnp.minimum(tile_expert, N_EXPERTS - 1)
    meta_ref[...] = jnp.where(lane == nlane - 1, end[N_EXPERTS - 1:N_EXPERTS, :], tile_expert)


def _route_fin(cnt, *, tm, meta_lanes):
    full = lambda a: pl.BlockSpec(a.shape, lambda i: (0,) * a.ndim)
    outs = [jax.ShapeDtypeStruct((1, meta_lanes), jnp.int32), jax.ShapeDtypeStruct(cnt.shape, jnp.int32)]
    return pl.pallas_call(
        functools.partial(_route_fin_kernel, tm=tm),
        grid=(1,),
        in_specs=[full(cnt)],
        out_specs=[full(o) for o in outs],
        out_shape=outs,
        compiler_params=_params("arbitrary"),
        name="route_fin",
    )(cnt)


HALF = D_MODEL // 2
SLAB = HALF // LANES
HIGH_BITS = 0xFFFF0000


def _to_slabs(ref, value):
    n = value.shape[0]
    bits = lax.bitcast_convert_type(value.astype(BF16).astype(F32), jnp.uint32)
    packed = (bits[:, HALF:] & jnp.uint32(HIGH_BITS)) | (bits[:, :HALF] >> 16)
    for s in range(SLAB):
        ref[pl.ds(s, n, stride=SLAB), :] = packed[:, s * LANES:(s + 1) * LANES]


def _from_slabs(ref, n):
    packed = jnp.concatenate([ref[pl.ds(s, n, stride=SLAB), :] for s in range(SLAB)], axis=1)
    low = lax.bitcast_convert_type(packed << 16, F32).astype(BF16)
    high = lax.bitcast_convert_type(packed & jnp.uint32(HIGH_BITS), F32).astype(BF16)
    return low, high


def _for_expert_runs(base_sm, bc_sm, cr_sm, max_run, fn):
    def per_expert(e, local_row):
        n = bc_sm[0, 0, e]
        _for_pieces(n, max_run, lambda done, size: fn(local_row + done, base_sm[e] + cr_sm[0, 0, e] + done, size))
        return local_row + n

    lax.fori_loop(0, N_EXPERTS, per_expert, 0)


def _for_pieces(n, max_piece, fn):
    done = 0
    bit = max_piece
    while bit >= 2:
        take = n & bit

        @pl.when(take != 0)
        def _(done=done, bit=bit):
            fn(done, bit)

        done = done + take
        bit //= 2


ROW_GROUPS = 7


def _local_rows(tb):
    rows = tb * TOP_K + N_EXPERTS
    assert rows % (ROW_GROUPS * 2 * SUBLANES) == 0, rows
    return rows


def _queue(nrows):
    return int(math.log2(nrows)) % 2


def _slab_rows(ref, row, nrows):
    return ref.at[pl.ds(pl.multiple_of(row * SLAB, 2 * SLAB), nrows * SLAB)]


def _dispatch_kernel(cnt_sm, base_sm, bc_sm, cr_sm, bc_prev, cr_prev, ls_ref, x_ref, mod_ref, g_ref, xs_hbm,
                     stage, zeros, sem, zsem, *, tb, tm):
    step = pl.program_id(0) * pl.num_programs(1) + pl.program_id(1)
    last_step = pl.num_programs(0) * pl.num_programs(1) - 1
    cur = lax.rem(step, 2)
    mod = mod_ref[0]
    hb = _rms_mod(x_ref[0], g_ref[...], mod[4:5, :], mod[3:4, :]).astype(BF16)
    ls = ls_ref[...]
    gr = _local_rows(tb) // ROW_GROUPS
    for grp in range(ROW_GROUPS):
        row = grp * gr + lax.broadcasted_iota(jnp.int32, (gr, tb), 0)
        hit = ls[0:1, :] == row
        for k in range(1, TOP_K):
            hit = hit | (ls[k:k + 1, :] == row)
        perm = jnp.where(hit, 1.0, 0.0).astype(BF16)
        _to_slabs(stage.at[cur, pl.ds(grp * gr * SLAB, gr * SLAB)], jnp.dot(perm, hb, preferred_element_type=F32))

    def run_copy(buf):
        return lambda local_row, sorted_row, nrows: pltpu.make_async_copy(
            _slab_rows(stage.at[buf], local_row, nrows), _slab_rows(xs_hbm, sorted_row, nrows), sem.at[buf])

    _for_expert_runs(base_sm, bc_sm, cr_sm, tb, lambda *a: run_copy(cur)(*a).start(priority=_queue(a[2])))

    @pl.when(step == 0)
    def _():
        zeros[...] = jnp.zeros_like(zeros)

        def per_expert(e, carry):
            n = cnt_sm[e]
            npad = lax.rem(tm - lax.rem(n, tm), tm)
            zcopy = lambda done, size: pltpu.make_async_copy(
                _slab_rows(zeros, 0, size), _slab_rows(xs_hbm, base_sm[e] + n + done, size), zsem)
            _for_pieces(npad, tm // 2, lambda *a: zcopy(*a).start())
            _for_pieces(npad, tm // 2, lambda *a: zcopy(*a).wait())
            return carry

        lax.fori_loop(0, N_EXPERTS, per_expert, 0)

    @pl.when(step > 0)
    def _():
        _for_expert_runs(base_sm, bc_prev, cr_prev, tb, lambda *a: run_copy(1 - cur)(*a).wait())

    @pl.when(step == last_step)
    def _():
        _for_expert_runs(base_sm, bc_sm, cr_sm, tb, lambda *a: run_copy(cur)(*a).wait())


def _dispatch(cnt, base, blk_cnt, blk_carry, ls, x, mod_l, norm_g, *, tb, tm, n_slots):
    B, S, D = x.shape
    nj = S // tb
    smem_block = lambda shift: pl.BlockSpec(
        (1, 1, LANES), lambda b, j, *_: (jnp.maximum(b * nj + j + shift, 0), 0, 0), memory_space=pltpu.SMEM)
    grid_spec = pltpu.PrefetchScalarGridSpec(
        num_scalar_prefetch=2,
        grid=(B, nj),
        in_specs=[smem_block(0), smem_block(0), smem_block(-1), smem_block(-1),
                  pl.BlockSpec((SUBLANES, tb), lambda b, j, *_: (0, b * nj + j)),
                  pl.BlockSpec((1, tb, D), lambda b, j, *_: (b, j, 0)),
                  pl.BlockSpec((1, N_MOD, D), lambda b, j, *_: (b, 0, 0)),
                  pl.BlockSpec(norm_g.shape, lambda b, j, *_: (0, 0))],
        out_specs=pl.BlockSpec(memory_space=pltpu.HBM),
        scratch_shapes=[pltpu.VMEM((2, _local_rows(tb) * SLAB, LANES), jnp.uint32),
                        pltpu.VMEM((tm // 2 * SLAB, LANES), jnp.uint32),
                        pltpu.SemaphoreType.DMA((2,)), pltpu.SemaphoreType.DMA])
    return pl.pallas_call(
        functools.partial(_dispatch_kernel, tb=tb, tm=tm),
        grid_spec=grid_spec,
        out_shape=jax.ShapeDtypeStruct((n_slots * SLAB, LANES), jnp.uint32),
        compiler_params=_params("arbitrary", "arbitrary"),
        name="dispatch",
    )(cnt, base, blk_cnt, blk_carry, blk_cnt, blk_carry, ls, x, mod_l, norm_g)


def _experts_kernel(te_sm, nt_sm, x_ref, w1_ref, w3_ref, w2_ref, y_ref, w1b, w3b, w2b, *, tm):
    i = pl.program_id(0)
    in_use = i < nt_sm[0]

    @pl.when(in_use)
    def _():
        @pl.when((i == 0) | (te_sm[i] != te_sm[jnp.maximum(i - 1, 0)]))
        def _():
            w1b[...] = w1_ref[0].astype(BF16)
            w3b[...] = w3_ref[0].astype(BF16)
            w2b[...] = w2_ref[0].astype(BF16)

        x_lo, x_hi = _from_slabs(x_ref, tm)
        up = lambda w: (jnp.dot(x_lo, w[:HALF, :], preferred_element_type=F32)
                        + jnp.dot(x_hi, w[HALF:, :], preferred_element_type=F32))
        hid = _silu(up(w1b)) * up(w3b)
        _to_slabs(y_ref, jnp.dot(hid.astype(BF16), w2b[...], preferred_element_type=F32))

    @pl.when(jnp.logical_not(in_use))
    def _():
        y_ref[...] = jnp.zeros_like(y_ref)


def _experts(tile_expert, n_tiles, xs, w1, w3, w2, *, tm):
    E, D, F = w1.shape
    n_slots = xs.shape[0] // SLAB
    last = lambda i, te, nt: jnp.minimum(i, nt[0] - 1)
    grid_spec = pltpu.PrefetchScalarGridSpec(
        num_scalar_prefetch=2,
        grid=(n_slots // tm,),
        in_specs=[pl.BlockSpec((tm * SLAB, LANES), lambda i, te, nt: (last(i, te, nt), 0)),
                  pl.BlockSpec((1, D, F), lambda i, te, nt: (te[last(i, te, nt)], 0, 0)),
                  pl.BlockSpec((1, D, F), lambda i, te, nt: (te[last(i, te, nt)], 0, 0)),
                  pl.BlockSpec((1, F, D), lambda i, te, nt: (te[last(i, te, nt)], 0, 0))],
        out_specs=pl.BlockSpec((tm * SLAB, LANES), lambda i, te, nt: (i, 0)),
        scratch_shapes=[pltpu.VMEM((D, F), BF16), pltpu.VMEM((D, F), BF16), pltpu.VMEM((F, D), BF16)])
    return pl.pallas_call(
        functools.partial(_experts_kernel, tm=tm),
        grid_spec=grid_spec,
        out_shape=jax.ShapeDtypeStruct(xs.shape, jnp.uint32),
        compiler_params=_params("arbitrary"),
        name="experts",
    )(tile_expert, n_tiles, xs, w1, w3, w2)


def _combine_kernel(base_sm, bc_sm, cr_sm, bc_next, cr_next, x_ref, mod_ref, g_ref, lst_ref, wt_ref,
                    sw1_ref, sw3_ref, sw2_ref, fg_ref, ys_hbm, o_ref, stage, sem, *, tb, final_norm):
    step = pl.program_id(0) * pl.num_programs(1) + pl.program_id(1)
    last_step = pl.num_programs(0) * pl.num_programs(1) - 1
    cur = lax.rem(step, 2)

    def run_copy(buf):
        return lambda local_row, sorted_row, nrows: pltpu.make_async_copy(
            _slab_rows(ys_hbm, sorted_row, nrows), _slab_rows(stage.at[buf], local_row, nrows), sem.at[buf])

    @pl.when(step == 0)
    def _():
        stage[...] = jnp.zeros_like(stage)
        _for_expert_runs(base_sm, bc_sm, cr_sm, tb, lambda *a: run_copy(cur)(*a).start(priority=_queue(a[2])))

    @pl.when(step < last_step)
    def _():
        _for_expert_runs(base_sm, bc_next, cr_next, tb, lambda *a: run_copy(1 - cur)(*a).start(priority=_queue(a[2])))

    x = x_ref[0]
    mod = mod_ref[0]
    hb = _rms_mod(x, g_ref[...], mod[4:5, :], mod[3:4, :]).astype(BF16)
    a = jnp.dot(hb, sw1_ref[...], preferred_element_type=F32)
    b = jnp.dot(hb, sw3_ref[...], preferred_element_type=F32)
    acc = jnp.dot((_silu(a) * b).astype(BF16), sw2_ref[...], preferred_element_type=F32)
    wt = wt_ref[0]
    lst = lst_ref[0]
    _for_expert_runs(base_sm, bc_sm, cr_sm, tb, lambda *a: run_copy(cur)(*a).wait())
    gr = _local_rows(tb) // ROW_GROUPS
    routed = [jnp.zeros((tb, HALF), F32)] * 2
    for grp in range(ROW_GROUPS):
        halves = _from_slabs(stage.at[cur, pl.ds(grp * gr * SLAB, gr * SLAB)], gr)
        row = grp * gr + lax.broadcasted_iota(jnp.int32, (1, gr), 1)
        gate = jnp.zeros((tb, gr), F32)
        for k in range(TOP_K):
            gate = gate + jnp.where(lst[:, k:k + 1] == row, wt[:, k:k + 1], 0.0)
        hi = gate.astype(BF16)
        lo = (gate - hi.astype(F32)).astype(BF16)
        routed = [r + jnp.dot(hi, yg, preferred_element_type=F32) + jnp.dot(lo, yg, preferred_element_type=F32)
                  for r, yg in zip(routed, halves)]
    acc = acc + jnp.concatenate(routed, axis=1)
    y = x + mod[5:6, :] * acc
    if final_norm:
        y = y * lax.rsqrt(jnp.mean(y * y, axis=-1, keepdims=True) + EPS) * fg_ref[...]
    o_ref[0] = y


def _combine(base, blk_cnt, blk_carry, x, mod_l, norm_g, lst, wt, sw1, sw3, sw2, final_g, ys, *, tb, final_norm):
    B, S, D = x.shape
    nj = S // tb
    full = lambda a: pl.BlockSpec(a.shape, lambda b, j, *_: (0,) * a.ndim)
    tok = lambda w: pl.BlockSpec((1, tb, w), lambda b, j, *_: (b, j, 0))
    smem_block = lambda shift: pl.BlockSpec(
        (1, 1, LANES), lambda b, j, *_: (jnp.minimum(b * nj + j + shift, B * nj - 1), 0, 0), memory_space=pltpu.SMEM)
    grid_spec = pltpu.PrefetchScalarGridSpec(
        num_scalar_prefetch=1,
        grid=(B, nj),
        in_specs=[smem_block(0), smem_block(0), smem_block(1), smem_block(1), tok(D),
                  pl.BlockSpec((1, N_MOD, D), lambda b, j, *_: (b, 0, 0)),
                  full(norm_g), tok(LANES), tok(LANES), full(sw1), full(sw3), full(sw2), full(final_g),
                  pl.BlockSpec(memory_space=pltpu.HBM)],
        out_specs=tok(D),
        scratch_shapes=[pltpu.VMEM((2, _local_rows(tb) * SLAB, LANES), jnp.uint32), pltpu.SemaphoreType.DMA((2,))])
    return pl.pallas_call(
        functools.partial(_combine_kernel, tb=tb, final_norm=final_norm),
        grid_spec=grid_spec,
        out_shape=jax.ShapeDtypeStruct((B, S, D), F32),
        compiler_params=_params("arbitrary", "arbitrary"),
        name="combine",
    )(base, blk_cnt, blk_carry, blk_cnt, blk_carry, x, mod_l, norm_g, lst, wt, sw1, sw3, sw2, final_g, ys)


def _block_diag(w):
    G, a, b = w.shape
    out = jnp.zeros((G * a, G * b), w.dtype)
    for g in range(G):
        out = out.at[g * a:(g + 1) * a, g * b:(g + 1) * b].set(w[g])
    return out


def kernel(x, c, w_in, b_f, pool_w, pool_scale, sg_ln_g, sg_ln_b, sg_w, sg_b, conv_w, conv_b, conv_ln_g,
           conv_ln_b, w_branch, w_out, mix_norm_g, ffn_norm_g, ada_w, ada_b, router_w, router_bias,
           exp_w1, exp_w3, exp_w2, shared_w1, shared_w3, shared_w2, final_norm_g):
    B, S, D = x.shape
    L = w_in.shape[0]
    tb = min(512, S)
    tq = tb
    tm = EXPERT_TILE
    n_tiles = -(-(B * (S // tb) * _local_rows(tb)) // tm) + N_EXPERTS
    meta_lanes = -(-(n_tiles + 1) // LANES) * LANES
    row = lambda a: a.reshape(1, -1)

    mod = _modulation(c, ada_w, ada_b)
    for l in range(L):
        wmix = w_in[l, :, :MIX_COLS].astype(BF16)
        wf = w_in[l, :, MIX_COLS:MIX_COLS + ATT_HEADS]
        wfc = jnp.pad(wf, ((0, 0), (0, LANES - ATT_HEADS))).astype(BF16)
        bfc = jnp.pad(b_f[l], (0, LANES - ATT_HEADS)).reshape(1, LANES)
        wgate = w_in[l, :, MIX_COLS + ATT_HEADS:].astype(BF16)
        poolw = _block_diag(pool_w[l]).astype(BF16)
        sgb = jnp.repeat(sg_b[l].T, SG_HD, axis=1)

        ya, yb, yc, q, k, v = _mixer_in(
            x, mod[l], row(mix_norm_g[l]), wmix, wfc, bfc, poolw, row(pool_scale[l]),
            row(sg_ln_g[l]), row(sg_ln_b[l]), sg_w[l], sgb, conv_w[l], row(conv_b[l]),
            row(conv_ln_g[l]), row(conv_ln_b[l]), tb=tb)
        yd = _attention(q, k, v, tq=tq)
        x = _merge(x, mod[l], row(mix_norm_g[l]), wgate, w_branch[l].astype(BF16), w_out[l].astype(BF16),
                   ya, yb, yc, yd, tb=tb)
        fnorm = row(ffn_norm_g[l])
        ls, lst, wt, blk_cnt, blk_carry, cnt = _route_call(
            x, mod[l], fnorm, router_w[l].T, router_bias[l].reshape(-1, 1), tb=tb)
        meta, base = _route_fin(cnt, tm=tm, meta_lanes=meta_lanes)
        xs = _dispatch(cnt[:, 0], base[:, 0], blk_cnt, blk_carry, ls, x, mod[l], fnorm,
                       tb=tb, tm=tm, n_slots=n_tiles * tm)
        ys = _experts(meta[0, :n_tiles], meta[0, meta_lanes - 1:], xs, exp_w1[l], exp_w3[l], exp_w2[l], tm=tm)
        x = _combine(base[:, 0], blk_cnt, blk_carry, x, mod[l], fnorm, lst, wt, shared_w1[l].astype(BF16),
                     shared_w3[l].astype(BF16), shared_w2[l].astype(BF16), row(final_norm_g), ys,
                     tb=tb, final_norm=(l == L - 1))
    return x
```

```python
import functools
import math

import jax
import jax.numpy as jnp
from jax import lax
from jax.experimental import pallas as pl
from jax.experimental.pallas import tpu as pltpu

F32 = jnp.float32
BF16 = jnp.bfloat16
HIGHEST = lax.Precision.HIGHEST

D_MODEL = 1024
BRANCH_W = 256
N_BRANCH = 4
POOL_WINDOWS = (2, 4, 8, 16)
POOL_GW = 64
POOL_TAIL = 16
SG_HEADS = 4
SG_CHUNK = 128
SG_HD = 64
CONV_WIDTH = 31
CONV_TAIL = 32
ATT_HEADS = 4
ATT_HD = 64
N_EXPERTS = 64
TOP_K = 6
N_GROUPS = 8
GROUP_SIZE = N_EXPERTS // N_GROUPS
TOPK_GROUPS = 4
EXPERT_FF = 256
SHARED_FF = 256
ROUTE_SCALE = 2.5
EPS = 1e-6
N_MOD = 6
MIX_COLS = 8 * BRANCH_W
LANES = 128
SUBLANES = 8
LOG2E = 1.4426950408889634
EXPERT_TILE = 256
VMEM_LIMIT = 56 * 1024 * 1024


def _sigmoid(x):
    return 1.0 / (1.0 + jnp.exp(-x))


def _silu(x):
    return x * _sigmoid(x)


def _rms_mod(x, g, scale, shift):
    y = x * lax.rsqrt(jnp.mean(x * x, axis=-1, keepdims=True) + EPS)
    return (y * g) * (1.0 + scale) + shift


def _layer_norm(x, g, b):
    mu = jnp.mean(x, axis=-1, keepdims=True)
    xc = x - mu
    var = jnp.mean(xc * xc, axis=-1, keepdims=True)
    return xc * lax.rsqrt(var + EPS) * g + b


def _params(*sem):
    return pltpu.CompilerParams(dimension_semantics=sem, vmem_limit_bytes=VMEM_LIMIT)


def _mod_kernel(c_ref, w_ref, b_ref, o_ref):
    c = c_ref[...]
    o_ref[0] = jnp.dot(_silu(c), w_ref[0], precision=HIGHEST, preferred_element_type=F32) + b_ref[0]


def _modulation(c, ada_w, ada_b):
    L, D, N = ada_w.shape
    B = c.shape[0]
    tn = 1536
    out = pl.pallas_call(
        _mod_kernel,
        grid=(L, N // tn),
        in_specs=[pl.BlockSpec((B, D), lambda l, j: (0, 0)),
                  pl.BlockSpec((1, D, tn), lambda l, j: (l, 0, j)),
                  pl.BlockSpec((1, 1, tn), lambda l, j: (l, 0, j))],
        out_specs=pl.BlockSpec((1, B, tn), lambda l, j: (l, 0, j)),
        out_shape=jax.ShapeDtypeStruct((L, B, N), F32),
        compiler_params=_params("arbitrary", "arbitrary"),
        name="modulation",
    )(c, ada_w, ada_b.reshape(L, 1, N))
    return out.reshape(L, B, N_MOD, D)


def _mixer_in_kernel(x_ref, mod_ref, g_ref, wmix_f32, wfc_f32, bfc_ref,
                     poolw_ref, pools_ref, sglg_ref, sglb_ref, sgw_ref, sgb_ref,
                     cw_ref, cb_ref, clg_ref, clb_ref,
                     ya_ref, yb_ref, yc_ref, q_ref, k_ref, v_ref,
                     pool_ext, conv_ext, cum_c, wmix_ref, wfc_ref, *, tb):
    j = pl.program_id(1)

    @pl.when((pl.program_id(0) == 0) & (j == 0))
    def _():
        wmix_ref[...] = wmix_f32[0].astype(BF16)
        wfc_ref[...] = wfc_f32[0].astype(BF16)

    @pl.when(j == 0)
    def _():
        pool_ext[0:POOL_TAIL, :] = jnp.zeros((POOL_TAIL, BRANCH_W), F32)
        conv_ext[0:CONV_TAIL, :] = jnp.zeros((CONV_TAIL, BRANCH_W), F32)
        cum_c[...] = jnp.zeros_like(cum_c)

    mod = mod_ref[0]
    h = _rms_mod(x_ref[0], g_ref[...], mod[1:2, :], mod[0:1, :])
    hb = h.astype(BF16)
    proj = jnp.dot(hb, wmix_ref[...], preferred_element_type=F32)

    lane = lax.broadcasted_iota(jnp.int32, (1, BRANCH_W), 1)
    row = lax.broadcasted_iota(jnp.int32, (tb, 1), 0)

    u = proj[:, 0:BRANCH_W]
    pool_ext[POOL_TAIL:POOL_TAIL + tb, :] = u
    ext = pool_ext[...]
    s2 = ext + pltpu.roll(ext, 1, 0)
    s4 = s2 + pltpu.roll(s2, 2, 0)
    s8 = s4 + pltpu.roll(s4, 4, 0)
    s16 = s8 + pltpu.roll(s8, 8, 0)
    grp = lane // POOL_GW
    wsum = jnp.where(grp == 0, s2, jnp.where(grp == 1, s4, jnp.where(grp == 2, s8, s16)))[POOL_TAIL:, :]
    win = jnp.where(grp == 0, 2.0, jnp.where(grp == 1, 4.0, jnp.where(grp == 2, 8.0, 16.0)))
    count = jnp.minimum((j * tb + row + 1).astype(F32), win)
    pooled = wsum / count - u
    ya = jnp.dot(pooled.astype(BF16), poolw_ref[...], preferred_element_type=F32) * pools_ref[...]
    ya_ref[0] = ya.astype(BF16)
    pool_ext[0:POOL_TAIL, :] = u[tb - POOL_TAIL:, :]

    z = proj[:, BRANCH_W:3 * BRANCH_W]
    z = 0.5 * z * (1.0 + jnp.tanh(math.sqrt(2.0 / math.pi) * (z + 0.044715 * (z * z * z))))
    su = z[:, 0:BRANCH_W]
    sv = _layer_norm(z[:, BRANCH_W:], sglg_ref[...], sglb_ref[...])
    r128 = lax.broadcasted_iota(jnp.int32, (SG_CHUNK, SG_CHUNK), 0)
    c128 = lax.broadcasted_iota(jnp.int32, (SG_CHUNK, SG_CHUNK), 1)
    wcat = jnp.concatenate(
        [jnp.where(r128 >= c128, sgw_ref[hh], 0.0) for hh in range(SG_HEADS)], axis=1).astype(BF16)
    head = lane // SG_HD
    for ci in range(tb // SG_CHUNK):
        rows = slice(ci * SG_CHUNK, (ci + 1) * SG_CHUNK)
        vch = sv[rows, :]
        vstack = jnp.concatenate(
            [jnp.where(head == hh, vch, 0.0) for hh in range(SG_HEADS)], axis=0).astype(BF16)
        s = jnp.dot(wcat, vstack, preferred_element_type=F32) + sgb_ref[...]
        yb_ref[0, rows, :] = (su[rows, :] * s).astype(BF16)

    glu = proj[:, 3 * BRANCH_W:4 * BRANCH_W] * _sigmoid(proj[:, 4 * BRANCH_W:5 * BRANCH_W])
    conv_ext[CONV_TAIL:CONV_TAIL + tb, :] = glu
    acc = jnp.zeros((tb, BRANCH_W), F32) + cb_ref[...]
    for kk in range(CONV_WIDTH):
        off = CONV_TAIL - (CONV_WIDTH - 1) + kk
        acc = acc + cw_ref[kk:kk + 1, :] * conv_ext[off:off + tb, :]
    yc_ref[0] = _silu(_layer_norm(acc, clg_ref[...], clb_ref[...])).astype(BF16)
    conv_ext[0:CONV_TAIL, :] = glu[tb - CONV_TAIL:, :]

    def log_sigmoid(t):
        return jnp.minimum(t, 0.0) - jnp.log(1.0 + jnp.exp(-jnp.abs(t)))

    rr = lax.broadcasted_iota(jnp.int32, (tb, tb), 0)
    cc = lax.broadcasted_iota(jnp.int32, (tb, tb), 1)
    lower = jnp.where(rr >= cc, 1.0, 0.0).astype(F32)
    lf_c = log_sigmoid(jnp.dot(hb, wfc_ref[...], preferred_element_type=F32) + bfc_ref[...])
    cs_c = jnp.dot(lower, lf_c, precision=HIGHEST, preferred_element_type=F32) + cum_c[...]
    cum_c[...] = cs_c[tb - 1:tb, :]
    cl2 = cs_c * LOG2E
    ln = lax.broadcasted_iota(jnp.int32, (1, LANES), 1)
    ones_q = jnp.where((ln >= ATT_HD + 3) & (ln < ATT_HD + 6), 1.0, 0.0)
    ones_k = jnp.where((ln >= ATT_HD) & (ln < ATT_HD + 3), 1.0, 0.0)
    ones_v = jnp.where(ln == ATT_HD, 1.0, 0.0)
    for hh in range(ATT_HEADS):
        pair = (hh // 2) * LANES
        qs = proj[:, 5 * BRANCH_W + pair:5 * BRANCH_W + pair + LANES] * (LOG2E / math.sqrt(ATT_HD))
        ks = proj[:, 6 * BRANCH_W + pair:6 * BRANCH_W + pair + LANES]
        vs = proj[:, 7 * BRANCH_W + pair:7 * BRANCH_W + pair + LANES]
        if hh % 2:
            qs, ks, vs = (pltpu.roll(a, ATT_HD, 1) for a in (qs, ks, vs))
        c = cl2[:, hh:hh + 1]
        hi = c.astype(BF16).astype(F32)
        mid = (c - hi).astype(BF16).astype(F32)
        lo = (c - hi) - mid
        q_extra = jnp.where(ln == ATT_HD, hi, jnp.where(ln == ATT_HD + 1, mid, jnp.where(ln == ATT_HD + 2, lo, ones_q)))
        k_extra = jnp.where(ln == ATT_HD + 3, -hi,
                            jnp.where(ln == ATT_HD + 4, -mid, jnp.where(ln == ATT_HD + 5, -lo, ones_k)))
        q_ref[0, hh, 0] = jnp.where(ln < ATT_HD, qs, q_extra).T.astype(BF16)
        k_ref[0, hh] = jnp.where(ln < ATT_HD, ks, k_extra).astype(BF16)
        v_ref[0, hh, 0] = jnp.where(ln < ATT_HD, vs, ones_v).T.astype(BF16)


def _w_in_cols(layer, first_col, ncols):
    return pl.BlockSpec((1, D_MODEL, ncols), lambda b, j: (layer, 0, first_col // ncols), pipeline_mode=pl.Buffered(1))


def _mixer_in(x, mod_l, norm_g, w_in, layer, bfc, poolw, pools, sglg, sglb, sgw, sgb, cw, cb, clg, clb, *, tb):
    B, S, D = x.shape
    full = lambda a: pl.BlockSpec(a.shape, lambda b, j: (0,) * a.ndim)
    tok = lambda w: pl.BlockSpec((1, tb, w), lambda b, j: (b, j, 0))
    head = pl.BlockSpec((1, ATT_HEADS, tb, LANES), lambda b, j: (b, 0, j, 0))
    head_t = pl.BlockSpec((1, ATT_HEADS, 1, LANES, tb), lambda b, j: (b, 0, j, 0, 0))
    consts = (bfc, poolw, pools, sglg, sglb, sgw, sgb, cw, cb, clg, clb)
    act = jax.ShapeDtypeStruct((B, S, BRANCH_W), BF16)
    att = jax.ShapeDtypeStruct((B, ATT_HEADS, S, LANES), BF16)
    att_t = jax.ShapeDtypeStruct((B, ATT_HEADS, S // tb, LANES, tb), BF16)
    return pl.pallas_call(
        functools.partial(_mixer_in_kernel, tb=tb),
        grid=(B, S // tb),
        in_specs=[tok(D), pl.BlockSpec((1, N_MOD, D), lambda b, j: (b, 0, 0)), full(norm_g),
                  _w_in_cols(layer, 0, MIX_COLS), _w_in_cols(layer, MIX_COLS, LANES)] + [full(a) for a in consts],
        out_specs=[tok(BRANCH_W)] * 3 + [head_t, head, head_t],
        out_shape=[act] * 3 + [att_t, att, att_t],
        scratch_shapes=[pltpu.VMEM((POOL_TAIL + tb, BRANCH_W), F32),
                        pltpu.VMEM((CONV_TAIL + tb, BRANCH_W), F32),
                        pltpu.VMEM((1, LANES), F32),
                        pltpu.VMEM((D, MIX_COLS), BF16), pltpu.VMEM((D, LANES), BF16)],
        compiler_params=_params("arbitrary", "arbitrary"),
        name="mixer_in",
    )(x, mod_l, norm_g, w_in, w_in, *consts)


def _attn_kernel(q_ref, k_ref, v_ref, o_ref, m_ref, acc_ref, *, tq):
    i = pl.program_id(1)
    key = lax.broadcasted_iota(jnp.int32, (tq, tq), 0)
    qry = lax.broadcasted_iota(jnp.int32, (tq, tq), 1)
    m_ref[...] = jnp.full(m_ref.shape, -jnp.inf, F32)
    acc_ref[...] = jnp.zeros(acc_ref.shape, F32)

    def block(kj, diagonal):
        ks = pl.multiple_of(kj * tq, tq)
        logits = [jnp.dot(k_ref[0, hh, pl.ds(ks, tq), :], q_ref[0, hh, 0], preferred_element_type=F32)
                  for hh in range(ATT_HEADS)]
        for hh in range(ATT_HEADS):
            s = logits[hh]
            if diagonal:
                s = jnp.where(key <= qry, s, -jnp.inf)
            m_old = m_ref[hh]
            m_new = jnp.maximum(m_old, jnp.max(s, axis=0, keepdims=True))
            p = jnp.exp2(s - m_new)
            pv = jnp.dot(v_ref[0, hh, kj], p.astype(BF16), preferred_element_type=F32)
            acc_ref[hh] = jnp.exp2(m_old - m_new) * acc_ref[hh] + pv
            m_ref[hh] = m_new

    def body(kj, carry):
        block(kj, False)
        return carry

    lax.fori_loop(0, i, body, 0)
    block(i, True)

    ln = lax.broadcasted_iota(jnp.int32, (1, LANES), 1)
    for pr in range(ATT_HEADS // 2):
        o = []
        for hh in (2 * pr, 2 * pr + 1):
            acc = acc_ref[hh]
            o.append((acc / acc[ATT_HD:ATT_HD + 1, :]).T)
        o_ref[0, :, pr * LANES:(pr + 1) * LANES] = jnp.where(ln < ATT_HD, o[0], pltpu.roll(o[1], ATT_HD, 1)).astype(BF16)


def _attention(q_t, k, v_t, *, tq):
    B, H, S, W = k.shape
    nblk = S // tq
    assert q_t.shape == (B, H, nblk, W, tq) and v_t.shape == q_t.shape
    return pl.pallas_call(
        functools.partial(_attn_kernel, tq=tq),
        grid=(B, nblk),
        in_specs=[pl.BlockSpec((1, H, 1, W, tq), lambda b, i: (b, 0, i, 0, 0)),
                  pl.BlockSpec((1, H, S, W), lambda b, i: (b, 0, 0, 0), pipeline_mode=pl.Buffered(1)),
                  pl.BlockSpec((1, H, nblk, W, tq), lambda b, i: (b, 0, 0, 0, 0), pipeline_mode=pl.Buffered(1))],
        out_specs=pl.BlockSpec((1, tq, BRANCH_W), lambda b, i: (b, i, 0)),
        out_shape=jax.ShapeDtypeStruct((B, S, BRANCH_W), BF16),
        scratch_shapes=[pltpu.VMEM((H, 1, tq), F32), pltpu.VMEM((H, W, tq), F32)],
        compiler_params=_params("arbitrary", "arbitrary"),
        name="attention",
    )(q_t, k, v_t)


def _merge_kernel(x_ref, mod_ref, g_ref, wa_f32, wb_f32, wt_f32, wbr_ref, wout_ref, ya_ref, yb_ref, yc_ref, yd_ref,
                  o_ref, wg_ref):
    @pl.when((pl.program_id(0) == 0) & (pl.program_id(1) == 0))
    def _():
        shift, wide = GATE_SHIFT, D_MODEL + LANES
        windows = (wa_f32[0, :, 0:wide],
                   jnp.concatenate([wa_f32[0, :, D_MODEL:], wb_f32[0, :, 0:LANES]], axis=1),
                   wb_f32[0, :, 0:wide],
                   jnp.concatenate([wb_f32[0, :, D_MODEL:], wt_f32[0]], axis=1))
        for n, win in enumerate(windows):
            wg_ref[:, n * D_MODEL:(n + 1) * D_MODEL] = pltpu.roll(win, wide - shift, 1)[:, :D_MODEL].astype(BF16)

    x = x_ref[0]
    mod = mod_ref[0]
    hb = _rms_mod(x, g_ref[...], mod[1:2, :], mod[0:1, :]).astype(BF16)
    merged = jnp.zeros(x.shape, F32)
    for n, y_ref in enumerate((ya_ref, yb_ref, yc_ref, yd_ref)):
        gate = _sigmoid(jnp.dot(hb, wg_ref[:, n * D_MODEL:(n + 1) * D_MODEL], preferred_element_type=F32))
        merged = merged + gate * jnp.dot(y_ref[0], wbr_ref[n], preferred_element_type=F32)
    out = jnp.dot(merged.astype(BF16), wout_ref[...], preferred_element_type=F32)
    o_ref[0] = x + mod[2:3, :] * out


GATE_SHIFT = ATT_HEADS


def _merge(x, mod_l, norm_g, w_in, layer, wbranch, wout, ya, yb, yc, yd, *, tb):
    B, S, D = x.shape
    full = lambda a: pl.BlockSpec(a.shape, lambda b, j: (0,) * a.ndim)
    tok = lambda w: pl.BlockSpec((1, tb, w), lambda b, j: (b, j, 0))
    assert w_in.shape[2] == MIX_COLS + GATE_SHIFT + N_BRANCH * D
    return pl.pallas_call(
        _merge_kernel,
        grid=(B, S // tb),
        in_specs=[tok(D), pl.BlockSpec((1, N_MOD, D), lambda b, j: (b, 0, 0)), full(norm_g),
                  _w_in_cols(layer, MIX_COLS, 2 * D), _w_in_cols(layer, MIX_COLS + 2 * D, 2 * D),
                  _w_in_cols(layer, MIX_COLS + 4 * D, LANES),
                  full(wbranch), full(wout)] + [tok(BRANCH_W)] * 4,
        out_specs=tok(D),
        out_shape=jax.ShapeDtypeStruct((B, S, D), F32),
        scratch_shapes=[pltpu.VMEM((D, N_BRANCH * D), BF16)],
        compiler_params=_params("arbitrary", "arbitrary"),
        name="merge",
    )(x, mod_l, norm_g, w_in, w_in, w_in, wbranch, wout, ya, yb, yc, yd)


def _route(scores_t, bias_t):
    E, n = scores_t.shape
    sel = scores_t + bias_t
    eidx = lax.broadcasted_iota(jnp.int32, (E, n), 0)
    neg = jnp.full((E, n), -jnp.inf, F32)
    gscore = []
    sub = lax.broadcasted_iota(jnp.int32, (GROUP_SIZE, n), 0)
    for g in range(N_GROUPS):
        blk = sel[g * GROUP_SIZE:(g + 1) * GROUP_SIZE, :]
        m1 = jnp.max(blk, axis=0, keepdims=True)
        first = jnp.min(jnp.where(blk == m1, sub, GROUP_SIZE), axis=0, keepdims=True)
        m2 = jnp.max(jnp.where(sub == first, -jnp.inf, blk), axis=0, keepdims=True)
        gscore.append(m1 + m2)
    emask = []
    for g in range(N_GROUPS):
        beaten = jnp.zeros((1, n), jnp.int32)
        for g2 in range(N_GROUPS):
            if g2 == g:
                continue
            wins = (gscore[g2] > gscore[g]) | ((gscore[g2] == gscore[g]) & (g2 < g))
            beaten = beaten + wins.astype(jnp.int32)
        emask.append(jnp.broadcast_to(beaten < TOPK_GROUPS, (GROUP_SIZE, n)))
    cur = jnp.where(jnp.concatenate(emask, axis=0), sel, neg)
    chosen = jnp.zeros((E, n), jnp.bool_)
    firsts = []
    for _ in range(TOP_K):
        m = jnp.max(cur, axis=0, keepdims=True)
        first = jnp.min(jnp.where(cur == m, eidx, E), axis=0, keepdims=True)
        hit = eidx == first
        chosen = chosen | hit
        cur = jnp.where(hit, neg, cur)
        firsts.append(first)
    w = jnp.where(chosen, scores_t, 0.0)
    return w / jnp.sum(w, axis=0, keepdims=True) * ROUTE_SCALE, chosen, firsts


def _route_kernel(x_ref, mod_ref, g_ref, rwt_ref, rb_ref, ls_ref, lst_ref, wt_ref, bc_ref, cr_ref, cnt_ref, count):
    first_step = (pl.program_id(0) == 0) & (pl.program_id(1) == 0)

    @pl.when(first_step)
    def _():
        count[...] = jnp.zeros_like(count)

    mod = mod_ref[0]
    h = _rms_mod(x_ref[0], g_ref[...], mod[4:5, :], mod[3:4, :])
    logits_t = lax.dot_general(rwt_ref[...], h, (((1,), (1,)), ((), ())),
                               precision=HIGHEST, preferred_element_type=F32)
    gates_t, chosen, firsts = _route(_sigmoid(logits_t), rb_ref[...])
    E, tb = gates_t.shape
    ones = jnp.where(chosen, 1.0, 0.0)
    rr = lax.broadcasted_iota(jnp.int32, (tb, tb), 0)
    cc = lax.broadcasted_iota(jnp.int32, (tb, tb), 1)
    upper = jnp.where(rr <= cc, 1.0, 0.0).astype(BF16)
    incl = jnp.dot(ones.astype(BF16), upper, preferred_element_type=F32)
    block_count = jnp.sum(ones, axis=1, keepdims=True)
    block_count = block_count + (block_count.astype(jnp.int32) & 1).astype(F32)
    er = lax.broadcasted_iota(jnp.int32, (E, E), 0)
    ec = lax.broadcasted_iota(jnp.int32, (E, E), 1)
    before = jnp.where(ec < er, 1.0, 0.0).astype(F32)
    local_base = jnp.dot(before, jnp.broadcast_to(block_count, (E, LANES)), precision=HIGHEST,
                         preferred_element_type=F32)[:, 0:1]
    row_all = local_base + (incl - ones)
    eidx = lax.broadcasted_iota(jnp.int32, (E, tb), 0)
    ls_rows = [jnp.sum(jnp.where(eidx == f, row_all, 0.0), axis=0, keepdims=True) for f in firsts]
    w_rows = [jnp.sum(jnp.where(eidx == f, gates_t, 0.0), axis=0, keepdims=True) for f in firsts]
    ls_ref[...] = jnp.concatenate(ls_rows + [jnp.zeros((SUBLANES - TOP_K, tb), F32)], axis=0).astype(jnp.int32)
    zpad = jnp.zeros((LANES - TOP_K, tb), F32)
    lst_ref[0] = jnp.concatenate(ls_rows + [zpad], axis=0).T.astype(jnp.int32)
    wt_ref[0] = jnp.concatenate(w_rows + [zpad], axis=0).T

    def as_row(col):
        sel = lax.broadcasted_iota(jnp.int32, (E, LANES), 0) == lax.broadcasted_iota(jnp.int32, (E, LANES), 1)
        return jnp.sum(jnp.where(sel, col, 0.0), axis=0, keepdims=True).astype(jnp.int32)

    bc_ref[0] = as_row(block_count)
    cr_ref[0] = as_row(count[...])
    count[...] = count[...] + block_count
    cnt_ref[...] = jnp.broadcast_to(count[...], cnt_ref.shape).astype(jnp.int32)


def _route_call(x, mod_l, norm_g, router_wt, router_b, *, tb):
    B, S, D = x.shape
    nj = S // tb
    full = lambda a: pl.BlockSpec(a.shape, lambda b, j: (0,) * a.ndim)
    tok = lambda w: pl.BlockSpec((1, tb, w), lambda b, j: (b, j, 0))
    kt = pl.BlockSpec((SUBLANES, tb), lambda b, j: (0, b * nj + j))
    per_block = pl.BlockSpec((1, 1, LANES), lambda b, j: (b * nj + j, 0, 0))
    return pl.pallas_call(
        _route_kernel,
        grid=(B, nj),
        in_specs=[tok(D), pl.BlockSpec((1, N_MOD, D), lambda b, j: (b, 0, 0)),
                  full(norm_g), full(router_wt), full(router_b)],
        out_specs=[kt, tok(LANES), tok(LANES), per_block, per_block,
                   pl.BlockSpec((N_EXPERTS, LANES), lambda b, j: (0, 0))],
        out_shape=[jax.ShapeDtypeStruct((SUBLANES, B * S), jnp.int32),
                   jax.ShapeDtypeStruct((B, S, LANES), jnp.int32),
                   jax.ShapeDtypeStruct((B, S, LANES), F32),
                   jax.ShapeDtypeStruct((B * nj, 1, LANES), jnp.int32),
                   jax.ShapeDtypeStruct((B * nj, 1, LANES), jnp.int32),
                   jax.ShapeDtypeStruct((N_EXPERTS, LANES), jnp.int32)],
        scratch_shapes=[pltpu.VMEM((N_EXPERTS, 1), F32)],
        compiler_params=_params("arbitrary", "arbitrary"),
        name="route",
    )(x, mod_l, norm_g, router_wt, router_b)


def _route_fin_kernel(cnt_ref, meta_ref, base_ref, *, tm):
    cnt = cnt_ref[...]
    ntile = lax.shift_right_logical(cnt + (tm - 1), int(math.log2(tm)))
    er = lax.broadcasted_iota(jnp.int32, (N_EXPERTS, N_EXPERTS), 0)
    ec = lax.broadcasted_iota(jnp.int32, (N_EXPERTS, N_EXPERTS), 1)
    before = jnp.where(ec < er, 1.0, 0.0).astype(F32)
    start = jnp.dot(before, ntile.astype(F32), precision=HIGHEST, preferred_element_type=F32).astype(jnp.int32)
    base = start * tm
    base_ref[...] = base
    nlane = meta_ref.shape[1]
    lane = lax.broadcasted_iota(jnp.int32, (1, nlane), 1)
    end = (start + ntile)[:, 0:1]
    tile_expert = jnp.sum(jnp.where(end <= lane, 1, 0), axis=0, keepdims=True)
    tile_expert = jnp.minimum(tile_expert, N_EXPERTS - 1)
    meta_ref[...] = jnp.where(lane == nlane - 1, end[N_EXPERTS - 1:N_EXPERTS, :], tile_expert)


def _route_fin(cnt, *, tm, meta_lanes):
    full = lambda a: pl.BlockSpec(a.shape, lambda i: (0,) * a.ndim)
    outs = [jax.ShapeDtypeStruct((1, meta_lanes), jnp.int32), jax.ShapeDtypeStruct(cnt.shape, jnp.int32)]
    return pl.pallas_call(
        functools.partial(_route_fin_kernel, tm=tm),
        grid=(1,),
        in_specs=[full(cnt)],
        out_specs=[full(o) for o in outs],
        out_shape=outs,
        compiler_params=_params("arbitrary"),
        name="route_fin",
    )(cnt)


HALF = D_MODEL // 2
SLAB = HALF // LANES
HIGH_BITS = 0xFFFF0000


def _to_slabs(ref, value):
    n = value.shape[0]
    bits = lax.bitcast_convert_type(value.astype(BF16).astype(F32), jnp.uint32)
    packed = (bits[:, HALF:] & jnp.uint32(HIGH_BITS)) | (bits[:, :HALF] >> 16)
    for s in range(SLAB):
        ref[pl.ds(s, n, stride=SLAB), :] = packed[:, s * LANES:(s + 1) * LANES]


def _from_slabs(ref, n):
    packed = jnp.concatenate([ref[pl.ds(s, n, stride=SLAB), :] for s in range(SLAB)], axis=1)
    low = lax.bitcast_convert_type(packed << 16, F32).astype(BF16)
    high = lax.bitcast_convert_type(packed & jnp.uint32(HIGH_BITS), F32).astype(BF16)
    return low, high


def _for_expert_runs(base_sm, bc_sm, cr_sm, max_run, fn):
    def per_expert(e, local_row):
        n = bc_sm[0, 0, e]
        _for_pieces(n, max_run, lambda done, size: fn(local_row + done, base_sm[e] + cr_sm[0, 0, e] + done, size))
        return local_row + n

    lax.fori_loop(0, N_EXPERTS, per_expert, 0)


def _for_pieces(n, max_piece, fn):
    done = 0
    bit = max_piece
    while bit >= 2:
        take = n & bit

        @pl.when(take != 0)
        def _(done=done, bit=bit):
            fn(done, bit)

        done = done + take
        bit //= 2


ROW_GROUPS = 7


def _local_rows(tb):
    rows = tb * TOP_K + N_EXPERTS
    assert rows % (ROW_GROUPS * 2 * SUBLANES) == 0, rows
    return rows


def _queue(nrows):
    return int(math.log2(nrows)) % 2


def _slab_rows(ref, row, nrows):
    return ref.at[pl.ds(pl.multiple_of(row * SLAB, 2 * SLAB), nrows * SLAB)]


def _dispatch_kernel(cnt_sm, base_sm, bc_sm, cr_sm, bc_prev, cr_prev, ls_ref, x_ref, mod_ref, g_ref, xs_hbm,
                     stage, zeros, sem, zsem, *, tb, tm):
    step = pl.program_id(0) * pl.num_programs(1) + pl.program_id(1)
    last_step = pl.num_programs(0) * pl.num_programs(1) - 1
    cur = lax.rem(step, 2)
    mod = mod_ref[0]
    hb = _rms_mod(x_ref[0], g_ref[...], mod[4:5, :], mod[3:4, :]).astype(BF16)
    ls = ls_ref[...]
    gr = _local_rows(tb) // ROW_GROUPS
    for grp in range(ROW_GROUPS):
        row = grp * gr + lax.broadcasted_iota(jnp.int32, (gr, tb), 0)
        hit = ls[0:1, :] == row
        for k in range(1, TOP_K):
            hit = hit | (ls[k:k + 1, :] == row)
        perm = jnp.where(hit, 1.0, 0.0).astype(BF16)
        _to_slabs(stage.at[cur, pl.ds(grp * gr * SLAB, gr * SLAB)], jnp.dot(perm, hb, preferred_element_type=F32))

    def run_copy(buf):
        return lambda local_row, sorted_row, nrows: pltpu.make_async_copy(
            _slab_rows(stage.at[buf], local_row, nrows), _slab_rows(xs_hbm, sorted_row, nrows), sem.at[buf])

    _for_expert_runs(base_sm, bc_sm, cr_sm, tb, lambda *a: run_copy(cur)(*a).start(priority=_queue(a[2])))

    @pl.when(step == 0)
    def _():
        zeros[...] = jnp.zeros_like(zeros)

        def per_expert(e, carry):
            n = cnt_sm[e]
            npad = lax.rem(tm - lax.rem(n, tm), tm)
            zcopy = lambda done, size: pltpu.make_async_copy(
                _slab_rows(zeros, 0, size), _slab_rows(xs_hbm, base_sm[e] + n + done, size), zsem)
            _for_pieces(npad, tm // 2, lambda *a: zcopy(*a).start())
            _for_pieces(npad, tm // 2, lambda *a: zcopy(*a).wait())
            return carry

        lax.fori_loop(0, N_EXPERTS, per_expert, 0)

    @pl.when(step > 0)
    def _():
        _for_expert_runs(base_sm, bc_prev, cr_prev, tb, lambda *a: run_copy(1 - cur)(*a).wait())

    @pl.when(step == last_step)
    def _():
        _for_expert_runs(base_sm, bc_sm, cr_sm, tb, lambda *a: run_copy(cur)(*a).wait())


def _dispatch(cnt, base, blk_cnt, blk_carry, ls, x, mod_l, norm_g, *, tb, tm, n_slots):
    B, S, D = x.shape
    nj = S // tb
    smem_block = lambda shift: pl.BlockSpec(
        (1, 1, LANES), lambda b, j, *_: (jnp.maximum(b * nj + j + shift, 0), 0, 0), memory_space=pltpu.SMEM)
    grid_spec = pltpu.PrefetchScalarGridSpec(
        num_scalar_prefetch=2,
        grid=(B, nj),
        in_specs=[smem_block(0), smem_block(0), smem_block(-1), smem_block(-1),
                  pl.BlockSpec((SUBLANES, tb), lambda b, j, *_: (0, b * nj + j)),
                  pl.BlockSpec((1, tb, D), lambda b, j, *_: (b, j, 0)),
                  pl.BlockSpec((1, N_MOD, D), lambda b, j, *_: (b, 0, 0)),
                  pl.BlockSpec(norm_g.shape, lambda b, j, *_: (0, 0))],
        out_specs=pl.BlockSpec(memory_space=pltpu.HBM),
        scratch_shapes=[pltpu.VMEM((2, _local_rows(tb) * SLAB, LANES), jnp.uint32),
                        pltpu.VMEM((tm // 2 * SLAB, LANES), jnp.uint32),
                        pltpu.SemaphoreType.DMA((2,)), pltpu.SemaphoreType.DMA])
    return pl.pallas_call(
        functools.partial(_dispatch_kernel, tb=tb, tm=tm),
        grid_spec=grid_spec,
        out_shape=jax.ShapeDtypeStruct((n_slots * SLAB, LANES), jnp.uint32),
        compiler_params=_params("arbitrary", "arbitrary"),
        name="dispatch",
    )(cnt, base, blk_cnt, blk_carry, blk_cnt, blk_carry, ls, x, mod_l, norm_g)


def _experts_kernel(te_sm, nt_sm, x_ref, w1_ref, w3_ref, w2_ref, y_ref, w1b, w3b, w2b, *, tm):
    i = pl.program_id(0)
    in_use = i < nt_sm[0]

    @pl.when(in_use)
    def _():
        @pl.when((i == 0) | (te_sm[i] != te_sm[jnp.maximum(i - 1, 0)]))
        def _():
            w1b[...] = w1_ref[0].astype(BF16)
            w3b[...] = w3_ref[0].astype(BF16)
            w2b[...] = w2_ref[0].astype(BF16)

        x_lo, x_hi = _from_slabs(x_ref, tm)
        up = lambda w: (jnp.dot(x_lo, w[:HALF, :], preferred_element_type=F32)
                        + jnp.dot(x_hi, w[HALF:, :], preferred_element_type=F32))
        hid = _silu(up(w1b)) * up(w3b)
        _to_slabs(y_ref, jnp.dot(hid.astype(BF16), w2b[...], preferred_element_type=F32))

    @pl.when(jnp.logical_not(in_use))
    def _():
        y_ref[...] = jnp.zeros_like(y_ref)


def _experts(tile_expert, n_tiles, xs, w1, w3, w2, *, tm):
    E, D, F = w1.shape
    n_slots = xs.shape[0] // SLAB
    last = lambda i, te, nt: jnp.minimum(i, nt[0] - 1)
    grid_spec = pltpu.PrefetchScalarGridSpec(
        num_scalar_prefetch=2,
        grid=(n_slots // tm,),
        in_specs=[pl.BlockSpec((tm * SLAB, LANES), lambda i, te, nt: (last(i, te, nt), 0)),
                  pl.BlockSpec((1, D, F), lambda i, te, nt: (te[last(i, te, nt)], 0, 0)),
                  pl.BlockSpec((1, D, F), lambda i, te, nt: (te[last(i, te, nt)], 0, 0)),
                  pl.BlockSpec((1, F, D), lambda i, te, nt: (te[last(i, te, nt)], 0, 0))],
        out_specs=pl.BlockSpec((tm * SLAB, LANES), lambda i, te, nt: (i, 0)),
        scratch_shapes=[pltpu.VMEM((D, F), BF16), pltpu.VMEM((D, F), BF16), pltpu.VMEM((F, D), BF16)])
    return pl.pallas_call(
        functools.partial(_experts_kernel, tm=tm),
        grid_spec=grid_spec,
        out_shape=jax.ShapeDtypeStruct(xs.shape, jnp.uint32),
        compiler_params=_params("arbitrary"),
        name="experts",
    )(tile_expert, n_tiles, xs, w1, w3, w2)


def _combine_kernel(base_sm, bc_sm, cr_sm, bc_next, cr_next, x_ref, mod_ref, g_ref, lst_ref, wt_ref,
                    sw1_ref, sw3_ref, sw2_ref, fg_ref, ys_hbm, o_ref, stage, sem, *, tb, final_norm):
    step = pl.program_id(0) * pl.num_programs(1) + pl.program_id(1)
    last_step = pl.num_programs(0) * pl.num_programs(1) - 1
    cur = lax.rem(step, 2)

    def run_copy(buf):
        return lambda local_row, sorted_row, nrows: pltpu.make_async_copy(
            _slab_rows(ys_hbm, sorted_row, nrows), _slab_rows(stage.at[buf], local_row, nrows), sem.at[buf])

    @pl.when(step == 0)
    def _():
        stage[...] = jnp.zeros_like(stage)
        _for_expert_runs(base_sm, bc_sm, cr_sm, tb, lambda *a: run_copy(cur)(*a).start(priority=_queue(a[2])))

    @pl.when(step < last_step)
    def _():
        _for_expert_runs(base_sm, bc_next, cr_next, tb, lambda *a: run_copy(1 - cur)(*a).start(priority=_queue(a[2])))

    x = x_ref[0]
    mod = mod_ref[0]
    hb = _rms_mod(x, g_ref[...], mod[4:5, :], mod[3:4, :]).astype(BF16)
    a = jnp.dot(hb, sw1_ref[...], preferred_element_type=F32)
    b = jnp.dot(hb, sw3_ref[...], preferred_element_type=F32)
    acc = jnp.dot((_silu(a) * b).astype(BF16), sw2_ref[...], preferred_element_type=F32)
    wt = wt_ref[0]
    lst = lst_ref[0]
    _for_expert_runs(base_sm, bc_sm, cr_sm, tb, lambda *a: run_copy(cur)(*a).wait())
    gr = _local_rows(tb) // ROW_GROUPS
    routed = [jnp.zeros((tb, HALF), F32)] * 2
    for grp in range(ROW_GROUPS):
        halves = _from_slabs(stage.at[cur, pl.ds(grp * gr * SLAB, gr * SLAB)], gr)
        row = grp * gr + lax.broadcasted_iota(jnp.int32, (1, gr), 1)
        gate = jnp.zeros((tb, gr), F32)
        for k in range(TOP_K):
            gate = gate + jnp.where(lst[:, k:k + 1] == row, wt[:, k:k + 1], 0.0)
        gate = gate.astype(BF16)
        routed = [r + jnp.dot(gate, yg, preferred_element_type=F32) for r, yg in zip(routed, halves)]
    acc = acc + jnp.concatenate(routed, axis=1)
    y = x + mod[5:6, :] * acc
    if final_norm:
        y = y * lax.rsqrt(jnp.mean(y * y, axis=-1, keepdims=True) + EPS) * fg_ref[...]
    o_ref[0] = y


def _combine(base, blk_cnt, blk_carry, x, mod_l, norm_g, lst, wt, sw1, sw3, sw2, final_g, ys, *, tb, final_norm):
    B, S, D = x.shape
    nj = S // tb
    full = lambda a: pl.BlockSpec(a.shape, lambda b, j, *_: (0,) * a.ndim)
    tok = lambda w: pl.BlockSpec((1, tb, w), lambda b, j, *_: (b, j, 0))
    smem_block = lambda shift: pl.BlockSpec(
        (1, 1, LANES), lambda b, j, *_: (jnp.minimum(b * nj + j + shift, B * nj - 1), 0, 0), memory_space=pltpu.SMEM)
    grid_spec = pltpu.PrefetchScalarGridSpec(
        num_scalar_prefetch=1,
        grid=(B, nj),
        in_specs=[smem_block(0), smem_block(0), smem_block(1), smem_block(1), tok(D),
                  pl.BlockSpec((1, N_MOD, D), lambda b, j, *_: (b, 0, 0)),
                  full(norm_g), tok(LANES), tok(LANES), full(sw1), full(sw3), full(sw2), full(final_g),
                  pl.BlockSpec(memory_space=pltpu.HBM)],
        out_specs=tok(D),
        scratch_shapes=[pltpu.VMEM((2, _local_rows(tb) * SLAB, LANES), jnp.uint32), pltpu.SemaphoreType.DMA((2,))])
    return pl.pallas_call(
        functools.partial(_combine_kernel, tb=tb, final_norm=final_norm),
        grid_spec=grid_spec,
        out_shape=jax.ShapeDtypeStruct((B, S, D), F32),
        compiler_params=_params("arbitrary", "arbitrary"),
        name="combine",
    )(base, blk_cnt, blk_carry, blk_cnt, blk_carry, x, mod_l, norm_g, lst, wt, sw1, sw3, sw2, final_g, ys)


def _block_diag(w):
    G, a, b = w.shape
    out = jnp.zeros((G * a, G * b), w.dtype)
    for g in range(G):
        out = out.at[g * a:(g + 1) * a, g * b:(g + 1) * b].set(w[g])
    return out


def kernel(x, c, w_in, b_f, pool_w, pool_scale, sg_ln_g, sg_ln_b, sg_w, sg_b, conv_w, conv_b, conv_ln_g,
           conv_ln_b, w_branch, w_out, mix_norm_g, ffn_norm_g, ada_w, ada_b, router_w, router_bias,
           exp_w1, exp_w3, exp_w2, shared_w1, shared_w3, shared_w2, final_norm_g):
    B, S, D = x.shape
    L = w_in.shape[0]
    tb = min(512, S)
    tq = tb
    tm = EXPERT_TILE
    n_tiles = -(-(B * (S // tb) * _local_rows(tb)) // tm) + N_EXPERTS
    meta_lanes = -(-(n_tiles + 1) // LANES) * LANES
    row = lambda a: a.reshape(1, -1)

    mod = _modulation(c, ada_w, ada_b)
    for l in range(L):
        bfc = jnp.pad(b_f[l], (0, LANES - ATT_HEADS)).reshape(1, LANES)
        poolw = _block_diag(pool_w[l]).astype(BF16)
        sgb = jnp.repeat(sg_b[l].T, SG_HD, axis=1)

        ya, yb, yc, q, k, v = _mixer_in(
            x, mod[l], row(mix_norm_g[l]), w_in, l, bfc, poolw, row(pool_scale[l]),
            row(sg_ln_g[l]), row(sg_ln_b[l]), sg_w[l], sgb, conv_w[l], row(conv_b[l]),
            row(conv_ln_g[l]), row(conv_ln_b[l]), tb=tb)
        yd = _attention(q, k, v, tq=tq)
        x = _merge(x, mod[l], row(mix_norm_g[l]), w_in, l, w_branch[l].astype(BF16), w_out[l].astype(BF16),
                   ya, yb, yc, yd, tb=tb)
        fnorm = row(ffn_norm_g[l])
        ls, lst, wt, blk_cnt, blk_carry, cnt = _route_call(
            x, mod[l], fnorm, router_w[l].T, router_bias[l].reshape(-1, 1), tb=tb)
        meta, base = _route_fin(cnt, tm=tm, meta_lanes=meta_lanes)
        xs = _dispatch(cnt[:, 0], base[:, 0], blk_cnt, blk_carry, ls, x, mod[l], fnorm,
                       tb=tb, tm=tm, n_slots=n_tiles * tm)
        ys = _experts(meta[0, :n_tiles], meta[0, meta_lanes - 1:], xs, exp_w1[l], exp_w3[l], exp_w2[l], tm=tm)
        x = _combine(base[:, 0], blk_cnt, blk_carry, x, mod[l], fnorm, lst, wt, shared_w1[l].astype(BF16),
                     shared_w3[l].astype(BF16), shared_w2[l].astype(BF16), row(final_norm_g), ys,
                     tb=tb, final_norm=(l == L - 1))
    return x
```

```python
import functools
import math

import jax
import jax.numpy as jnp
from jax import lax
from jax.experimental import pallas as pl
from jax.experimental.pallas import tpu as pltpu

F32 = jnp.float32
BF16 = jnp.bfloat16
HIGHEST = lax.Precision.HIGHEST

D_MODEL = 1024
BRANCH_W = 256
N_BRANCH = 4
POOL_WINDOWS = (2, 4, 8, 16)
POOL_GW = 64
POOL_TAIL = 16
SG_HEADS = 4
SG_CHUNK = 128
SG_HD = 64
CONV_WIDTH = 31
CONV_TAIL = 32
ATT_HEADS = 4
ATT_HD = 64
N_EXPERTS = 64
TOP_K = 6
N_GROUPS = 8
GROUP_SIZE = N_EXPERTS // N_GROUPS
TOPK_GROUPS = 4
EXPERT_FF = 256
SHARED_FF = 256
ROUTE_SCALE = 2.5
EPS = 1e-6
N_MOD = 6
MIX_COLS = 8 * BRANCH_W
LANES = 128
SUBLANES = 8
LOG2E = 1.4426950408889634
EXPERT_TILE = 256
VMEM_LIMIT = 56 * 1024 * 1024


def _sigmoid(x):
    return 1.0 / (1.0 + jnp.exp(-x))


def _silu(x):
    return x * _sigmoid(x)


def _rms_mod(x, g, scale, shift):
    y = x * lax.rsqrt(jnp.mean(x * x, axis=-1, keepdims=True) + EPS)
    return (y * g) * (1.0 + scale) + shift


def _layer_norm(x, g, b):
    mu = jnp.mean(x, axis=-1, keepdims=True)
    xc = x - mu
    var = jnp.mean(xc * xc, axis=-1, keepdims=True)
    return xc * lax.rsqrt(var + EPS) * g + b


def _params(*sem):
    return pltpu.CompilerParams(dimension_semantics=sem, vmem_limit_bytes=VMEM_LIMIT)


def _mod_kernel(c_ref, w_ref, b_ref, o_ref):
    c = c_ref[...]
    o_ref[0] = jnp.dot(_silu(c), w_ref[0], precision=HIGHEST, preferred_element_type=F32) + b_ref[0]


def _modulation(c, ada_w, ada_b):
    L, D, N = ada_w.shape
    B = c.shape[0]
    tn = 1536
    out = pl.pallas_call(
        _mod_kernel,
        grid=(L, N // tn),
        in_specs=[pl.BlockSpec((B, D), lambda l, j: (0, 0)),
                  pl.BlockSpec((1, D, tn), lambda l, j: (l, 0, j)),
                  pl.BlockSpec((1, 1, tn), lambda l, j: (l, 0, j))],
        out_specs=pl.BlockSpec((1, B, tn), lambda l, j: (l, 0, j)),
        out_shape=jax.ShapeDtypeStruct((L, B, N), F32),
        compiler_params=_params("arbitrary", "arbitrary"),
        name="modulation",
    )(c, ada_w, ada_b.reshape(L, 1, N))
    return out.reshape(L, B, N_MOD, D)


def _mixer_in_kernel(x_ref, mod_ref, g_ref, wmix_f32, wfc_f32, bfc_ref,
                     poolw_ref, pools_ref, sglg_ref, sglb_ref, sgw_ref, sgb_ref,
                     cw_ref, cb_ref, clg_ref, clb_ref,
                     ya_ref, yb_ref, yc_ref, q_ref, k_ref, v_ref,
                     pool_ext, conv_ext, cum_c, wmix_ref, wfc_ref, *, tb):
    j = pl.program_id(1)

    @pl.when((pl.program_id(0) == 0) & (j == 0))
    def _():
        wmix_ref[...] = wmix_f32[0].astype(BF16)
        wfc_ref[...] = wfc_f32[0].astype(BF16)

    @pl.when(j == 0)
    def _():
        pool_ext[0:POOL_TAIL, :] = jnp.zeros((POOL_TAIL, BRANCH_W), F32)
        conv_ext[0:CONV_TAIL, :] = jnp.zeros((CONV_TAIL, BRANCH_W), F32)
        cum_c[...] = jnp.zeros_like(cum_c)

    mod = mod_ref[0]
    h = _rms_mod(x_ref[0], g_ref[...], mod[1:2, :], mod[0:1, :])
    hb = h.astype(BF16)
    proj = jnp.dot(hb, wmix_ref[...], preferred_element_type=F32)

    lane = lax.broadcasted_iota(jnp.int32, (1, BRANCH_W), 1)
    row = lax.broadcasted_iota(jnp.int32, (tb, 1), 0)

    u = proj[:, 0:BRANCH_W]
    pool_ext[POOL_TAIL:POOL_TAIL + tb, :] = u
    ext = pool_ext[...]
    s2 = ext + pltpu.roll(ext, 1, 0)
    s4 = s2 + pltpu.roll(s2, 2, 0)
    s8 = s4 + pltpu.roll(s4, 4, 0)
    s16 = s8 + pltpu.roll(s8, 8, 0)
    grp = lane // POOL_GW
    wsum = jnp.where(grp == 0, s2, jnp.where(grp == 1, s4, jnp.where(grp == 2, s8, s16)))[POOL_TAIL:, :]
    win = jnp.where(grp == 0, 2.0, jnp.where(grp == 1, 4.0, jnp.where(grp == 2, 8.0, 16.0)))
    count = jnp.minimum((j * tb + row + 1).astype(F32), win)
    pooled = wsum / count - u
    ya = jnp.dot(pooled.astype(BF16), poolw_ref[...], preferred_element_type=F32) * pools_ref[...]
    ya_ref[0] = ya.astype(BF16)
    pool_ext[0:POOL_TAIL, :] = u[tb - POOL_TAIL:, :]

    z = proj[:, BRANCH_W:3 * BRANCH_W]
    z = 0.5 * z * (1.0 + jnp.tanh(math.sqrt(2.0 / math.pi) * (z + 0.044715 * (z * z * z))))
    su = z[:, 0:BRANCH_W]
    sv = _layer_norm(z[:, BRANCH_W:], sglg_ref[...], sglb_ref[...])
    r128 = lax.broadcasted_iota(jnp.int32, (SG_CHUNK, SG_CHUNK), 0)
    c128 = lax.broadcasted_iota(jnp.int32, (SG_CHUNK, SG_CHUNK), 1)
    wcat = jnp.concatenate(
        [jnp.where(r128 >= c128, sgw_ref[hh], 0.0) for hh in range(SG_HEADS)], axis=1).astype(BF16)
    head = lane // SG_HD
    for ci in range(tb // SG_CHUNK):
        rows = slice(ci * SG_CHUNK, (ci + 1) * SG_CHUNK)
        vch = sv[rows, :]
        vstack = jnp.concatenate(
            [jnp.where(head == hh, vch, 0.0) for hh in range(SG_HEADS)], axis=0).astype(BF16)
        s = jnp.dot(wcat, vstack, preferred_element_type=F32) + sgb_ref[...]
        yb_ref[0, rows, :] = (su[rows, :] * s).astype(BF16)

    glu = proj[:, 3 * BRANCH_W:4 * BRANCH_W] * _sigmoid(proj[:, 4 * BRANCH_W:5 * BRANCH_W])
    conv_ext[CONV_TAIL:CONV_TAIL + tb, :] = glu
    acc = jnp.zeros((tb, BRANCH_W), F32) + cb_ref[...]
    for kk in range(CONV_WIDTH):
        off = CONV_TAIL - (CONV_WIDTH - 1) + kk
        acc = acc + cw_ref[kk:kk + 1, :] * conv_ext[off:off + tb, :]
    yc_ref[0] = _silu(_layer_norm(acc, clg_ref[...], clb_ref[...])).astype(BF16)
    conv_ext[0:CONV_TAIL, :] = glu[tb - CONV_TAIL:, :]

    def log_sigmoid(t):
        return jnp.minimum(t, 0.0) - jnp.log(1.0 + jnp.exp(-jnp.abs(t)))

    rr = lax.broadcasted_iota(jnp.int32, (tb, tb), 0)
    cc = lax.broadcasted_iota(jnp.int32, (tb, tb), 1)
    lower = jnp.where(rr >= cc, 1.0, 0.0).astype(F32)
    lf_c = log_sigmoid(jnp.dot(hb, wfc_ref[...], preferred_element_type=F32) + bfc_ref[...])
    cs_c = jnp.dot(lower, lf_c, precision=HIGHEST, preferred_element_type=F32) + cum_c[...]
    cum_c[...] = cs_c[tb - 1:tb, :]
    cl2 = cs_c * LOG2E
    ln = lax.broadcasted_iota(jnp.int32, (1, LANES), 1)
    ones_q = jnp.where((ln >= ATT_HD + 3) & (ln < ATT_HD + 6), 1.0, 0.0)
    ones_k = jnp.where((ln >= ATT_HD) & (ln < ATT_HD + 3), 1.0, 0.0)
    ones_v = jnp.where(ln == ATT_HD, 1.0, 0.0)
    for hh in range(ATT_HEADS):
        pair = (hh // 2) * LANES
        qs = proj[:, 5 * BRANCH_W + pair:5 * BRANCH_W + pair + LANES] * (LOG2E / math.sqrt(ATT_HD))
        ks = proj[:, 6 * BRANCH_W + pair:6 * BRANCH_W + pair + LANES]
        vs = proj[:, 7 * BRANCH_W + pair:7 * BRANCH_W + pair + LANES]
        if hh % 2:
            qs, ks, vs = (pltpu.roll(a, ATT_HD, 1) for a in (qs, ks, vs))
        c = cl2[:, hh:hh + 1]
        hi = c.astype(BF16).astype(F32)
        mid = (c - hi).astype(BF16).astype(F32)
        lo = (c - hi) - mid
        q_extra = jnp.where(ln == ATT_HD, hi, jnp.where(ln == ATT_HD + 1, mid, jnp.where(ln == ATT_HD + 2, lo, ones_q)))
        k_extra = jnp.where(ln == ATT_HD + 3, -hi,
                            jnp.where(ln == ATT_HD + 4, -mid, jnp.where(ln == ATT_HD + 5, -lo, ones_k)))
        q_ref[0, hh, 0] = jnp.where(ln < ATT_HD, qs, q_extra).T.astype(BF16)
        k_ref[0, hh] = jnp.where(ln < ATT_HD, ks, k_extra).astype(BF16)
        v_ref[0, hh, 0] = jnp.where(ln < ATT_HD, vs, ones_v).T.astype(BF16)


def _w_in_cols(layer, first_col, ncols):
    return pl.BlockSpec((1, D_MODEL, ncols), lambda b, j: (layer, 0, first_col // ncols), pipeline_mode=pl.Buffered(1))


def _mixer_in(x, mod_l, norm_g, w_in, layer, bfc, poolw, pools, sglg, sglb, sgw, sgb, cw, cb, clg, clb, *, tb):
    B, S, D = x.shape
    full = lambda a: pl.BlockSpec(a.shape, lambda b, j: (0,) * a.ndim)
    tok = lambda w: pl.BlockSpec((1, tb, w), lambda b, j: (b, j, 0))
    head = pl.BlockSpec((1, ATT_HEADS, tb, LANES), lambda b, j: (b, 0, j, 0))
    head_t = pl.BlockSpec((1, ATT_HEADS, 1, LANES, tb), lambda b, j: (b, 0, j, 0, 0))
    consts = (bfc, poolw, pools, sglg, sglb, sgw, sgb, cw, cb, clg, clb)
    act = jax.ShapeDtypeStruct((B, S, BRANCH_W), BF16)
    att = jax.ShapeDtypeStruct((B, ATT_HEADS, S, LANES), BF16)
    att_t = jax.ShapeDtypeStruct((B, ATT_HEADS, S // tb, LANES, tb), BF16)
    return pl.pallas_call(
        functools.partial(_mixer_in_kernel, tb=tb),
        grid=(B, S // tb),
        in_specs=[tok(D), pl.BlockSpec((1, N_MOD, D), lambda b, j: (b, 0, 0)), full(norm_g),
                  _w_in_cols(layer, 0, MIX_COLS), _w_in_cols(layer, MIX_COLS, LANES)] + [full(a) for a in consts],
        out_specs=[tok(BRANCH_W)] * 3 + [head_t, head, head_t],
        out_shape=[act] * 3 + [att_t, att, att_t],
        scratch_shapes=[pltpu.VMEM((POOL_TAIL + tb, BRANCH_W), F32),
                        pltpu.VMEM((CONV_TAIL + tb, BRANCH_W), F32),
                        pltpu.VMEM((1, LANES), F32),
                        pltpu.VMEM((D, MIX_COLS), BF16), pltpu.VMEM((D, LANES), BF16)],
        compiler_params=_params("arbitrary", "arbitrary"),
        name="mixer_in",
    )(x, mod_l, norm_g, w_in, w_in, *consts)


def _attn_kernel(q_ref, k_ref, v_ref, o_ref, m_ref, acc_ref, *, tq):
    i = pl.program_id(1)
    key = lax.broadcasted_iota(jnp.int32, (tq, tq), 0)
    qry = lax.broadcasted_iota(jnp.int32, (tq, tq), 1)
    m_ref[...] = jnp.full(m_ref.shape, -jnp.inf, F32)
    acc_ref[...] = jnp.zeros(acc_ref.shape, F32)

    def block(kj, diagonal):
        ks = pl.multiple_of(kj * tq, tq)
        logits = [jnp.dot(k_ref[0, hh, pl.ds(ks, tq), :], q_ref[0, hh, 0], preferred_element_type=F32)
                  for hh in range(ATT_HEADS)]
        for hh in range(ATT_HEADS):
            s = logits[hh]
            if diagonal:
                s = jnp.where(key <= qry, s, -jnp.inf)
            m_old = m_ref[hh]
            m_new = jnp.maximum(m_old, jnp.max(s, axis=0, keepdims=True))
            p = jnp.exp2(s - m_new)
            pv = jnp.dot(v_ref[0, hh, kj], p.astype(BF16), preferred_element_type=F32)
            acc_ref[hh] = jnp.exp2(m_old - m_new) * acc_ref[hh] + pv
            m_ref[hh] = m_new

    def body(kj, carry):
        block(kj, False)
        return carry

    lax.fori_loop(0, i, body, 0)
    block(i, True)

    ln = lax.broadcasted_iota(jnp.int32, (1, LANES), 1)
    for pr in range(ATT_HEADS // 2):
        o = []
        for hh in (2 * pr, 2 * pr + 1):
            acc = acc_ref[hh]
            o.append((acc / acc[ATT_HD:ATT_HD + 1, :]).T)
        o_ref[0, :, pr * LANES:(pr + 1) * LANES] = jnp.where(ln < ATT_HD, o[0], pltpu.roll(o[1], ATT_HD, 1)).astype(BF16)


def _attention(q_t, k, v_t, *, tq):
    B, H, S, W = k.shape
    nblk = S // tq
    assert q_t.shape == (B, H, nblk, W, tq) and v_t.shape == q_t.shape
    return pl.pallas_call(
        functools.partial(_attn_kernel, tq=tq),
        grid=(B, nblk),
        in_specs=[pl.BlockSpec((1, H, 1, W, tq), lambda b, i: (b, 0, i, 0, 0)),
                  pl.BlockSpec((1, H, S, W), lambda b, i: (b, 0, 0, 0), pipeline_mode=pl.Buffered(1)),
                  pl.BlockSpec((1, H, nblk, W, tq), lambda b, i: (b, 0, 0, 0, 0), pipeline_mode=pl.Buffered(1))],
        out_specs=pl.BlockSpec((1, tq, BRANCH_W), lambda b, i: (b, i, 0)),
        out_shape=jax.ShapeDtypeStruct((B, S, BRANCH_W), BF16),
        scratch_shapes=[pltpu.VMEM((H, 1, tq), F32), pltpu.VMEM((H, W, tq), F32)],
        compiler_params=_params("arbitrary", "arbitrary"),
        name="attention",
    )(q_t, k, v_t)


def _merge_kernel(x_ref, mod_ref, g_ref, wa_f32, wb_f32, wt_f32, wbr_ref, wout_ref, ya_ref, yb_ref, yc_ref, yd_ref,
                  o_ref, wg_ref):
    @pl.when((pl.program_id(0) == 0) & (pl.program_id(1) == 0))
    def _():
        shift, wide = GATE_SHIFT, D_MODEL + LANES
        windows = (wa_f32[0, :, 0:wide],
                   jnp.concatenate([wa_f32[0, :, D_MODEL:], wb_f32[0, :, 0:LANES]], axis=1),
                   wb_f32[0, :, 0:wide],
                   jnp.concatenate([wb_f32[0, :, D_MODEL:], wt_f32[0]], axis=1))
        for n, win in enumerate(windows):
            wg_ref[:, n * D_MODEL:(n + 1) * D_MODEL] = pltpu.roll(win, wide - shift, 1)[:, :D_MODEL].astype(BF16)

    x = x_ref[0]
    mod = mod_ref[0]
    hb = _rms_mod(x, g_ref[...], mod[1:2, :], mod[0:1, :]).astype(BF16)
    merged = jnp.zeros(x.shape, F32)
    for n, y_ref in enumerate((ya_ref, yb_ref, yc_ref, yd_ref)):
        gate = _sigmoid(jnp.dot(hb, wg_ref[:, n * D_MODEL:(n + 1) * D_MODEL], preferred_element_type=F32))
        merged = merged + gate * jnp.dot(y_ref[0], wbr_ref[n], preferred_element_type=F32)
    out = jnp.dot(merged.astype(BF16), wout_ref[...], preferred_element_type=F32)
    o_ref[0] = x + mod[2:3, :] * out


GATE_SHIFT = ATT_HEADS


def _merge(x, mod_l, norm_g, w_in, layer, wbranch, wout, ya, yb, yc, yd, *, tb):
    B, S, D = x.shape
    full = lambda a: pl.BlockSpec(a.shape, lambda b, j: (0,) * a.ndim)
    tok = lambda w: pl.BlockSpec((1, tb, w), lambda b, j: (b, j, 0))
    assert w_in.shape[2] == MIX_COLS + GATE_SHIFT + N_BRANCH * D
    return pl.pallas_call(
        _merge_kernel,
        grid=(B, S // tb),
        in_specs=[tok(D), pl.BlockSpec((1, N_MOD, D), lambda b, j: (b, 0, 0)), full(norm_g),
                  _w_in_cols(layer, MIX_COLS, 2 * D), _w_in_cols(layer, MIX_COLS + 2 * D, 2 * D),
                  _w_in_cols(layer, MIX_COLS + 4 * D, LANES),
                  full(wbranch), full(wout)] + [tok(BRANCH_W)] * 4,
        out_specs=tok(D),
        out_shape=jax.ShapeDtypeStruct((B, S, D), F32),
        scratch_shapes=[pltpu.VMEM((D, N_BRANCH * D), BF16)],
        compiler_params=_params("arbitrary", "arbitrary"),
        name="merge",
    )(x, mod_l, norm_g, w_in, w_in, w_in, wbranch, wout, ya, yb, yc, yd)


def _route(scores_t, bias_t):
    E, n = scores_t.shape
    sel = scores_t + bias_t
    eidx = lax.broadcasted_iota(jnp.int32, (E, n), 0)
    neg = jnp.full((E, n), -jnp.inf, F32)
    gscore = []
    sub = lax.broadcasted_iota(jnp.int32, (GROUP_SIZE, n), 0)
    for g in range(N_GROUPS):
        blk = sel[g * GROUP_SIZE:(g + 1) * GROUP_SIZE, :]
        m1 = jnp.max(blk, axis=0, keepdims=True)
        first = jnp.min(jnp.where(blk == m1, sub, GROUP_SIZE), axis=0, keepdims=True)
        m2 = jnp.max(jnp.where(sub == first, -jnp.inf, blk), axis=0, keepdims=True)
        gscore.append(m1 + m2)
    emask = []
    for g in range(N_GROUPS):
        beaten = jnp.zeros((1, n), jnp.int32)
        for g2 in range(N_GROUPS):
            if g2 == g:
                continue
            wins = (gscore[g2] > gscore[g]) | ((gscore[g2] == gscore[g]) & (g2 < g))
            beaten = beaten + wins.astype(jnp.int32)
        emask.append(jnp.broadcast_to(beaten < TOPK_GROUPS, (GROUP_SIZE, n)))
    cur = jnp.where(jnp.concatenate(emask, axis=0), sel, neg)
    chosen = jnp.zeros((E, n), jnp.bool_)
    firsts = []
    for _ in range(TOP_K):
        m = jnp.max(cur, axis=0, keepdims=True)
        first = jnp.min(jnp.where(cur == m, eidx, E), axis=0, keepdims=True)
        hit = eidx == first
        chosen = chosen | hit
        cur = jnp.where(hit, neg, cur)
        firsts.append(first)
    w = jnp.where(chosen, scores_t, 0.0)
    return w / jnp.sum(w, axis=0, keepdims=True) * ROUTE_SCALE, chosen, firsts


def _route_kernel(x_ref, mod_ref, g_ref, rwt_ref, rb_ref, ls_ref, lst_ref, wt_ref, bc_ref, cr_ref, cnt_ref, count):
    first_step = (pl.program_id(0) == 0) & (pl.program_id(1) == 0)

    @pl.when(first_step)
    def _():
        count[...] = jnp.zeros_like(count)

    mod = mod_ref[0]
    h = _rms_mod(x_ref[0], g_ref[...], mod[4:5, :], mod[3:4, :])
    logits_t = lax.dot_general(rwt_ref[...], h, (((1,), (1,)), ((), ())),
                               precision=HIGHEST, preferred_element_type=F32)
    gates_t, chosen, firsts = _route(_sigmoid(logits_t), rb_ref[...])
    E, tb = gates_t.shape
    ones = jnp.where(chosen, 1.0, 0.0)
    rr = lax.broadcasted_iota(jnp.int32, (tb, tb), 0)
    cc = lax.broadcasted_iota(jnp.int32, (tb, tb), 1)
    upper = jnp.where(rr <= cc, 1.0, 0.0).astype(BF16)
    incl = jnp.dot(ones.astype(BF16), upper, preferred_element_type=F32)
    block_count = jnp.sum(ones, axis=1, keepdims=True)
    block_count = block_count + (block_count.astype(jnp.int32) & 1).astype(F32)
    er = lax.broadcasted_iota(jnp.int32, (E, E), 0)
    ec = lax.broadcasted_iota(jnp.int32, (E, E), 1)
    before = jnp.where(ec < er, 1.0, 0.0).astype(F32)
    local_base = jnp.dot(before, jnp.broadcast_to(block_count, (E, LANES)), precision=HIGHEST,
                         preferred_element_type=F32)[:, 0:1]
    row_all = local_base + (incl - ones)
    eidx = lax.broadcasted_iota(jnp.int32, (E, tb), 0)
    ls_rows = [jnp.sum(jnp.where(eidx == f, row_all, 0.0), axis=0, keepdims=True) for f in firsts]
    w_rows = [jnp.sum(jnp.where(eidx == f, gates_t, 0.0), axis=0, keepdims=True) for f in firsts]
    ls_ref[...] = jnp.concatenate(ls_rows + [jnp.zeros((SUBLANES - TOP_K, tb), F32)], axis=0).astype(jnp.int32)
    zpad = jnp.zeros((LANES - TOP_K, tb), F32)
    lst_ref[0] = jnp.concatenate(ls_rows + [zpad], axis=0).T.astype(jnp.int32)
    wt_ref[0] = jnp.concatenate(w_rows + [zpad], axis=0).T

    def as_row(col):
        sel = lax.broadcasted_iota(jnp.int32, (E, LANES), 0) == lax.broadcasted_iota(jnp.int32, (E, LANES), 1)
        return jnp.sum(jnp.where(sel, col, 0.0), axis=0, keepdims=True).astype(jnp.int32)

    bc_ref[0] = as_row(block_count)
    cr_ref[0] = as_row(count[...])
    count[...] = count[...] + block_count
    cnt_ref[...] = jnp.broadcast_to(count[...], cnt_ref.shape).astype(jnp.int32)


def _route_call(x, mod_l, norm_g, router_wt, router_b, *, tb):
    B, S, D = x.shape
    nj = S // tb
    full = lambda a: pl.BlockSpec(a.shape, lambda b, j: (0,) * a.ndim)
    tok = lambda w: pl.BlockSpec((1, tb, w), lambda b, j: (b, j, 0))
    kt = pl.BlockSpec((SUBLANES, tb), lambda b, j: (0, b * nj + j))
    per_block = pl.BlockSpec((1, 1, LANES), lambda b, j: (b * nj + j, 0, 0))
    return pl.pallas_call(
        _route_kernel,
        grid=(B, nj),
        in_specs=[tok(D), pl.BlockSpec((1, N_MOD, D), lambda b, j: (b, 0, 0)),
                  full(norm_g), full(router_wt), full(router_b)],
        out_specs=[kt, tok(LANES), tok(LANES), per_block, per_block,
                   pl.BlockSpec((N_EXPERTS, LANES), lambda b, j: (0, 0))],
        out_shape=[jax.ShapeDtypeStruct((SUBLANES, B * S), jnp.int32),
                   jax.ShapeDtypeStruct((B, S, LANES), jnp.int32),
                   jax.ShapeDtypeStruct((B, S, LANES), F32),
                   jax.ShapeDtypeStruct((B * nj, 1, LANES), jnp.int32),
                   jax.ShapeDtypeStruct((B * nj, 1, LANES), jnp.int32),
                   jax.ShapeDtypeStruct((N_EXPERTS, LANES), jnp.int32)],
        scratch_shapes=[pltpu.VMEM((N_EXPERTS, 1), F32)],
        compiler_params=_params("arbitrary", "arbitrary"),
        name="route",
    )(x, mod_l, norm_g, router_wt, router_b)


def _route_fin_kernel(cnt_ref, meta_ref, base_ref, *, tm):
    cnt = cnt_ref[...]
    ntile = lax.shift_right_logical(cnt + (tm - 1), int(math.log2(tm)))
    er = lax.broadcasted_iota(jnp.int32, (N_EXPERTS, N_EXPERTS), 0)
    ec = lax.broadcasted_iota(jnp.int32, (N_EXPERTS, N_EXPERTS), 1)
    before = jnp.where(ec < er, 1.0, 0.0).astype(F32)
    start = jnp.dot(before, ntile.astype(F32), precision=HIGHEST, preferred_element_type=F32).astype(jnp.int32)
    base = start * tm
    base_ref[...] = base
    nlane = meta_ref.shape[1]
    lane = lax.broadcasted_iota(jnp.int32, (1, nlane), 1)
    end = (start + ntile)[:, 0:1]
    tile_expert = jnp.sum(jnp.where(end <= lane, 1, 0), axis=0, keepdims=True)
    tile_expert = jnp.minimum(tile_expert, N_EXPERTS - 1)
    meta_ref[...] = jnp.where(lane == nlane - 1, end[N_EXPERTS - 1:N_EXPERTS, :], tile_expert)


def _route_fin(cnt, *, tm, meta_lanes):
    full = lambda a: pl.BlockSpec(a.shape, lambda i: (0,) * a.ndim)
    outs = [jax.ShapeDtypeStruct((1, meta_lanes), jnp.int32), jax.ShapeDtypeStruct(cnt.shape, jnp.int32)]
    return pl.pallas_call(
        functools.partial(_route_fin_kernel, tm=tm),
        grid=(1,),
        in_specs=[full(cnt)],
        out_specs=[full(o) for o in outs],
        out_shape=outs,
        compiler_params=_params("arbitrary"),
        name="route_fin",
    )(cnt)


HALF = D_MODEL // 2
SLAB = HALF // LANES
HIGH_BITS = 0xFFFF0000


def _to_slabs(ref, value):
    n = value.shape[0]
    bits = lax.bitcast_convert_type(value.astype(BF16).astype(F32), jnp.uint32)
    packed = (bits[:, HALF:] & jnp.uint32(HIGH_BITS)) | (bits[:, :HALF] >> 16)
    for s in range(SLAB):
        ref[pl.ds(s, n, stride=SLAB), :] = packed[:, s * LANES:(s + 1) * LANES]


def _from_slabs(ref, n):
    packed = jnp.concatenate([ref[pl.ds(s, n, stride=SLAB), :] for s in range(SLAB)], axis=1)
    low = lax.bitcast_convert_type(packed << 16, F32).astype(BF16)
    high = lax.bitcast_convert_type(packed & jnp.uint32(HIGH_BITS), F32).astype(BF16)
    return low, high


def _for_expert_runs(base_sm, bc_sm, cr_sm, max_run, fn):
    def per_expert(e, local_row):
        n = bc_sm[0, 0, e]
        _for_pieces(n, max_run, lambda done, size: fn(local_row + done, base_sm[e] + cr_sm[0, 0, e] + done, size))
        return local_row + n

    lax.fori_loop(0, N_EXPERTS, per_expert, 0)


def _for_pieces(n, max_piece, fn):
    done = 0
    bit = max_piece
    while bit >= 2:
        take = n & bit

        @pl.when(take != 0)
        def _(done=done, bit=bit):
            fn(done, bit)

        done = done + take
        bit //= 2


ROW_GROUPS = 7


def _local_rows(tb):
    rows = tb * TOP_K + N_EXPERTS
    assert rows % (ROW_GROUPS * 2 * SUBLANES) == 0, rows
    return rows


def _queue(nrows):
    return int(math.log2(nrows)) % 2


def _slab_rows(ref, row, nrows):
    return ref.at[pl.ds(pl.multiple_of(row * SLAB, 2 * SLAB), nrows * SLAB)]


def _dispatch_kernel(cnt_sm, base_sm, bc_sm, cr_sm, bc_prev, cr_prev, ls_ref, x_ref, mod_ref, g_ref, xs_hbm,
                     stage, zeros, sem, zsem, *, tb, tm):
    step = pl.program_id(0) * pl.num_programs(1) + pl.program_id(1)
    last_step = pl.num_programs(0) * pl.num_programs(1) - 1
    cur = lax.rem(step, 2)
    mod = mod_ref[0]
    hb = _rms_mod(x_ref[0], g_ref[...], mod[4:5, :], mod[3:4, :]).astype(BF16)
    ls = ls_ref[...]
    gr = _local_rows(tb) // ROW_GROUPS
    for grp in range(ROW_GROUPS):
        row = grp * gr + lax.broadcasted_iota(jnp.int32, (gr, tb), 0)
        hit = ls[0:1, :] == row
        for k in range(1, TOP_K):
            hit = hit | (ls[k:k + 1, :] == row)
        perm = jnp.where(hit, 1.0, 0.0).astype(BF16)
        _to_slabs(stage.at[cur, pl.ds(grp * gr * SLAB, gr * SLAB)], jnp.dot(perm, hb, preferred_element_type=F32))

    def run_copy(buf):
        return lambda local_row, sorted_row, nrows: pltpu.make_async_copy(
            _slab_rows(stage.at[buf], local_row, nrows), _slab_rows(xs_hbm, sorted_row, nrows), sem.at[buf])

    _for_expert_runs(base_sm, bc_sm, cr_sm, tb, lambda *a: run_copy(cur)(*a).start(priority=_queue(a[2])))

    @pl.when(step == 0)
    def _():
        zeros[...] = jnp.zeros_like(zeros)

        def per_expert(e, carry):
            n = cnt_sm[e]
            npad = lax.rem(tm - lax.rem(n, tm), tm)
            zcopy = lambda done, size: pltpu.make_async_copy(
                _slab_rows(zeros, 0, size), _slab_rows(xs_hbm, base_sm[e] + n + done, size), zsem)
            _for_pieces(npad, tm // 2, lambda *a: zcopy(*a).start())
            _for_pieces(npad, tm // 2, lambda *a: zcopy(*a).wait())
            return carry

        lax.fori_loop(0, N_EXPERTS, per_expert, 0)

    @pl.when(step > 0)
    def _():
        _for_expert_runs(base_sm, bc_prev, cr_prev, tb, lambda *a: run_copy(1 - cur)(*a).wait())

    @pl.when(step == last_step)
    def _():
        _for_expert_runs(base_sm, bc_sm, cr_sm, tb, lambda *a: run_copy(cur)(*a).wait())


def _dispatch(cnt, base, blk_cnt, blk_carry, ls, x, mod_l, norm_g, *, tb, tm, n_slots):
    B, S, D = x.shape
    nj = S // tb
    smem_block = lambda shift: pl.BlockSpec(
        (1, 1, LANES), lambda b, j, *_: (jnp.maximum(b * nj + j + shift, 0), 0, 0), memory_space=pltpu.SMEM)
    grid_spec = pltpu.PrefetchScalarGridSpec(
        num_scalar_prefetch=2,
        grid=(B, nj),
        in_specs=[smem_block(0), smem_block(0), smem_block(-1), smem_block(-1),
                  pl.BlockSpec((SUBLANES, tb), lambda b, j, *_: (0, b * nj + j)),
                  pl.BlockSpec((1, tb, D), lambda b, j, *_: (b, j, 0)),
                  pl.BlockSpec((1, N_MOD, D), lambda b, j, *_: (b, 0, 0)),
                  pl.BlockSpec(norm_g.shape, lambda b, j, *_: (0, 0))],
        out_specs=pl.BlockSpec(memory_space=pltpu.HBM),
        scratch_shapes=[pltpu.VMEM((2, _local_rows(tb) * SLAB, LANES), jnp.uint32),
                        pltpu.VMEM((tm // 2 * SLAB, LANES), jnp.uint32),
                        pltpu.SemaphoreType.DMA((2,)), pltpu.SemaphoreType.DMA])
    return pl.pallas_call(
        functools.partial(_dispatch_kernel, tb=tb, tm=tm),
        grid_spec=grid_spec,
        out_shape=jax.ShapeDtypeStruct((n_slots * SLAB, LANES), jnp.uint32),
        compiler_params=_params("arbitrary", "arbitrary"),
        name="dispatch",
    )(cnt, base, blk_cnt, blk_carry, blk_cnt, blk_carry, ls, x, mod_l, norm_g)


def _experts_kernel(te_sm, nt_sm, x_ref, w1_ref, w3_ref, w2_ref, y_ref, w1b, w3b, w2b, *, tm):
    i = pl.program_id(0)
    in_use = i < nt_sm[0]

    @pl.when(in_use)
    def _():
        @pl.when((i == 0) | (te_sm[i] != te_sm[jnp.maximum(i - 1, 0)]))
        def _():
            w1b[...] = w1_ref[0].astype(BF16)
            w3b[...] = w3_ref[0].astype(BF16)
            w2b[...] = w2_ref[0].astype(BF16)

        x_lo, x_hi = _from_slabs(x_ref, tm)
        up = lambda w: (jnp.dot(x_lo, w[:HALF, :], preferred_element_type=F32)
                        + jnp.dot(x_hi, w[HALF:, :], preferred_element_type=F32))
        hid = _silu(up(w1b)) * up(w3b)
        _to_slabs(y_ref, jnp.dot(hid.astype(BF16), w2b[...], preferred_element_type=F32))

    @pl.when(jnp.logical_not(in_use))
    def _():
        y_ref[...] = jnp.zeros_like(y_ref)


def _experts(tile_expert, n_tiles, xs, w1, w3, w2, layer, *, tm):
    _, E, D, F = w1.shape
    n_slots = xs.shape[0] // SLAB
    last = lambda i, te, nt: jnp.minimum(i, nt[0] - 1)
    grid_spec = pltpu.PrefetchScalarGridSpec(
        num_scalar_prefetch=2,
        grid=(n_slots // tm,),
        in_specs=[pl.BlockSpec((tm * SLAB, LANES), lambda i, te, nt: (last(i, te, nt), 0)),
                  pl.BlockSpec((None, 1, D, F), lambda i, te, nt: (layer, te[last(i, te, nt)], 0, 0)),
                  pl.BlockSpec((None, 1, D, F), lambda i, te, nt: (layer, te[last(i, te, nt)], 0, 0)),
                  pl.BlockSpec((None, 1, F, D), lambda i, te, nt: (layer, te[last(i, te, nt)], 0, 0))],
        out_specs=pl.BlockSpec((tm * SLAB, LANES), lambda i, te, nt: (i, 0)),
        scratch_shapes=[pltpu.VMEM((D, F), BF16), pltpu.VMEM((D, F), BF16), pltpu.VMEM((F, D), BF16)])
    return pl.pallas_call(
        functools.partial(_experts_kernel, tm=tm),
        grid_spec=grid_spec,
        out_shape=jax.ShapeDtypeStruct(xs.shape, jnp.uint32),
        compiler_params=_params("arbitrary"),
        name="experts",
    )(tile_expert, n_tiles, xs, w1, w3, w2)


def _combine_kernel(base_sm, bc_sm, cr_sm, bc_next, cr_next, x_ref, mod_ref, g_ref, lst_ref, wt_ref,
                    sw1_ref, sw3_ref, sw2_ref, fg_ref, ys_hbm, o_ref, stage, sem, *, tb, final_norm):
    step = pl.program_id(0) * pl.num_programs(1) + pl.program_id(1)
    last_step = pl.num_programs(0) * pl.num_programs(1) - 1
    cur = lax.rem(step, 2)

    def run_copy(buf):
        return lambda local_row, sorted_row, nrows: pltpu.make_async_copy(
            _slab_rows(ys_hbm, sorted_row, nrows), _slab_rows(stage.at[buf], local_row, nrows), sem.at[buf])

    @pl.when(step == 0)
    def _():
        stage[...] = jnp.zeros_like(stage)
        _for_expert_runs(base_sm, bc_sm, cr_sm, tb, lambda *a: run_copy(cur)(*a).start(priority=_queue(a[2])))

    @pl.when(step < last_step)
    def _():
        _for_expert_runs(base_sm, bc_next, cr_next, tb, lambda *a: run_copy(1 - cur)(*a).start(priority=_queue(a[2])))

    x = x_ref[0]
    mod = mod_ref[0]
    hb = _rms_mod(x, g_ref[...], mod[4:5, :], mod[3:4, :]).astype(BF16)
    a = jnp.dot(hb, sw1_ref[...], preferred_element_type=F32)
    b = jnp.dot(hb, sw3_ref[...], preferred_element_type=F32)
    acc = jnp.dot((_silu(a) * b).astype(BF16), sw2_ref[...], preferred_element_type=F32)
    wt = wt_ref[0]
    lst = lst_ref[0]
    _for_expert_runs(base_sm, bc_sm, cr_sm, tb, lambda *a: run_copy(cur)(*a).wait())
    gr = _local_rows(tb) // ROW_GROUPS
    routed = [jnp.zeros((tb, HALF), F32)] * 2
    for grp in range(ROW_GROUPS):
        halves = _from_slabs(stage.at[cur, pl.ds(grp * gr * SLAB, gr * SLAB)], gr)
        row = grp * gr + lax.broadcasted_iota(jnp.int32, (1, gr), 1)
        gate = jnp.zeros((tb, gr), F32)
        for k in range(TOP_K):
            gate = gate + jnp.where(lst[:, k:k + 1] == row, wt[:, k:k + 1], 0.0)
        gate = gate.astype(BF16)
        routed = [r + jnp.dot(gate, yg, preferred_element_type=F32) for r, yg in zip(routed, halves)]
    acc = acc + jnp.concatenate(routed, axis=1)
    y = x + mod[5:6, :] * acc
    if final_norm:
        y = y * lax.rsqrt(jnp.mean(y * y, axis=-1, keepdims=True) + EPS) * fg_ref[...]
    o_ref[0] = y


def _combine(base, blk_cnt, blk_carry, x, mod_l, norm_g, lst, wt, sw1, sw3, sw2, final_g, ys, *, tb, final_norm):
    B, S, D = x.shape
    nj = S // tb
    full = lambda a: pl.BlockSpec(a.shape, lambda b, j, *_: (0,) * a.ndim)
    tok = lambda w: pl.BlockSpec((1, tb, w), lambda b, j, *_: (b, j, 0))
    smem_block = lambda shift: pl.BlockSpec(
        (1, 1, LANES), lambda b, j, *_: (jnp.minimum(b * nj + j + shift, B * nj - 1), 0, 0), memory_space=pltpu.SMEM)
    grid_spec = pltpu.PrefetchScalarGridSpec(
        num_scalar_prefetch=1,
        grid=(B, nj),
        in_specs=[smem_block(0), smem_block(0), smem_block(1), smem_block(1), tok(D),
                  pl.BlockSpec((1, N_MOD, D), lambda b, j, *_: (b, 0, 0)),
                  full(norm_g), tok(LANES), tok(LANES), full(sw1), full(sw3), full(sw2), full(final_g),
                  pl.BlockSpec(memory_space=pltpu.HBM)],
        out_specs=tok(D),
        scratch_shapes=[pltpu.VMEM((2, _local_rows(tb) * SLAB, LANES), jnp.uint32), pltpu.SemaphoreType.DMA((2,))])
    return pl.pallas_call(
        functools.partial(_combine_kernel, tb=tb, final_norm=final_norm),
        grid_spec=grid_spec,
        out_shape=jax.ShapeDtypeStruct((B, S, D), F32),
        compiler_params=_params("arbitrary", "arbitrary"),
        name="combine",
    )(base, blk_cnt, blk_carry, blk_cnt, blk_carry, x, mod_l, norm_g, lst, wt, sw1, sw3, sw2, final_g, ys)


def _block_diag(w):
    G, a, b = w.shape
    out = jnp.zeros((G * a, G * b), w.dtype)
    for g in range(G):
        out = out.at[g * a:(g + 1) * a, g * b:(g + 1) * b].set(w[g])
    return out


def kernel(x, c, w_in, b_f, pool_w, pool_scale, sg_ln_g, sg_ln_b, sg_w, sg_b, conv_w, conv_b, conv_ln_g,
           conv_ln_b, w_branch, w_out, mix_norm_g, ffn_norm_g, ada_w, ada_b, router_w, router_bias,
           exp_w1, exp_w3, exp_w2, shared_w1, shared_w3, shared_w2, final_norm_g):
    B, S, D = x.shape
    L = w_in.shape[0]
    tb = min(512, S)
    tq = tb
    tm = EXPERT_TILE
    n_tiles = -(-(B * (S // tb) * _local_rows(tb)) // tm) + N_EXPERTS
    meta_lanes = -(-(n_tiles + 1) // LANES) * LANES
    row = lambda a: a.reshape(1, -1)

    mod = _modulation(c, ada_w, ada_b)
    for l in range(L):
        bfc = jnp.pad(b_f[l], (0, LANES - ATT_HEADS)).reshape(1, LANES)
        poolw = _block_diag(pool_w[l]).astype(BF16)
        sgb = jnp.repeat(sg_b[l].T, SG_HD, axis=1)

        ya, yb, yc, q, k, v = _mixer_in(
            x, mod[l], row(mix_norm_g[l]), w_in, l, bfc, poolw, row(pool_scale[l]),
            row(sg_ln_g[l]), row(sg_ln_b[l]), sg_w[l], sgb, conv_w[l], row(conv_b[l]),
            row(conv_ln_g[l]), row(conv_ln_b[l]), tb=tb)
        yd = _attention(q, k, v, tq=tq)
        x = _merge(x, mod[l], row(mix_norm_g[l]), w_in, l, w_branch[l].astype(BF16), w_out[l].astype(BF16),
                   ya, yb, yc, yd, tb=tb)
        fnorm = row(ffn_norm_g[l])
        ls, lst, wt, blk_cnt, blk_carry, cnt = _route_call(
            x, mod[l], fnorm, router_w[l].T, router_bias[l].reshape(-1, 1), tb=tb)
        meta, base = _route_fin(cnt, tm=tm, meta_lanes=meta_lanes)
        xs = _dispatch(cnt[:, 0], base[:, 0], blk_cnt, blk_carry, ls, x, mod[l], fnorm,
                       tb=tb, tm=tm, n_slots=n_tiles * tm)
        ys = _experts(meta[0, :n_tiles], meta[0, meta_lanes - 1:], xs, exp_w1, exp_w3, exp_w2, l, tm=tm)
        x = _combine(base[:, 0], blk_cnt, blk_carry, x, mod[l], fnorm, lst, wt, shared_w1[l].astype(BF16),
                     shared_w3[l].astype(BF16), shared_w2[l].astype(BF16), row(final_norm_g), ys,
                     tb=tb, final_norm=(l == L - 1))
    return x
```

```python
import functools
import math

import jax
import jax.numpy as jnp
from jax import lax
from jax.experimental import pallas as pl
from jax.experimental.pallas import tpu as pltpu

F32 = jnp.float32
BF16 = jnp.bfloat16
HIGHEST = lax.Precision.HIGHEST

D_MODEL = 1024
BRANCH_W = 256
N_BRANCH = 4
POOL_WINDOWS = (2, 4, 8, 16)
POOL_GW = 64
POOL_TAIL = 16
SG_HEADS = 4
SG_CHUNK = 128
SG_HD = 64
CONV_WIDTH = 31
CONV_TAIL = 32
ATT_HEADS = 4
ATT_HD = 64
N_EXPERTS = 64
TOP_K = 6
N_GROUPS = 8
GROUP_SIZE = N_EXPERTS // N_GROUPS
TOPK_GROUPS = 4
EXPERT_FF = 256
SHARED_FF = 256
ROUTE_SCALE = 2.5
EPS = 1e-6
N_MOD = 6
MIX_COLS = 8 * BRANCH_W
LANES = 128
SUBLANES = 8
LOG2E = 1.4426950408889634
EXPERT_TILE = 512
EXPERT_CHUNK = 256
VMEM_LIMIT = 56 * 1024 * 1024


def _sigmoid(x):
    return 1.0 / (1.0 + jnp.exp(-x))


def _silu(x):
    return x * _sigmoid(x)


def _rms_mod(x, g, scale, shift):
    y = x * lax.rsqrt(jnp.mean(x * x, axis=-1, keepdims=True) + EPS)
    return (y * g) * (1.0 + scale) + shift


def _layer_norm(x, g, b):
    mu = jnp.mean(x, axis=-1, keepdims=True)
    xc = x - mu
    var = jnp.mean(xc * xc, axis=-1, keepdims=True)
    return xc * lax.rsqrt(var + EPS) * g + b


def _params(*sem):
    return pltpu.CompilerParams(dimension_semantics=sem, vmem_limit_bytes=VMEM_LIMIT)


def _mod_kernel(c_ref, w_ref, b_ref, o_ref):
    c = c_ref[...]
    o_ref[0] = jnp.dot(_silu(c), w_ref[0], precision=HIGHEST, preferred_element_type=F32) + b_ref[0]


def _modulation(c, ada_w, ada_b):
    L, D, N = ada_w.shape
    B = c.shape[0]
    tn = 1536
    out = pl.pallas_call(
        _mod_kernel,
        grid=(L, N // tn),
        in_specs=[pl.BlockSpec((B, D), lambda l, j: (0, 0)),
                  pl.BlockSpec((1, D, tn), lambda l, j: (l, 0, j)),
                  pl.BlockSpec((1, 1, tn), lambda l, j: (l, 0, j))],
        out_specs=pl.BlockSpec((1, B, tn), lambda l, j: (l, 0, j)),
        out_shape=jax.ShapeDtypeStruct((L, B, N), F32),
        compiler_params=_params("arbitrary", "arbitrary"),
        name="modulation",
    )(c, ada_w, ada_b.reshape(L, 1, N))
    return out.reshape(L, B, N_MOD, D)


def _mixer_in_kernel(x_ref, mod_ref, g_ref, wmix_f32, wfc_f32, bfc_ref,
                     poolw_ref, pools_ref, sglg_ref, sglb_ref, sgw_ref, sgb_ref,
                     cw_ref, cb_ref, clg_ref, clb_ref,
                     ya_ref, yb_ref, yc_ref, q_ref, k_ref, v_ref,
                     pool_ext, conv_ext, cum_c, wmix_ref, wfc_ref, *, tb):
    j = pl.program_id(1)

    @pl.when((pl.program_id(0) == 0) & (j == 0))
    def _():
        wmix_ref[...] = wmix_f32[0].astype(BF16)
        wfc_ref[...] = wfc_f32[0].astype(BF16)

    @pl.when(j == 0)
    def _():
        pool_ext[0:POOL_TAIL, :] = jnp.zeros((POOL_TAIL, BRANCH_W), F32)
        conv_ext[0:CONV_TAIL, :] = jnp.zeros((CONV_TAIL, BRANCH_W), F32)
        cum_c[...] = jnp.zeros_like(cum_c)

    mod = mod_ref[0]
    h = _rms_mod(x_ref[0], g_ref[...], mod[1:2, :], mod[0:1, :])
    hb = h.astype(BF16)
    proj = jnp.dot(hb, wmix_ref[...], preferred_element_type=F32)

    lane = lax.broadcasted_iota(jnp.int32, (1, BRANCH_W), 1)
    row = lax.broadcasted_iota(jnp.int32, (tb, 1), 0)

    u = proj[:, 0:BRANCH_W]
    pool_ext[POOL_TAIL:POOL_TAIL + tb, :] = u
    ext = pool_ext[...]
    s2 = ext + pltpu.roll(ext, 1, 0)
    s4 = s2 + pltpu.roll(s2, 2, 0)
    s8 = s4 + pltpu.roll(s4, 4, 0)
    s16 = s8 + pltpu.roll(s8, 8, 0)
    grp = lane // POOL_GW
    wsum = jnp.where(grp == 0, s2, jnp.where(grp == 1, s4, jnp.where(grp == 2, s8, s16)))[POOL_TAIL:, :]
    win = jnp.where(grp == 0, 2.0, jnp.where(grp == 1, 4.0, jnp.where(grp == 2, 8.0, 16.0)))
    count = jnp.minimum((j * tb + row + 1).astype(F32), win)
    pooled = wsum / count - u
    ya = jnp.dot(pooled.astype(BF16), poolw_ref[...], preferred_element_type=F32) * pools_ref[...]
    ya_ref[0] = ya.astype(BF16)
    pool_ext[0:POOL_TAIL, :] = u[tb - POOL_TAIL:, :]

    z = proj[:, BRANCH_W:3 * BRANCH_W]
    z = 0.5 * z * (1.0 + jnp.tanh(math.sqrt(2.0 / math.pi) * (z + 0.044715 * (z * z * z))))
    su = z[:, 0:BRANCH_W]
    sv = _layer_norm(z[:, BRANCH_W:], sglg_ref[...], sglb_ref[...])
    r128 = lax.broadcasted_iota(jnp.int32, (SG_CHUNK, SG_CHUNK), 0)
    c128 = lax.broadcasted_iota(jnp.int32, (SG_CHUNK, SG_CHUNK), 1)
    wcat = jnp.concatenate(
        [jnp.where(r128 >= c128, sgw_ref[hh], 0.0) for hh in range(SG_HEADS)], axis=1).astype(BF16)
    head = lane // SG_HD
    for ci in range(tb // SG_CHUNK):
        rows = slice(ci * SG_CHUNK, (ci + 1) * SG_CHUNK)
        vch = sv[rows, :]
        vstack = jnp.concatenate(
            [jnp.where(head == hh, vch, 0.0) for hh in range(SG_HEADS)], axis=0).astype(BF16)
        s = jnp.dot(wcat, vstack, preferred_element_type=F32) + sgb_ref[...]
        yb_ref[0, rows, :] = (su[rows, :] * s).astype(BF16)

    glu = proj[:, 3 * BRANCH_W:4 * BRANCH_W] * _sigmoid(proj[:, 4 * BRANCH_W:5 * BRANCH_W])
    conv_ext[CONV_TAIL:CONV_TAIL + tb, :] = glu
    acc = jnp.zeros((tb, BRANCH_W), F32) + cb_ref[...]
    for kk in range(CONV_WIDTH):
        off = CONV_TAIL - (CONV_WIDTH - 1) + kk
        acc = acc + cw_ref[kk:kk + 1, :] * conv_ext[off:off + tb, :]
    yc_ref[0] = _silu(_layer_norm(acc, clg_ref[...], clb_ref[...])).astype(BF16)
    conv_ext[0:CONV_TAIL, :] = glu[tb - CONV_TAIL:, :]

    def log_sigmoid(t):
        return jnp.minimum(t, 0.0) - jnp.log(1.0 + jnp.exp(-jnp.abs(t)))

    rr = lax.broadcasted_iota(jnp.int32, (tb, tb), 0)
    cc = lax.broadcasted_iota(jnp.int32, (tb, tb), 1)
    lower = jnp.where(rr >= cc, 1.0, 0.0).astype(F32)
    lf_c = log_sigmoid(jnp.dot(hb, wfc_ref[...], preferred_element_type=F32) + bfc_ref[...])
    cs_c = jnp.dot(lower, lf_c, precision=HIGHEST, preferred_element_type=F32) + cum_c[...]
    cum_c[...] = cs_c[tb - 1:tb, :]
    cl2 = cs_c * LOG2E
    ln = lax.broadcasted_iota(jnp.int32, (1, LANES), 1)
    ones_q = jnp.where((ln >= ATT_HD + 3) & (ln < ATT_HD + 6), 1.0, 0.0)
    ones_k = jnp.where((ln >= ATT_HD) & (ln < ATT_HD + 3), 1.0, 0.0)
    ones_v = jnp.where(ln == ATT_HD, 1.0, 0.0)
    for hh in range(ATT_HEADS):
        pair = (hh // 2) * LANES
        qs = proj[:, 5 * BRANCH_W + pair:5 * BRANCH_W + pair + LANES] * (LOG2E / math.sqrt(ATT_HD))
        ks = proj[:, 6 * BRANCH_W + pair:6 * BRANCH_W + pair + LANES]
        vs = proj[:, 7 * BRANCH_W + pair:7 * BRANCH_W + pair + LANES]
        if hh % 2:
            qs, ks, vs = (pltpu.roll(a, ATT_HD, 1) for a in (qs, ks, vs))
        c = cl2[:, hh:hh + 1]
        hi = c.astype(BF16).astype(F32)
        mid = (c - hi).astype(BF16).astype(F32)
        lo = (c - hi) - mid
        q_extra = jnp.where(ln == ATT_HD, hi, jnp.where(ln == ATT_HD + 1, mid, jnp.where(ln == ATT_HD + 2, lo, ones_q)))
        k_extra = jnp.where(ln == ATT_HD + 3, -hi,
                            jnp.where(ln == ATT_HD + 4, -mid, jnp.where(ln == ATT_HD + 5, -lo, ones_k)))
        q_ref[0, hh, 0] = jnp.where(ln < ATT_HD, qs, q_extra).T.astype(BF16)
        k_ref[0, hh] = jnp.where(ln < ATT_HD, ks, k_extra).astype(BF16)
        v_ref[0, hh, 0] = jnp.where(ln < ATT_HD, vs, ones_v).T.astype(BF16)


def _w_in_cols(layer, first_col, ncols):
    return pl.BlockSpec((1, D_MODEL, ncols), lambda b, j: (layer, 0, first_col // ncols), pipeline_mode=pl.Buffered(1))


def _mixer_in(x, mod_l, norm_g, w_in, layer, bfc, poolw, pools, sglg, sglb, sgw, sgb, cw, cb, clg, clb, *, tb):
    B, S, D = x.shape
    full = lambda a: pl.BlockSpec(a.shape, lambda b, j: (0,) * a.ndim)
    tok = lambda w: pl.BlockSpec((1, tb, w), lambda b, j: (b, j, 0))
    head = pl.BlockSpec((1, ATT_HEADS, tb, LANES), lambda b, j: (b, 0, j, 0))
    head_t = pl.BlockSpec((1, ATT_HEADS, 1, LANES, tb), lambda b, j: (b, 0, j, 0, 0))
    consts = (bfc, poolw, pools, sglg, sglb, sgw, sgb, cw, cb, clg, clb)
    act = jax.ShapeDtypeStruct((B, S, BRANCH_W), BF16)
    att = jax.ShapeDtypeStruct((B, ATT_HEADS, S, LANES), BF16)
    att_t = jax.ShapeDtypeStruct((B, ATT_HEADS, S // tb, LANES, tb), BF16)
    return pl.pallas_call(
        functools.partial(_mixer_in_kernel, tb=tb),
        grid=(B, S // tb),
        in_specs=[tok(D), pl.BlockSpec((1, N_MOD, D), lambda b, j: (b, 0, 0)), full(norm_g),
                  _w_in_cols(layer, 0, MIX_COLS), _w_in_cols(layer, MIX_COLS, LANES)] + [full(a) for a in consts],
        out_specs=[tok(BRANCH_W)] * 3 + [head_t, head, head_t],
        out_shape=[act] * 3 + [att_t, att, att_t],
        scratch_shapes=[pltpu.VMEM((POOL_TAIL + tb, BRANCH_W), F32),
                        pltpu.VMEM((CONV_TAIL + tb, BRANCH_W), F32),
                        pltpu.VMEM((1, LANES), F32),
                        pltpu.VMEM((D, MIX_COLS), BF16), pltpu.VMEM((D, LANES), BF16)],
        compiler_params=_params("arbitrary", "arbitrary"),
        name="mixer_in",
    )(x, mod_l, norm_g, w_in, w_in, *consts)


def _attn_kernel(q_ref, k_ref, v_ref, o_ref, m_ref, acc_ref, *, tq):
    i = pl.program_id(1)
    key = lax.broadcasted_iota(jnp.int32, (tq, tq), 0)
    qry = lax.broadcasted_iota(jnp.int32, (tq, tq), 1)
    m_ref[...] = jnp.full(m_ref.shape, -jnp.inf, F32)
    acc_ref[...] = jnp.zeros(acc_ref.shape, F32)

    def block(kj, diagonal):
        ks = pl.multiple_of(kj * tq, tq)
        logits = [jnp.dot(k_ref[0, hh, pl.ds(ks, tq), :], q_ref[0, hh, 0], preferred_element_type=F32)
                  for hh in range(ATT_HEADS)]
        for hh in range(ATT_HEADS):
            s = logits[hh]
            if diagonal:
                s = jnp.where(key <= qry, s, -jnp.inf)
            m_old = m_ref[hh]
            m_new = jnp.maximum(m_old, jnp.max(s, axis=0, keepdims=True))
            p = jnp.exp2(s - m_new)
            pv = jnp.dot(v_ref[0, hh, kj], p.astype(BF16), preferred_element_type=F32)
            acc_ref[hh] = jnp.exp2(m_old - m_new) * acc_ref[hh] + pv
            m_ref[hh] = m_new

    def body(kj, carry):
        block(kj, False)
        return carry

    lax.fori_loop(0, i, body, 0)
    block(i, True)

    ln = lax.broadcasted_iota(jnp.int32, (1, LANES), 1)
    for pr in range(ATT_HEADS // 2):
        o = []
        for hh in (2 * pr, 2 * pr + 1):
            acc = acc_ref[hh]
            o.append((acc / acc[ATT_HD:ATT_HD + 1, :]).T)
        o_ref[0, :, pr * LANES:(pr + 1) * LANES] = jnp.where(ln < ATT_HD, o[0], pltpu.roll(o[1], ATT_HD, 1)).astype(BF16)


def _attention(q_t, k, v_t, *, tq):
    B, H, S, W = k.shape
    nblk = S // tq
    assert q_t.shape == (B, H, nblk, W, tq) and v_t.shape == q_t.shape
    return pl.pallas_call(
        functools.partial(_attn_kernel, tq=tq),
        grid=(B, nblk),
        in_specs=[pl.BlockSpec((1, H, 1, W, tq), lambda b, i: (b, 0, i, 0, 0)),
                  pl.BlockSpec((1, H, S, W), lambda b, i: (b, 0, 0, 0), pipeline_mode=pl.Buffered(1)),
                  pl.BlockSpec((1, H, nblk, W, tq), lambda b, i: (b, 0, 0, 0, 0), pipeline_mode=pl.Buffered(1))],
        out_specs=pl.BlockSpec((1, tq, BRANCH_W), lambda b, i: (b, i, 0)),
        out_shape=jax.ShapeDtypeStruct((B, S, BRANCH_W), BF16),
        scratch_shapes=[pltpu.VMEM((H, 1, tq), F32), pltpu.VMEM((H, W, tq), F32)],
        compiler_params=_params("arbitrary", "arbitrary"),
        name="attention",
    )(q_t, k, v_t)


def _merge_kernel(x_ref, mod_ref, g_ref, wa_f32, wb_f32, wt_f32, wbr_ref, wout_ref, ya_ref, yb_ref, yc_ref, yd_ref,
                  o_ref, wg_ref):
    @pl.when((pl.program_id(0) == 0) & (pl.program_id(1) == 0))
    def _():
        shift, wide = GATE_SHIFT, D_MODEL + LANES
        windows = (wa_f32[0, :, 0:wide],
                   jnp.concatenate([wa_f32[0, :, D_MODEL:], wb_f32[0, :, 0:LANES]], axis=1),
                   wb_f32[0, :, 0:wide],
                   jnp.concatenate([wb_f32[0, :, D_MODEL:], wt_f32[0]], axis=1))
        for n, win in enumerate(windows):
            wg_ref[:, n * D_MODEL:(n + 1) * D_MODEL] = pltpu.roll(win, wide - shift, 1)[:, :D_MODEL].astype(BF16)

    x = x_ref[0]
    mod = mod_ref[0]
    hb = _rms_mod(x, g_ref[...], mod[1:2, :], mod[0:1, :]).astype(BF16)
    merged = jnp.zeros(x.shape, F32)
    for n, y_ref in enumerate((ya_ref, yb_ref, yc_ref, yd_ref)):
        gate = _sigmoid(jnp.dot(hb, wg_ref[:, n * D_MODEL:(n + 1) * D_MODEL], preferred_element_type=F32))
        merged = merged + gate * jnp.dot(y_ref[0], wbr_ref[n], preferred_element_type=F32)
    out = jnp.dot(merged.astype(BF16), wout_ref[...], preferred_element_type=F32)
    o_ref[0] = x + mod[2:3, :] * out


GATE_SHIFT = ATT_HEADS


def _merge(x, mod_l, norm_g, w_in, layer, wbranch, wout, ya, yb, yc, yd, *, tb):
    B, S, D = x.shape
    full = lambda a: pl.BlockSpec(a.shape, lambda b, j: (0,) * a.ndim)
    tok = lambda w: pl.BlockSpec((1, tb, w), lambda b, j: (b, j, 0))
    assert w_in.shape[2] == MIX_COLS + GATE_SHIFT + N_BRANCH * D
    return pl.pallas_call(
        _merge_kernel,
        grid=(B, S // tb),
        in_specs=[tok(D), pl.BlockSpec((1, N_MOD, D), lambda b, j: (b, 0, 0)), full(norm_g),
                  _w_in_cols(layer, MIX_COLS, 2 * D), _w_in_cols(layer, MIX_COLS + 2 * D, 2 * D),
                  _w_in_cols(layer, MIX_COLS + 4 * D, LANES),
                  full(wbranch), full(wout)] + [tok(BRANCH_W)] * 4,
        out_specs=tok(D),
        out_shape=jax.ShapeDtypeStruct((B, S, D), F32),
        scratch_shapes=[pltpu.VMEM((D, N_BRANCH * D), BF16)],
        compiler_params=_params("arbitrary", "arbitrary"),
        name="merge",
    )(x, mod_l, norm_g, w_in, w_in, w_in, wbranch, wout, ya, yb, yc, yd)


def _route(scores_t, bias_t):
    E, n = scores_t.shape
    sel = scores_t + bias_t
    eidx = lax.broadcasted_iota(jnp.int32, (E, n), 0)
    neg = jnp.full((E, n), -jnp.inf, F32)
    gscore = []
    sub = lax.broadcasted_iota(jnp.int32, (GROUP_SIZE, n), 0)
    for g in range(N_GROUPS):
        blk = sel[g * GROUP_SIZE:(g + 1) * GROUP_SIZE, :]
        m1 = jnp.max(blk, axis=0, keepdims=True)
        first = jnp.min(jnp.where(blk == m1, sub, GROUP_SIZE), axis=0, keepdims=True)
        m2 = jnp.max(jnp.where(sub == first, -jnp.inf, blk), axis=0, keepdims=True)
        gscore.append(m1 + m2)
    emask = []
    for g in range(N_GROUPS):
        beaten = jnp.zeros((1, n), jnp.int32)
        for g2 in range(N_GROUPS):
            if g2 == g:
                continue
            wins = (gscore[g2] > gscore[g]) | ((gscore[g2] == gscore[g]) & (g2 < g))
            beaten = beaten + wins.astype(jnp.int32)
        emask.append(jnp.broadcast_to(beaten < TOPK_GROUPS, (GROUP_SIZE, n)))
    cur = jnp.where(jnp.concatenate(emask, axis=0), sel, neg)
    chosen = jnp.zeros((E, n), jnp.bool_)
    firsts = []
    for _ in range(TOP_K):
        m = jnp.max(cur, axis=0, keepdims=True)
        first = jnp.min(jnp.where(cur == m, eidx, E), axis=0, keepdims=True)
        hit = eidx == first
        chosen = chosen | hit
        cur = jnp.where(hit, neg, cur)
        firsts.append(first)
    w = jnp.where(chosen, scores_t, 0.0)
    return w / jnp.sum(w, axis=0, keepdims=True) * ROUTE_SCALE, chosen, firsts


def _route_kernel(x_ref, mod_ref, g_ref, rwt_ref, rb_ref, ls_ref, lst_ref, wt_ref, bc_ref, cr_ref, cnt_ref, count):
    first_step = (pl.program_id(0) == 0) & (pl.program_id(1) == 0)

    @pl.when(first_step)
    def _():
        count[...] = jnp.zeros_like(count)

    mod = mod_ref[0]
    h = _rms_mod(x_ref[0], g_ref[...], mod[4:5, :], mod[3:4, :])
    logits_t = lax.dot_general(rwt_ref[...], h, (((1,), (1,)), ((), ())),
                               precision=HIGHEST, preferred_element_type=F32)
    gates_t, chosen, firsts = _route(_sigmoid(logits_t), rb_ref[...])
    E, tb = gates_t.shape
    ones = jnp.where(chosen, 1.0, 0.0)
    rr = lax.broadcasted_iota(jnp.int32, (tb, tb), 0)
    cc = lax.broadcasted_iota(jnp.int32, (tb, tb), 1)
    upper = jnp.where(rr <= cc, 1.0, 0.0).astype(BF16)
    incl = jnp.dot(ones.astype(BF16), upper, preferred_element_type=F32)
    block_count = jnp.sum(ones, axis=1, keepdims=True)
    block_count = block_count + (block_count.astype(jnp.int32) & 1).astype(F32)
    er = lax.broadcasted_iota(jnp.int32, (E, E), 0)
    ec = lax.broadcasted_iota(jnp.int32, (E, E), 1)
    before = jnp.where(ec < er, 1.0, 0.0).astype(F32)
    local_base = jnp.dot(before, jnp.broadcast_to(block_count, (E, LANES)), precision=HIGHEST,
                         preferred_element_type=F32)[:, 0:1]
    row_all = local_base + (incl - ones)
    eidx = lax.broadcasted_iota(jnp.int32, (E, tb), 0)
    ls_rows = [jnp.sum(jnp.where(eidx == f, row_all, 0.0), axis=0, keepdims=True) for f in firsts]
    w_rows = [jnp.sum(jnp.where(eidx == f, gates_t, 0.0), axis=0, keepdims=True) for f in firsts]
    ls_ref[...] = jnp.concatenate(ls_rows + [jnp.zeros((SUBLANES - TOP_K, tb), F32)], axis=0).astype(jnp.int32)
    zpad = jnp.zeros((LANES - TOP_K, tb), F32)
    lst_ref[0] = jnp.concatenate(ls_rows + [zpad], axis=0).T.astype(jnp.int32)
    wt_ref[0] = jnp.concatenate(w_rows + [zpad], axis=0).T

    def as_row(col):
        sel = lax.broadcasted_iota(jnp.int32, (E, LANES), 0) == lax.broadcasted_iota(jnp.int32, (E, LANES), 1)
        return jnp.sum(jnp.where(sel, col, 0.0), axis=0, keepdims=True).astype(jnp.int32)

    bc_ref[0] = as_row(block_count)
    cr_ref[0] = as_row(count[...])
    count[...] = count[...] + block_count
    cnt_ref[...] = jnp.broadcast_to(count[...], cnt_ref.shape).astype(jnp.int32)


def _route_call(x, mod_l, norm_g, router_wt, router_b, *, tb):
    B, S, D = x.shape
    nj = S // tb
    full = lambda a: pl.BlockSpec(a.shape, lambda b, j: (0,) * a.ndim)
    tok = lambda w: pl.BlockSpec((1, tb, w), lambda b, j: (b, j, 0))
    kt = pl.BlockSpec((SUBLANES, tb), lambda b, j: (0, b * nj + j))
    per_block = pl.BlockSpec((1, 1, LANES), lambda b, j: (b * nj + j, 0, 0))
    return pl.pallas_call(
        _route_kernel,
        grid=(B, nj),
        in_specs=[tok(D), pl.BlockSpec((1, N_MOD, D), lambda b, j: (b, 0, 0)),
                  full(norm_g), full(router_wt), full(router_b)],
        out_specs=[kt, tok(LANES), tok(LANES), per_block, per_block,
                   pl.BlockSpec((N_EXPERTS, LANES), lambda b, j: (0, 0))],
        out_shape=[jax.ShapeDtypeStruct((SUBLANES, B * S), jnp.int32),
                   jax.ShapeDtypeStruct((B, S, LANES), jnp.int32),
                   jax.ShapeDtypeStruct((B, S, LANES), F32),
                   jax.ShapeDtypeStruct((B * nj, 1, LANES), jnp.int32),
                   jax.ShapeDtypeStruct((B * nj, 1, LANES), jnp.int32),
                   jax.ShapeDtypeStruct((N_EXPERTS, LANES), jnp.int32)],
        scratch_shapes=[pltpu.VMEM((N_EXPERTS, 1), F32)],
        compiler_params=_params("arbitrary", "arbitrary"),
        name="route",
    )(x, mod_l, norm_g, router_wt, router_b)


def _route_fin_kernel(cnt_ref, meta_ref, base_ref, *, tm):
    cnt = cnt_ref[...]
    ntile = lax.shift_right_logical(cnt + (tm - 1), int(math.log2(tm)))
    er = lax.broadcasted_iota(jnp.int32, (N_EXPERTS, N_EXPERTS), 0)
    ec = lax.broadcasted_iota(jnp.int32, (N_EXPERTS, N_EXPERTS), 1)
    before = jnp.where(ec < er, 1.0, 0.0).astype(F32)
    start = jnp.dot(before, ntile.astype(F32), precision=HIGHEST, preferred_element_type=F32).astype(jnp.int32)
    base = start * tm
    base_ref[...] = base
    nlane = meta_ref.shape[1]
    lane = lax.broadcasted_iota(jnp.int32, (1, nlane), 1)
    end = (start + ntile)[:, 0:1]
    tile_expert = jnp.sum(jnp.where(end <= lane, 1, 0), axis=0, keepdims=True)
    tile_expert = jnp.minimum(tile_expert, N_EXPERTS - 1)
    meta_ref[...] = jnp.where(lane == nlane - 1, end[N_EXPERTS - 1:N_EXPERTS, :], tile_expert)


def _route_fin(cnt, *, tm, meta_lanes):
    full = lambda a: pl.BlockSpec(a.shape, lambda i: (0,) * a.ndim)
    outs = [jax.ShapeDtypeStruct((1, meta_lanes), jnp.int32), jax.ShapeDtypeStruct(cnt.shape, jnp.int32)]
    return pl.pallas_call(
        functools.partial(_route_fin_kernel, tm=tm),
        grid=(1,),
        in_specs=[full(cnt)],
        out_specs=[full(o) for o in outs],
        out_shape=outs,
        compiler_params=_params("arbitrary"),
        name="route_fin",
    )(cnt)


HALF = D_MODEL // 2
SLAB = HALF // LANES
HIGH_BITS = 0xFFFF0000


def _to_slabs(ref, value):
    n = value.shape[0]
    bits = lax.bitcast_convert_type(value.astype(BF16).astype(F32), jnp.uint32)
    packed = (bits[:, HALF:] & jnp.uint32(HIGH_BITS)) | (bits[:, :HALF] >> 16)
    for s in range(SLAB):
        ref[pl.ds(s, n, stride=SLAB), :] = packed[:, s * LANES:(s + 1) * LANES]


def _from_slabs(ref, n):
    packed = jnp.concatenate([ref[pl.ds(s, n, stride=SLAB), :] for s in range(SLAB)], axis=1)
    low = lax.bitcast_convert_type(packed << 16, F32).astype(BF16)
    high = lax.bitcast_convert_type(packed & jnp.uint32(HIGH_BITS), F32).astype(BF16)
    return low, high


def _for_expert_runs(base_sm, bc_sm, cr_sm, max_run, fn):
    def per_expert(e, local_row):
        n = bc_sm[0, 0, e]
        sorted_row = base_sm[e] + cr_sm[0, 0, e]
        _for_pieces(n, max_run, lambda done, size: fn(local_row + done, sorted_row + done, size))
        return local_row + n

    lax.fori_loop(0, N_EXPERTS, per_expert, 0)


def _for_pieces(n, max_piece, fn):
    done = 0
    bit = max_piece
    while bit >= 2:
        take = n & bit

        @pl.when(take != 0)
        def _(done=done, bit=bit):
            fn(done, bit)

        done = done + take
        bit //= 2


ROW_GROUPS = 7


def _local_rows(tb):
    rows = tb * TOP_K + N_EXPERTS
    assert rows % (ROW_GROUPS * 2 * SUBLANES) == 0, rows
    return rows


def _queue(nrows):
    return int(math.log2(nrows)) % 2


def _slab_rows(ref, row, nrows):
    return ref.at[pl.ds(pl.multiple_of(row * SLAB, 2 * SLAB), nrows * SLAB)]


def _dispatch_kernel(cnt_sm, base_sm, bc_sm, cr_sm, bc_prev, cr_prev, ls_ref, x_ref, mod_ref, g_ref, xs_hbm,
                     stage, zeros, sem, zsem, *, tb, tm):
    step = pl.program_id(0) * pl.num_programs(1) + pl.program_id(1)
    last_step = pl.num_programs(0) * pl.num_programs(1) - 1
    cur = lax.rem(step, 2)
    mod = mod_ref[0]
    hb = _rms_mod(x_ref[0], g_ref[...], mod[4:5, :], mod[3:4, :]).astype(BF16)
    ls = ls_ref[...]
    gr = _local_rows(tb) // ROW_GROUPS
    for grp in range(ROW_GROUPS):
        row = grp * gr + lax.broadcasted_iota(jnp.int32, (gr, tb), 0)
        hit = ls[0:1, :] == row
        for k in range(1, TOP_K):
            hit = hit | (ls[k:k + 1, :] == row)
        perm = jnp.where(hit, 1.0, 0.0).astype(BF16)
        _to_slabs(stage.at[cur, pl.ds(grp * gr * SLAB, gr * SLAB)], jnp.dot(perm, hb, preferred_element_type=F32))

    def run_copy(buf):
        return lambda local_row, sorted_row, nrows: pltpu.make_async_copy(
            _slab_rows(stage.at[buf], local_row, nrows), _slab_rows(xs_hbm, sorted_row, nrows), sem.at[buf])

    _for_expert_runs(base_sm, bc_sm, cr_sm, tb, lambda *a: run_copy(cur)(*a).start(priority=_queue(a[2])))

    @pl.when(step == 0)
    def _():
        zeros[...] = jnp.zeros_like(zeros)

        def per_expert(e, carry):
            n = cnt_sm[e]
            npad = lax.rem(tm - lax.rem(n, tm), tm)
            zcopy = lambda done, size: pltpu.make_async_copy(
                _slab_rows(zeros, 0, size), _slab_rows(xs_hbm, base_sm[e] + n + done, size), zsem)
            _for_pieces(npad, tm // 2, lambda *a: zcopy(*a).start())
            _for_pieces(npad, tm // 2, lambda *a: zcopy(*a).wait())
            return carry

        lax.fori_loop(0, N_EXPERTS, per_expert, 0)

    @pl.when(step > 0)
    def _():
        _for_expert_runs(base_sm, bc_prev, cr_prev, tb, lambda *a: run_copy(1 - cur)(*a).wait())

    @pl.when(step == last_step)
    def _():
        _for_expert_runs(base_sm, bc_sm, cr_sm, tb, lambda *a: run_copy(cur)(*a).wait())


def _dispatch(cnt, base, blk_cnt, blk_carry, ls, x, mod_l, norm_g, *, tb, tm, n_slots):
    B, S, D = x.shape
    nj = S // tb
    smem_block = lambda shift: pl.BlockSpec(
        (1, 1, LANES), lambda b, j, *_: (jnp.maximum(b * nj + j + shift, 0), 0, 0), memory_space=pltpu.SMEM)
    grid_spec = pltpu.PrefetchScalarGridSpec(
        num_scalar_prefetch=2,
        grid=(B, nj),
        in_specs=[smem_block(0), smem_block(0), smem_block(-1), smem_block(-1),
                  pl.BlockSpec((SUBLANES, tb), lambda b, j, *_: (0, b * nj + j)),
                  pl.BlockSpec((1, tb, D), lambda b, j, *_: (b, j, 0)),
                  pl.BlockSpec((1, N_MOD, D), lambda b, j, *_: (b, 0, 0)),
                  pl.BlockSpec(norm_g.shape, lambda b, j, *_: (0, 0))],
        out_specs=pl.BlockSpec(memory_space=pltpu.HBM),
        scratch_shapes=[pltpu.VMEM((2, _local_rows(tb) * SLAB, LANES), jnp.uint32),
                        pltpu.VMEM((tm // 2 * SLAB, LANES), jnp.uint32),
                        pltpu.SemaphoreType.DMA((2,)), pltpu.SemaphoreType.DMA])
    return pl.pallas_call(
        functools.partial(_dispatch_kernel, tb=tb, tm=tm),
        grid_spec=grid_spec,
        out_shape=jax.ShapeDtypeStruct((n_slots * SLAB, LANES), jnp.uint32),
        compiler_params=_params("arbitrary", "arbitrary"),
        name="dispatch",
    )(cnt, base, blk_cnt, blk_carry, blk_cnt, blk_carry, ls, x, mod_l, norm_g)


def _experts_kernel(te_sm, nt_sm, x_ref, w1_ref, w3_ref, w2_ref, y_ref, w1b, w3b, w2b, *, tm):
    i = pl.program_id(0)
    in_use = i < nt_sm[0]

    @pl.when(in_use)
    def _():
        @pl.when((i == 0) | (te_sm[i] != te_sm[jnp.maximum(i - 1, 0)]))
        def _():
            w1b[...] = w1_ref[0].astype(BF16)
            w3b[...] = w3_ref[0].astype(BF16)
            w2b[...] = w2_ref[0].astype(BF16)

        for c in range(tm // EXPERT_CHUNK):
            rows = pl.ds(c * EXPERT_CHUNK * SLAB, EXPERT_CHUNK * SLAB)
            x_lo, x_hi = _from_slabs(x_ref.at[rows], EXPERT_CHUNK)
            up = lambda w: (jnp.dot(x_lo, w[:HALF, :], preferred_element_type=F32)
                            + jnp.dot(x_hi, w[HALF:, :], preferred_element_type=F32))
            hid = _silu(up(w1b)) * up(w3b)
            _to_slabs(y_ref.at[rows], jnp.dot(hid.astype(BF16), w2b[...], preferred_element_type=F32))

    @pl.when(jnp.logical_not(in_use))
    def _():
        y_ref[...] = jnp.zeros_like(y_ref)


def _experts(tile_expert, n_tiles, xs, w1, w3, w2, layer, *, tm):
    _, E, D, F = w1.shape
    n_slots = xs.shape[0] // SLAB
    last = lambda i, te, nt: jnp.minimum(i, nt[0] - 1)
    grid_spec = pltpu.PrefetchScalarGridSpec(
        num_scalar_prefetch=2,
        grid=(n_slots // tm,),
        in_specs=[pl.BlockSpec((tm * SLAB, LANES), lambda i, te, nt: (last(i, te, nt), 0)),
                  pl.BlockSpec((None, 1, D, F), lambda i, te, nt: (layer, te[last(i, te, nt)], 0, 0)),
                  pl.BlockSpec((None, 1, D, F), lambda i, te, nt: (layer, te[last(i, te, nt)], 0, 0)),
                  pl.BlockSpec((None, 1, F, D), lambda i, te, nt: (layer, te[last(i, te, nt)], 0, 0))],
        out_specs=pl.BlockSpec((tm * SLAB, LANES), lambda i, te, nt: (i, 0)),
        scratch_shapes=[pltpu.VMEM((D, F), BF16), pltpu.VMEM((D, F), BF16), pltpu.VMEM((F, D), BF16)])
    return pl.pallas_call(
        functools.partial(_experts_kernel, tm=tm),
        grid_spec=grid_spec,
        out_shape=jax.ShapeDtypeStruct(xs.shape, jnp.uint32),
        compiler_params=_params("arbitrary"),
        name="experts",
    )(tile_expert, n_tiles, xs, w1, w3, w2)


def _combine_kernel(base_sm, bc_sm, cr_sm, bc_next, cr_next, x_ref, mod_ref, g_ref, lst_ref, wt_ref,
                    sw1_ref, sw3_ref, sw2_ref, fg_ref, ys_hbm, o_ref, stage, sem, *, tb, final_norm):
    step = pl.program_id(0) * pl.num_programs(1) + pl.program_id(1)
    last_step = pl.num_programs(0) * pl.num_programs(1) - 1
    cur = lax.rem(step, 2)

    def run_copy(buf):
        return lambda local_row, sorted_row, nrows: pltpu.make_async_copy(
            _slab_rows(ys_hbm, sorted_row, nrows), _slab_rows(stage.at[buf], local_row, nrows), sem.at[buf])

    @pl.when(step == 0)
    def _():
        stage[...] = jnp.zeros_like(stage)
        _for_expert_runs(base_sm, bc_sm, cr_sm, tb, lambda *a: run_copy(cur)(*a).start(priority=_queue(a[2])))

    @pl.when(step < last_step)
    def _():
        _for_expert_runs(base_sm, bc_next, cr_next, tb, lambda *a: run_copy(1 - cur)(*a).start(priority=_queue(a[2])))

    x = x_ref[0]
    mod = mod_ref[0]
    hb = _rms_mod(x, g_ref[...], mod[4:5, :], mod[3:4, :]).astype(BF16)
    a = jnp.dot(hb, sw1_ref[...], preferred_element_type=F32)
    b = jnp.dot(hb, sw3_ref[...], preferred_element_type=F32)
    acc = jnp.dot((_silu(a) * b).astype(BF16), sw2_ref[...], preferred_element_type=F32)
    wt = wt_ref[0]
    lst = lst_ref[0]
    _for_expert_runs(base_sm, bc_sm, cr_sm, tb, lambda *a: run_copy(cur)(*a).wait())
    gr = _local_rows(tb) // ROW_GROUPS
    routed = [jnp.zeros((tb, HALF), F32)] * 2
    for grp in range(ROW_GROUPS):
        halves = _from_slabs(stage.at[cur, pl.ds(grp * gr * SLAB, gr * SLAB)], gr)
        row = grp * gr + lax.broadcasted_iota(jnp.int32, (1, gr), 1)
        gate = jnp.zeros((tb, gr), F32)
        for k in range(TOP_K):
            gate = gate + jnp.where(lst[:, k:k + 1] == row, wt[:, k:k + 1], 0.0)
        gate = gate.astype(BF16)
        routed = [r + jnp.dot(gate, yg, preferred_element_type=F32) for r, yg in zip(routed, halves)]
    acc = acc + jnp.concatenate(routed, axis=1)
    y = x + mod[5:6, :] * acc
    if final_norm:
        y = y * lax.rsqrt(jnp.mean(y * y, axis=-1, keepdims=True) + EPS) * fg_ref[...]
    o_ref[0] = y


def _combine(base, blk_cnt, blk_carry, x, mod_l, norm_g, lst, wt, sw1, sw3, sw2, final_g, ys, *, tb, final_norm):
    B, S, D = x.shape
    nj = S // tb
    full = lambda a: pl.BlockSpec(a.shape, lambda b, j, *_: (0,) * a.ndim)
    tok = lambda w: pl.BlockSpec((1, tb, w), lambda b, j, *_: (b, j, 0))
    smem_block = lambda shift: pl.BlockSpec(
        (1, 1, LANES), lambda b, j, *_: (jnp.minimum(b * nj + j + shift, B * nj - 1), 0, 0), memory_space=pltpu.SMEM)
    grid_spec = pltpu.PrefetchScalarGridSpec(
        num_scalar_prefetch=1,
        grid=(B, nj),
        in_specs=[smem_block(0), smem_block(0), smem_block(1), smem_block(1), tok(D),
                  pl.BlockSpec((1, N_MOD, D), lambda b, j, *_: (b, 0, 0)),
                  full(norm_g), tok(LANES), tok(LANES), full(sw1), full(sw3), full(sw2), full(final_g),
                  pl.BlockSpec(memory_space=pltpu.HBM)],
        out_specs=tok(D),
        scratch_shapes=[pltpu.VMEM((2, _local_rows(tb) * SLAB, LANES), jnp.uint32), pltpu.SemaphoreType.DMA((2,))])
    return pl.pallas_call(
        functools.partial(_combine_kernel, tb=tb, final_norm=final_norm),
        grid_spec=grid_spec,
        out_shape=jax.ShapeDtypeStruct((B, S, D), F32),
        compiler_params=_params("arbitrary", "arbitrary"),
        name="combine",
    )(base, blk_cnt, blk_carry, blk_cnt, blk_carry, x, mod_l, norm_g, lst, wt, sw1, sw3, sw2, final_g, ys)


def _block_diag(w):
    G, a, b = w.shape
    out = jnp.zeros((G * a, G * b), w.dtype)
    for g in range(G):
        out = out.at[g * a:(g + 1) * a, g * b:(g + 1) * b].set(w[g])
    return out


def kernel(x, c, w_in, b_f, pool_w, pool_scale, sg_ln_g, sg_ln_b, sg_w, sg_b, conv_w, conv_b, conv_ln_g,
           conv_ln_b, w_branch, w_out, mix_norm_g, ffn_norm_g, ada_w, ada_b, router_w, router_bias,
           exp_w1, exp_w3, exp_w2, shared_w1, shared_w3, shared_w2, final_norm_g):
    B, S, D = x.shape
    L = w_in.shape[0]
    tb = min(512, S)
    tq = tb
    tm = EXPERT_TILE
    n_tiles = -(-(B * (S // tb) * _local_rows(tb)) // tm) + N_EXPERTS
    meta_lanes = -(-(n_tiles + 1) // LANES) * LANES
    row = lambda a: a.reshape(1, -1)

    mod = _modulation(c, ada_w, ada_b)
    for l in range(L):
        bfc = jnp.pad(b_f[l], (0, LANES - ATT_HEADS)).reshape(1, LANES)
        poolw = _block_diag(pool_w[l]).astype(BF16)
        sgb = jnp.repeat(sg_b[l].T, SG_HD, axis=1)

        ya, yb, yc, q, k, v = _mixer_in(
            x, mod[l], row(mix_norm_g[l]), w_in, l, bfc, poolw, row(pool_scale[l]),
            row(sg_ln_g[l]), row(sg_ln_b[l]), sg_w[l], sgb, conv_w[l], row(conv_b[l]),
            row(conv_ln_g[l]), row(conv_ln_b[l]), tb=tb)
        yd = _attention(q, k, v, tq=tq)
        x = _merge(x, mod[l], row(mix_norm_g[l]), w_in, l, w_branch[l].astype(BF16), w_out[l].astype(BF16),
                   ya, yb, yc, yd, tb=tb)
        fnorm = row(ffn_norm_g[l])
        ls, lst, wt, blk_cnt, blk_carry, cnt = _route_call(
            x, mod[l], fnorm, router_w[l].T, router_bias[l].reshape(-1, 1), tb=tb)
        meta, base = _route_fin(cnt, tm=tm, meta_lanes=meta_lanes)
        xs = _dispatch(cnt[:, 0], base[:, 0], blk_cnt, blk_carry, ls, x, mod[l], fnorm,
                       tb=tb, tm=tm, n_slots=n_tiles * tm)
        ys = _experts(meta[0, :n_tiles], meta[0, meta_lanes - 1:], xs, exp_w1, exp_w3, exp_w2, l, tm=tm)
        x = _combine(base[:, 0], blk_cnt, blk_carry, x, mod[l], fnorm, lst, wt, shared_w1[l].astype(BF16),
                     shared_w3[l].astype(BF16), shared_w2[l].astype(BF16), row(final_norm_g), ys,
                     tb=tb, final_norm=(l == L - 1))
    return x
```

```python
import functools
import math

import jax
import jax.numpy as jnp
from jax import lax
from jax.experimental import pallas as pl
from jax.experimental.pallas import tpu as pltpu

F32 = jnp.float32
BF16 = jnp.bfloat16
HIGHEST = lax.Precision.HIGHEST

D_MODEL = 1024
BRANCH_W = 256
N_BRANCH = 4
POOL_WINDOWS = (2, 4, 8, 16)
POOL_GW = 64
POOL_TAIL = 16
SG_HEADS = 4
SG_CHUNK = 128
SG_HD = 64
CONV_WIDTH = 31
CONV_TAIL = 32
ATT_HEADS = 4
ATT_HD = 64
N_EXPERTS = 64
TOP_K = 6
N_GROUPS = 8
GROUP_SIZE = N_EXPERTS // N_GROUPS
TOPK_GROUPS = 4
EXPERT_FF = 256
SHARED_FF = 256
ROUTE_SCALE = 2.5
EPS = 1e-6
N_MOD = 6
MIX_COLS = 8 * BRANCH_W
LANES = 128
SUBLANES = 8
LOG2E = 1.4426950408889634
EXPERT_TILE = 512
EXPERT_CHUNK = 256
VMEM_LIMIT = 56 * 1024 * 1024


def _sigmoid(x):
    return 1.0 / (1.0 + jnp.exp(-x))


def _silu(x):
    return x * _sigmoid(x)


def _rms_mod(x, g, scale, shift):
    y = x * lax.rsqrt(jnp.mean(x * x, axis=-1, keepdims=True) + EPS)
    return (y * g) * (1.0 + scale) + shift


def _layer_norm(x, g, b):
    mu = jnp.mean(x, axis=-1, keepdims=True)
    xc = x - mu
    var = jnp.mean(xc * xc, axis=-1, keepdims=True)
    return xc * lax.rsqrt(var + EPS) * g + b


def _params(*sem):
    return pltpu.CompilerParams(dimension_semantics=sem, vmem_limit_bytes=VMEM_LIMIT)


def _mod_kernel(c_ref, w_ref, b_ref, o_ref):
    c = c_ref[...]
    o_ref[0] = jnp.dot(_silu(c), w_ref[0], precision=HIGHEST, preferred_element_type=F32) + b_ref[0]


def _modulation(c, ada_w, ada_b):
    L, D, N = ada_w.shape
    B = c.shape[0]
    tn = 1536
    out = pl.pallas_call(
        _mod_kernel,
        grid=(L, N // tn),
        in_specs=[pl.BlockSpec((B, D), lambda l, j: (0, 0)),
                  pl.BlockSpec((1, D, tn), lambda l, j: (l, 0, j)),
                  pl.BlockSpec((1, 1, tn), lambda l, j: (l, 0, j))],
        out_specs=pl.BlockSpec((1, B, tn), lambda l, j: (l, 0, j)),
        out_shape=jax.ShapeDtypeStruct((L, B, N), F32),
        compiler_params=_params("arbitrary", "arbitrary"),
        name="modulation",
    )(c, ada_w, ada_b.reshape(L, 1, N))
    return out.reshape(L, B, N_MOD, D)


def _mixer_in_kernel(x_ref, mod_ref, g_ref, wmix_f32, wfc_f32, bfc_ref,
                     poolw_ref, pools_ref, sglg_ref, sglb_ref, sgw_ref, sgb_ref,
                     cw_ref, cb_ref, clg_ref, clb_ref,
                     ya_ref, yb_ref, yc_ref, q_ref, k_ref, v_ref,
                     pool_ext, conv_ext, cum_c, wmix_ref, wfc_ref, *, tb):
    j = pl.program_id(1)

    @pl.when((pl.program_id(0) == 0) & (j == 0))
    def _():
        wmix_ref[...] = wmix_f32[0].astype(BF16)
        wfc_ref[...] = wfc_f32[0].astype(BF16)

    @pl.when(j == 0)
    def _():
        pool_ext[0:POOL_TAIL, :] = jnp.zeros((POOL_TAIL, BRANCH_W), F32)
        conv_ext[0:CONV_TAIL, :] = jnp.zeros((CONV_TAIL, BRANCH_W), F32)
        cum_c[...] = jnp.zeros_like(cum_c)

    mod = mod_ref[0]
    h = _rms_mod(x_ref[0], g_ref[...], mod[1:2, :], mod[0:1, :])
    hb = h.astype(BF16)
    proj = jnp.dot(hb, wmix_ref[...], preferred_element_type=F32)

    lane = lax.broadcasted_iota(jnp.int32, (1, BRANCH_W), 1)
    row = lax.broadcasted_iota(jnp.int32, (tb, 1), 0)

    u = proj[:, 0:BRANCH_W]
    pool_ext[POOL_TAIL:POOL_TAIL + tb, :] = u
    ext = pool_ext[...]
    s2 = ext + pltpu.roll(ext, 1, 0)
    s4 = s2 + pltpu.roll(s2, 2, 0)
    s8 = s4 + pltpu.roll(s4, 4, 0)
    s16 = s8 + pltpu.roll(s8, 8, 0)
    grp = lane // POOL_GW
    wsum = jnp.where(grp == 0, s2, jnp.where(grp == 1, s4, jnp.where(grp == 2, s8, s16)))[POOL_TAIL:, :]
    win = jnp.where(grp == 0, 2.0, jnp.where(grp == 1, 4.0, jnp.where(grp == 2, 8.0, 16.0)))
    count = jnp.minimum((j * tb + row + 1).astype(F32), win)
    pooled = wsum / count - u
    ya = jnp.dot(pooled.astype(BF16), poolw_ref[...], preferred_element_type=F32) * pools_ref[...]
    ya_ref[0] = ya.astype(BF16)
    pool_ext[0:POOL_TAIL, :] = u[tb - POOL_TAIL:, :]

    z = proj[:, BRANCH_W:3 * BRANCH_W]
    z = 0.5 * z * (1.0 + jnp.tanh(math.sqrt(2.0 / math.pi) * (z + 0.044715 * (z * z * z))))
    su = z[:, 0:BRANCH_W]
    sv = _layer_norm(z[:, BRANCH_W:], sglg_ref[...], sglb_ref[...])
    r128 = lax.broadcasted_iota(jnp.int32, (SG_CHUNK, SG_CHUNK), 0)
    c128 = lax.broadcasted_iota(jnp.int32, (SG_CHUNK, SG_CHUNK), 1)
    wcat = jnp.concatenate(
        [jnp.where(r128 >= c128, sgw_ref[hh], 0.0) for hh in range(SG_HEADS)], axis=1).astype(BF16)
    head = lane // SG_HD
    for ci in range(tb // SG_CHUNK):
        rows = slice(ci * SG_CHUNK, (ci + 1) * SG_CHUNK)
        vch = sv[rows, :]
        vstack = jnp.concatenate(
            [jnp.where(head == hh, vch, 0.0) for hh in range(SG_HEADS)], axis=0).astype(BF16)
        s = jnp.dot(wcat, vstack, preferred_element_type=F32) + sgb_ref[...]
        yb_ref[0, rows, :] = (su[rows, :] * s).astype(BF16)

    glu = proj[:, 3 * BRANCH_W:4 * BRANCH_W] * _sigmoid(proj[:, 4 * BRANCH_W:5 * BRANCH_W])
    conv_ext[CONV_TAIL:CONV_TAIL + tb, :] = glu
    acc = jnp.zeros((tb, BRANCH_W), F32) + cb_ref[...]
    for kk in range(CONV_WIDTH):
        off = CONV_TAIL - (CONV_WIDTH - 1) + kk
        acc = acc + cw_ref[kk:kk + 1, :] * conv_ext[off:off + tb, :]
    yc_ref[0] = _silu(_layer_norm(acc, clg_ref[...], clb_ref[...])).astype(BF16)
    conv_ext[0:CONV_TAIL, :] = glu[tb - CONV_TAIL:, :]

    def log_sigmoid(t):
        return jnp.minimum(t, 0.0) - jnp.log(1.0 + jnp.exp(-jnp.abs(t)))

    rr = lax.broadcasted_iota(jnp.int32, (tb, tb), 0)
    cc = lax.broadcasted_iota(jnp.int32, (tb, tb), 1)
    lower = jnp.where(rr >= cc, 1.0, 0.0).astype(BF16)
    lf_c = log_sigmoid(jnp.dot(hb, wfc_ref[...], preferred_element_type=F32) + bfc_ref[...])
    lf_hi = lf_c.astype(BF16)
    lf_mid = (lf_c - lf_hi.astype(F32)).astype(BF16)
    lf_lo = ((lf_c - lf_hi.astype(F32)) - lf_mid.astype(F32)).astype(BF16)
    cs_c = cum_c[...] + sum(jnp.dot(lower, piece, preferred_element_type=F32) for piece in (lf_lo, lf_mid, lf_hi))
    cum_c[...] = cs_c[tb - 1:tb, :]
    cl2 = cs_c * LOG2E
    ln = lax.broadcasted_iota(jnp.int32, (1, LANES), 1)
    ones_q = jnp.where((ln >= ATT_HD + 3) & (ln < ATT_HD + 6), 1.0, 0.0)
    ones_k = jnp.where((ln >= ATT_HD) & (ln < ATT_HD + 3), 1.0, 0.0)
    ones_v = jnp.where(ln == ATT_HD, 1.0, 0.0)
    for hh in range(ATT_HEADS):
        pair = (hh // 2) * LANES
        qs = proj[:, 5 * BRANCH_W + pair:5 * BRANCH_W + pair + LANES] * (LOG2E / math.sqrt(ATT_HD))
        ks = proj[:, 6 * BRANCH_W + pair:6 * BRANCH_W + pair + LANES]
        vs = proj[:, 7 * BRANCH_W + pair:7 * BRANCH_W + pair + LANES]
        if hh % 2:
            qs, ks, vs = (pltpu.roll(a, ATT_HD, 1) for a in (qs, ks, vs))
        c = cl2[:, hh:hh + 1]
        hi = c.astype(BF16).astype(F32)
        mid = (c - hi).astype(BF16).astype(F32)
        lo = (c - hi) - mid
        q_extra = jnp.where(ln == ATT_HD, hi, jnp.where(ln == ATT_HD + 1, mid, jnp.where(ln == ATT_HD + 2, lo, ones_q)))
        k_extra = jnp.where(ln == ATT_HD + 3, -hi,
                            jnp.where(ln == ATT_HD + 4, -mid, jnp.where(ln == ATT_HD + 5, -lo, ones_k)))
        q_ref[0, hh, 0] = jnp.where(ln < ATT_HD, qs, q_extra).T.astype(BF16)
        k_ref[0, hh] = jnp.where(ln < ATT_HD, ks, k_extra).astype(BF16)
        v_ref[0, hh, 0] = jnp.where(ln < ATT_HD, vs, ones_v).T.astype(BF16)


def _w_in_cols(layer, first_col, ncols):
    return pl.BlockSpec((1, D_MODEL, ncols), lambda b, j: (layer, 0, first_col // ncols), pipeline_mode=pl.Buffered(1))


def _mixer_in(x, mod_l, norm_g, w_in, layer, bfc, poolw, pools, sglg, sglb, sgw, sgb, cw, cb, clg, clb, *, tb):
    B, S, D = x.shape
    full = lambda a: pl.BlockSpec(a.shape, lambda b, j: (0,) * a.ndim)
    tok = lambda w: pl.BlockSpec((1, tb, w), lambda b, j: (b, j, 0))
    head = pl.BlockSpec((1, ATT_HEADS, tb, LANES), lambda b, j: (b, 0, j, 0))
    head_t = pl.BlockSpec((1, ATT_HEADS, 1, LANES, tb), lambda b, j: (b, 0, j, 0, 0))
    consts = (bfc, poolw, pools, sglg, sglb, sgw, sgb, cw, cb, clg, clb)
    act = jax.ShapeDtypeStruct((B, S, BRANCH_W), BF16)
    att = jax.ShapeDtypeStruct((B, ATT_HEADS, S, LANES), BF16)
    att_t = jax.ShapeDtypeStruct((B, ATT_HEADS, S // tb, LANES, tb), BF16)
    return pl.pallas_call(
        functools.partial(_mixer_in_kernel, tb=tb),
        grid=(B, S // tb),
        in_specs=[tok(D), pl.BlockSpec((1, N_MOD, D), lambda b, j: (b, 0, 0)), full(norm_g),
                  _w_in_cols(layer, 0, MIX_COLS), _w_in_cols(layer, MIX_COLS, LANES)] + [full(a) for a in consts],
        out_specs=[tok(BRANCH_W)] * 3 + [head_t, head, head_t],
        out_shape=[act] * 3 + [att_t, att, att_t],
        scratch_shapes=[pltpu.VMEM((POOL_TAIL + tb, BRANCH_W), F32),
                        pltpu.VMEM((CONV_TAIL + tb, BRANCH_W), F32),
                        pltpu.VMEM((1, LANES), F32),
                        pltpu.VMEM((D, MIX_COLS), BF16), pltpu.VMEM((D, LANES), BF16)],
        compiler_params=_params("arbitrary", "arbitrary"),
        name="mixer_in",
    )(x, mod_l, norm_g, w_in, w_in, *consts)


def _attn_kernel(q_ref, k_ref, v_ref, o_ref, m_ref, acc_ref, *, tq):
    i = pl.program_id(1)
    key = lax.broadcasted_iota(jnp.int32, (tq, tq), 0)
    qry = lax.broadcasted_iota(jnp.int32, (tq, tq), 1)
    m_ref[...] = jnp.full(m_ref.shape, -jnp.inf, F32)
    acc_ref[...] = jnp.zeros(acc_ref.shape, F32)

    def block(kj, diagonal):
        ks = pl.multiple_of(kj * tq, tq)
        logits = [jnp.dot(k_ref[0, hh, pl.ds(ks, tq), :], q_ref[0, hh, 0], preferred_element_type=F32)
                  for hh in range(ATT_HEADS)]
        for hh in range(ATT_HEADS):
            s = logits[hh]
            if diagonal:
                s = jnp.where(key <= qry, s, -jnp.inf)
            m_old = m_ref[hh]
            m_new = jnp.maximum(m_old, jnp.max(s, axis=0, keepdims=True))
            p = jnp.exp2(s - m_new)
            pv = jnp.dot(v_ref[0, hh, kj], p.astype(BF16), preferred_element_type=F32)
            acc_ref[hh] = jnp.exp2(m_old - m_new) * acc_ref[hh] + pv
            m_ref[hh] = m_new

    def body(kj, carry):
        block(kj, False)
        return carry

    lax.fori_loop(0, i, body, 0)
    block(i, True)

    ln = lax.broadcasted_iota(jnp.int32, (1, LANES), 1)
    for pr in range(ATT_HEADS // 2):
        o = []
        for hh in (2 * pr, 2 * pr + 1):
            acc = acc_ref[hh]
            o.append((acc / acc[ATT_HD:ATT_HD + 1, :]).T)
        o_ref[0, :, pr * LANES:(pr + 1) * LANES] = jnp.where(ln < ATT_HD, o[0], pltpu.roll(o[1], ATT_HD, 1)).astype(BF16)


def _attention(q_t, k, v_t, *, tq):
    B, H, S, W = k.shape
    nblk = S // tq
    assert q_t.shape == (B, H, nblk, W, tq) and v_t.shape == q_t.shape
    return pl.pallas_call(
        functools.partial(_attn_kernel, tq=tq),
        grid=(B, nblk),
        in_specs=[pl.BlockSpec((1, H, 1, W, tq), lambda b, i: (b, 0, i, 0, 0)),
                  pl.BlockSpec((1, H, S, W), lambda b, i: (b, 0, 0, 0), pipeline_mode=pl.Buffered(1)),
                  pl.BlockSpec((1, H, nblk, W, tq), lambda b, i: (b, 0, 0, 0, 0), pipeline_mode=pl.Buffered(1))],
        out_specs=pl.BlockSpec((1, tq, BRANCH_W), lambda b, i: (b, i, 0)),
        out_shape=jax.ShapeDtypeStruct((B, S, BRANCH_W), BF16),
        scratch_shapes=[pltpu.VMEM((H, 1, tq), F32), pltpu.VMEM((H, W, tq), F32)],
        compiler_params=_params("arbitrary", "arbitrary"),
        name="attention",
    )(q_t, k, v_t)


def _merge_kernel(x_ref, mod_ref, g_ref, wa_f32, wb_f32, wt_f32, wbr_ref, wout_ref, ya_ref, yb_ref, yc_ref, yd_ref,
                  o_ref, wg_ref):
    @pl.when((pl.program_id(0) == 0) & (pl.program_id(1) == 0))
    def _():
        shift, wide = GATE_SHIFT, D_MODEL + LANES
        windows = (wa_f32[0, :, 0:wide],
                   jnp.concatenate([wa_f32[0, :, D_MODEL:], wb_f32[0, :, 0:LANES]], axis=1),
                   wb_f32[0, :, 0:wide],
                   jnp.concatenate([wb_f32[0, :, D_MODEL:], wt_f32[0]], axis=1))
        for n, win in enumerate(windows):
            wg_ref[:, n * D_MODEL:(n + 1) * D_MODEL] = pltpu.roll(win, wide - shift, 1)[:, :D_MODEL].astype(BF16)

    x = x_ref[0]
    mod = mod_ref[0]
    hb = _rms_mod(x, g_ref[...], mod[1:2, :], mod[0:1, :]).astype(BF16)
    merged = jnp.zeros(x.shape, F32)
    for n, y_ref in enumerate((ya_ref, yb_ref, yc_ref, yd_ref)):
        gate = _sigmoid(jnp.dot(hb, wg_ref[:, n * D_MODEL:(n + 1) * D_MODEL], preferred_element_type=F32))
        merged = merged + gate * jnp.dot(y_ref[0], wbr_ref[n], preferred_element_type=F32)
    out = jnp.dot(merged.astype(BF16), wout_ref[...], preferred_element_type=F32)
    o_ref[0] = x + mod[2:3, :] * out


GATE_SHIFT = ATT_HEADS


def _merge(x, mod_l, norm_g, w_in, layer, wbranch, wout, ya, yb, yc, yd, *, tb):
    B, S, D = x.shape
    full = lambda a: pl.BlockSpec(a.shape, lambda b, j: (0,) * a.ndim)
    tok = lambda w: pl.BlockSpec((1, tb, w), lambda b, j: (b, j, 0))
    assert w_in.shape[2] == MIX_COLS + GATE_SHIFT + N_BRANCH * D
    return pl.pallas_call(
        _merge_kernel,
        grid=(B, S // tb),
        in_specs=[tok(D), pl.BlockSpec((1, N_MOD, D), lambda b, j: (b, 0, 0)), full(norm_g),
                  _w_in_cols(layer, MIX_COLS, 2 * D), _w_in_cols(layer, MIX_COLS + 2 * D, 2 * D),
                  _w_in_cols(layer, MIX_COLS + 4 * D, LANES),
                  full(wbranch), full(wout)] + [tok(BRANCH_W)] * 4,
        out_specs=tok(D),
        out_shape=jax.ShapeDtypeStruct((B, S, D), F32),
        scratch_shapes=[pltpu.VMEM((D, N_BRANCH * D), BF16)],
        compiler_params=_params("arbitrary", "arbitrary"),
        name="merge",
    )(x, mod_l, norm_g, w_in, w_in, w_in, wbranch, wout, ya, yb, yc, yd)


def _route(scores_t, bias_t):
    E, n = scores_t.shape
    sel = scores_t + bias_t
    eidx = lax.broadcasted_iota(jnp.int32, (E, n), 0)
    neg = jnp.full((E, n), -jnp.inf, F32)
    gscore = []
    sub = lax.broadcasted_iota(jnp.int32, (GROUP_SIZE, n), 0)
    for g in range(N_GROUPS):
        blk = sel[g * GROUP_SIZE:(g + 1) * GROUP_SIZE, :]
        m1 = jnp.max(blk, axis=0, keepdims=True)
        first = jnp.min(jnp.where(blk == m1, sub, GROUP_SIZE), axis=0, keepdims=True)
        m2 = jnp.max(jnp.where(sub == first, -jnp.inf, blk), axis=0, keepdims=True)
        gscore.append(m1 + m2)
    emask = []
    for g in range(N_GROUPS):
        beaten = jnp.zeros((1, n), jnp.int32)
        for g2 in range(N_GROUPS):
            if g2 == g:
                continue
            wins = (gscore[g2] > gscore[g]) | ((gscore[g2] == gscore[g]) & (g2 < g))
            beaten = beaten + wins.astype(jnp.int32)
        emask.append(jnp.broadcast_to(beaten < TOPK_GROUPS, (GROUP_SIZE, n)))
    cur = jnp.where(jnp.concatenate(emask, axis=0), sel, neg)
    chosen = jnp.zeros((E, n), jnp.bool_)
    firsts = []
    for _ in range(TOP_K):
        m = jnp.max(cur, axis=0, keepdims=True)
        first = jnp.min(jnp.where(cur == m, eidx, E), axis=0, keepdims=True)
        hit = eidx == first
        chosen = chosen | hit
        cur = jnp.where(hit, neg, cur)
        firsts.append(first)
    w = jnp.where(chosen, scores_t, 0.0)
    return w / jnp.sum(w, axis=0, keepdims=True) * ROUTE_SCALE, chosen, firsts


def _route_kernel(x_ref, mod_ref, g_ref, rwt_ref, rb_ref, ls_ref, lst_ref, wt_ref, bc_ref, cr_ref, cnt_ref, count):
    first_step = (pl.program_id(0) == 0) & (pl.program_id(1) == 0)

    @pl.when(first_step)
    def _():
        count[...] = jnp.zeros_like(count)

    mod = mod_ref[0]
    h = _rms_mod(x_ref[0], g_ref[...], mod[4:5, :], mod[3:4, :])
    logits_t = lax.dot_general(rwt_ref[...], h, (((1,), (1,)), ((), ())),
                               precision=HIGHEST, preferred_element_type=F32)
    gates_t, chosen, firsts = _route(_sigmoid(logits_t), rb_ref[...])
    E, tb = gates_t.shape
    ones = jnp.where(chosen, 1.0, 0.0)
    rr = lax.broadcasted_iota(jnp.int32, (tb, tb), 0)
    cc = lax.broadcasted_iota(jnp.int32, (tb, tb), 1)
    upper = jnp.where(rr <= cc, 1.0, 0.0).astype(BF16)
    incl = jnp.dot(ones.astype(BF16), upper, preferred_element_type=F32)
    block_count = jnp.sum(ones, axis=1, keepdims=True)
    block_count = block_count + (block_count.astype(jnp.int32) & 1).astype(F32)
    er = lax.broadcasted_iota(jnp.int32, (E, E), 0)
    ec = lax.broadcasted_iota(jnp.int32, (E, E), 1)
    before = jnp.where(ec < er, 1.0, 0.0).astype(F32)
    local_base = jnp.dot(before, jnp.broadcast_to(block_count, (E, LANES)), precision=HIGHEST,
                         preferred_element_type=F32)[:, 0:1]
    row_all = local_base + (incl - ones)
    eidx = lax.broadcasted_iota(jnp.int32, (E, tb), 0)
    ls_rows = [jnp.sum(jnp.where(eidx == f, row_all, 0.0), axis=0, keepdims=True) for f in firsts]
    w_rows = [jnp.sum(jnp.where(eidx == f, gates_t, 0.0), axis=0, keepdims=True) for f in firsts]
    ls_ref[...] = jnp.concatenate(ls_rows + [jnp.zeros((SUBLANES - TOP_K, tb), F32)], axis=0).astype(jnp.int32)
    zpad = jnp.zeros((LANES - TOP_K, tb), F32)
    lst_ref[0] = jnp.concatenate(ls_rows + [zpad], axis=0).T.astype(jnp.int32)
    wt_ref[0] = jnp.concatenate(w_rows + [zpad], axis=0).T

    def as_row(col):
        sel = lax.broadcasted_iota(jnp.int32, (E, LANES), 0) == lax.broadcasted_iota(jnp.int32, (E, LANES), 1)
        return jnp.sum(jnp.where(sel, col, 0.0), axis=0, keepdims=True).astype(jnp.int32)

    bc_ref[0] = as_row(block_count)
    cr_ref[0] = as_row(count[...])
    count[...] = count[...] + block_count
    cnt_ref[...] = jnp.broadcast_to(count[...], cnt_ref.shape).astype(jnp.int32)


def _route_call(x, mod_l, norm_g, router_wt, router_b, *, tb):
    B, S, D = x.shape
    nj = S // tb
    full = lambda a: pl.BlockSpec(a.shape, lambda b, j: (0,) * a.ndim)
    tok = lambda w: pl.BlockSpec((1, tb, w), lambda b, j: (b, j, 0))
    kt = pl.BlockSpec((SUBLANES, tb), lambda b, j: (0, b * nj + j))
    per_block = pl.BlockSpec((1, 1, LANES), lambda b, j: (b * nj + j, 0, 0))
    return pl.pallas_call(
        _route_kernel,
        grid=(B, nj),
        in_specs=[tok(D), pl.BlockSpec((1, N_MOD, D), lambda b, j: (b, 0, 0)),
                  full(norm_g), full(router_wt), full(router_b)],
        out_specs=[kt, tok(LANES), tok(LANES), per_block, per_block,
                   pl.BlockSpec((N_EXPERTS, LANES), lambda b, j: (0, 0))],
        out_shape=[jax.ShapeDtypeStruct((SUBLANES, B * S), jnp.int32),
                   jax.ShapeDtypeStruct((B, S, LANES), jnp.int32),
                   jax.ShapeDtypeStruct((B, S, LANES), F32),
                   jax.ShapeDtypeStruct((B * nj, 1, LANES), jnp.int32),
                   jax.ShapeDtypeStruct((B * nj, 1, LANES), jnp.int32),
                   jax.ShapeDtypeStruct((N_EXPERTS, LANES), jnp.int32)],
        scratch_shapes=[pltpu.VMEM((N_EXPERTS, 1), F32)],
        compiler_params=_params("arbitrary", "arbitrary"),
        name="route",
    )(x, mod_l, norm_g, router_wt, router_b)


def _route_fin_kernel(cnt_ref, meta_ref, base_ref, *, tm):
    cnt = cnt_ref[...]
    ntile = lax.shift_right_logical(cnt + (tm - 1), int(math.log2(tm)))
    er = lax.broadcasted_iota(jnp.int32, (N_EXPERTS, N_EXPERTS), 0)
    ec = lax.broadcasted_iota(jnp.int32, (N_EXPERTS, N_EXPERTS), 1)
    before = jnp.where(ec < er, 1.0, 0.0).astype(F32)
    start = jnp.dot(before, ntile.astype(F32), precision=HIGHEST, preferred_element_type=F32).astype(jnp.int32)
    base = start * tm
    base_ref[...] = base
    nlane = meta_ref.shape[1]
    lane = lax.broadcasted_iota(jnp.int32, (1, nlane), 1)
    end = (start + ntile)[:, 0:1]
    tile_expert = jnp.sum(jnp.where(end <= lane, 1, 0), axis=0, keepdims=True)
    tile_expert = jnp.minimum(tile_expert, N_EXPERTS - 1)
    meta_ref[...] = jnp.where(lane == nlane - 1, end[N_EXPERTS - 1:N_EXPERTS, :], tile_expert)


def _route_fin(cnt, *, tm, meta_lanes):
    full = lambda a: pl.BlockSpec(a.shape, lambda i: (0,) * a.ndim)
    outs = [jax.ShapeDtypeStruct((1, meta_lanes), jnp.int32), jax.ShapeDtypeStruct(cnt.shape, jnp.int32)]
    return pl.pallas_call(
        functools.partial(_route_fin_kernel, tm=tm),
        grid=(1,),
        in_specs=[full(cnt)],
        out_specs=[full(o) for o in outs],
        out_shape=outs,
        compiler_params=_params("arbitrary"),
        name="route_fin",
    )(cnt)


HALF = D_MODEL // 2
SLAB = HALF // LANES
HIGH_BITS = 0xFFFF0000


def _to_slabs(ref, value):
    n = value.shape[0]
    bits = lax.bitcast_convert_type(value.astype(BF16).astype(F32), jnp.uint32)
    packed = (bits[:, HALF:] & jnp.uint32(HIGH_BITS)) | (bits[:, :HALF] >> 16)
    for s in range(SLAB):
        ref[pl.ds(s, n, stride=SLAB), :] = packed[:, s * LANES:(s + 1) * LANES]


def _from_slabs(ref, n):
    packed = jnp.concatenate([ref[pl.ds(s, n, stride=SLAB), :] for s in range(SLAB)], axis=1)
    low = lax.bitcast_convert_type(packed << 16, F32).astype(BF16)
    high = lax.bitcast_convert_type(packed & jnp.uint32(HIGH_BITS), F32).astype(BF16)
    return low, high


def _for_expert_runs(base_sm, bc_sm, cr_sm, max_run, fn):
    def per_expert(e, local_row):
        n = bc_sm[0, 0, e]
        sorted_row = base_sm[e] + cr_sm[0, 0, e]
        _for_pieces(n, max_run, lambda done, size: fn(local_row + done, sorted_row + done, size))
        return local_row + n

    lax.fori_loop(0, N_EXPERTS, per_expert, 0)


def _for_pieces(n, max_piece, fn):
    done = 0
    bit = max_piece
    while bit >= 2:
        take = n & bit

        @pl.when(take != 0)
        def _(done=done, bit=bit):
            fn(done, bit)

        done = done + take
        bit //= 2


ROW_GROUPS = 7


def _local_rows(tb):
    rows = tb * TOP_K + N_EXPERTS
    assert rows % (ROW_GROUPS * 2 * SUBLANES) == 0, rows
    return rows


def _queue(nrows):
    return int(math.log2(nrows)) % 2


def _slab_rows(ref, row, nrows):
    return ref.at[pl.ds(pl.multiple_of(row * SLAB, 2 * SLAB), nrows * SLAB)]


def _dispatch_kernel(cnt_sm, base_sm, bc_sm, cr_sm, bc_prev, cr_prev, ls_ref, x_ref, mod_ref, g_ref, xs_hbm,
                     stage, zeros, sem, zsem, *, tb, tm):
    step = pl.program_id(0) * pl.num_programs(1) + pl.program_id(1)
    last_step = pl.num_programs(0) * pl.num_programs(1) - 1
    cur = lax.rem(step, 2)
    mod = mod_ref[0]
    hb = _rms_mod(x_ref[0], g_ref[...], mod[4:5, :], mod[3:4, :]).astype(BF16)
    ls = ls_ref[...]
    gr = _local_rows(tb) // ROW_GROUPS
    for grp in range(ROW_GROUPS):
        row = grp * gr + lax.broadcasted_iota(jnp.int32, (gr, tb), 0)
        hit = ls[0:1, :] == row
        for k in range(1, TOP_K):
            hit = hit | (ls[k:k + 1, :] == row)
        perm = jnp.where(hit, 1.0, 0.0).astype(BF16)
        _to_slabs(stage.at[cur, pl.ds(grp * gr * SLAB, gr * SLAB)], jnp.dot(perm, hb, preferred_element_type=F32))

    def run_copy(buf):
        return lambda local_row, sorted_row, nrows: pltpu.make_async_copy(
            _slab_rows(stage.at[buf], local_row, nrows), _slab_rows(xs_hbm, sorted_row, nrows), sem.at[buf])

    _for_expert_runs(base_sm, bc_sm, cr_sm, tb, lambda *a: run_copy(cur)(*a).start(priority=_queue(a[2])))

    @pl.when(step == 0)
    def _():
        zeros[...] = jnp.zeros_like(zeros)

        def per_expert(e, carry):
            n = cnt_sm[e]
            npad = lax.rem(tm - lax.rem(n, tm), tm)
            zcopy = lambda done, size: pltpu.make_async_copy(
                _slab_rows(zeros, 0, size), _slab_rows(xs_hbm, base_sm[e] + n + done, size), zsem)
            _for_pieces(npad, tm // 2, lambda *a: zcopy(*a).start())
            _for_pieces(npad, tm // 2, lambda *a: zcopy(*a).wait())
            return carry

        lax.fori_loop(0, N_EXPERTS, per_expert, 0)

    @pl.when(step > 0)
    def _():
        _for_expert_runs(base_sm, bc_prev, cr_prev, tb, lambda *a: run_copy(1 - cur)(*a).wait())

    @pl.when(step == last_step)
    def _():
        _for_expert_runs(base_sm, bc_sm, cr_sm, tb, lambda *a: run_copy(cur)(*a).wait())


def _dispatch(cnt, base, blk_cnt, blk_carry, ls, x, mod_l, norm_g, *, tb, tm, n_slots):
    B, S, D = x.shape
    nj = S // tb
    smem_block = lambda shift: pl.BlockSpec(
        (1, 1, LANES), lambda b, j, *_: (jnp.maximum(b * nj + j + shift, 0), 0, 0), memory_space=pltpu.SMEM)
    grid_spec = pltpu.PrefetchScalarGridSpec(
        num_scalar_prefetch=2,
        grid=(B, nj),
        in_specs=[smem_block(0), smem_block(0), smem_block(-1), smem_block(-1),
                  pl.BlockSpec((SUBLANES, tb), lambda b, j, *_: (0, b * nj + j)),
                  pl.BlockSpec((1, tb, D), lambda b, j, *_: (b, j, 0)),
                  pl.BlockSpec((1, N_MOD, D), lambda b, j, *_: (b, 0, 0)),
                  pl.BlockSpec(norm_g.shape, lambda b, j, *_: (0, 0))],
        out_specs=pl.BlockSpec(memory_space=pltpu.HBM),
        scratch_shapes=[pltpu.VMEM((2, _local_rows(tb) * SLAB, LANES), jnp.uint32),
                        pltpu.VMEM((tm // 2 * SLAB, LANES), jnp.uint32),
                        pltpu.SemaphoreType.DMA((2,)), pltpu.SemaphoreType.DMA])
    return pl.pallas_call(
        functools.partial(_dispatch_kernel, tb=tb, tm=tm),
        grid_spec=grid_spec,
        out_shape=jax.ShapeDtypeStruct((n_slots * SLAB, LANES), jnp.uint32),
        compiler_params=_params("arbitrary", "arbitrary"),
        name="dispatch",
    )(cnt, base, blk_cnt, blk_carry, blk_cnt, blk_carry, ls, x, mod_l, norm_g)


def _experts_kernel(te_sm, nt_sm, x_ref, w1_ref, w3_ref, w2_ref, y_ref, w1b, w3b, w2b, *, tm):
    i = pl.program_id(0)
    in_use = i < nt_sm[0]

    @pl.when(in_use)
    def _():
        @pl.when((i == 0) | (te_sm[i] != te_sm[jnp.maximum(i - 1, 0)]))
        def _():
            w1b[...] = w1_ref[0].astype(BF16)
            w3b[...] = w3_ref[0].astype(BF16)
            w2b[...] = w2_ref[0].astype(BF16)

        for c in range(tm // EXPERT_CHUNK):
            rows = pl.ds(c * EXPERT_CHUNK * SLAB, EXPERT_CHUNK * SLAB)
            x_lo, x_hi = _from_slabs(x_ref.at[rows], EXPERT_CHUNK)
            up = lambda w: (jnp.dot(x_lo, w[:HALF, :], preferred_element_type=F32)
                            + jnp.dot(x_hi, w[HALF:, :], preferred_element_type=F32))
            hid = _silu(up(w1b)) * up(w3b)
            _to_slabs(y_ref.at[rows], jnp.dot(hid.astype(BF16), w2b[...], preferred_element_type=F32))

    @pl.when(jnp.logical_not(in_use))
    def _():
        y_ref[...] = jnp.zeros_like(y_ref)


def _experts(tile_expert, n_tiles, xs, w1, w3, w2, layer, *, tm):
    _, E, D, F = w1.shape
    n_slots = xs.shape[0] // SLAB
    last = lambda i, te, nt: jnp.minimum(i, nt[0] - 1)
    grid_spec = pltpu.PrefetchScalarGridSpec(
        num_scalar_prefetch=2,
        grid=(n_slots // tm,),
        in_specs=[pl.BlockSpec((tm * SLAB, LANES), lambda i, te, nt: (last(i, te, nt), 0)),
                  pl.BlockSpec((None, 1, D, F), lambda i, te, nt: (layer, te[last(i, te, nt)], 0, 0)),
                  pl.BlockSpec((None, 1, D, F), lambda i, te, nt: (layer, te[last(i, te, nt)], 0, 0)),
                  pl.BlockSpec((None, 1, F, D), lambda i, te, nt: (layer, te[last(i, te, nt)], 0, 0))],
        out_specs=pl.BlockSpec((tm * SLAB, LANES), lambda i, te, nt: (i, 0)),
        scratch_shapes=[pltpu.VMEM((D, F), BF16), pltpu.VMEM((D, F), BF16), pltpu.VMEM((F, D), BF16)])
    return pl.pallas_call(
        functools.partial(_experts_kernel, tm=tm),
        grid_spec=grid_spec,
        out_shape=jax.ShapeDtypeStruct(xs.shape, jnp.uint32),
        compiler_params=_params("arbitrary"),
        name="experts",
    )(tile_expert, n_tiles, xs, w1, w3, w2)


def _combine_kernel(base_sm, bc_sm, cr_sm, bc_next, cr_next, x_ref, mod_ref, g_ref, lst_ref, wt_ref,
                    sw1_ref, sw3_ref, sw2_ref, fg_ref, ys_hbm, o_ref, stage, sem, *, tb, final_norm):
    step = pl.program_id(0) * pl.num_programs(1) + pl.program_id(1)
    last_step = pl.num_programs(0) * pl.num_programs(1) - 1
    cur = lax.rem(step, 2)

    def run_copy(buf):
        return lambda local_row, sorted_row, nrows: pltpu.make_async_copy(
            _slab_rows(ys_hbm, sorted_row, nrows), _slab_rows(stage.at[buf], local_row, nrows), sem.at[buf])

    @pl.when(step == 0)
    def _():
        stage[...] = jnp.zeros_like(stage)
        _for_expert_runs(base_sm, bc_sm, cr_sm, tb, lambda *a: run_copy(cur)(*a).start(priority=_queue(a[2])))

    @pl.when(step < last_step)
    def _():
        _for_expert_runs(base_sm, bc_next, cr_next, tb, lambda *a: run_copy(1 - cur)(*a).start(priority=_queue(a[2])))

    x = x_ref[0]
    mod = mod_ref[0]
    hb = _rms_mod(x, g_ref[...], mod[4:5, :], mod[3:4, :]).astype(BF16)
    a = jnp.dot(hb, sw1_ref[...], preferred_element_type=F32)
    b = jnp.dot(hb, sw3_ref[...], preferred_element_type=F32)
    acc = jnp.dot((_silu(a) * b).astype(BF16), sw2_ref[...], preferred_element_type=F32)
    wt = wt_ref[0]
    lst = lst_ref[0]
    _for_expert_runs(base_sm, bc_sm, cr_sm, tb, lambda *a: run_copy(cur)(*a).wait())
    gr = _local_rows(tb) // ROW_GROUPS
    routed = [jnp.zeros((tb, HALF), F32)] * 2
    for grp in range(ROW_GROUPS):
        halves = _from_slabs(stage.at[cur, pl.ds(grp * gr * SLAB, gr * SLAB)], gr)
        row = grp * gr + lax.broadcasted_iota(jnp.int32, (1, gr), 1)
        gate = jnp.zeros((tb, gr), F32)
        for k in range(TOP_K):
            gate = jnp.where(lst[:, k:k + 1] == row, wt[:, k:k + 1], gate)
        gate = gate.astype(BF16)
        routed = [r + jnp.dot(gate, yg, preferred_element_type=F32) for r, yg in zip(routed, halves)]
    acc = acc + jnp.concatenate(routed, axis=1)
    y = x + mod[5:6, :] * acc
    if final_norm:
        y = y * lax.rsqrt(jnp.mean(y * y, axis=-1, keepdims=True) + EPS) * fg_ref[...]
    o_ref[0] = y


def _combine(base, blk_cnt, blk_carry, x, mod_l, norm_g, lst, wt, sw1, sw3, sw2, final_g, ys, *, tb, final_norm):
    B, S, D = x.shape
    nj = S // tb
    full = lambda a: pl.BlockSpec(a.shape, lambda b, j, *_: (0,) * a.ndim)
    tok = lambda w: pl.BlockSpec((1, tb, w), lambda b, j, *_: (b, j, 0))
    smem_block = lambda shift: pl.BlockSpec(
        (1, 1, LANES), lambda b, j, *_: (jnp.minimum(b * nj + j + shift, B * nj - 1), 0, 0), memory_space=pltpu.SMEM)
    grid_spec = pltpu.PrefetchScalarGridSpec(
        num_scalar_prefetch=1,
        grid=(B, nj),
        in_specs=[smem_block(0), smem_block(0), smem_block(1), smem_block(1), tok(D),
                  pl.BlockSpec((1, N_MOD, D), lambda b, j, *_: (b, 0, 0)),
                  full(norm_g), tok(LANES), tok(LANES), full(sw1), full(sw3), full(sw2), full(final_g),
                  pl.BlockSpec(memory_space=pltpu.HBM)],
        out_specs=tok(D),
        scratch_shapes=[pltpu.VMEM((2, _local_rows(tb) * SLAB, LANES), jnp.uint32), pltpu.SemaphoreType.DMA((2,))])
    return pl.pallas_call(
        functools.partial(_combine_kernel, tb=tb, final_norm=final_norm),
        grid_spec=grid_spec,
        out_shape=jax.ShapeDtypeStruct((B, S, D), F32),
        compiler_params=_params("arbitrary", "arbitrary"),
        name="combine",
    )(base, blk_cnt, blk_carry, blk_cnt, blk_carry, x, mod_l, norm_g, lst, wt, sw1, sw3, sw2, final_g, ys)


def _block_diag(w):
    G, a, b = w.shape
    out = jnp.zeros((G * a, G * b), w.dtype)
    for g in range(G):
        out = out.at[g * a:(g + 1) * a, g * b:(g + 1) * b].set(w[g])
    return out


def kernel(x, c, w_in, b_f, pool_w, pool_scale, sg_ln_g, sg_ln_b, sg_w, sg_b, conv_w, conv_b, conv_ln_g,
           conv_ln_b, w_branch, w_out, mix_norm_g, ffn_norm_g, ada_w, ada_b, router_w, router_bias,
           exp_w1, exp_w3, exp_w2, shared_w1, shared_w3, shared_w2, final_norm_g):
    B, S, D = x.shape
    L = w_in.shape[0]
    tb = min(512, S)
    tq = tb
    tm = EXPERT_TILE
    n_tiles = -(-(B * (S // tb) * _local_rows(tb)) // tm) + N_EXPERTS
    meta_lanes = -(-(n_tiles + 1) // LANES) * LANES
    row = lambda a: a.reshape(1, -1)

    mod = _modulation(c, ada_w, ada_b)
    for l in range(L):
        bfc = jnp.pad(b_f[l], (0, LANES - ATT_HEADS)).reshape(1, LANES)
        poolw = _block_diag(pool_w[l]).astype(BF16)
        sgb = jnp.repeat(sg_b[l].T, SG_HD, axis=1)

        ya, yb, yc, q, k, v = _mixer_in(
            x, mod[l], row(mix_norm_g[l]), w_in, l, bfc, poolw, row(pool_scale[l]),
            row(sg_ln_g[l]), row(sg_ln_b[l]), sg_w[l], sgb, conv_w[l], row(conv_b[l]),
            row(conv_ln_g[l]), row(conv_ln_b[l]), tb=tb)
        yd = _attention(q, k, v, tq=tq)
        x = _merge(x, mod[l], row(mix_norm_g[l]), w_in, l, w_branch[l].astype(BF16), w_out[l].astype(BF16),
                   ya, yb, yc, yd, tb=tb)
        fnorm = row(ffn_norm_g[l])
        ls, lst, wt, blk_cnt, blk_carry, cnt = _route_call(
            x, mod[l], fnorm, router_w[l].T, router_bias[l].reshape(-1, 1), tb=tb)
        meta, base = _route_fin(cnt, tm=tm, meta_lanes=meta_lanes)
        xs = _dispatch(cnt[:, 0], base[:, 0], blk_cnt, blk_carry, ls, x, mod[l], fnorm,
                       tb=tb, tm=tm, n_slots=n_tiles * tm)
        ys = _experts(meta[0, :n_tiles], meta[0, meta_lanes - 1:], xs, exp_w1, exp_w3, exp_w2, l, tm=tm)
        x = _combine(base[:, 0], blk_cnt, blk_carry, x, mod[l], fnorm, lst, wt, shared_w1[l].astype(BF16),
                     shared_w3[l].astype(BF16), shared_w2[l].astype(BF16), row(final_norm_g), ys,
                     tb=tb, final_norm=(l == L - 1))
    return x
```

```python
import functools
import math

import jax
import jax.numpy as jnp
from jax import lax
from jax.experimental import pallas as pl
from jax.experimental.pallas import tpu as pltpu

F32 = jnp.float32
BF16 = jnp.bfloat16
HIGHEST = lax.Precision.HIGHEST

D_MODEL = 1024
BRANCH_W = 256
N_BRANCH = 4
POOL_WINDOWS = (2, 4, 8, 16)
POOL_GW = 64
POOL_TAIL = 16
SG_HEADS = 4
SG_CHUNK = 128
SG_HD = 64
CONV_WIDTH = 31
CONV_TAIL = 32
ATT_HEADS = 4
ATT_HD = 64
N_EXPERTS = 64
TOP_K = 6
N_GROUPS = 8
GROUP_SIZE = N_EXPERTS // N_GROUPS
TOPK_GROUPS = 4
EXPERT_FF = 256
SHARED_FF = 256
ROUTE_SCALE = 2.5
EPS = 1e-6
N_MOD = 6
MIX_COLS = 8 * BRANCH_W
LANES = 128
SUBLANES = 8
LOG2E = 1.4426950408889634
EXPERT_TILE = 512
EXPERT_CHUNK = 256
VMEM_LIMIT = 56 * 1024 * 1024


def _sigmoid(x):
    return 1.0 / (1.0 + jnp.exp(-x))


def _silu(x):
    return x * _sigmoid(x)


def _rms_mod(x, g, scale, shift):
    y = x * lax.rsqrt(jnp.mean(x * x, axis=-1, keepdims=True) + EPS)
    return (y * g) * (1.0 + scale) + shift


def _layer_norm(x, g, b):
    mu = jnp.mean(x, axis=-1, keepdims=True)
    xc = x - mu
    var = jnp.mean(xc * xc, axis=-1, keepdims=True)
    return xc * lax.rsqrt(var + EPS) * g + b


def _params(*sem):
    return pltpu.CompilerParams(dimension_semantics=sem, vmem_limit_bytes=VMEM_LIMIT)


def _mod_kernel(c_ref, w_ref, b_ref, o_ref):
    c = c_ref[...]
    o_ref[0] = jnp.dot(_silu(c), w_ref[0], precision=HIGHEST, preferred_element_type=F32) + b_ref[0]


def _modulation(c, ada_w, ada_b):
    L, D, N = ada_w.shape
    B = c.shape[0]
    tn = 1536
    out = pl.pallas_call(
        _mod_kernel,
        grid=(L, N // tn),
        in_specs=[pl.BlockSpec((B, D), lambda l, j: (0, 0)),
                  pl.BlockSpec((1, D, tn), lambda l, j: (l, 0, j)),
                  pl.BlockSpec((1, 1, tn), lambda l, j: (l, 0, j))],
        out_specs=pl.BlockSpec((1, B, tn), lambda l, j: (l, 0, j)),
        out_shape=jax.ShapeDtypeStruct((L, B, N), F32),
        compiler_params=_params("arbitrary", "arbitrary"),
        name="modulation",
    )(c, ada_w, ada_b.reshape(L, 1, N))
    return out.reshape(L, B, N_MOD, D)


def _mixer_in_kernel(x_ref, mod_ref, g_ref, wmix_f32, wfc_f32, bfc_ref,
                     poolw_ref, pools_ref, sglg_ref, sglb_ref, sgw_ref, sgb_ref,
                     cw_ref, cb_ref, clg_ref, clb_ref,
                     ya_ref, yb_ref, yc_ref, q_ref, k_ref, v_ref,
                     pool_ext, conv_ext, cum_c, wmix_ref, wfc_ref, *, tb):
    j = pl.program_id(1)

    @pl.when((pl.program_id(0) == 0) & (j == 0))
    def _():
        wmix_ref[...] = wmix_f32[0].astype(BF16)
        wfc_ref[...] = wfc_f32[0].astype(BF16)

    @pl.when(j == 0)
    def _():
        pool_ext[0:POOL_TAIL, :] = jnp.zeros((POOL_TAIL, BRANCH_W), F32)
        conv_ext[0:CONV_TAIL, :] = jnp.zeros((CONV_TAIL, BRANCH_W), F32)
        cum_c[...] = jnp.zeros_like(cum_c)

    mod = mod_ref[0]
    h = _rms_mod(x_ref[0], g_ref[...], mod[1:2, :], mod[0:1, :])
    hb = h.astype(BF16)
    proj = jnp.dot(hb, wmix_ref[...], preferred_element_type=F32)

    lane = lax.broadcasted_iota(jnp.int32, (1, BRANCH_W), 1)
    row = lax.broadcasted_iota(jnp.int32, (tb, 1), 0)

    u = proj[:, 0:BRANCH_W]
    pool_ext[POOL_TAIL:POOL_TAIL + tb, :] = u
    ext = pool_ext[...]
    s2 = ext + pltpu.roll(ext, 1, 0)
    s4 = s2 + pltpu.roll(s2, 2, 0)
    s8 = s4 + pltpu.roll(s4, 4, 0)
    s16 = s8 + pltpu.roll(s8, 8, 0)
    grp = lane // POOL_GW
    wsum = jnp.where(grp == 0, s2, jnp.where(grp == 1, s4, jnp.where(grp == 2, s8, s16)))[POOL_TAIL:, :]
    win = jnp.where(grp == 0, 2.0, jnp.where(grp == 1, 4.0, jnp.where(grp == 2, 8.0, 16.0)))
    count = jnp.minimum((j * tb + row + 1).astype(F32), win)
    pooled = wsum / count - u
    ya = jnp.dot(pooled.astype(BF16), poolw_ref[...], preferred_element_type=F32) * pools_ref[...]
    ya_ref[0] = ya.astype(BF16)
    pool_ext[0:POOL_TAIL, :] = u[tb - POOL_TAIL:, :]

    z = proj[:, BRANCH_W:3 * BRANCH_W]
    z = 0.5 * z * (1.0 + jnp.tanh(math.sqrt(2.0 / math.pi) * (z + 0.044715 * (z * z * z))))
    su = z[:, 0:BRANCH_W]
    sv = _layer_norm(z[:, BRANCH_W:], sglg_ref[...], sglb_ref[...])
    r128 = lax.broadcasted_iota(jnp.int32, (SG_CHUNK, SG_CHUNK), 0)
    c128 = lax.broadcasted_iota(jnp.int32, (SG_CHUNK, SG_CHUNK), 1)
    wcat = jnp.concatenate(
        [jnp.where(r128 >= c128, sgw_ref[hh], 0.0) for hh in range(SG_HEADS)], axis=1).astype(BF16)
    head = lane // SG_HD
    for ci in range(tb // SG_CHUNK):
        rows = slice(ci * SG_CHUNK, (ci + 1) * SG_CHUNK)
        vch = sv[rows, :]
        vstack = jnp.concatenate(
            [jnp.where(head == hh, vch, 0.0) for hh in range(SG_HEADS)], axis=0).astype(BF16)
        s = jnp.dot(wcat, vstack, preferred_element_type=F32) + sgb_ref[...]
        yb_ref[0, rows, :] = (su[rows, :] * s).astype(BF16)

    glu = proj[:, 3 * BRANCH_W:4 * BRANCH_W] * _sigmoid(proj[:, 4 * BRANCH_W:5 * BRANCH_W])
    conv_ext[CONV_TAIL:CONV_TAIL + tb, :] = glu
    acc = jnp.zeros((tb, BRANCH_W), F32) + cb_ref[...]
    for kk in range(CONV_WIDTH):
        off = CONV_TAIL - (CONV_WIDTH - 1) + kk
        acc = acc + cw_ref[kk:kk + 1, :] * conv_ext[off:off + tb, :]
    yc_ref[0] = _silu(_layer_norm(acc, clg_ref[...], clb_ref[...])).astype(BF16)
    conv_ext[0:CONV_TAIL, :] = glu[tb - CONV_TAIL:, :]

    def log_sigmoid(t):
        return jnp.minimum(t, 0.0) - jnp.log(1.0 + jnp.exp(-jnp.abs(t)))

    rr = lax.broadcasted_iota(jnp.int32, (tb, tb), 0)
    cc = lax.broadcasted_iota(jnp.int32, (tb, tb), 1)
    lower = jnp.where(rr >= cc, 1.0, 0.0).astype(BF16)
    lf_c = log_sigmoid(jnp.dot(hb, wfc_ref[...], preferred_element_type=F32) + bfc_ref[...])
    lf_hi = lf_c.astype(BF16)
    lf_mid = (lf_c - lf_hi.astype(F32)).astype(BF16)
    lf_lo = ((lf_c - lf_hi.astype(F32)) - lf_mid.astype(F32)).astype(BF16)
    cs_c = cum_c[...] + sum(jnp.dot(lower, piece, preferred_element_type=F32) for piece in (lf_lo, lf_mid, lf_hi))
    cum_c[...] = cs_c[tb - 1:tb, :]
    cl2 = cs_c * LOG2E
    ln = lax.broadcasted_iota(jnp.int32, (1, LANES), 1)
    ones_q = jnp.where((ln >= ATT_HD + 3) & (ln < ATT_HD + 6), 1.0, 0.0)
    ones_k = jnp.where((ln >= ATT_HD) & (ln < ATT_HD + 3), 1.0, 0.0)
    ones_v = jnp.where(ln == ATT_HD, 1.0, 0.0)
    for hh in range(ATT_HEADS):
        pair = (hh // 2) * LANES
        qs = proj[:, 5 * BRANCH_W + pair:5 * BRANCH_W + pair + LANES] * (LOG2E / math.sqrt(ATT_HD))
        ks = proj[:, 6 * BRANCH_W + pair:6 * BRANCH_W + pair + LANES]
        vs = proj[:, 7 * BRANCH_W + pair:7 * BRANCH_W + pair + LANES]
        if hh % 2:
            qs, ks, vs = (pltpu.roll(a, ATT_HD, 1) for a in (qs, ks, vs))
        c = cl2[:, hh:hh + 1]
        hi = c.astype(BF16).astype(F32)
        mid = (c - hi).astype(BF16).astype(F32)
        lo = (c - hi) - mid
        q_extra = jnp.where(ln == ATT_HD, hi, jnp.where(ln == ATT_HD + 1, mid, jnp.where(ln == ATT_HD + 2, lo, ones_q)))
        k_extra = jnp.where(ln == ATT_HD + 3, -hi,
                            jnp.where(ln == ATT_HD + 4, -mid, jnp.where(ln == ATT_HD + 5, -lo, ones_k)))
        q_ref[0, hh, 0] = jnp.where(ln < ATT_HD, qs, q_extra).T.astype(BF16)
        k_ref[0, hh] = jnp.where(ln < ATT_HD, ks, k_extra).astype(BF16)
        v_ref[0, hh, 0] = jnp.where(ln < ATT_HD, vs, ones_v).T.astype(BF16)


def _w_in_cols(layer, first_col, ncols):
    return pl.BlockSpec((1, D_MODEL, ncols), lambda b, j: (layer, 0, first_col // ncols), pipeline_mode=pl.Buffered(1))


def _mixer_in(x, mod_l, norm_g, w_in, layer, bfc, poolw, pools, sglg, sglb, sgw, sgb, cw, cb, clg, clb, *, tb):
    B, S, D = x.shape
    full = lambda a: pl.BlockSpec(a.shape, lambda b, j: (0,) * a.ndim)
    tok = lambda w: pl.BlockSpec((1, tb, w), lambda b, j: (b, j, 0))
    head = pl.BlockSpec((1, ATT_HEADS, tb, LANES), lambda b, j: (b, 0, j, 0))
    head_t = pl.BlockSpec((1, ATT_HEADS, 1, LANES, tb), lambda b, j: (b, 0, j, 0, 0))
    consts = (bfc, poolw, pools, sglg, sglb, sgw, sgb, cw, cb, clg, clb)
    act = jax.ShapeDtypeStruct((B, S, BRANCH_W), BF16)
    att = jax.ShapeDtypeStruct((B, ATT_HEADS, S, LANES), BF16)
    att_t = jax.ShapeDtypeStruct((B, ATT_HEADS, S // tb, LANES, tb), BF16)
    return pl.pallas_call(
        functools.partial(_mixer_in_kernel, tb=tb),
        grid=(B, S // tb),
        in_specs=[tok(D), pl.BlockSpec((1, N_MOD, D), lambda b, j: (b, 0, 0)), full(norm_g),
                  _w_in_cols(layer, 0, MIX_COLS), _w_in_cols(layer, MIX_COLS, LANES)] + [full(a) for a in consts],
        out_specs=[tok(BRANCH_W)] * 3 + [head_t, head, head_t],
        out_shape=[act] * 3 + [att_t, att, att_t],
        scratch_shapes=[pltpu.VMEM((POOL_TAIL + tb, BRANCH_W), F32),
                        pltpu.VMEM((CONV_TAIL + tb, BRANCH_W), F32),
                        pltpu.VMEM((1, LANES), F32),
                        pltpu.VMEM((D, MIX_COLS), BF16), pltpu.VMEM((D, LANES), BF16)],
        compiler_params=_params("arbitrary", "arbitrary"),
        name="mixer_in",
    )(x, mod_l, norm_g, w_in, w_in, *consts)


def _attn_kernel(q_ref, k_ref, v_ref, o_ref, m_ref, acc_ref, *, tq):
    i = pl.program_id(1)
    key = lax.broadcasted_iota(jnp.int32, (tq, tq), 0)
    qry = lax.broadcasted_iota(jnp.int32, (tq, tq), 1)
    m_ref[...] = jnp.full(m_ref.shape, -jnp.inf, F32)
    acc_ref[...] = jnp.zeros(acc_ref.shape, F32)

    def block(kj, diagonal):
        ks = pl.multiple_of(kj * tq, tq)
        logits = [jnp.dot(k_ref[0, hh, pl.ds(ks, tq), :], q_ref[0, hh, 0], preferred_element_type=F32)
                  for hh in range(ATT_HEADS)]
        for hh in range(ATT_HEADS):
            s = logits[hh]
            if diagonal:
                s = jnp.where(key <= qry, s, -jnp.inf)
            m_old = m_ref[hh]
            m_new = jnp.maximum(m_old, jnp.max(s, axis=0, keepdims=True))
            p = jnp.exp2(s - m_new)
            pv = jnp.dot(v_ref[0, hh, kj], p.astype(BF16), preferred_element_type=F32)
            acc_ref[hh] = jnp.exp2(m_old - m_new) * acc_ref[hh] + pv
            m_ref[hh] = m_new

    def body(kj, carry):
        block(kj, False)
        return carry

    lax.fori_loop(0, i, body, 0)
    block(i, True)

    ln = lax.broadcasted_iota(jnp.int32, (1, LANES), 1)
    for pr in range(ATT_HEADS // 2):
        o = []
        for hh in (2 * pr, 2 * pr + 1):
            acc = acc_ref[hh]
            o.append((acc / acc[ATT_HD:ATT_HD + 1, :]).T)
        o_ref[0, :, pr * LANES:(pr + 1) * LANES] = jnp.where(ln < ATT_HD, o[0], pltpu.roll(o[1], ATT_HD, 1)).astype(BF16)


def _attention(q_t, k, v_t, *, tq):
    B, H, S, W = k.shape
    nblk = S // tq
    assert q_t.shape == (B, H, nblk, W, tq) and v_t.shape == q_t.shape
    return pl.pallas_call(
        functools.partial(_attn_kernel, tq=tq),
        grid=(B, nblk),
        in_specs=[pl.BlockSpec((1, H, 1, W, tq), lambda b, i: (b, 0, i, 0, 0)),
                  pl.BlockSpec((1, H, S, W), lambda b, i: (b, 0, 0, 0), pipeline_mode=pl.Buffered(1)),
                  pl.BlockSpec((1, H, nblk, W, tq), lambda b, i: (b, 0, 0, 0, 0), pipeline_mode=pl.Buffered(1))],
        out_specs=pl.BlockSpec((1, tq, BRANCH_W), lambda b, i: (b, i, 0)),
        out_shape=jax.ShapeDtypeStruct((B, S, BRANCH_W), BF16),
        scratch_shapes=[pltpu.VMEM((H, 1, tq), F32), pltpu.VMEM((H, W, tq), F32)],
        compiler_params=_params("arbitrary", "arbitrary"),
        name="attention",
    )(q_t, k, v_t)


def _merge_kernel(x_ref, mod_ref, g_ref, wa_f32, wb_f32, wt_f32, wbr_ref, wout_ref, ya_ref, yb_ref, yc_ref, yd_ref,
                  o_ref, wg_ref):
    @pl.when((pl.program_id(0) == 0) & (pl.program_id(1) == 0))
    def _():
        shift, wide = GATE_SHIFT, D_MODEL + LANES
        windows = (wa_f32[0, :, 0:wide],
                   jnp.concatenate([wa_f32[0, :, D_MODEL:], wb_f32[0, :, 0:LANES]], axis=1),
                   wb_f32[0, :, 0:wide],
                   jnp.concatenate([wb_f32[0, :, D_MODEL:], wt_f32[0]], axis=1))
        for n, win in enumerate(windows):
            wg_ref[:, n * D_MODEL:(n + 1) * D_MODEL] = pltpu.roll(win, wide - shift, 1)[:, :D_MODEL].astype(BF16)

    x = x_ref[0]
    mod = mod_ref[0]
    hb = _rms_mod(x, g_ref[...], mod[1:2, :], mod[0:1, :]).astype(BF16)
    merged = jnp.zeros(x.shape, F32)
    for n, y_ref in enumerate((ya_ref, yb_ref, yc_ref, yd_ref)):
        gate = _sigmoid(jnp.dot(hb, wg_ref[:, n * D_MODEL:(n + 1) * D_MODEL], preferred_element_type=F32))
        merged = merged + gate * jnp.dot(y_ref[0], wbr_ref[n], preferred_element_type=F32)
    out = jnp.dot(merged.astype(BF16), wout_ref[...], preferred_element_type=F32)
    o_ref[0] = x + mod[2:3, :] * out


GATE_SHIFT = ATT_HEADS


def _merge(x, mod_l, norm_g, w_in, layer, wbranch, wout, ya, yb, yc, yd, *, tb):
    B, S, D = x.shape
    full = lambda a: pl.BlockSpec(a.shape, lambda b, j: (0,) * a.ndim)
    tok = lambda w: pl.BlockSpec((1, tb, w), lambda b, j: (b, j, 0))
    assert w_in.shape[2] == MIX_COLS + GATE_SHIFT + N_BRANCH * D
    return pl.pallas_call(
        _merge_kernel,
        grid=(B, S // tb),
        in_specs=[tok(D), pl.BlockSpec((1, N_MOD, D), lambda b, j: (b, 0, 0)), full(norm_g),
                  _w_in_cols(layer, MIX_COLS, 2 * D), _w_in_cols(layer, MIX_COLS + 2 * D, 2 * D),
                  _w_in_cols(layer, MIX_COLS + 4 * D, LANES),
                  full(wbranch), full(wout)] + [tok(BRANCH_W)] * 4,
        out_specs=tok(D),
        out_shape=jax.ShapeDtypeStruct((B, S, D), F32),
        scratch_shapes=[pltpu.VMEM((D, N_BRANCH * D), BF16)],
        compiler_params=_params("arbitrary", "arbitrary"),
        name="merge",
    )(x, mod_l, norm_g, w_in, w_in, w_in, wbranch, wout, ya, yb, yc, yd)


def _route(scores_t, bias_t):
    E, n = scores_t.shape
    sel = scores_t + bias_t
    eidx = lax.broadcasted_iota(jnp.int32, (E, n), 0)
    neg = jnp.full((E, n), -jnp.inf, F32)
    gscore = []
    sub = lax.broadcasted_iota(jnp.int32, (GROUP_SIZE, n), 0)
    for g in range(N_GROUPS):
        blk = sel[g * GROUP_SIZE:(g + 1) * GROUP_SIZE, :]
        m1 = jnp.max(blk, axis=0, keepdims=True)
        first = jnp.min(jnp.where(blk == m1, sub, GROUP_SIZE), axis=0, keepdims=True)
        m2 = jnp.max(jnp.where(sub == first, -jnp.inf, blk), axis=0, keepdims=True)
        gscore.append(m1 + m2)
    emask = []
    for g in range(N_GROUPS):
        beaten = jnp.zeros((1, n), jnp.int32)
        for g2 in range(N_GROUPS):
            if g2 == g:
                continue
            wins = (gscore[g2] > gscore[g]) | ((gscore[g2] == gscore[g]) & (g2 < g))
            beaten = beaten + wins.astype(jnp.int32)
        emask.append(jnp.broadcast_to(beaten < TOPK_GROUPS, (GROUP_SIZE, n)))
    cur = jnp.where(jnp.concatenate(emask, axis=0), sel, neg)
    chosen = jnp.zeros((E, n), jnp.bool_)
    firsts = []
    for _ in range(TOP_K):
        m = jnp.max(cur, axis=0, keepdims=True)
        first = jnp.min(jnp.where(cur == m, eidx, E), axis=0, keepdims=True)
        hit = eidx == first
        chosen = chosen | hit
        cur = jnp.where(hit, neg, cur)
        firsts.append(first)
    w = jnp.where(chosen, scores_t, 0.0)
    return w / jnp.sum(w, axis=0, keepdims=True) * ROUTE_SCALE, chosen, firsts


def _route_kernel(x_ref, mod_ref, g_ref, rwt_ref, rb_ref, ls_ref, lst_ref, wt_ref, bc_ref, cr_ref, cnt_ref, count):
    first_step = (pl.program_id(0) == 0) & (pl.program_id(1) == 0)

    @pl.when(first_step)
    def _():
        count[...] = jnp.zeros_like(count)

    mod = mod_ref[0]
    h = _rms_mod(x_ref[0], g_ref[...], mod[4:5, :], mod[3:4, :])
    def split(a):
        hi = a.astype(BF16)
        return hi, (a - hi.astype(F32)).astype(BF16)

    nt_dot = lambda a, b: lax.dot_general(a, b, (((1,), (1,)), ((), ())), preferred_element_type=F32)
    (w_hi, w_lo), (h_hi, h_lo) = split(rwt_ref[...]), split(h)
    logits_t = nt_dot(w_hi, h_lo) + nt_dot(w_lo, h_hi) + nt_dot(w_hi, h_hi)
    gates_t, chosen, firsts = _route(_sigmoid(logits_t), rb_ref[...])
    E, tb = gates_t.shape
    ones = jnp.where(chosen, 1.0, 0.0)
    rr = lax.broadcasted_iota(jnp.int32, (tb, tb), 0)
    cc = lax.broadcasted_iota(jnp.int32, (tb, tb), 1)
    upper = jnp.where(rr <= cc, 1.0, 0.0).astype(BF16)
    incl = jnp.dot(ones.astype(BF16), upper, preferred_element_type=F32)
    block_count = jnp.sum(ones, axis=1, keepdims=True)
    block_count = block_count + (block_count.astype(jnp.int32) & 1).astype(F32)
    er = lax.broadcasted_iota(jnp.int32, (E, E), 0)
    ec = lax.broadcasted_iota(jnp.int32, (E, E), 1)
    before = jnp.where(ec < er, 1.0, 0.0).astype(F32)
    local_base = jnp.dot(before, jnp.broadcast_to(block_count, (E, LANES)), precision=HIGHEST,
                         preferred_element_type=F32)[:, 0:1]
    row_all = local_base + (incl - ones)
    eidx = lax.broadcasted_iota(jnp.int32, (E, tb), 0)
    ls_rows = [jnp.sum(jnp.where(eidx == f, row_all, 0.0), axis=0, keepdims=True) for f in firsts]
    w_rows = [jnp.sum(jnp.where(eidx == f, gates_t, 0.0), axis=0, keepdims=True) for f in firsts]
    ls_ref[...] = jnp.concatenate(ls_rows + [jnp.zeros((SUBLANES - TOP_K, tb), F32)], axis=0).astype(jnp.int32)
    zpad = jnp.zeros((LANES - TOP_K, tb), F32)
    lst_ref[0] = jnp.concatenate(ls_rows + [zpad], axis=0).T.astype(jnp.int32)
    wt_ref[0] = jnp.concatenate(w_rows + [zpad], axis=0).T

    def as_row(col):
        sel = lax.broadcasted_iota(jnp.int32, (E, LANES), 0) == lax.broadcasted_iota(jnp.int32, (E, LANES), 1)
        return jnp.sum(jnp.where(sel, col, 0.0), axis=0, keepdims=True).astype(jnp.int32)

    bc_ref[0] = as_row(block_count)
    cr_ref[0] = as_row(count[...])
    count[...] = count[...] + block_count
    cnt_ref[...] = jnp.broadcast_to(count[...], cnt_ref.shape).astype(jnp.int32)


def _route_call(x, mod_l, norm_g, router_wt, router_b, *, tb):
    B, S, D = x.shape
    nj = S // tb
    full = lambda a: pl.BlockSpec(a.shape, lambda b, j: (0,) * a.ndim)
    tok = lambda w: pl.BlockSpec((1, tb, w), lambda b, j: (b, j, 0))
    kt = pl.BlockSpec((SUBLANES, tb), lambda b, j: (0, b * nj + j))
    per_block = pl.BlockSpec((1, 1, LANES), lambda b, j: (b * nj + j, 0, 0))
    return pl.pallas_call(
        _route_kernel,
        grid=(B, nj),
        in_specs=[tok(D), pl.BlockSpec((1, N_MOD, D), lambda b, j: (b, 0, 0)),
                  full(norm_g), full(router_wt), full(router_b)],
        out_specs=[kt, tok(LANES), tok(LANES), per_block, per_block,
                   pl.BlockSpec((N_EXPERTS, LANES), lambda b, j: (0, 0))],
        out_shape=[jax.ShapeDtypeStruct((SUBLANES, B * S), jnp.int32),
                   jax.ShapeDtypeStruct((B, S, LANES), jnp.int32),
                   jax.ShapeDtypeStruct((B, S, LANES), F32),
                   jax.ShapeDtypeStruct((B * nj, 1, LANES), jnp.int32),
                   jax.ShapeDtypeStruct((B * nj, 1, LANES), jnp.int32),
                   jax.ShapeDtypeStruct((N_EXPERTS, LANES), jnp.int32)],
        scratch_shapes=[pltpu.VMEM((N_EXPERTS, 1), F32)],
        compiler_params=_params("arbitrary", "arbitrary"),
        name="route",
    )(x, mod_l, norm_g, router_wt, router_b)


def _route_fin_kernel(cnt_ref, meta_ref, base_ref, *, tm):
    cnt = cnt_ref[...]
    ntile = lax.shift_right_logical(cnt + (tm - 1), int(math.log2(tm)))
    er = lax.broadcasted_iota(jnp.int32, (N_EXPERTS, N_EXPERTS), 0)
    ec = lax.broadcasted_iota(jnp.int32, (N_EXPERTS, N_EXPERTS), 1)
    before = jnp.where(ec < er, 1.0, 0.0).astype(F32)
    start = jnp.dot(before, ntile.astype(F32), precision=HIGHEST, preferred_element_type=F32).astype(jnp.int32)
    base = start * tm
    base_ref[...] = base
    nlane = meta_ref.shape[1]
    lane = lax.broadcasted_iota(jnp.int32, (1, nlane), 1)
    end = (start + ntile)[:, 0:1]
    tile_expert = jnp.sum(jnp.where(end <= lane, 1, 0), axis=0, keepdims=True)
    tile_expert = jnp.minimum(tile_expert, N_EXPERTS - 1)
    meta_ref[...] = jnp.where(lane == nlane - 1, end[N_EXPERTS - 1:N_EXPERTS, :], tile_expert)


def _route_fin(cnt, *, tm, meta_lanes):
    full = lambda a: pl.BlockSpec(a.shape, lambda i: (0,) * a.ndim)
    outs = [jax.ShapeDtypeStruct((1, meta_lanes), jnp.int32), jax.ShapeDtypeStruct(cnt.shape, jnp.int32)]
    return pl.pallas_call(
        functools.partial(_route_fin_kernel, tm=tm),
        grid=(1,),
        in_specs=[full(cnt)],
        out_specs=[full(o) for o in outs],
        out_shape=outs,
        compiler_params=_params("arbitrary"),
        name="route_fin",
    )(cnt)


HALF = D_MODEL // 2
SLAB = HALF // LANES
HIGH_BITS = 0xFFFF0000


def _to_slabs(ref, value, is_bf16_exact=False):
    n = value.shape[0]
    if not is_bf16_exact:
        value = value.astype(BF16).astype(F32)
    bits = lax.bitcast_convert_type(value, jnp.uint32)
    packed = (bits[:, HALF:] & jnp.uint32(HIGH_BITS)) | (bits[:, :HALF] >> 16)
    for s in range(SLAB):
        ref[pl.ds(s, n, stride=SLAB), :] = packed[:, s * LANES:(s + 1) * LANES]


def _from_slabs(ref, n):
    packed = jnp.concatenate([ref[pl.ds(s, n, stride=SLAB), :] for s in range(SLAB)], axis=1)
    low = lax.bitcast_convert_type(packed << 16, F32).astype(BF16)
    high = lax.bitcast_convert_type(packed & jnp.uint32(HIGH_BITS), F32).astype(BF16)
    return low, high


def _for_expert_runs(base_sm, bc_sm, cr_sm, max_run, fn):
    def per_expert(e, local_row):
        n = bc_sm[0, 0, e]
        sorted_row = base_sm[e] + cr_sm[0, 0, e]
        _for_pieces(n, max_run, lambda done, size: fn(local_row + done, sorted_row + done, size))
        return local_row + n

    lax.fori_loop(0, N_EXPERTS, per_expert, 0)


def _for_pieces(n, max_piece, fn):
    done = 0
    bit = max_piece
    while bit >= 2:
        take = n & bit

        @pl.when(take != 0)
        def _(done=done, bit=bit):
            fn(done, bit)

        done = done + take
        bit //= 2


ROW_GROUPS = 7


def _local_rows(tb):
    rows = tb * TOP_K + N_EXPERTS
    assert rows % (ROW_GROUPS * 2 * SUBLANES) == 0, rows
    return rows


def _queue(nrows):
    return int(math.log2(nrows)) % 2


def _slab_rows(ref, row, nrows):
    return ref.at[pl.ds(pl.multiple_of(row * SLAB, 2 * SLAB), nrows * SLAB)]


def _dispatch_kernel(cnt_sm, base_sm, bc_sm, cr_sm, bc_prev, cr_prev, ls_ref, x_ref, mod_ref, g_ref, xs_hbm,
                     stage, zeros, sem, zsem, *, tb, tm):
    step = pl.program_id(0) * pl.num_programs(1) + pl.program_id(1)
    last_step = pl.num_programs(0) * pl.num_programs(1) - 1
    cur = lax.rem(step, 2)
    mod = mod_ref[0]
    hb = _rms_mod(x_ref[0], g_ref[...], mod[4:5, :], mod[3:4, :]).astype(BF16)
    ls = ls_ref[...]
    gr = _local_rows(tb) // ROW_GROUPS
    for grp in range(ROW_GROUPS):
        row = grp * gr + lax.broadcasted_iota(jnp.int32, (gr, tb), 0)
        hit = ls[0:1, :] == row
        for k in range(1, TOP_K):
            hit = hit | (ls[k:k + 1, :] == row)
        perm = jnp.where(hit, 1.0, 0.0).astype(BF16)
        _to_slabs(stage.at[cur, pl.ds(grp * gr * SLAB, gr * SLAB)], jnp.dot(perm, hb, preferred_element_type=F32),
                  is_bf16_exact=True)

    def run_copy(buf):
        return lambda local_row, sorted_row, nrows: pltpu.make_async_copy(
            _slab_rows(stage.at[buf], local_row, nrows), _slab_rows(xs_hbm, sorted_row, nrows), sem.at[buf])

    _for_expert_runs(base_sm, bc_sm, cr_sm, tb, lambda *a: run_copy(cur)(*a).start(priority=_queue(a[2])))

    @pl.when(step == 0)
    def _():
        zeros[...] = jnp.zeros_like(zeros)

        def per_expert(e, carry):
            n = cnt_sm[e]
            npad = lax.rem(tm - lax.rem(n, tm), tm)
            zcopy = lambda done, size: pltpu.make_async_copy(
                _slab_rows(zeros, 0, size), _slab_rows(xs_hbm, base_sm[e] + n + done, size), zsem)
            _for_pieces(npad, tm // 2, lambda *a: zcopy(*a).start())
            _for_pieces(npad, tm // 2, lambda *a: zcopy(*a).wait())
            return carry

        lax.fori_loop(0, N_EXPERTS, per_expert, 0)

    @pl.when(step > 0)
    def _():
        _for_expert_runs(base_sm, bc_prev, cr_prev, tb, lambda *a: run_copy(1 - cur)(*a).wait())

    @pl.when(step == last_step)
    def _():
        _for_expert_runs(base_sm, bc_sm, cr_sm, tb, lambda *a: run_copy(cur)(*a).wait())


def _dispatch(cnt, base, blk_cnt, blk_carry, ls, x, mod_l, norm_g, *, tb, tm, n_slots):
    B, S, D = x.shape
    nj = S // tb
    smem_block = lambda shift: pl.BlockSpec(
        (1, 1, LANES), lambda b, j, *_: (jnp.maximum(b * nj + j + shift, 0), 0, 0), memory_space=pltpu.SMEM)
    grid_spec = pltpu.PrefetchScalarGridSpec(
        num_scalar_prefetch=2,
        grid=(B, nj),
        in_specs=[smem_block(0), smem_block(0), smem_block(-1), smem_block(-1),
                  pl.BlockSpec((SUBLANES, tb), lambda b, j, *_: (0, b * nj + j)),
                  pl.BlockSpec((1, tb, D), lambda b, j, *_: (b, j, 0)),
                  pl.BlockSpec((1, N_MOD, D), lambda b, j, *_: (b, 0, 0)),
                  pl.BlockSpec(norm_g.shape, lambda b, j, *_: (0, 0))],
        out_specs=pl.BlockSpec(memory_space=pltpu.HBM),
        scratch_shapes=[pltpu.VMEM((2, _local_rows(tb) * SLAB, LANES), jnp.uint32),
                        pltpu.VMEM((tm // 2 * SLAB, LANES), jnp.uint32),
                        pltpu.SemaphoreType.DMA((2,)), pltpu.SemaphoreType.DMA])
    return pl.pallas_call(
        functools.partial(_dispatch_kernel, tb=tb, tm=tm),
        grid_spec=grid_spec,
        out_shape=jax.ShapeDtypeStruct((n_slots * SLAB, LANES), jnp.uint32),
        compiler_params=_params("arbitrary", "arbitrary"),
        name="dispatch",
    )(cnt, base, blk_cnt, blk_carry, blk_cnt, blk_carry, ls, x, mod_l, norm_g)


def _experts_kernel(te_sm, nt_sm, x_ref, w1_ref, w3_ref, w2_ref, y_ref, w1b, w3b, w2b, *, tm):
    i = pl.program_id(0)
    in_use = i < nt_sm[0]

    @pl.when(in_use)
    def _():
        @pl.when((i == 0) | (te_sm[i] != te_sm[jnp.maximum(i - 1, 0)]))
        def _():
            w1b[...] = w1_ref[0].astype(BF16)
            w3b[...] = w3_ref[0].astype(BF16)
            w2b[...] = w2_ref[0].astype(BF16)

        for c in range(tm // EXPERT_CHUNK):
            rows = pl.ds(c * EXPERT_CHUNK * SLAB, EXPERT_CHUNK * SLAB)
            x_lo, x_hi = _from_slabs(x_ref.at[rows], EXPERT_CHUNK)
            up = lambda w: (jnp.dot(x_lo, w[:HALF, :], preferred_element_type=F32)
                            + jnp.dot(x_hi, w[HALF:, :], preferred_element_type=F32))
            hid = _silu(up(w1b)) * up(w3b)
            _to_slabs(y_ref.at[rows], jnp.dot(hid.astype(BF16), w2b[...], preferred_element_type=F32))

    @pl.when(jnp.logical_not(in_use))
    def _():
        y_ref[...] = jnp.zeros_like(y_ref)


def _experts(tile_expert, n_tiles, xs, w1, w3, w2, layer, *, tm):
    _, E, D, F = w1.shape
    n_slots = xs.shape[0] // SLAB
    last = lambda i, te, nt: jnp.minimum(i, nt[0] - 1)
    grid_spec = pltpu.PrefetchScalarGridSpec(
        num_scalar_prefetch=2,
        grid=(n_slots // tm,),
        in_specs=[pl.BlockSpec((tm * SLAB, LANES), lambda i, te, nt: (last(i, te, nt), 0)),
                  pl.BlockSpec((None, 1, D, F), lambda i, te, nt: (layer, te[last(i, te, nt)], 0, 0)),
                  pl.BlockSpec((None, 1, D, F), lambda i, te, nt: (layer, te[last(i, te, nt)], 0, 0)),
                  pl.BlockSpec((None, 1, F, D), lambda i, te, nt: (layer, te[last(i, te, nt)], 0, 0))],
        out_specs=pl.BlockSpec((tm * SLAB, LANES), lambda i, te, nt: (i, 0)),
        scratch_shapes=[pltpu.VMEM((D, F), BF16), pltpu.VMEM((D, F), BF16), pltpu.VMEM((F, D), BF16)])
    return pl.pallas_call(
        functools.partial(_experts_kernel, tm=tm),
        grid_spec=grid_spec,
        out_shape=jax.ShapeDtypeStruct(xs.shape, jnp.uint32),
        compiler_params=_params("arbitrary"),
        name="experts",
    )(tile_expert, n_tiles, xs, w1, w3, w2)


def _combine_kernel(base_sm, bc_sm, cr_sm, bc_next, cr_next, x_ref, mod_ref, g_ref, lst_ref, wt_ref,
                    sw1_ref, sw3_ref, sw2_ref, fg_ref, ys_hbm, o_ref, stage, sem, *, tb, final_norm):
    step = pl.program_id(0) * pl.num_programs(1) + pl.program_id(1)
    last_step = pl.num_programs(0) * pl.num_programs(1) - 1
    cur = lax.rem(step, 2)

    def run_copy(buf):
        return lambda local_row, sorted_row, nrows: pltpu.make_async_copy(
            _slab_rows(ys_hbm, sorted_row, nrows), _slab_rows(stage.at[buf], local_row, nrows), sem.at[buf])

    @pl.when(step == 0)
    def _():
        stage[...] = jnp.zeros_like(stage)
        _for_expert_runs(base_sm, bc_sm, cr_sm, tb, lambda *a: run_copy(cur)(*a).start(priority=_queue(a[2])))

    @pl.when(step < last_step)
    def _():
        _for_expert_runs(base_sm, bc_next, cr_next, tb, lambda *a: run_copy(1 - cur)(*a).start(priority=_queue(a[2])))

    x = x_ref[0]
    mod = mod_ref[0]
    hb = _rms_mod(x, g_ref[...], mod[4:5, :], mod[3:4, :]).astype(BF16)
    a = jnp.dot(hb, sw1_ref[...], preferred_element_type=F32)
    b = jnp.dot(hb, sw3_ref[...], preferred_element_type=F32)
    acc = jnp.dot((_silu(a) * b).astype(BF16), sw2_ref[...], preferred_element_type=F32)
    wt = wt_ref[0]
    lst = lst_ref[0]
    _for_expert_runs(base_sm, bc_sm, cr_sm, tb, lambda *a: run_copy(cur)(*a).wait())
    gr = _local_rows(tb) // ROW_GROUPS
    routed = [jnp.zeros((tb, HALF), F32)] * 2
    for grp in range(ROW_GROUPS):
        halves = _from_slabs(stage.at[cur, pl.ds(grp * gr * SLAB, gr * SLAB)], gr)
        row = grp * gr + lax.broadcasted_iota(jnp.int32, (1, gr), 1)
        gate = jnp.zeros((tb, gr), F32)
        for k in range(TOP_K):
            gate = jnp.where(lst[:, k:k + 1] == row, wt[:, k:k + 1], gate)
        gate = gate.astype(BF16)
        routed = [r + jnp.dot(gate, yg, preferred_element_type=F32) for r, yg in zip(routed, halves)]
    acc = acc + jnp.concatenate(routed, axis=1)
    y = x + mod[5:6, :] * acc
    if final_norm:
        y = y * lax.rsqrt(jnp.mean(y * y, axis=-1, keepdims=True) + EPS) * fg_ref[...]
    o_ref[0] = y


def _combine(base, blk_cnt, blk_carry, x, mod_l, norm_g, lst, wt, sw1, sw3, sw2, final_g, ys, *, tb, final_norm):
    B, S, D = x.shape
    nj = S // tb
    full = lambda a: pl.BlockSpec(a.shape, lambda b, j, *_: (0,) * a.ndim)
    tok = lambda w: pl.BlockSpec((1, tb, w), lambda b, j, *_: (b, j, 0))
    smem_block = lambda shift: pl.BlockSpec(
        (1, 1, LANES), lambda b, j, *_: (jnp.minimum(b * nj + j + shift, B * nj - 1), 0, 0), memory_space=pltpu.SMEM)
    grid_spec = pltpu.PrefetchScalarGridSpec(
        num_scalar_prefetch=1,
        grid=(B, nj),
        in_specs=[smem_block(0), smem_block(0), smem_block(1), smem_block(1), tok(D),
                  pl.BlockSpec((1, N_MOD, D), lambda b, j, *_: (b, 0, 0)),
                  full(norm_g), tok(LANES), tok(LANES), full(sw1), full(sw3), full(sw2), full(final_g),
                  pl.BlockSpec(memory_space=pltpu.HBM)],
        out_specs=tok(D),
        scratch_shapes=[pltpu.VMEM((2, _local_rows(tb) * SLAB, LANES), jnp.uint32), pltpu.SemaphoreType.DMA((2,))])
    return pl.pallas_call(
        functools.partial(_combine_kernel, tb=tb, final_norm=final_norm),
        grid_spec=grid_spec,
        out_shape=jax.ShapeDtypeStruct((B, S, D), F32),
        compiler_params=_params("arbitrary", "arbitrary"),
        name="combine",
    )(base, blk_cnt, blk_carry, blk_cnt, blk_carry, x, mod_l, norm_g, lst, wt, sw1, sw3, sw2, final_g, ys)


def _block_diag(w):
    G, a, b = w.shape
    out = jnp.zeros((G * a, G * b), w.dtype)
    for g in range(G):
        out = out.at[g * a:(g + 1) * a, g * b:(g + 1) * b].set(w[g])
    return out


def kernel(x, c, w_in, b_f, pool_w, pool_scale, sg_ln_g, sg_ln_b, sg_w, sg_b, conv_w, conv_b, conv_ln_g,
           conv_ln_b, w_branch, w_out, mix_norm_g, ffn_norm_g, ada_w, ada_b, router_w, router_bias,
           exp_w1, exp_w3, exp_w2, shared_w1, shared_w3, shared_w2, final_norm_g):
    B, S, D = x.shape
    L = w_in.shape[0]
    tb = min(512, S)
    tq = tb
    tm = EXPERT_TILE
    n_tiles = -(-(B * (S // tb) * _local_rows(tb)) // tm) + N_EXPERTS
    meta_lanes = -(-(n_tiles + 1) // LANES) * LANES
    row = lambda a: a.reshape(1, -1)

    mod = _modulation(c, ada_w, ada_b)
    for l in range(L):
        bfc = jnp.pad(b_f[l], (0, LANES - ATT_HEADS)).reshape(1, LANES)
        poolw = _block_diag(pool_w[l]).astype(BF16)
        sgb = jnp.repeat(sg_b[l].T, SG_HD, axis=1)

        ya, yb, yc, q, k, v = _mixer_in(
            x, mod[l], row(mix_norm_g[l]), w_in, l, bfc, poolw, row(pool_scale[l]),
            row(sg_ln_g[l]), row(sg_ln_b[l]), sg_w[l], sgb, conv_w[l], row(conv_b[l]),
            row(conv_ln_g[l]), row(conv_ln_b[l]), tb=tb)
        yd = _attention(q, k, v, tq=tq)
        x = _merge(x, mod[l], row(mix_norm_g[l]), w_in, l, w_branch[l].astype(BF16), w_out[l].astype(BF16),
                   ya, yb, yc, yd, tb=tb)
        fnorm = row(ffn_norm_g[l])
        ls, lst, wt, blk_cnt, blk_carry, cnt = _route_call(
            x, mod[l], fnorm, router_w[l].T, router_bias[l].reshape(-1, 1), tb=tb)
        meta, base = _route_fin(cnt, tm=tm, meta_lanes=meta_lanes)
        xs = _dispatch(cnt[:, 0], base[:, 0], blk_cnt, blk_carry, ls, x, mod[l], fnorm,
                       tb=tb, tm=tm, n_slots=n_tiles * tm)
        ys = _experts(meta[0, :n_tiles], meta[0, meta_lanes - 1:], xs, exp_w1, exp_w3, exp_w2, l, tm=tm)
        x = _combine(base[:, 0], blk_cnt, blk_carry, x, mod[l], fnorm, lst, wt, shared_w1[l].astype(BF16),
                     shared_w3[l].astype(BF16), shared_w2[l].astype(BF16), row(final_norm_g), ys,
                     tb=tb, final_norm=(l == L - 1))
    return x
```

```python
import functools
import math

import jax
import jax.numpy as jnp
from jax import lax
from jax.experimental import pallas as pl
from jax.experimental.pallas import tpu as pltpu

F32 = jnp.float32
BF16 = jnp.bfloat16
HIGHEST = lax.Precision.HIGHEST

D_MODEL = 1024
BRANCH_W = 256
N_BRANCH = 4
POOL_WINDOWS = (2, 4, 8, 16)
POOL_GW = 64
POOL_TAIL = 16
SG_HEADS = 4
SG_CHUNK = 128
SG_HD = 64
CONV_WIDTH = 31
CONV_TAIL = 32
ATT_HEADS = 4
ATT_HD = 64
N_EXPERTS = 64
TOP_K = 6
N_GROUPS = 8
GROUP_SIZE = N_EXPERTS // N_GROUPS
TOPK_GROUPS = 4
EXPERT_FF = 256
SHARED_FF = 256
ROUTE_SCALE = 2.5
EPS = 1e-6
N_MOD = 6
MIX_COLS = 8 * BRANCH_W
LANES = 128
SUBLANES = 8
LOG2E = 1.4426950408889634
EXPERT_TILE = 512
EXPERT_CHUNK = 256
VMEM_LIMIT = 56 * 1024 * 1024


def _sigmoid(x):
    return 1.0 / (1.0 + jnp.exp(-x))


def _silu(x):
    return x * _sigmoid(x)


def _rms_mod(x, g, scale, shift):
    y = x * lax.rsqrt(jnp.mean(x * x, axis=-1, keepdims=True) + EPS)
    return (y * g) * (1.0 + scale) + shift


def _layer_norm(x, g, b):
    mu = jnp.mean(x, axis=-1, keepdims=True)
    xc = x - mu
    var = jnp.mean(xc * xc, axis=-1, keepdims=True)
    return xc * lax.rsqrt(var + EPS) * g + b


def _params(*sem):
    return pltpu.CompilerParams(dimension_semantics=sem, vmem_limit_bytes=VMEM_LIMIT)


def _mod_kernel(c_ref, w_ref, b_ref, o_ref):
    c = c_ref[...]
    o_ref[0] = jnp.dot(_silu(c), w_ref[0], precision=HIGHEST, preferred_element_type=F32) + b_ref[0]


def _modulation(c, ada_w, ada_b):
    L, D, N = ada_w.shape
    B = c.shape[0]
    tn = 1536
    out = pl.pallas_call(
        _mod_kernel,
        grid=(L, N // tn),
        in_specs=[pl.BlockSpec((B, D), lambda l, j: (0, 0)),
                  pl.BlockSpec((1, D, tn), lambda l, j: (l, 0, j)),
                  pl.BlockSpec((1, 1, tn), lambda l, j: (l, 0, j))],
        out_specs=pl.BlockSpec((1, B, tn), lambda l, j: (l, 0, j)),
        out_shape=jax.ShapeDtypeStruct((L, B, N), F32),
        compiler_params=_params("arbitrary", "arbitrary"),
        name="modulation",
    )(c, ada_w, ada_b.reshape(L, 1, N))
    return out.reshape(L, B, N_MOD, D)


def _mixer_in_kernel(x_ref, mod_ref, g_ref, wmix_f32, wfc_f32, bfc_ref,
                     poolw_ref, pools_ref, sglg_ref, sglb_ref, sgw_ref, sgb_ref,
                     cw_ref, cb_ref, clg_ref, clb_ref,
                     ya_ref, yb_ref, yc_ref, q_ref, k_ref, v_ref,
                     pool_ext, conv_ext, cum_c, wmix_ref, wfc_ref, *, tb):
    j = pl.program_id(1)

    @pl.when((pl.program_id(0) == 0) & (j == 0))
    def _():
        wmix_ref[...] = wmix_f32[0].astype(BF16)
        wfc_ref[...] = wfc_f32[0].astype(BF16)

    @pl.when(j == 0)
    def _():
        pool_ext[0:POOL_TAIL, :] = jnp.zeros((POOL_TAIL, BRANCH_W), F32)
        conv_ext[0:CONV_TAIL, :] = jnp.zeros((CONV_TAIL, BRANCH_W), F32)
        cum_c[...] = jnp.zeros_like(cum_c)

    mod = mod_ref[0]
    h = _rms_mod(x_ref[0], g_ref[...], mod[1:2, :], mod[0:1, :])
    hb = h.astype(BF16)
    proj = jnp.dot(hb, wmix_ref[...], preferred_element_type=F32)

    lane = lax.broadcasted_iota(jnp.int32, (1, BRANCH_W), 1)
    row = lax.broadcasted_iota(jnp.int32, (tb, 1), 0)

    u = proj[:, 0:BRANCH_W]
    pool_ext[POOL_TAIL:POOL_TAIL + tb, :] = u
    ext = pool_ext[...]
    s2 = ext + pltpu.roll(ext, 1, 0)
    s4 = s2 + pltpu.roll(s2, 2, 0)
    s8 = s4 + pltpu.roll(s4, 4, 0)
    s16 = s8 + pltpu.roll(s8, 8, 0)
    grp = lane // POOL_GW
    wsum = jnp.where(grp == 0, s2, jnp.where(grp == 1, s4, jnp.where(grp == 2, s8, s16)))[POOL_TAIL:, :]
    win = jnp.where(grp == 0, 2.0, jnp.where(grp == 1, 4.0, jnp.where(grp == 2, 8.0, 16.0)))
    count = jnp.minimum((j * tb + row + 1).astype(F32), win)
    pooled = wsum / count - u
    ya = jnp.dot(pooled.astype(BF16), poolw_ref[...], preferred_element_type=F32) * pools_ref[...]
    ya_ref[0] = ya.astype(BF16)
    pool_ext[0:POOL_TAIL, :] = u[tb - POOL_TAIL:, :]

    z = proj[:, BRANCH_W:3 * BRANCH_W]
    z = 0.5 * z * (1.0 + jnp.tanh(math.sqrt(2.0 / math.pi) * (z + 0.044715 * (z * z * z))))
    su = z[:, 0:BRANCH_W]
    sv = _layer_norm(z[:, BRANCH_W:], sglg_ref[...], sglb_ref[...])
    r128 = lax.broadcasted_iota(jnp.int32, (SG_CHUNK, SG_CHUNK), 0)
    c128 = lax.broadcasted_iota(jnp.int32, (SG_CHUNK, SG_CHUNK), 1)
    wcat = jnp.concatenate(
        [jnp.where(r128 >= c128, sgw_ref[hh], 0.0) for hh in range(SG_HEADS)], axis=1).astype(BF16)
    head = lane // SG_HD
    for ci in range(tb // SG_CHUNK):
        rows = slice(ci * SG_CHUNK, (ci + 1) * SG_CHUNK)
        vch = sv[rows, :]
        vstack = jnp.concatenate(
            [jnp.where(head == hh, vch, 0.0) for hh in range(SG_HEADS)], axis=0).astype(BF16)
        s = jnp.dot(wcat, vstack, preferred_element_type=F32) + sgb_ref[...]
        yb_ref[0, rows, :] = (su[rows, :] * s).astype(BF16)

    glu = proj[:, 3 * BRANCH_W:4 * BRANCH_W] * _sigmoid(proj[:, 4 * BRANCH_W:5 * BRANCH_W])
    conv_ext[CONV_TAIL:CONV_TAIL + tb, :] = glu
    acc = jnp.zeros((tb, BRANCH_W), F32) + cb_ref[...]
    for kk in range(CONV_WIDTH):
        off = CONV_TAIL - (CONV_WIDTH - 1) + kk
        acc = acc + cw_ref[kk:kk + 1, :] * conv_ext[off:off + tb, :]
    yc_ref[0] = _silu(_layer_norm(acc, clg_ref[...], clb_ref[...])).astype(BF16)
    conv_ext[0:CONV_TAIL, :] = glu[tb - CONV_TAIL:, :]

    def log_sigmoid(t):
        return jnp.minimum(t, 0.0) - jnp.log(1.0 + jnp.exp(-jnp.abs(t)))

    rr = lax.broadcasted_iota(jnp.int32, (tb, tb), 0)
    cc = lax.broadcasted_iota(jnp.int32, (tb, tb), 1)
    lower = jnp.where(rr >= cc, 1.0, 0.0).astype(BF16)
    lf_c = log_sigmoid(jnp.dot(hb, wfc_ref[...], preferred_element_type=F32) + bfc_ref[...])
    lf_hi = lf_c.astype(BF16)
    lf_mid = (lf_c - lf_hi.astype(F32)).astype(BF16)
    lf_lo = ((lf_c - lf_hi.astype(F32)) - lf_mid.astype(F32)).astype(BF16)
    cs_c = cum_c[...] + sum(jnp.dot(lower, piece, preferred_element_type=F32) for piece in (lf_lo, lf_mid, lf_hi))
    cum_c[...] = cs_c[tb - 1:tb, :]
    cl2 = cs_c * LOG2E
    ln = lax.broadcasted_iota(jnp.int32, (1, LANES), 1)
    ones_q = jnp.where((ln >= ATT_HD + 3) & (ln < ATT_HD + 6), 1.0, 0.0)
    ones_k = jnp.where((ln >= ATT_HD) & (ln < ATT_HD + 3), 1.0, 0.0)
    ones_v = jnp.where(ln == ATT_HD, 1.0, 0.0)
    for hh in range(ATT_HEADS):
        pair = (hh // 2) * LANES
        qs = proj[:, 5 * BRANCH_W + pair:5 * BRANCH_W + pair + LANES] * (LOG2E / math.sqrt(ATT_HD))
        ks = proj[:, 6 * BRANCH_W + pair:6 * BRANCH_W + pair + LANES]
        vs = proj[:, 7 * BRANCH_W + pair:7 * BRANCH_W + pair + LANES]
        if hh % 2:
            qs, ks, vs = (pltpu.roll(a, ATT_HD, 1) for a in (qs, ks, vs))
        c = cl2[:, hh:hh + 1]
        hi = c.astype(BF16).astype(F32)
        mid = (c - hi).astype(BF16).astype(F32)
        lo = (c - hi) - mid
        q_extra = jnp.where(ln == ATT_HD, hi, jnp.where(ln == ATT_HD + 1, mid, jnp.where(ln == ATT_HD + 2, lo, ones_q)))
        k_extra = jnp.where(ln == ATT_HD + 3, -hi,
                            jnp.where(ln == ATT_HD + 4, -mid, jnp.where(ln == ATT_HD + 5, -lo, ones_k)))
        q_ref[0, hh, 0] = jnp.where(ln < ATT_HD, qs, q_extra).T.astype(BF16)
        k_ref[0, hh] = jnp.where(ln < ATT_HD, ks, k_extra).astype(BF16)
        v_ref[0, hh, 0] = jnp.where(ln < ATT_HD, vs, ones_v).T.astype(BF16)


def _w_in_cols(layer, first_col, ncols):
    return pl.BlockSpec((1, D_MODEL, ncols), lambda b, j: (layer, 0, first_col // ncols), pipeline_mode=pl.Buffered(1))


def _mixer_in(x, mod_l, norm_g, w_in, layer, bfc, poolw, pools, sglg, sglb, sgw, sgb, cw, cb, clg, clb, *, tb):
    B, S, D = x.shape
    full = lambda a: pl.BlockSpec(a.shape, lambda b, j: (0,) * a.ndim)
    tok = lambda w: pl.BlockSpec((1, tb, w), lambda b, j: (b, j, 0))
    head = pl.BlockSpec((1, ATT_HEADS, tb, LANES), lambda b, j: (b, 0, j, 0))
    head_t = pl.BlockSpec((1, ATT_HEADS, 1, LANES, tb), lambda b, j: (b, 0, j, 0, 0))
    consts = (bfc, poolw, pools, sglg, sglb, sgw, sgb, cw, cb, clg, clb)
    act = jax.ShapeDtypeStruct((B, S, BRANCH_W), BF16)
    att = jax.ShapeDtypeStruct((B, ATT_HEADS, S, LANES), BF16)
    att_t = jax.ShapeDtypeStruct((B, ATT_HEADS, S // tb, LANES, tb), BF16)
    return pl.pallas_call(
        functools.partial(_mixer_in_kernel, tb=tb),
        grid=(B, S // tb),
        in_specs=[tok(D), pl.BlockSpec((1, N_MOD, D), lambda b, j: (b, 0, 0)), full(norm_g),
                  _w_in_cols(layer, 0, MIX_COLS), _w_in_cols(layer, MIX_COLS, LANES)] + [full(a) for a in consts],
        out_specs=[tok(BRANCH_W)] * 3 + [head_t, head, head_t],
        out_shape=[act] * 3 + [att_t, att, att_t],
        scratch_shapes=[pltpu.VMEM((POOL_TAIL + tb, BRANCH_W), F32),
                        pltpu.VMEM((CONV_TAIL + tb, BRANCH_W), F32),
                        pltpu.VMEM((1, LANES), F32),
                        pltpu.VMEM((D, MIX_COLS), BF16), pltpu.VMEM((D, LANES), BF16)],
        compiler_params=_params("arbitrary", "arbitrary"),
        name="mixer_in",
    )(x, mod_l, norm_g, w_in, w_in, *consts)


def _attn_kernel(q_ref, k_ref, v_ref, o_ref, m_ref, acc_ref, *, tq):
    i = pl.program_id(1)
    key = lax.broadcasted_iota(jnp.int32, (tq, tq), 0)
    qry = lax.broadcasted_iota(jnp.int32, (tq, tq), 1)
    m_ref[...] = jnp.full(m_ref.shape, -jnp.inf, F32)
    acc_ref[...] = jnp.zeros(acc_ref.shape, F32)

    def block(kj, diagonal):
        ks = pl.multiple_of(kj * tq, tq)
        logits = [jnp.dot(k_ref[0, hh, pl.ds(ks, tq), :], q_ref[0, hh, 0], preferred_element_type=F32)
                  for hh in range(ATT_HEADS)]
        for hh in range(ATT_HEADS):
            s = logits[hh]
            if diagonal:
                s = jnp.where(key <= qry, s, -jnp.inf)
            m_old = m_ref[hh]
            m_new = jnp.maximum(m_old, jnp.max(s, axis=0, keepdims=True))
            p = jnp.exp2(s - m_new)
            pv = jnp.dot(v_ref[0, hh, kj], p.astype(BF16), preferred_element_type=F32)
            acc_ref[hh] = jnp.exp2(m_old - m_new) * acc_ref[hh] + pv
            m_ref[hh] = m_new

    def body(kj, carry):
        block(kj, False)
        return carry

    lax.fori_loop(0, i, body, 0)
    block(i, True)

    ln = lax.broadcasted_iota(jnp.int32, (1, LANES), 1)
    for pr in range(ATT_HEADS // 2):
        o = []
        for hh in (2 * pr, 2 * pr + 1):
            acc = acc_ref[hh]
            o.append((acc / acc[ATT_HD:ATT_HD + 1, :]).T)
        o_ref[0, :, pr * LANES:(pr + 1) * LANES] = jnp.where(ln < ATT_HD, o[0], pltpu.roll(o[1], ATT_HD, 1)).astype(BF16)


def _attention(q_t, k, v_t, *, tq):
    B, H, S, W = k.shape
    nblk = S // tq
    assert q_t.shape == (B, H, nblk, W, tq) and v_t.shape == q_t.shape
    return pl.pallas_call(
        functools.partial(_attn_kernel, tq=tq),
        grid=(B, nblk),
        in_specs=[pl.BlockSpec((1, H, 1, W, tq), lambda b, i: (b, 0, i, 0, 0)),
                  pl.BlockSpec((1, H, S, W), lambda b, i: (b, 0, 0, 0), pipeline_mode=pl.Buffered(1)),
                  pl.BlockSpec((1, H, nblk, W, tq), lambda b, i: (b, 0, 0, 0, 0), pipeline_mode=pl.Buffered(1))],
        out_specs=pl.BlockSpec((1, tq, BRANCH_W), lambda b, i: (b, i, 0)),
        out_shape=jax.ShapeDtypeStruct((B, S, BRANCH_W), BF16),
        scratch_shapes=[pltpu.VMEM((H, 1, tq), F32), pltpu.VMEM((H, W, tq), F32)],
        compiler_params=_params("arbitrary", "arbitrary"),
        name="attention",
    )(q_t, k, v_t)


def _merge_kernel(x_ref, mod_ref, g_ref, wa_f32, wb_f32, wt_f32, wbr_ref, wout_ref, ya_ref, yb_ref, yc_ref, yd_ref,
                  o_ref, wg_ref):
    @pl.when((pl.program_id(0) == 0) & (pl.program_id(1) == 0))
    def _():
        shift, wide = GATE_SHIFT, D_MODEL + LANES
        windows = (wa_f32[0, :, 0:wide],
                   jnp.concatenate([wa_f32[0, :, D_MODEL:], wb_f32[0, :, 0:LANES]], axis=1),
                   wb_f32[0, :, 0:wide],
                   jnp.concatenate([wb_f32[0, :, D_MODEL:], wt_f32[0]], axis=1))
        for n, win in enumerate(windows):
            wg_ref[:, n * D_MODEL:(n + 1) * D_MODEL] = pltpu.roll(win, wide - shift, 1)[:, :D_MODEL].astype(BF16)

    x = x_ref[0]
    mod = mod_ref[0]
    hb = _rms_mod(x, g_ref[...], mod[1:2, :], mod[0:1, :]).astype(BF16)
    merged = jnp.zeros(x.shape, F32)
    for n, y_ref in enumerate((ya_ref, yb_ref, yc_ref, yd_ref)):
        gate = _sigmoid(jnp.dot(hb, wg_ref[:, n * D_MODEL:(n + 1) * D_MODEL], preferred_element_type=F32))
        merged = merged + gate * jnp.dot(y_ref[0], wbr_ref[n], preferred_element_type=F32)
    out = jnp.dot(merged.astype(BF16), wout_ref[...], preferred_element_type=F32)
    o_ref[0] = x + mod[2:3, :] * out


GATE_SHIFT = ATT_HEADS


def _merge(x, mod_l, norm_g, w_in, layer, wbranch, wout, ya, yb, yc, yd, *, tb):
    B, S, D = x.shape
    full = lambda a: pl.BlockSpec(a.shape, lambda b, j: (0,) * a.ndim)
    tok = lambda w: pl.BlockSpec((1, tb, w), lambda b, j: (b, j, 0))
    assert w_in.shape[2] == MIX_COLS + GATE_SHIFT + N_BRANCH * D
    return pl.pallas_call(
        _merge_kernel,
        grid=(B, S // tb),
        in_specs=[tok(D), pl.BlockSpec((1, N_MOD, D), lambda b, j: (b, 0, 0)), full(norm_g),
                  _w_in_cols(layer, MIX_COLS, 2 * D), _w_in_cols(layer, MIX_COLS + 2 * D, 2 * D),
                  _w_in_cols(layer, MIX_COLS + 4 * D, LANES),
                  full(wbranch), full(wout)] + [tok(BRANCH_W)] * 4,
        out_specs=tok(D),
        out_shape=jax.ShapeDtypeStruct((B, S, D), F32),
        scratch_shapes=[pltpu.VMEM((D, N_BRANCH * D), BF16)],
        compiler_params=_params("arbitrary", "arbitrary"),
        name="merge",
    )(x, mod_l, norm_g, w_in, w_in, w_in, wbranch, wout, ya, yb, yc, yd)


def _route(scores_t, bias_t):
    E, n = scores_t.shape
    sel = scores_t + bias_t
    eidx = lax.broadcasted_iota(jnp.int32, (E, n), 0)
    neg = jnp.full((E, n), -jnp.inf, F32)
    gscore = []
    sub = lax.broadcasted_iota(jnp.int32, (GROUP_SIZE, n), 0)
    for g in range(N_GROUPS):
        blk = sel[g * GROUP_SIZE:(g + 1) * GROUP_SIZE, :]
        m1 = jnp.max(blk, axis=0, keepdims=True)
        first = jnp.min(jnp.where(blk == m1, sub, GROUP_SIZE), axis=0, keepdims=True)
        m2 = jnp.max(jnp.where(sub == first, -jnp.inf, blk), axis=0, keepdims=True)
        gscore.append(m1 + m2)
    emask = []
    for g in range(N_GROUPS):
        beaten = jnp.zeros((1, n), jnp.int32)
        for g2 in range(N_GROUPS):
            if g2 == g:
                continue
            wins = (gscore[g2] > gscore[g]) | ((gscore[g2] == gscore[g]) & (g2 < g))
            beaten = beaten + wins.astype(jnp.int32)
        emask.append(jnp.broadcast_to(beaten < TOPK_GROUPS, (GROUP_SIZE, n)))
    cur = jnp.where(jnp.concatenate(emask, axis=0), sel, neg)
    chosen = jnp.zeros((E, n), jnp.bool_)
    firsts = []
    for _ in range(TOP_K):
        m = jnp.max(cur, axis=0, keepdims=True)
        first = jnp.min(jnp.where(cur == m, eidx, E), axis=0, keepdims=True)
        hit = eidx == first
        chosen = chosen | hit
        cur = jnp.where(hit, neg, cur)
        firsts.append(first)
    w = jnp.where(chosen, scores_t, 0.0)
    return w / jnp.sum(w, axis=0, keepdims=True) * ROUTE_SCALE, chosen, firsts


def _route_kernel(x_ref, mod_ref, g_ref, rwt_ref, rb_ref, ls_ref, lst_ref, wt_ref, bc_ref, cr_ref, cnt_ref, count):
    first_step = (pl.program_id(0) == 0) & (pl.program_id(1) == 0)

    @pl.when(first_step)
    def _():
        count[...] = jnp.zeros_like(count)

    mod = mod_ref[0]
    h = _rms_mod(x_ref[0], g_ref[...], mod[4:5, :], mod[3:4, :])
    def split(a):
        hi = a.astype(BF16)
        return hi, (a - hi.astype(F32)).astype(BF16)

    nt_dot = lambda a, b: lax.dot_general(a, b, (((1,), (1,)), ((), ())), preferred_element_type=F32)
    (w_hi, w_lo), (h_hi, h_lo) = split(rwt_ref[...]), split(h)
    logits_t = nt_dot(w_hi, h_lo) + nt_dot(w_lo, h_hi) + nt_dot(w_hi, h_hi)
    gates_t, chosen, firsts = _route(_sigmoid(logits_t), rb_ref[...])
    E, tb = gates_t.shape
    ones = jnp.where(chosen, 1.0, 0.0)
    rr = lax.broadcasted_iota(jnp.int32, (tb, tb), 0)
    cc = lax.broadcasted_iota(jnp.int32, (tb, tb), 1)
    upper = jnp.where(rr <= cc, 1.0, 0.0).astype(BF16)
    incl = jnp.dot(ones.astype(BF16), upper, preferred_element_type=F32)
    block_count = jnp.sum(ones, axis=1, keepdims=True)
    block_count = block_count + (block_count.astype(jnp.int32) & 1).astype(F32)
    er = lax.broadcasted_iota(jnp.int32, (E, E), 0)
    ec = lax.broadcasted_iota(jnp.int32, (E, E), 1)
    before = jnp.where(ec < er, 1.0, 0.0).astype(F32)
    local_base = jnp.dot(before, jnp.broadcast_to(block_count, (E, LANES)), precision=HIGHEST,
                         preferred_element_type=F32)[:, 0:1]
    row_all = local_base + (incl - ones)
    eidx = lax.broadcasted_iota(jnp.int32, (E, tb), 0)
    ls_rows = [jnp.sum(jnp.where(eidx == f, row_all, 0.0), axis=0, keepdims=True) for f in firsts]
    w_rows = [jnp.sum(jnp.where(eidx == f, gates_t, 0.0), axis=0, keepdims=True) for f in firsts]
    ls_ref[...] = jnp.concatenate(ls_rows + [jnp.zeros((SUBLANES - TOP_K, tb), F32)], axis=0).astype(jnp.int32)
    zpad = jnp.zeros((LANES - TOP_K, tb), F32)
    lst_ref[0] = jnp.concatenate(ls_rows + [zpad], axis=0).T.astype(jnp.int32)
    wt_ref[0] = jnp.concatenate(w_rows + [zpad], axis=0).T

    def as_row(col):
        sel = lax.broadcasted_iota(jnp.int32, (E, LANES), 0) == lax.broadcasted_iota(jnp.int32, (E, LANES), 1)
        return jnp.sum(jnp.where(sel, col, 0.0), axis=0, keepdims=True).astype(jnp.int32)

    total_lane = lax.broadcasted_iota(jnp.int32, (1, LANES), 1) == N_EXPERTS
    bc_ref[0] = as_row(block_count) + jnp.where(total_lane, jnp.sum(block_count).astype(jnp.int32), 0)
    cr_ref[0] = as_row(count[...])
    count[...] = count[...] + block_count
    cnt_ref[...] = jnp.broadcast_to(count[...], cnt_ref.shape).astype(jnp.int32)


def _route_call(x, mod_l, norm_g, router_wt, router_b, *, tb):
    B, S, D = x.shape
    nj = S // tb
    full = lambda a: pl.BlockSpec(a.shape, lambda b, j: (0,) * a.ndim)
    tok = lambda w: pl.BlockSpec((1, tb, w), lambda b, j: (b, j, 0))
    kt = pl.BlockSpec((SUBLANES, tb), lambda b, j: (0, b * nj + j))
    per_block = pl.BlockSpec((1, 1, LANES), lambda b, j: (b * nj + j, 0, 0))
    return pl.pallas_call(
        _route_kernel,
        grid=(B, nj),
        in_specs=[tok(D), pl.BlockSpec((1, N_MOD, D), lambda b, j: (b, 0, 0)),
                  full(norm_g), full(router_wt), full(router_b)],
        out_specs=[kt, tok(LANES), tok(LANES), per_block, per_block,
                   pl.BlockSpec((N_EXPERTS, LANES), lambda b, j: (0, 0))],
        out_shape=[jax.ShapeDtypeStruct((SUBLANES, B * S), jnp.int32),
                   jax.ShapeDtypeStruct((B, S, LANES), jnp.int32),
                   jax.ShapeDtypeStruct((B, S, LANES), F32),
                   jax.ShapeDtypeStruct((B * nj, 1, LANES), jnp.int32),
                   jax.ShapeDtypeStruct((B * nj, 1, LANES), jnp.int32),
                   jax.ShapeDtypeStruct((N_EXPERTS, LANES), jnp.int32)],
        scratch_shapes=[pltpu.VMEM((N_EXPERTS, 1), F32)],
        compiler_params=_params("arbitrary", "arbitrary"),
        name="route",
    )(x, mod_l, norm_g, router_wt, router_b)


def _route_fin_kernel(cnt_ref, meta_ref, base_ref, *, tm):
    cnt = cnt_ref[...]
    ntile = lax.shift_right_logical(cnt + (tm - 1), int(math.log2(tm)))
    er = lax.broadcasted_iota(jnp.int32, (N_EXPERTS, N_EXPERTS), 0)
    ec = lax.broadcasted_iota(jnp.int32, (N_EXPERTS, N_EXPERTS), 1)
    before = jnp.where(ec < er, 1.0, 0.0).astype(F32)
    start = jnp.dot(before, ntile.astype(F32), precision=HIGHEST, preferred_element_type=F32).astype(jnp.int32)
    base = start * tm
    base_ref[...] = base
    nlane = meta_ref.shape[1]
    lane = lax.broadcasted_iota(jnp.int32, (1, nlane), 1)
    end = (start + ntile)[:, 0:1]
    tile_expert = jnp.sum(jnp.where(end <= lane, 1, 0), axis=0, keepdims=True)
    tile_expert = jnp.minimum(tile_expert, N_EXPERTS - 1)
    meta_ref[...] = jnp.where(lane == nlane - 1, end[N_EXPERTS - 1:N_EXPERTS, :], tile_expert)


def _route_fin(cnt, *, tm, meta_lanes):
    full = lambda a: pl.BlockSpec(a.shape, lambda i: (0,) * a.ndim)
    outs = [jax.ShapeDtypeStruct((1, meta_lanes), jnp.int32), jax.ShapeDtypeStruct(cnt.shape, jnp.int32)]
    return pl.pallas_call(
        functools.partial(_route_fin_kernel, tm=tm),
        grid=(1,),
        in_specs=[full(cnt)],
        out_specs=[full(o) for o in outs],
        out_shape=outs,
        compiler_params=_params("arbitrary"),
        name="route_fin",
    )(cnt)


HALF = D_MODEL // 2
SLAB = HALF // LANES
HIGH_BITS = 0xFFFF0000


def _to_slabs(ref, value, is_bf16_exact=False):
    n = value.shape[0]
    if not is_bf16_exact:
        value = value.astype(BF16).astype(F32)
    bits = lax.bitcast_convert_type(value, jnp.uint32)
    packed = (bits[:, HALF:] & jnp.uint32(HIGH_BITS)) | (bits[:, :HALF] >> 16)
    for s in range(SLAB):
        ref[pl.ds(s, n, stride=SLAB), :] = packed[:, s * LANES:(s + 1) * LANES]


def _from_slabs(ref, n):
    packed = jnp.concatenate([ref[pl.ds(s, n, stride=SLAB), :] for s in range(SLAB)], axis=1)
    low = lax.bitcast_convert_type(packed << 16, F32).astype(BF16)
    high = lax.bitcast_convert_type(packed & jnp.uint32(HIGH_BITS), F32).astype(BF16)
    return low, high


def _for_expert_runs(base_sm, bc_sm, cr_sm, max_run, fn):
    def per_expert(e, local_row):
        n = bc_sm[0, 0, e]
        sorted_row = base_sm[e] + cr_sm[0, 0, e]
        _for_pieces(n, max_run, lambda done, size: fn(local_row + done, sorted_row + done, size))
        return local_row + n

    lax.fori_loop(0, N_EXPERTS, per_expert, 0)


def _for_pieces(n, max_piece, fn):
    done = 0
    bit = max_piece
    while bit >= 2:
        take = n & bit

        @pl.when(take != 0)
        def _(done=done, bit=bit):
            fn(done, bit)

        done = done + take
        bit //= 2


ROW_GROUPS = 7


def _local_rows(tb):
    rows = tb * TOP_K + N_EXPERTS
    assert rows % (ROW_GROUPS * 2 * SUBLANES) == 0, rows
    return rows


def _wait_all_runs(src, dst, sem, total_rows):
    nrows = pl.multiple_of(total_rows * SLAB, 2 * SLAB)
    pltpu.make_async_copy(src.at[pl.ds(0, nrows)], dst.at[pl.ds(0, nrows)], sem).wait()


def _queue(nrows):
    return int(math.log2(nrows)) % 2


def _slab_rows(ref, row, nrows):
    return ref.at[pl.ds(pl.multiple_of(row * SLAB, 2 * SLAB), nrows * SLAB)]


def _dispatch_kernel(cnt_sm, base_sm, bc_sm, cr_sm, bc_prev, cr_prev, ls_ref, x_ref, mod_ref, g_ref, xs_hbm,
                     stage, zeros, sem, zsem, *, tb, tm):
    step = pl.program_id(0) * pl.num_programs(1) + pl.program_id(1)
    last_step = pl.num_programs(0) * pl.num_programs(1) - 1
    cur = lax.rem(step, 2)
    mod = mod_ref[0]
    hb = _rms_mod(x_ref[0], g_ref[...], mod[4:5, :], mod[3:4, :]).astype(BF16)
    ls = ls_ref[...]
    gr = _local_rows(tb) // ROW_GROUPS
    for grp in range(ROW_GROUPS):
        row = grp * gr + lax.broadcasted_iota(jnp.int32, (gr, tb), 0)
        hit = ls[0:1, :] == row
        for k in range(1, TOP_K):
            hit = hit | (ls[k:k + 1, :] == row)
        perm = jnp.where(hit, 1.0, 0.0).astype(BF16)
        _to_slabs(stage.at[cur, pl.ds(grp * gr * SLAB, gr * SLAB)], jnp.dot(perm, hb, preferred_element_type=F32),
                  is_bf16_exact=True)

    def run_copy(buf):
        return lambda local_row, sorted_row, nrows: pltpu.make_async_copy(
            _slab_rows(stage.at[buf], local_row, nrows), _slab_rows(xs_hbm, sorted_row, nrows), sem.at[buf])

    _for_expert_runs(base_sm, bc_sm, cr_sm, tb, lambda *a: run_copy(cur)(*a).start(priority=_queue(a[2])))

    @pl.when(step == 0)
    def _():
        zeros[...] = jnp.zeros_like(zeros)

        def per_expert(e, carry):
            n = cnt_sm[e]
            npad = lax.rem(tm - lax.rem(n, tm), tm)
            zcopy = lambda done, size: pltpu.make_async_copy(
                _slab_rows(zeros, 0, size), _slab_rows(xs_hbm, base_sm[e] + n + done, size), zsem)
            _for_pieces(npad, tm // 2, lambda *a: zcopy(*a).start())
            _for_pieces(npad, tm // 2, lambda *a: zcopy(*a).wait())
            return carry

        lax.fori_loop(0, N_EXPERTS, per_expert, 0)

    @pl.when(step > 0)
    def _():
        _wait_all_runs(stage.at[1 - cur], xs_hbm, sem.at[1 - cur], bc_prev[0, 0, N_EXPERTS])

    @pl.when(step == last_step)
    def _():
        _wait_all_runs(stage.at[cur], xs_hbm, sem.at[cur], bc_sm[0, 0, N_EXPERTS])


def _dispatch(cnt, base, blk_cnt, blk_carry, ls, x, mod_l, norm_g, *, tb, tm, n_slots):
    B, S, D = x.shape
    nj = S // tb
    smem_block = lambda shift: pl.BlockSpec(
        (1, 1, LANES), lambda b, j, *_: (jnp.maximum(b * nj + j + shift, 0), 0, 0), memory_space=pltpu.SMEM)
    grid_spec = pltpu.PrefetchScalarGridSpec(
        num_scalar_prefetch=2,
        grid=(B, nj),
        in_specs=[smem_block(0), smem_block(0), smem_block(-1), smem_block(-1),
                  pl.BlockSpec((SUBLANES, tb), lambda b, j, *_: (0, b * nj + j)),
                  pl.BlockSpec((1, tb, D), lambda b, j, *_: (b, j, 0)),
                  pl.BlockSpec((1, N_MOD, D), lambda b, j, *_: (b, 0, 0)),
                  pl.BlockSpec(norm_g.shape, lambda b, j, *_: (0, 0))],
        out_specs=pl.BlockSpec(memory_space=pltpu.HBM),
        scratch_shapes=[pltpu.VMEM((2, _local_rows(tb) * SLAB, LANES), jnp.uint32),
                        pltpu.VMEM((tm // 2 * SLAB, LANES), jnp.uint32),
                        pltpu.SemaphoreType.DMA((2,)), pltpu.SemaphoreType.DMA])
    return pl.pallas_call(
        functools.partial(_dispatch_kernel, tb=tb, tm=tm),
        grid_spec=grid_spec,
        out_shape=jax.ShapeDtypeStruct((n_slots * SLAB, LANES), jnp.uint32),
        compiler_params=_params("arbitrary", "arbitrary"),
        name="dispatch",
    )(cnt, base, blk_cnt, blk_carry, blk_cnt, blk_carry, ls, x, mod_l, norm_g)


def _experts_kernel(te_sm, nt_sm, x_ref, w1_ref, w3_ref, w2_ref, y_ref, w1b, w3b, w2b, *, tm):
    i = pl.program_id(0)
    in_use = i < nt_sm[0]

    @pl.when(in_use)
    def _():
        @pl.when((i == 0) | (te_sm[i] != te_sm[jnp.maximum(i - 1, 0)]))
        def _():
            w1b[...] = w1_ref[0].astype(BF16)
            w3b[...] = w3_ref[0].astype(BF16)
            w2b[...] = w2_ref[0].astype(BF16)

        for c in range(tm // EXPERT_CHUNK):
            rows = pl.ds(c * EXPERT_CHUNK * SLAB, EXPERT_CHUNK * SLAB)
            x_lo, x_hi = _from_slabs(x_ref.at[rows], EXPERT_CHUNK)
            up = lambda w: (jnp.dot(x_lo, w[:HALF, :], preferred_element_type=F32)
                            + jnp.dot(x_hi, w[HALF:, :], preferred_element_type=F32))
            hid = _silu(up(w1b)) * up(w3b)
            _to_slabs(y_ref.at[rows], jnp.dot(hid.astype(BF16), w2b[...], preferred_element_type=F32))

    @pl.when(jnp.logical_not(in_use))
    def _():
        y_ref[...] = jnp.zeros_like(y_ref)


def _experts(tile_expert, n_tiles, xs, w1, w3, w2, layer, *, tm):
    _, E, D, F = w1.shape
    n_slots = xs.shape[0] // SLAB
    last = lambda i, te, nt: jnp.minimum(i, nt[0] - 1)
    grid_spec = pltpu.PrefetchScalarGridSpec(
        num_scalar_prefetch=2,
        grid=(n_slots // tm,),
        in_specs=[pl.BlockSpec((tm * SLAB, LANES), lambda i, te, nt: (last(i, te, nt), 0)),
                  pl.BlockSpec((None, 1, D, F), lambda i, te, nt: (layer, te[last(i, te, nt)], 0, 0)),
                  pl.BlockSpec((None, 1, D, F), lambda i, te, nt: (layer, te[last(i, te, nt)], 0, 0)),
                  pl.BlockSpec((None, 1, F, D), lambda i, te, nt: (layer, te[last(i, te, nt)], 0, 0))],
        out_specs=pl.BlockSpec((tm * SLAB, LANES), lambda i, te, nt: (i, 0)),
        scratch_shapes=[pltpu.VMEM((D, F), BF16), pltpu.VMEM((D, F), BF16), pltpu.VMEM((F, D), BF16)])
    return pl.pallas_call(
        functools.partial(_experts_kernel, tm=tm),
        grid_spec=grid_spec,
        out_shape=jax.ShapeDtypeStruct(xs.shape, jnp.uint32),
        compiler_params=_params("arbitrary"),
        name="experts",
    )(tile_expert, n_tiles, xs, w1, w3, w2)


def _combine_kernel(base_sm, bc_sm, cr_sm, bc_next, cr_next, x_ref, mod_ref, g_ref, lst_ref, wt_ref,
                    sw1_ref, sw3_ref, sw2_ref, fg_ref, ys_hbm, o_ref, stage, sem, *, tb, final_norm):
    step = pl.program_id(0) * pl.num_programs(1) + pl.program_id(1)
    last_step = pl.num_programs(0) * pl.num_programs(1) - 1
    cur = lax.rem(step, 2)

    def run_copy(buf):
        return lambda local_row, sorted_row, nrows: pltpu.make_async_copy(
            _slab_rows(ys_hbm, sorted_row, nrows), _slab_rows(stage.at[buf], local_row, nrows), sem.at[buf])

    @pl.when(step == 0)
    def _():
        stage[...] = jnp.zeros_like(stage)
        _for_expert_runs(base_sm, bc_sm, cr_sm, tb, lambda *a: run_copy(cur)(*a).start(priority=_queue(a[2])))

    @pl.when(step < last_step)
    def _():
        _for_expert_runs(base_sm, bc_next, cr_next, tb, lambda *a: run_copy(1 - cur)(*a).start(priority=_queue(a[2])))

    x = x_ref[0]
    mod = mod_ref[0]
    hb = _rms_mod(x, g_ref[...], mod[4:5, :], mod[3:4, :]).astype(BF16)
    a = jnp.dot(hb, sw1_ref[...], preferred_element_type=F32)
    b = jnp.dot(hb, sw3_ref[...], preferred_element_type=F32)
    acc = jnp.dot((_silu(a) * b).astype(BF16), sw2_ref[...], preferred_element_type=F32)
    wt = wt_ref[0]
    lst = lst_ref[0]
    _wait_all_runs(ys_hbm, stage.at[cur], sem.at[cur], bc_sm[0, 0, N_EXPERTS])
    gr = _local_rows(tb) // ROW_GROUPS
    routed = [jnp.zeros((tb, HALF), F32)] * 2
    for grp in range(ROW_GROUPS):
        halves = _from_slabs(stage.at[cur, pl.ds(grp * gr * SLAB, gr * SLAB)], gr)
        row = grp * gr + lax.broadcasted_iota(jnp.int32, (1, gr), 1)
        gate = jnp.zeros((tb, gr), F32)
        for k in range(TOP_K):
            gate = jnp.where(lst[:, k:k + 1] == row, wt[:, k:k + 1], gate)
        gate = gate.astype(BF16)
        routed = [r + jnp.dot(gate, yg, preferred_element_type=F32) for r, yg in zip(routed, halves)]
    acc = acc + jnp.concatenate(routed, axis=1)
    y = x + mod[5:6, :] * acc
    if final_norm:
        y = y * lax.rsqrt(jnp.mean(y * y, axis=-1, keepdims=True) + EPS) * fg_ref[...]
    o_ref[0] = y


def _combine(base, blk_cnt, blk_carry, x, mod_l, norm_g, lst, wt, sw1, sw3, sw2, final_g, ys, *, tb, final_norm):
    B, S, D = x.shape
    nj = S // tb
    full = lambda a: pl.BlockSpec(a.shape, lambda b, j, *_: (0,) * a.ndim)
    tok = lambda w: pl.BlockSpec((1, tb, w), lambda b, j, *_: (b, j, 0))
    smem_block = lambda shift: pl.BlockSpec(
        (1, 1, LANES), lambda b, j, *_: (jnp.minimum(b * nj + j + shift, B * nj - 1), 0, 0), memory_space=pltpu.SMEM)
    grid_spec = pltpu.PrefetchScalarGridSpec(
        num_scalar_prefetch=1,
        grid=(B, nj),
        in_specs=[smem_block(0), smem_block(0), smem_block(1), smem_block(1), tok(D),
                  pl.BlockSpec((1, N_MOD, D), lambda b, j, *_: (b, 0, 0)),
                  full(norm_g), tok(LANES), tok(LANES), full(sw1), full(sw3), full(sw2), full(final_g),
                  pl.BlockSpec(memory_space=pltpu.HBM)],
        out_specs=tok(D),
        scratch_shapes=[pltpu.VMEM((2, _local_rows(tb) * SLAB, LANES), jnp.uint32), pltpu.SemaphoreType.DMA((2,))])
    return pl.pallas_call(
        functools.partial(_combine_kernel, tb=tb, final_norm=final_norm),
        grid_spec=grid_spec,
        out_shape=jax.ShapeDtypeStruct((B, S, D), F32),
        compiler_params=_params("arbitrary", "arbitrary"),
        name="combine",
    )(base, blk_cnt, blk_carry, blk_cnt, blk_carry, x, mod_l, norm_g, lst, wt, sw1, sw3, sw2, final_g, ys)


def _block_diag(w):
    G, a, b = w.shape
    out = jnp.zeros((G * a, G * b), w.dtype)
    for g in range(G):
        out = out.at[g * a:(g + 1) * a, g * b:(g + 1) * b].set(w[g])
    return out


def kernel(x, c, w_in, b_f, pool_w, pool_scale, sg_ln_g, sg_ln_b, sg_w, sg_b, conv_w, conv_b, conv_ln_g,
           conv_ln_b, w_branch, w_out, mix_norm_g, ffn_norm_g, ada_w, ada_b, router_w, router_bias,
           exp_w1, exp_w3, exp_w2, shared_w1, shared_w3, shared_w2, final_norm_g):
    B, S, D = x.shape
    L = w_in.shape[0]
    tb = min(512, S)
    tq = tb
    tm = EXPERT_TILE
    n_tiles = -(-(B * (S // tb) * _local_rows(tb)) // tm) + N_EXPERTS
    meta_lanes = -(-(n_tiles + 1) // LANES) * LANES
    row = lambda a: a.reshape(1, -1)

    mod = _modulation(c, ada_w, ada_b)
    for l in range(L):
        bfc = jnp.pad(b_f[l], (0, LANES - ATT_HEADS)).reshape(1, LANES)
        poolw = _block_diag(pool_w[l]).astype(BF16)
        sgb = jnp.repeat(sg_b[l].T, SG_HD, axis=1)

        ya, yb, yc, q, k, v = _mixer_in(
            x, mod[l], row(mix_norm_g[l]), w_in, l, bfc, poolw, row(pool_scale[l]),
            row(sg_ln_g[l]), row(sg_ln_b[l]), sg_w[l], sgb, conv_w[l], row(conv_b[l]),
            row(conv_ln_g[l]), row(conv_ln_b[l]), tb=tb)
        yd = _attention(q, k, v, tq=tq)
        x = _merge(x, mod[l], row(mix_norm_g[l]), w_in, l, w_branch[l].astype(BF16), w_out[l].astype(BF16),
                   ya, yb, yc, yd, tb=tb)
        fnorm = row(ffn_norm_g[l])
        ls, lst, wt, blk_cnt, blk_carry, cnt = _route_call(
            x, mod[l], fnorm, router_w[l].T, router_bias[l].reshape(-1, 1), tb=tb)
        meta, base = _route_fin(cnt, tm=tm, meta_lanes=meta_lanes)
        xs = _dispatch(cnt[:, 0], base[:, 0], blk_cnt, blk_carry, ls, x, mod[l], fnorm,
                       tb=tb, tm=tm, n_slots=n_tiles * tm)
        ys = _experts(meta[0, :n_tiles], meta[0, meta_lanes - 1:], xs, exp_w1, exp_w3, exp_w2, l, tm=tm)
        x = _combine(base[:, 0], blk_cnt, blk_carry, x, mod[l], fnorm, lst, wt, shared_w1[l].astype(BF16),
                     shared_w3[l].astype(BF16), shared_w2[l].astype(BF16), row(final_norm_g), ys,
                     tb=tb, final_norm=(l == L - 1))
    return x
```

```python
import functools
import math

import jax
import jax.numpy as jnp
from jax import lax
from jax.experimental import pallas as pl
from jax.experimental.pallas import tpu as pltpu

F32 = jnp.float32
BF16 = jnp.bfloat16
HIGHEST = lax.Precision.HIGHEST

D_MODEL = 1024
BRANCH_W = 256
N_BRANCH = 4
POOL_WINDOWS = (2, 4, 8, 16)
POOL_GW = 64
POOL_TAIL = 16
SG_HEADS = 4
SG_CHUNK = 128
SG_HD = 64
CONV_WIDTH = 31
CONV_TAIL = 32
ATT_HEADS = 4
ATT_HD = 64
N_EXPERTS = 64
TOP_K = 6
N_GROUPS = 8
GROUP_SIZE = N_EXPERTS // N_GROUPS
TOPK_GROUPS = 4
EXPERT_FF = 256
SHARED_FF = 256
ROUTE_SCALE = 2.5
EPS = 1e-6
N_MOD = 6
MIX_COLS = 8 * BRANCH_W
LANES = 128
SUBLANES = 8
LOG2E = 1.4426950408889634
EXPERT_TILE = 512
EXPERT_CHUNK = 256
VMEM_LIMIT = 56 * 1024 * 1024


def _sigmoid(x):
    return 1.0 / (1.0 + jnp.exp(-x))


def _silu(x):
    return x * _sigmoid(x)


def _rms_mod(x, g, scale, shift):
    y = x * lax.rsqrt(jnp.mean(x * x, axis=-1, keepdims=True) + EPS)
    return (y * g) * (1.0 + scale) + shift


def _layer_norm(x, g, b):
    mu = jnp.mean(x, axis=-1, keepdims=True)
    xc = x - mu
    var = jnp.mean(xc * xc, axis=-1, keepdims=True)
    return xc * lax.rsqrt(var + EPS) * g + b


def _params(*sem):
    return pltpu.CompilerParams(dimension_semantics=sem, vmem_limit_bytes=VMEM_LIMIT)


def _mod_kernel(c_ref, w_ref, b_ref, o_ref):
    c = c_ref[...]
    o_ref[0] = jnp.dot(_silu(c), w_ref[0], precision=HIGHEST, preferred_element_type=F32) + b_ref[0]


def _modulation(c, ada_w, ada_b):
    L, D, N = ada_w.shape
    B = c.shape[0]
    tn = 1536
    out = pl.pallas_call(
        _mod_kernel,
        grid=(L, N // tn),
        in_specs=[pl.BlockSpec((B, D), lambda l, j: (0, 0)),
                  pl.BlockSpec((1, D, tn), lambda l, j: (l, 0, j)),
                  pl.BlockSpec((1, 1, tn), lambda l, j: (l, 0, j))],
        out_specs=pl.BlockSpec((1, B, tn), lambda l, j: (l, 0, j)),
        out_shape=jax.ShapeDtypeStruct((L, B, N), F32),
        compiler_params=_params("arbitrary", "arbitrary"),
        name="modulation",
    )(c, ada_w, ada_b.reshape(L, 1, N))
    return out.reshape(L, B, N_MOD, D)


def _mixer_in_kernel(x_ref, mod_ref, g_ref, wmix_f32, wfc_f32, bfc_ref,
                     poolw_ref, pools_ref, sglg_ref, sglb_ref, sgw_ref, sgb_ref,
                     cw_ref, cb_ref, clg_ref, clb_ref,
                     ya_ref, yb_ref, yc_ref, q_ref, k_ref, v_ref,
                     pool_ext, conv_ext, cum_c, wmix_ref, wfc_ref, *, tb):
    j = pl.program_id(1)

    @pl.when((pl.program_id(0) == 0) & (j == 0))
    def _():
        wmix_ref[...] = wmix_f32[0].astype(BF16)
        wfc_ref[...] = wfc_f32[0].astype(BF16)

    @pl.when(j == 0)
    def _():
        pool_ext[0:POOL_TAIL, :] = jnp.zeros((POOL_TAIL, BRANCH_W), F32)
        conv_ext[0:CONV_TAIL, :] = jnp.zeros((CONV_TAIL, BRANCH_W), F32)
        cum_c[...] = jnp.zeros_like(cum_c)

    mod = mod_ref[0]
    h = _rms_mod(x_ref[0], g_ref[...], mod[1:2, :], mod[0:1, :])
    hb = h.astype(BF16)
    proj = jnp.dot(hb, wmix_ref[...], preferred_element_type=F32)

    lane = lax.broadcasted_iota(jnp.int32, (1, BRANCH_W), 1)
    row = lax.broadcasted_iota(jnp.int32, (tb, 1), 0)

    u = proj[:, 0:BRANCH_W]
    pool_ext[POOL_TAIL:POOL_TAIL + tb, :] = u
    ext = pool_ext[...]
    s2 = ext + pltpu.roll(ext, 1, 0)
    s4 = s2 + pltpu.roll(s2, 2, 0)
    s8 = s4 + pltpu.roll(s4, 4, 0)
    s16 = s8 + pltpu.roll(s8, 8, 0)
    grp = lane // POOL_GW
    wsum = jnp.where(grp == 0, s2, jnp.where(grp == 1, s4, jnp.where(grp == 2, s8, s16)))[POOL_TAIL:, :]
    win = jnp.where(grp == 0, 2.0, jnp.where(grp == 1, 4.0, jnp.where(grp == 2, 8.0, 16.0)))
    count = jnp.minimum((j * tb + row + 1).astype(F32), win)
    pooled = wsum / count - u
    ya = jnp.dot(pooled.astype(BF16), poolw_ref[...], preferred_element_type=F32) * pools_ref[...]
    ya_ref[0] = ya.astype(BF16)
    pool_ext[0:POOL_TAIL, :] = u[tb - POOL_TAIL:, :]

    z = proj[:, BRANCH_W:3 * BRANCH_W]
    z = 0.5 * z * (1.0 + jnp.tanh(math.sqrt(2.0 / math.pi) * (z + 0.044715 * (z * z * z))))
    su = z[:, 0:BRANCH_W]
    sv = _layer_norm(z[:, BRANCH_W:], sglg_ref[...], sglb_ref[...])
    r128 = lax.broadcasted_iota(jnp.int32, (SG_CHUNK, SG_CHUNK), 0)
    c128 = lax.broadcasted_iota(jnp.int32, (SG_CHUNK, SG_CHUNK), 1)
    wcat = jnp.concatenate(
        [jnp.where(r128 >= c128, sgw_ref[hh], 0.0) for hh in range(SG_HEADS)], axis=1).astype(BF16)
    head = lane // SG_HD
    for ci in range(tb // SG_CHUNK):
        rows = slice(ci * SG_CHUNK, (ci + 1) * SG_CHUNK)
        vch = sv[rows, :]
        vstack = jnp.concatenate(
            [jnp.where(head == hh, vch, 0.0) for hh in range(SG_HEADS)], axis=0).astype(BF16)
        s = jnp.dot(wcat, vstack, preferred_element_type=F32) + sgb_ref[...]
        yb_ref[0, rows, :] = (su[rows, :] * s).astype(BF16)

    glu = proj[:, 3 * BRANCH_W:4 * BRANCH_W] * _sigmoid(proj[:, 4 * BRANCH_W:5 * BRANCH_W])
    conv_ext[CONV_TAIL:CONV_TAIL + tb, :] = glu
    acc = jnp.zeros((tb, BRANCH_W), F32) + cb_ref[...]
    for kk in range(CONV_WIDTH):
        off = CONV_TAIL - (CONV_WIDTH - 1) + kk
        acc = acc + cw_ref[kk:kk + 1, :] * conv_ext[off:off + tb, :]
    yc_ref[0] = _silu(_layer_norm(acc, clg_ref[...], clb_ref[...])).astype(BF16)
    conv_ext[0:CONV_TAIL, :] = glu[tb - CONV_TAIL:, :]

    def log_sigmoid(t):
        return jnp.minimum(t, 0.0) - jnp.log(1.0 + jnp.exp(-jnp.abs(t)))

    rr = lax.broadcasted_iota(jnp.int32, (tb, tb), 0)
    cc = lax.broadcasted_iota(jnp.int32, (tb, tb), 1)
    lower = jnp.where(rr >= cc, 1.0, 0.0).astype(BF16)
    lf_c = log_sigmoid(jnp.dot(hb, wfc_ref[...], preferred_element_type=F32) + bfc_ref[...])
    lf_hi = lf_c.astype(BF16)
    lf_mid = (lf_c - lf_hi.astype(F32)).astype(BF16)
    lf_lo = ((lf_c - lf_hi.astype(F32)) - lf_mid.astype(F32)).astype(BF16)
    cs_c = cum_c[...] + sum(jnp.dot(lower, piece, preferred_element_type=F32) for piece in (lf_lo, lf_mid, lf_hi))
    cum_c[...] = cs_c[tb - 1:tb, :]
    cl2 = cs_c * LOG2E
    ln = lax.broadcasted_iota(jnp.int32, (1, LANES), 1)
    ones_q = jnp.where((ln >= ATT_HD + 3) & (ln < ATT_HD + 6), 1.0, 0.0)
    ones_k = jnp.where((ln >= ATT_HD) & (ln < ATT_HD + 3), 1.0, 0.0)
    ones_v = jnp.where(ln == ATT_HD, 1.0, 0.0)
    for hh in range(ATT_HEADS):
        pair = (hh // 2) * LANES
        qs = proj[:, 5 * BRANCH_W + pair:5 * BRANCH_W + pair + LANES] * (LOG2E / math.sqrt(ATT_HD))
        ks = proj[:, 6 * BRANCH_W + pair:6 * BRANCH_W + pair + LANES]
        vs = proj[:, 7 * BRANCH_W + pair:7 * BRANCH_W + pair + LANES]
        if hh % 2:
            qs, ks, vs = (pltpu.roll(a, ATT_HD, 1) for a in (qs, ks, vs))
        c = cl2[:, hh:hh + 1]
        hi = c.astype(BF16).astype(F32)
        mid = (c - hi).astype(BF16).astype(F32)
        lo = (c - hi) - mid
        q_extra = jnp.where(ln == ATT_HD, hi, jnp.where(ln == ATT_HD + 1, mid, jnp.where(ln == ATT_HD + 2, lo, ones_q)))
        k_extra = jnp.where(ln == ATT_HD + 3, -hi,
                            jnp.where(ln == ATT_HD + 4, -mid, jnp.where(ln == ATT_HD + 5, -lo, ones_k)))
        q_ref[0, hh, 0] = jnp.where(ln < ATT_HD, qs, q_extra).T.astype(BF16)
        k_ref[0, hh] = jnp.where(ln < ATT_HD, ks, k_extra).astype(BF16)
        v_ref[0, hh, 0] = jnp.where(ln < ATT_HD, vs, ones_v).T.astype(BF16)


def _w_in_cols(layer, first_col, ncols):
    return pl.BlockSpec((1, D_MODEL, ncols), lambda b, j: (layer, 0, first_col // ncols), pipeline_mode=pl.Buffered(1))


def _mixer_in(x, mod_l, norm_g, w_in, layer, bfc, poolw, pools, sglg, sglb, sgw, sgb, cw, cb, clg, clb, *, tb):
    B, S, D = x.shape
    full = lambda a: pl.BlockSpec(a.shape, lambda b, j: (0,) * a.ndim)
    tok = lambda w: pl.BlockSpec((1, tb, w), lambda b, j: (b, j, 0))
    head = pl.BlockSpec((1, ATT_HEADS, tb, LANES), lambda b, j: (b, 0, j, 0))
    head_t = pl.BlockSpec((1, ATT_HEADS, 1, LANES, tb), lambda b, j: (b, 0, j, 0, 0))
    consts = (bfc, poolw, pools, sglg, sglb, sgw, sgb, cw, cb, clg, clb)
    act = jax.ShapeDtypeStruct((B, S, BRANCH_W), BF16)
    att = jax.ShapeDtypeStruct((B, ATT_HEADS, S, LANES), BF16)
    att_t = jax.ShapeDtypeStruct((B, ATT_HEADS, S // tb, LANES, tb), BF16)
    return pl.pallas_call(
        functools.partial(_mixer_in_kernel, tb=tb),
        grid=(B, S // tb),
        in_specs=[tok(D), pl.BlockSpec((1, N_MOD, D), lambda b, j: (b, 0, 0)), full(norm_g),
                  _w_in_cols(layer, 0, MIX_COLS), _w_in_cols(layer, MIX_COLS, LANES)] + [full(a) for a in consts],
        out_specs=[tok(BRANCH_W)] * 3 + [head_t, head, head_t],
        out_shape=[act] * 3 + [att_t, att, att_t],
        scratch_shapes=[pltpu.VMEM((POOL_TAIL + tb, BRANCH_W), F32),
                        pltpu.VMEM((CONV_TAIL + tb, BRANCH_W), F32),
                        pltpu.VMEM((1, LANES), F32),
                        pltpu.VMEM((D, MIX_COLS), BF16), pltpu.VMEM((D, LANES), BF16)],
        compiler_params=_params("arbitrary", "arbitrary"),
        name="mixer_in",
    )(x, mod_l, norm_g, w_in, w_in, *consts)


def _attn_kernel(q_ref, k_ref, v_ref, o_ref, m_ref, acc_ref, *, tq):
    i = pl.program_id(1)
    key = lax.broadcasted_iota(jnp.int32, (tq, tq), 0)
    qry = lax.broadcasted_iota(jnp.int32, (tq, tq), 1)
    m_ref[...] = jnp.full(m_ref.shape, -jnp.inf, F32)
    acc_ref[...] = jnp.zeros(acc_ref.shape, F32)

    def block(kj, diagonal):
        ks = pl.multiple_of(kj * tq, tq)
        logits = [jnp.dot(k_ref[0, hh, pl.ds(ks, tq), :], q_ref[0, hh, 0], preferred_element_type=F32)
                  for hh in range(ATT_HEADS)]
        for hh in range(ATT_HEADS):
            s = logits[hh]
            if diagonal:
                s = jnp.where(key <= qry, s, -jnp.inf)
            m_old = m_ref[hh]
            m_new = jnp.maximum(m_old, jnp.max(s, axis=0, keepdims=True))
            p = jnp.exp2(s - m_new)
            pv = jnp.dot(v_ref[0, hh, kj], p.astype(BF16), preferred_element_type=F32)
            acc_ref[hh] = jnp.exp2(m_old - m_new) * acc_ref[hh] + pv
            m_ref[hh] = m_new

    def body(kj, carry):
        block(kj, False)
        return carry

    lax.fori_loop(0, i, body, 0)
    block(i, True)

    ln = lax.broadcasted_iota(jnp.int32, (1, LANES), 1)
    for pr in range(ATT_HEADS // 2):
        o = []
        for hh in (2 * pr, 2 * pr + 1):
            acc = acc_ref[hh]
            o.append((acc / acc[ATT_HD:ATT_HD + 1, :]).T)
        o_ref[0, :, pr * LANES:(pr + 1) * LANES] = jnp.where(ln < ATT_HD, o[0], pltpu.roll(o[1], ATT_HD, 1)).astype(BF16)


def _attention(q_t, k, v_t, *, tq):
    B, H, S, W = k.shape
    nblk = S // tq
    assert q_t.shape == (B, H, nblk, W, tq) and v_t.shape == q_t.shape
    return pl.pallas_call(
        functools.partial(_attn_kernel, tq=tq),
        grid=(B, nblk),
        in_specs=[pl.BlockSpec((1, H, 1, W, tq), lambda b, i: (b, 0, i, 0, 0)),
                  pl.BlockSpec((1, H, S, W), lambda b, i: (b, 0, 0, 0), pipeline_mode=pl.Buffered(1)),
                  pl.BlockSpec((1, H, nblk, W, tq), lambda b, i: (b, 0, 0, 0, 0), pipeline_mode=pl.Buffered(1))],
        out_specs=pl.BlockSpec((1, tq, BRANCH_W), lambda b, i: (b, i, 0)),
        out_shape=jax.ShapeDtypeStruct((B, S, BRANCH_W), BF16),
        scratch_shapes=[pltpu.VMEM((H, 1, tq), F32), pltpu.VMEM((H, W, tq), F32)],
        compiler_params=_params("arbitrary", "arbitrary"),
        name="attention",
    )(q_t, k, v_t)


def _merge_kernel(x_ref, mod_ref, g_ref, wa_f32, wb_f32, wt_f32, wbr_ref, wout_ref, ya_ref, yb_ref, yc_ref, yd_ref,
                  o_ref, wg_ref):
    @pl.when((pl.program_id(0) == 0) & (pl.program_id(1) == 0))
    def _():
        shift, wide = GATE_SHIFT, D_MODEL + LANES
        windows = (wa_f32[0, :, 0:wide],
                   jnp.concatenate([wa_f32[0, :, D_MODEL:], wb_f32[0, :, 0:LANES]], axis=1),
                   wb_f32[0, :, 0:wide],
                   jnp.concatenate([wb_f32[0, :, D_MODEL:], wt_f32[0]], axis=1))
        for n, win in enumerate(windows):
            wg_ref[:, n * D_MODEL:(n + 1) * D_MODEL] = pltpu.roll(win, wide - shift, 1)[:, :D_MODEL].astype(BF16)

    x = x_ref[0]
    mod = mod_ref[0]
    hb = _rms_mod(x, g_ref[...], mod[1:2, :], mod[0:1, :]).astype(BF16)
    merged = jnp.zeros(x.shape, F32)
    for n, y_ref in enumerate((ya_ref, yb_ref, yc_ref, yd_ref)):
        gate = _sigmoid(jnp.dot(hb, wg_ref[:, n * D_MODEL:(n + 1) * D_MODEL], preferred_element_type=F32))
        merged = merged + gate * jnp.dot(y_ref[0], wbr_ref[n], preferred_element_type=F32)
    out = jnp.dot(merged.astype(BF16), wout_ref[...], preferred_element_type=F32)
    o_ref[0] = x + mod[2:3, :] * out


GATE_SHIFT = ATT_HEADS


def _merge(x, mod_l, norm_g, w_in, layer, wbranch, wout, ya, yb, yc, yd, *, tb):
    B, S, D = x.shape
    full = lambda a: pl.BlockSpec(a.shape, lambda b, j: (0,) * a.ndim)
    tok = lambda w: pl.BlockSpec((1, tb, w), lambda b, j: (b, j, 0))
    assert w_in.shape[2] == MIX_COLS + GATE_SHIFT + N_BRANCH * D
    return pl.pallas_call(
        _merge_kernel,
        grid=(B, S // tb),
        in_specs=[tok(D), pl.BlockSpec((1, N_MOD, D), lambda b, j: (b, 0, 0)), full(norm_g),
                  _w_in_cols(layer, MIX_COLS, 2 * D), _w_in_cols(layer, MIX_COLS + 2 * D, 2 * D),
                  _w_in_cols(layer, MIX_COLS + 4 * D, LANES),
                  full(wbranch), full(wout)] + [tok(BRANCH_W)] * 4,
        out_specs=tok(D),
        out_shape=jax.ShapeDtypeStruct((B, S, D), F32),
        scratch_shapes=[pltpu.VMEM((D, N_BRANCH * D), BF16)],
        compiler_params=_params("arbitrary", "arbitrary"),
        name="merge",
    )(x, mod_l, norm_g, w_in, w_in, w_in, wbranch, wout, ya, yb, yc, yd)


def _route(scores_t, bias_t):
    E, n = scores_t.shape
    sel = scores_t + bias_t
    eidx = lax.broadcasted_iota(jnp.int32, (E, n), 0)
    neg = jnp.full((E, n), -jnp.inf, F32)
    gscore = []
    sub = lax.broadcasted_iota(jnp.int32, (GROUP_SIZE, n), 0)
    for g in range(N_GROUPS):
        blk = sel[g * GROUP_SIZE:(g + 1) * GROUP_SIZE, :]
        m1 = jnp.max(blk, axis=0, keepdims=True)
        first = jnp.min(jnp.where(blk == m1, sub, GROUP_SIZE), axis=0, keepdims=True)
        m2 = jnp.max(jnp.where(sub == first, -jnp.inf, blk), axis=0, keepdims=True)
        gscore.append(m1 + m2)
    emask = []
    for g in range(N_GROUPS):
        beaten = jnp.zeros((1, n), jnp.int32)
        for g2 in range(N_GROUPS):
            if g2 == g:
                continue
            wins = (gscore[g2] > gscore[g]) | ((gscore[g2] == gscore[g]) & (g2 < g))
            beaten = beaten + wins.astype(jnp.int32)
        emask.append(jnp.broadcast_to(beaten < TOPK_GROUPS, (GROUP_SIZE, n)))
    cur = jnp.where(jnp.concatenate(emask, axis=0), sel, neg)
    chosen = jnp.zeros((E, n), jnp.bool_)
    firsts = []
    for _ in range(TOP_K):
        m = jnp.max(cur, axis=0, keepdims=True)
        first = jnp.min(jnp.where(cur == m, eidx, E), axis=0, keepdims=True)
        hit = eidx == first
        chosen = chosen | hit
        cur = jnp.where(hit, neg, cur)
        firsts.append(first)
    w = jnp.where(chosen, scores_t, 0.0)
    return w / jnp.sum(w, axis=0, keepdims=True) * ROUTE_SCALE, chosen, firsts


def _route_kernel(x_ref, mod_ref, g_ref, rwt_ref, rb_ref, ls_ref, lst_ref, wt_ref, bc_ref, cr_ref, cnt_ref, count):
    first_step = (pl.program_id(0) == 0) & (pl.program_id(1) == 0)

    @pl.when(first_step)
    def _():
        count[...] = jnp.zeros_like(count)

    mod = mod_ref[0]
    h = _rms_mod(x_ref[0], g_ref[...], mod[4:5, :], mod[3:4, :])
    def split(a):
        hi = a.astype(BF16)
        return hi, (a - hi.astype(F32)).astype(BF16)

    nt_dot = lambda a, b: lax.dot_general(a, b, (((1,), (1,)), ((), ())), preferred_element_type=F32)
    (w_hi, w_lo), (h_hi, h_lo) = split(rwt_ref[...]), split(h)
    logits_t = nt_dot(w_hi, h_lo) + nt_dot(w_lo, h_hi) + nt_dot(w_hi, h_hi)
    gates_t, chosen, firsts = _route(_sigmoid(logits_t), rb_ref[...])
    E, tb = gates_t.shape
    ones = jnp.where(chosen, 1.0, 0.0)
    rr = lax.broadcasted_iota(jnp.int32, (tb, tb), 0)
    cc = lax.broadcasted_iota(jnp.int32, (tb, tb), 1)
    upper = jnp.where(rr <= cc, 1.0, 0.0).astype(BF16)
    incl = jnp.dot(ones.astype(BF16), upper, preferred_element_type=F32)
    block_count = jnp.sum(ones, axis=1, keepdims=True)
    block_count = block_count + (block_count.astype(jnp.int32) & 1).astype(F32)
    er = lax.broadcasted_iota(jnp.int32, (E, E), 0)
    ec = lax.broadcasted_iota(jnp.int32, (E, E), 1)
    before = jnp.where(ec < er, 1.0, 0.0).astype(F32)
    local_base = jnp.dot(before, jnp.broadcast_to(block_count, (E, LANES)), precision=HIGHEST,
                         preferred_element_type=F32)[:, 0:1]
    row_all = local_base + (incl - ones)
    eidx = lax.broadcasted_iota(jnp.int32, (E, tb), 0)
    ls_rows = [jnp.sum(jnp.where(eidx == f, row_all, 0.0), axis=0, keepdims=True) for f in firsts]
    w_rows = [jnp.sum(jnp.where(eidx == f, gates_t, 0.0), axis=0, keepdims=True) for f in firsts]
    ls_ref[...] = jnp.concatenate(ls_rows + [jnp.zeros((SUBLANES - TOP_K, tb), F32)], axis=0).astype(jnp.int32)
    zpad = jnp.zeros((LANES - TOP_K, tb), F32)
    lst_ref[0] = jnp.concatenate(ls_rows + [zpad], axis=0).T.astype(jnp.int32)
    wt_ref[0] = jnp.concatenate(w_rows + [zpad], axis=0).T

    def as_row(col):
        sel = lax.broadcasted_iota(jnp.int32, (E, LANES), 0) == lax.broadcasted_iota(jnp.int32, (E, LANES), 1)
        return jnp.sum(jnp.where(sel, col, 0.0), axis=0, keepdims=True).astype(jnp.int32)

    total_lane = lax.broadcasted_iota(jnp.int32, (1, LANES), 1) == N_EXPERTS
    bc_ref[0] = as_row(block_count) + jnp.where(total_lane, jnp.sum(block_count).astype(jnp.int32), 0)
    cr_ref[0] = as_row(count[...])
    count[...] = count[...] + block_count
    cnt_ref[...] = jnp.broadcast_to(count[...], cnt_ref.shape).astype(jnp.int32)


def _route_call(x, mod_l, norm_g, router_wt, router_b, *, tb):
    B, S, D = x.shape
    nj = S // tb
    full = lambda a: pl.BlockSpec(a.shape, lambda b, j: (0,) * a.ndim)
    tok = lambda w: pl.BlockSpec((1, tb, w), lambda b, j: (b, j, 0))
    kt = pl.BlockSpec((SUBLANES, tb), lambda b, j: (0, b * nj + j))
    per_block = pl.BlockSpec((1, 1, LANES), lambda b, j: (b * nj + j, 0, 0))
    return pl.pallas_call(
        _route_kernel,
        grid=(B, nj),
        in_specs=[tok(D), pl.BlockSpec((1, N_MOD, D), lambda b, j: (b, 0, 0)),
                  full(norm_g), full(router_wt), full(router_b)],
        out_specs=[kt, tok(LANES), tok(LANES), per_block, per_block,
                   pl.BlockSpec((N_EXPERTS, LANES), lambda b, j: (0, 0))],
        out_shape=[jax.ShapeDtypeStruct((SUBLANES, B * S), jnp.int32),
                   jax.ShapeDtypeStruct((B, S, LANES), jnp.int32),
                   jax.ShapeDtypeStruct((B, S, LANES), F32),
                   jax.ShapeDtypeStruct((B * nj, 1, LANES), jnp.int32),
                   jax.ShapeDtypeStruct((B * nj, 1, LANES), jnp.int32),
                   jax.ShapeDtypeStruct((N_EXPERTS, LANES), jnp.int32)],
        scratch_shapes=[pltpu.VMEM((N_EXPERTS, 1), F32)],
        compiler_params=_params("arbitrary", "arbitrary"),
        name="route",
    )(x, mod_l, norm_g, router_wt, router_b)


def _route_fin_kernel(cnt_ref, meta_ref, base_ref, *, tm):
    cnt = cnt_ref[...]
    ntile = lax.shift_right_logical(cnt + (tm - 1), int(math.log2(tm)))
    er = lax.broadcasted_iota(jnp.int32, (N_EXPERTS, N_EXPERTS), 0)
    ec = lax.broadcasted_iota(jnp.int32, (N_EXPERTS, N_EXPERTS), 1)
    before = jnp.where(ec < er, 1.0, 0.0).astype(F32)
    start = jnp.dot(before, ntile.astype(F32), precision=HIGHEST, preferred_element_type=F32).astype(jnp.int32)
    base = start * tm
    base_ref[...] = base
    nlane = meta_ref.shape[1]
    lane = lax.broadcasted_iota(jnp.int32, (1, nlane), 1)
    end = (start + ntile)[:, 0:1]
    tile_expert = jnp.sum(jnp.where(end <= lane, 1, 0), axis=0, keepdims=True)
    tile_expert = jnp.minimum(tile_expert, N_EXPERTS - 1)
    meta_ref[...] = jnp.where(lane == nlane - 1, end[N_EXPERTS - 1:N_EXPERTS, :], tile_expert)


def _route_fin(cnt, *, tm, meta_lanes):
    full = lambda a: pl.BlockSpec(a.shape, lambda i: (0,) * a.ndim)
    outs = [jax.ShapeDtypeStruct((1, meta_lanes), jnp.int32), jax.ShapeDtypeStruct(cnt.shape, jnp.int32)]
    return pl.pallas_call(
        functools.partial(_route_fin_kernel, tm=tm),
        grid=(1,),
        in_specs=[full(cnt)],
        out_specs=[full(o) for o in outs],
        out_shape=outs,
        compiler_params=_params("arbitrary"),
        name="route_fin",
    )(cnt)


HALF = D_MODEL // 2
SLAB = HALF // LANES
HIGH_BITS = 0xFFFF0000


def _to_slabs(ref, value, is_bf16_exact=False):
    n = value.shape[0]
    if not is_bf16_exact:
        value = value.astype(BF16).astype(F32)
    bits = lax.bitcast_convert_type(value, jnp.uint32)
    packed = (bits[:, HALF:] & jnp.uint32(HIGH_BITS)) | (bits[:, :HALF] >> 16)
    for s in range(SLAB):
        ref[pl.ds(s, n, stride=SLAB), :] = packed[:, s * LANES:(s + 1) * LANES]


def _from_slabs(ref, n):
    packed = jnp.concatenate([ref[pl.ds(s, n, stride=SLAB), :] for s in range(SLAB)], axis=1)
    low = lax.bitcast_convert_type(packed << 16, F32).astype(BF16)
    high = lax.bitcast_convert_type(packed & jnp.uint32(HIGH_BITS), F32).astype(BF16)
    return low, high


def _for_expert_runs(base_sm, bc_sm, cr_sm, max_run, fn):
    def per_expert(e, local_row):
        n = bc_sm[0, 0, e]
        sorted_row = base_sm[e] + cr_sm[0, 0, e]

        @pl.when(n > 0)
        def _():
            fn(local_row, sorted_row, n)

        return local_row + n

    lax.fori_loop(0, N_EXPERTS, per_expert, 0)


def _for_pieces(n, max_piece, fn):
    done = 0
    bit = max_piece
    while bit >= 2:
        take = n & bit

        @pl.when(take != 0)
        def _(done=done, bit=bit):
            fn(done, bit)

        done = done + take
        bit //= 2


ROW_GROUPS = 7


def _local_rows(tb):
    rows = tb * TOP_K + N_EXPERTS
    assert rows % (ROW_GROUPS * 2 * SUBLANES) == 0, rows
    return rows


def _wait_all_runs(src, dst, sem, total_rows):
    nrows = pl.multiple_of(total_rows * SLAB, 2 * SLAB)
    pltpu.make_async_copy(src.at[pl.ds(0, nrows)], dst.at[pl.ds(0, nrows)], sem).wait()


def _queue(nrows):
    return int(math.log2(nrows)) % 2


def _slab_rows(ref, row, nrows):
    return ref.at[pl.ds(pl.multiple_of(row * SLAB, 2 * SLAB), nrows * SLAB)]


def _dispatch_kernel(cnt_sm, base_sm, bc_sm, cr_sm, bc_prev, cr_prev, ls_ref, x_ref, mod_ref, g_ref, xs_hbm,
                     stage, zeros, sem, zsem, *, tb, tm):
    step = pl.program_id(0) * pl.num_programs(1) + pl.program_id(1)
    last_step = pl.num_programs(0) * pl.num_programs(1) - 1
    cur = lax.rem(step, 2)
    mod = mod_ref[0]
    hb = _rms_mod(x_ref[0], g_ref[...], mod[4:5, :], mod[3:4, :]).astype(BF16)
    ls = ls_ref[...]
    gr = _local_rows(tb) // ROW_GROUPS
    for grp in range(ROW_GROUPS):
        row = grp * gr + lax.broadcasted_iota(jnp.int32, (gr, tb), 0)
        hit = ls[0:1, :] == row
        for k in range(1, TOP_K):
            hit = hit | (ls[k:k + 1, :] == row)
        perm = jnp.where(hit, 1.0, 0.0).astype(BF16)
        _to_slabs(stage.at[cur, pl.ds(grp * gr * SLAB, gr * SLAB)], jnp.dot(perm, hb, preferred_element_type=F32),
                  is_bf16_exact=True)

    def run_copy(buf):
        return lambda local_row, sorted_row, nrows: pltpu.make_async_copy(
            _slab_rows(stage.at[buf], local_row, nrows), _slab_rows(xs_hbm, sorted_row, nrows), sem.at[buf])

    _for_expert_runs(base_sm, bc_sm, cr_sm, tb, lambda *a: run_copy(cur)(*a).start())

    @pl.when(step == 0)
    def _():
        zeros[...] = jnp.zeros_like(zeros)

        def per_expert(e, carry):
            n = cnt_sm[e]
            npad = lax.rem(tm - lax.rem(n, tm), tm)
            zcopy = lambda done, size: pltpu.make_async_copy(
                _slab_rows(zeros, 0, size), _slab_rows(xs_hbm, base_sm[e] + n + done, size), zsem)
            _for_pieces(npad, tm // 2, lambda *a: zcopy(*a).start())
            _for_pieces(npad, tm // 2, lambda *a: zcopy(*a).wait())
            return carry

        lax.fori_loop(0, N_EXPERTS, per_expert, 0)

    @pl.when(step > 0)
    def _():
        _wait_all_runs(stage.at[1 - cur], xs_hbm, sem.at[1 - cur], bc_prev[0, 0, N_EXPERTS])

    @pl.when(step == last_step)
    def _():
        _wait_all_runs(stage.at[cur], xs_hbm, sem.at[cur], bc_sm[0, 0, N_EXPERTS])


def _dispatch(cnt, base, blk_cnt, blk_carry, ls, x, mod_l, norm_g, *, tb, tm, n_slots):
    B, S, D = x.shape
    nj = S // tb
    smem_block = lambda shift: pl.BlockSpec(
        (1, 1, LANES), lambda b, j, *_: (jnp.maximum(b * nj + j + shift, 0), 0, 0), memory_space=pltpu.SMEM)
    grid_spec = pltpu.PrefetchScalarGridSpec(
        num_scalar_prefetch=2,
        grid=(B, nj),
        in_specs=[smem_block(0), smem_block(0), smem_block(-1), smem_block(-1),
                  pl.BlockSpec((SUBLANES, tb), lambda b, j, *_: (0, b * nj + j)),
                  pl.BlockSpec((1, tb, D), lambda b, j, *_: (b, j, 0)),
                  pl.BlockSpec((1, N_MOD, D), lambda b, j, *_: (b, 0, 0)),
                  pl.BlockSpec(norm_g.shape, lambda b, j, *_: (0, 0))],
        out_specs=pl.BlockSpec(memory_space=pltpu.HBM),
        scratch_shapes=[pltpu.VMEM((2, _local_rows(tb) * SLAB, LANES), jnp.uint32),
                        pltpu.VMEM((tm // 2 * SLAB, LANES), jnp.uint32),
                        pltpu.SemaphoreType.DMA((2,)), pltpu.SemaphoreType.DMA])
    return pl.pallas_call(
        functools.partial(_dispatch_kernel, tb=tb, tm=tm),
        grid_spec=grid_spec,
        out_shape=jax.ShapeDtypeStruct((n_slots * SLAB, LANES), jnp.uint32),
        compiler_params=_params("arbitrary", "arbitrary"),
        name="dispatch",
    )(cnt, base, blk_cnt, blk_carry, blk_cnt, blk_carry, ls, x, mod_l, norm_g)


def _experts_kernel(te_sm, nt_sm, x_ref, w1_ref, w3_ref, w2_ref, y_ref, w1b, w3b, w2b, *, tm):
    i = pl.program_id(0)
    in_use = i < nt_sm[0]

    @pl.when(in_use)
    def _():
        @pl.when((i == 0) | (te_sm[i] != te_sm[jnp.maximum(i - 1, 0)]))
        def _():
            w1b[...] = w1_ref[0].astype(BF16)
            w3b[...] = w3_ref[0].astype(BF16)
            w2b[...] = w2_ref[0].astype(BF16)

        for c in range(tm // EXPERT_CHUNK):
            rows = pl.ds(c * EXPERT_CHUNK * SLAB, EXPERT_CHUNK * SLAB)
            x_lo, x_hi = _from_slabs(x_ref.at[rows], EXPERT_CHUNK)
            up = lambda w: (jnp.dot(x_lo, w[:HALF, :], preferred_element_type=F32)
                            + jnp.dot(x_hi, w[HALF:, :], preferred_element_type=F32))
            hid = _silu(up(w1b)) * up(w3b)
            _to_slabs(y_ref.at[rows], jnp.dot(hid.astype(BF16), w2b[...], preferred_element_type=F32))

    @pl.when(jnp.logical_not(in_use))
    def _():
        y_ref[...] = jnp.zeros_like(y_ref)


def _experts(tile_expert, n_tiles, xs, w1, w3, w2, layer, *, tm):
    _, E, D, F = w1.shape
    n_slots = xs.shape[0] // SLAB
    last = lambda i, te, nt: jnp.minimum(i, nt[0] - 1)
    grid_spec = pltpu.PrefetchScalarGridSpec(
        num_scalar_prefetch=2,
        grid=(n_slots // tm,),
        in_specs=[pl.BlockSpec((tm * SLAB, LANES), lambda i, te, nt: (last(i, te, nt), 0)),
                  pl.BlockSpec((None, 1, D, F), lambda i, te, nt: (layer, te[last(i, te, nt)], 0, 0)),
                  pl.BlockSpec((None, 1, D, F), lambda i, te, nt: (layer, te[last(i, te, nt)], 0, 0)),
                  pl.BlockSpec((None, 1, F, D), lambda i, te, nt: (layer, te[last(i, te, nt)], 0, 0))],
        out_specs=pl.BlockSpec((tm * SLAB, LANES), lambda i, te, nt: (i, 0)),
        scratch_shapes=[pltpu.VMEM((D, F), BF16), pltpu.VMEM((D, F), BF16), pltpu.VMEM((F, D), BF16)])
    return pl.pallas_call(
        functools.partial(_experts_kernel, tm=tm),
        grid_spec=grid_spec,
        out_shape=jax.ShapeDtypeStruct(xs.shape, jnp.uint32),
        compiler_params=_params("arbitrary"),
        name="experts",
    )(tile_expert, n_tiles, xs, w1, w3, w2)


def _combine_kernel(base_sm, bc_sm, cr_sm, bc_next, cr_next, x_ref, mod_ref, g_ref, lst_ref, wt_ref,
                    sw1_ref, sw3_ref, sw2_ref, fg_ref, ys_hbm, o_ref, stage, sem, *, tb, final_norm):
    step = pl.program_id(0) * pl.num_programs(1) + pl.program_id(1)
    last_step = pl.num_programs(0) * pl.num_programs(1) - 1
    cur = lax.rem(step, 2)

    def run_copy(buf):
        return lambda local_row, sorted_row, nrows: pltpu.make_async_copy(
            _slab_rows(ys_hbm, sorted_row, nrows), _slab_rows(stage.at[buf], local_row, nrows), sem.at[buf])

    @pl.when(step == 0)
    def _():
        stage[...] = jnp.zeros_like(stage)
        _for_expert_runs(base_sm, bc_sm, cr_sm, tb, lambda *a: run_copy(cur)(*a).start())

    @pl.when(step < last_step)
    def _():
        _for_expert_runs(base_sm, bc_next, cr_next, tb, lambda *a: run_copy(1 - cur)(*a).start())

    x = x_ref[0]
    mod = mod_ref[0]
    hb = _rms_mod(x, g_ref[...], mod[4:5, :], mod[3:4, :]).astype(BF16)
    a = jnp.dot(hb, sw1_ref[...], preferred_element_type=F32)
    b = jnp.dot(hb, sw3_ref[...], preferred_element_type=F32)
    acc = jnp.dot((_silu(a) * b).astype(BF16), sw2_ref[...], preferred_element_type=F32)
    wt = wt_ref[0]
    lst = lst_ref[0]
    _wait_all_runs(ys_hbm, stage.at[cur], sem.at[cur], bc_sm[0, 0, N_EXPERTS])
    gr = _local_rows(tb) // ROW_GROUPS
    routed = [jnp.zeros((tb, HALF), F32)] * 2
    for grp in range(ROW_GROUPS):
        halves = _from_slabs(stage.at[cur, pl.ds(grp * gr * SLAB, gr * SLAB)], gr)
        row = grp * gr + lax.broadcasted_iota(jnp.int32, (1, gr), 1)
        gate = jnp.zeros((tb, gr), F32)
        for k in range(TOP_K):
            gate = jnp.where(lst[:, k:k + 1] == row, wt[:, k:k + 1], gate)
        gate = gate.astype(BF16)
        routed = [r + jnp.dot(gate, yg, preferred_element_type=F32) for r, yg in zip(routed, halves)]
    acc = acc + jnp.concatenate(routed, axis=1)
    y = x + mod[5:6, :] * acc
    if final_norm:
        y = y * lax.rsqrt(jnp.mean(y * y, axis=-1, keepdims=True) + EPS) * fg_ref[...]
    o_ref[0] = y


def _combine(base, blk_cnt, blk_carry, x, mod_l, norm_g, lst, wt, sw1, sw3, sw2, final_g, ys, *, tb, final_norm):
    B, S, D = x.shape
    nj = S // tb
    full = lambda a: pl.BlockSpec(a.shape, lambda b, j, *_: (0,) * a.ndim)
    tok = lambda w: pl.BlockSpec((1, tb, w), lambda b, j, *_: (b, j, 0))
    smem_block = lambda shift: pl.BlockSpec(
        (1, 1, LANES), lambda b, j, *_: (jnp.minimum(b * nj + j + shift, B * nj - 1), 0, 0), memory_space=pltpu.SMEM)
    grid_spec = pltpu.PrefetchScalarGridSpec(
        num_scalar_prefetch=1,
        grid=(B, nj),
        in_specs=[smem_block(0), smem_block(0), smem_block(1), smem_block(1), tok(D),
                  pl.BlockSpec((1, N_MOD, D), lambda b, j, *_: (b, 0, 0)),
                  full(norm_g), tok(LANES), tok(LANES), full(sw1), full(sw3), full(sw2), full(final_g),
                  pl.BlockSpec(memory_space=pltpu.HBM)],
        out_specs=tok(D),
        scratch_shapes=[pltpu.VMEM((2, _local_rows(tb) * SLAB, LANES), jnp.uint32), pltpu.SemaphoreType.DMA((2,))])
    return pl.pallas_call(
        functools.partial(_combine_kernel, tb=tb, final_norm=final_norm),
        grid_spec=grid_spec,
        out_shape=jax.ShapeDtypeStruct((B, S, D), F32),
        compiler_params=_params("arbitrary", "arbitrary"),
        name="combine",
    )(base, blk_cnt, blk_carry, blk_cnt, blk_carry, x, mod_l, norm_g, lst, wt, sw1, sw3, sw2, final_g, ys)


def _block_diag(w):
    G, a, b = w.shape
    out = jnp.zeros((G * a, G * b), w.dtype)
    for g in range(G):
        out = out.at[g * a:(g + 1) * a, g * b:(g + 1) * b].set(w[g])
    return out


def kernel(x, c, w_in, b_f, pool_w, pool_scale, sg_ln_g, sg_ln_b, sg_w, sg_b, conv_w, conv_b, conv_ln_g,
           conv_ln_b, w_branch, w_out, mix_norm_g, ffn_norm_g, ada_w, ada_b, router_w, router_bias,
           exp_w1, exp_w3, exp_w2, shared_w1, shared_w3, shared_w2, final_norm_g):
    B, S, D = x.shape
    L = w_in.shape[0]
    tb = min(512, S)
    tq = tb
    tm = EXPERT_TILE
    n_tiles = -(-(B * (S // tb) * _local_rows(tb)) // tm) + N_EXPERTS
    meta_lanes = -(-(n_tiles + 1) // LANES) * LANES
    row = lambda a: a.reshape(1, -1)

    mod = _modulation(c, ada_w, ada_b)
    for l in range(L):
        bfc = jnp.pad(b_f[l], (0, LANES - ATT_HEADS)).reshape(1, LANES)
        poolw = _block_diag(pool_w[l]).astype(BF16)
        sgb = jnp.repeat(sg_b[l].T, SG_HD, axis=1)

        ya, yb, yc, q, k, v = _mixer_in(
            x, mod[l], row(mix_norm_g[l]), w_in, l, bfc, poolw, row(pool_scale[l]),
            row(sg_ln_g[l]), row(sg_ln_b[l]), sg_w[l], sgb, conv_w[l], row(conv_b[l]),
            row(conv_ln_g[l]), row(conv_ln_b[l]), tb=tb)
        yd = _attention(q, k, v, tq=tq)
        x = _merge(x, mod[l], row(mix_norm_g[l]), w_in, l, w_branch[l].astype(BF16), w_out[l].astype(BF16),
                   ya, yb, yc, yd, tb=tb)
        fnorm = row(ffn_norm_g[l])
        ls, lst, wt, blk_cnt, blk_carry, cnt = _route_call(
            x, mod[l], fnorm, router_w[l].T, router_bias[l].reshape(-1, 1), tb=tb)
        meta, base = _route_fin(cnt, tm=tm, meta_lanes=meta_lanes)
        xs = _dispatch(cnt[:, 0], base[:, 0], blk_cnt, blk_carry, ls, x, mod[l], fnorm,
                       tb=tb, tm=tm, n_slots=n_tiles * tm)
        ys = _experts(meta[0, :n_tiles], meta[0, meta_lanes - 1:], xs, exp_w1, exp_w3, exp_w2, l, tm=tm)
        x = _combine(base[:, 0], blk_cnt, blk_carry, x, mod[l], fnorm, lst, wt, shared_w1[l].astype(BF16),
                     shared_w3[l].astype(BF16), shared_w2[l].astype(BF16), row(final_norm_g), ys,
                     tb=tb, final_norm=(l == L - 1))
    return x
```

```python
import functools
import math

import jax
import jax.numpy as jnp
from jax import lax
from jax.experimental import pallas as pl
from jax.experimental.pallas import tpu as pltpu

F32 = jnp.float32
BF16 = jnp.bfloat16
HIGHEST = lax.Precision.HIGHEST

D_MODEL = 1024
BRANCH_W = 256
N_BRANCH = 4
POOL_WINDOWS = (2, 4, 8, 16)
POOL_GW = 64
POOL_TAIL = 16
SG_HEADS = 4
SG_CHUNK = 128
SG_HD = 64
CONV_WIDTH = 31
CONV_TAIL = 32
ATT_HEADS = 4
ATT_HD = 64
N_EXPERTS = 64
TOP_K = 6
N_GROUPS = 8
GROUP_SIZE = N_EXPERTS // N_GROUPS
TOPK_GROUPS = 4
EXPERT_FF = 256
SHARED_FF = 256
ROUTE_SCALE = 2.5
EPS = 1e-6
N_MOD = 6
MIX_COLS = 8 * BRANCH_W
LANES = 128
SUBLANES = 8
LOG2E = 1.4426950408889634
EXPERT_TILE = 512
EXPERT_CHUNK = 256
MOE_BLOCK = 512
VMEM_LIMIT = 56 * 1024 * 1024


def _sigmoid(x):
    return 1.0 / (1.0 + jnp.exp(-x))


def _silu(x):
    return x * _sigmoid(x)


def _rms_mod(x, g, scale, shift):
    y = x * lax.rsqrt(jnp.mean(x * x, axis=-1, keepdims=True) + EPS)
    return (y * g) * (1.0 + scale) + shift


def _layer_norm(x, g, b):
    mu = jnp.mean(x, axis=-1, keepdims=True)
    xc = x - mu
    var = jnp.mean(xc * xc, axis=-1, keepdims=True)
    return xc * lax.rsqrt(var + EPS) * g + b


def _params(*sem):
    return pltpu.CompilerParams(dimension_semantics=sem, vmem_limit_bytes=VMEM_LIMIT)


def _mod_kernel(c_ref, w_ref, b_ref, o_ref):
    c = c_ref[...]
    o_ref[0] = jnp.dot(_silu(c), w_ref[0], precision=HIGHEST, preferred_element_type=F32) + b_ref[0]


def _modulation(c, ada_w, ada_b):
    L, D, N = ada_w.shape
    B = c.shape[0]
    tn = 1536
    out = pl.pallas_call(
        _mod_kernel,
        grid=(L, N // tn),
        in_specs=[pl.BlockSpec((B, D), lambda l, j: (0, 0)),
                  pl.BlockSpec((1, D, tn), lambda l, j: (l, 0, j)),
                  pl.BlockSpec((1, 1, tn), lambda l, j: (l, 0, j))],
        out_specs=pl.BlockSpec((1, B, tn), lambda l, j: (l, 0, j)),
        out_shape=jax.ShapeDtypeStruct((L, B, N), F32),
        compiler_params=_params("arbitrary", "arbitrary"),
        name="modulation",
    )(c, ada_w, ada_b.reshape(L, 1, N))
    return out.reshape(L, B, N_MOD, D)


def _mixer_in_kernel(x_ref, mod_ref, g_ref, wmix_f32, wfc_f32, bfc_ref,
                     poolw_ref, pools_ref, sglg_ref, sglb_ref, sgw_ref, sgb_ref,
                     cw_ref, cb_ref, clg_ref, clb_ref,
                     ya_ref, yb_ref, yc_ref, q_ref, k_ref, v_ref,
                     pool_ext, conv_ext, cum_c, wmix_ref, wfc_ref, *, tb):
    j = pl.program_id(1)

    @pl.when((pl.program_id(0) == 0) & (j == 0))
    def _():
        wmix_ref[...] = wmix_f32[0].astype(BF16)
        wfc_ref[...] = wfc_f32[0].astype(BF16)

    @pl.when(j == 0)
    def _():
        pool_ext[0:POOL_TAIL, :] = jnp.zeros((POOL_TAIL, BRANCH_W), F32)
        conv_ext[0:CONV_TAIL, :] = jnp.zeros((CONV_TAIL, BRANCH_W), F32)
        cum_c[...] = jnp.zeros_like(cum_c)

    mod = mod_ref[0]
    h = _rms_mod(x_ref[0], g_ref[...], mod[1:2, :], mod[0:1, :])
    hb = h.astype(BF16)
    proj = jnp.dot(hb, wmix_ref[...], preferred_element_type=F32)

    lane = lax.broadcasted_iota(jnp.int32, (1, BRANCH_W), 1)
    row = lax.broadcasted_iota(jnp.int32, (tb, 1), 0)

    u = proj[:, 0:BRANCH_W]
    pool_ext[POOL_TAIL:POOL_TAIL + tb, :] = u
    ext = pool_ext[...]
    s2 = ext + pltpu.roll(ext, 1, 0)
    s4 = s2 + pltpu.roll(s2, 2, 0)
    s8 = s4 + pltpu.roll(s4, 4, 0)
    s16 = s8 + pltpu.roll(s8, 8, 0)
    grp = lane // POOL_GW
    wsum = jnp.where(grp == 0, s2, jnp.where(grp == 1, s4, jnp.where(grp == 2, s8, s16)))[POOL_TAIL:, :]
    win = jnp.where(grp == 0, 2.0, jnp.where(grp == 1, 4.0, jnp.where(grp == 2, 8.0, 16.0)))
    count = jnp.minimum((j * tb + row + 1).astype(F32), win)
    pooled = wsum / count - u
    ya = jnp.dot(pooled.astype(BF16), poolw_ref[...], preferred_element_type=F32) * pools_ref[...]
    ya_ref[0] = ya.astype(BF16)
    pool_ext[0:POOL_TAIL, :] = u[tb - POOL_TAIL:, :]

    z = proj[:, BRANCH_W:3 * BRANCH_W]
    z = 0.5 * z * (1.0 + jnp.tanh(math.sqrt(2.0 / math.pi) * (z + 0.044715 * (z * z * z))))
    su = z[:, 0:BRANCH_W]
    sv = _layer_norm(z[:, BRANCH_W:], sglg_ref[...], sglb_ref[...])
    r128 = lax.broadcasted_iota(jnp.int32, (SG_CHUNK, SG_CHUNK), 0)
    c128 = lax.broadcasted_iota(jnp.int32, (SG_CHUNK, SG_CHUNK), 1)
    wcat = jnp.concatenate(
        [jnp.where(r128 >= c128, sgw_ref[hh], 0.0) for hh in range(SG_HEADS)], axis=1).astype(BF16)
    head = lane // SG_HD
    for ci in range(tb // SG_CHUNK):
        rows = slice(ci * SG_CHUNK, (ci + 1) * SG_CHUNK)
        vch = sv[rows, :]
        vstack = jnp.concatenate(
            [jnp.where(head == hh, vch, 0.0) for hh in range(SG_HEADS)], axis=0).astype(BF16)
        s = jnp.dot(wcat, vstack, preferred_element_type=F32) + sgb_ref[...]
        yb_ref[0, rows, :] = (su[rows, :] * s).astype(BF16)

    glu = proj[:, 3 * BRANCH_W:4 * BRANCH_W] * _sigmoid(proj[:, 4 * BRANCH_W:5 * BRANCH_W])
    conv_ext[CONV_TAIL:CONV_TAIL + tb, :] = glu
    acc = jnp.zeros((tb, BRANCH_W), F32) + cb_ref[...]
    for kk in range(CONV_WIDTH):
        off = CONV_TAIL - (CONV_WIDTH - 1) + kk
        acc = acc + cw_ref[kk:kk + 1, :] * conv_ext[off:off + tb, :]
    yc_ref[0] = _silu(_layer_norm(acc, clg_ref[...], clb_ref[...])).astype(BF16)
    conv_ext[0:CONV_TAIL, :] = glu[tb - CONV_TAIL:, :]

    def log_sigmoid(t):
        return jnp.minimum(t, 0.0) - jnp.log(1.0 + jnp.exp(-jnp.abs(t)))

    rr = lax.broadcasted_iota(jnp.int32, (tb, tb), 0)
    cc = lax.broadcasted_iota(jnp.int32, (tb, tb), 1)
    lower = jnp.where(rr >= cc, 1.0, 0.0).astype(BF16)
    lf_c = log_sigmoid(jnp.dot(hb, wfc_ref[...], preferred_element_type=F32) + bfc_ref[...])
    lf_hi = lf_c.astype(BF16)
    lf_mid = (lf_c - lf_hi.astype(F32)).astype(BF16)
    lf_lo = ((lf_c - lf_hi.astype(F32)) - lf_mid.astype(F32)).astype(BF16)
    cs_c = cum_c[...] + sum(jnp.dot(lower, piece, preferred_element_type=F32) for piece in (lf_lo, lf_mid, lf_hi))
    cum_c[...] = cs_c[tb - 1:tb, :]
    cl2 = cs_c * LOG2E
    ln = lax.broadcasted_iota(jnp.int32, (1, LANES), 1)
    ones_q = jnp.where((ln >= ATT_HD + 3) & (ln < ATT_HD + 6), 1.0, 0.0)
    ones_k = jnp.where((ln >= ATT_HD) & (ln < ATT_HD + 3), 1.0, 0.0)
    ones_v = jnp.where(ln == ATT_HD, 1.0, 0.0)
    for hh in range(ATT_HEADS):
        pair = (hh // 2) * LANES
        qs = proj[:, 5 * BRANCH_W + pair:5 * BRANCH_W + pair + LANES] * (LOG2E / math.sqrt(ATT_HD))
        ks = proj[:, 6 * BRANCH_W + pair:6 * BRANCH_W + pair + LANES]
        vs = proj[:, 7 * BRANCH_W + pair:7 * BRANCH_W + pair + LANES]
        if hh % 2:
            qs, ks, vs = (pltpu.roll(a, ATT_HD, 1) for a in (qs, ks, vs))
        c = cl2[:, hh:hh + 1]
        hi = c.astype(BF16).astype(F32)
        mid = (c - hi).astype(BF16).astype(F32)
        lo = (c - hi) - mid
        q_extra = jnp.where(ln == ATT_HD, hi, jnp.where(ln == ATT_HD + 1, mid, jnp.where(ln == ATT_HD + 2, lo, ones_q)))
        k_extra = jnp.where(ln == ATT_HD + 3, -hi,
                            jnp.where(ln == ATT_HD + 4, -mid, jnp.where(ln == ATT_HD + 5, -lo, ones_k)))
        q_ref[0, hh, 0] = jnp.where(ln < ATT_HD, qs, q_extra).T.astype(BF16)
        k_ref[0, hh] = jnp.where(ln < ATT_HD, ks, k_extra).astype(BF16)
        v_ref[0, hh, 0] = jnp.where(ln < ATT_HD, vs, ones_v).T.astype(BF16)


def _w_in_cols(layer, first_col, ncols):
    return pl.BlockSpec((1, D_MODEL, ncols), lambda b, j: (layer, 0, first_col // ncols), pipeline_mode=pl.Buffered(1))


def _mixer_in(x, mod_l, norm_g, w_in, layer, bfc, poolw, pools, sglg, sglb, sgw, sgb, cw, cb, clg, clb, *, tb):
    B, S, D = x.shape
    full = lambda a: pl.BlockSpec(a.shape, lambda b, j: (0,) * a.ndim)
    tok = lambda w: pl.BlockSpec((1, tb, w), lambda b, j: (b, j, 0))
    head = pl.BlockSpec((1, ATT_HEADS, tb, LANES), lambda b, j: (b, 0, j, 0))
    head_t = pl.BlockSpec((1, ATT_HEADS, 1, LANES, tb), lambda b, j: (b, 0, j, 0, 0))
    consts = (bfc, poolw, pools, sglg, sglb, sgw, sgb, cw, cb, clg, clb)
    act = jax.ShapeDtypeStruct((B, S, BRANCH_W), BF16)
    att = jax.ShapeDtypeStruct((B, ATT_HEADS, S, LANES), BF16)
    att_t = jax.ShapeDtypeStruct((B, ATT_HEADS, S // tb, LANES, tb), BF16)
    return pl.pallas_call(
        functools.partial(_mixer_in_kernel, tb=tb),
        grid=(B, S // tb),
        in_specs=[tok(D), pl.BlockSpec((1, N_MOD, D), lambda b, j: (b, 0, 0)), full(norm_g),
                  _w_in_cols(layer, 0, MIX_COLS), _w_in_cols(layer, MIX_COLS, LANES)] + [full(a) for a in consts],
        out_specs=[tok(BRANCH_W)] * 3 + [head_t, head, head_t],
        out_shape=[act] * 3 + [att_t, att, att_t],
        scratch_shapes=[pltpu.VMEM((POOL_TAIL + tb, BRANCH_W), F32),
                        pltpu.VMEM((CONV_TAIL + tb, BRANCH_W), F32),
                        pltpu.VMEM((1, LANES), F32),
                        pltpu.VMEM((D, MIX_COLS), BF16), pltpu.VMEM((D, LANES), BF16)],
        compiler_params=_params("arbitrary", "arbitrary"),
        name="mixer_in",
    )(x, mod_l, norm_g, w_in, w_in, *consts)


def _attn_kernel(q_ref, k_ref, v_ref, o_ref, m_ref, acc_ref, *, tq):
    i = pl.program_id(1)
    key = lax.broadcasted_iota(jnp.int32, (tq, tq), 0)
    qry = lax.broadcasted_iota(jnp.int32, (tq, tq), 1)
    m_ref[...] = jnp.full(m_ref.shape, -jnp.inf, F32)
    acc_ref[...] = jnp.zeros(acc_ref.shape, F32)

    def block(kj, diagonal):
        ks = pl.multiple_of(kj * tq, tq)
        logits = [jnp.dot(k_ref[0, hh, pl.ds(ks, tq), :], q_ref[0, hh, 0], preferred_element_type=F32)
                  for hh in range(ATT_HEADS)]
        for hh in range(ATT_HEADS):
            s = logits[hh]
            if diagonal:
                s = jnp.where(key <= qry, s, -jnp.inf)
            m_old = m_ref[hh]
            m_new = jnp.maximum(m_old, jnp.max(s, axis=0, keepdims=True))
            p = jnp.exp2(s - m_new)
            pv = jnp.dot(v_ref[0, hh, kj], p.astype(BF16), preferred_element_type=F32)
            acc_ref[hh] = jnp.exp2(m_old - m_new) * acc_ref[hh] + pv
            m_ref[hh] = m_new

    def body(kj, carry):
        block(kj, False)
        return carry

    lax.fori_loop(0, i, body, 0)
    block(i, True)

    ln = lax.broadcasted_iota(jnp.int32, (1, LANES), 1)
    for pr in range(ATT_HEADS // 2):
        o = []
        for hh in (2 * pr, 2 * pr + 1):
            acc = acc_ref[hh]
            o.append((acc / acc[ATT_HD:ATT_HD + 1, :]).T)
        o_ref[0, :, pr * LANES:(pr + 1) * LANES] = jnp.where(ln < ATT_HD, o[0], pltpu.roll(o[1], ATT_HD, 1)).astype(BF16)


def _attention(q_t, k, v_t, *, tq):
    B, H, S, W = k.shape
    nblk = S // tq
    assert q_t.shape == (B, H, nblk, W, tq) and v_t.shape == q_t.shape
    return pl.pallas_call(
        functools.partial(_attn_kernel, tq=tq),
        grid=(B, nblk),
        in_specs=[pl.BlockSpec((1, H, 1, W, tq), lambda b, i: (b, 0, i, 0, 0)),
                  pl.BlockSpec((1, H, S, W), lambda b, i: (b, 0, 0, 0), pipeline_mode=pl.Buffered(1)),
                  pl.BlockSpec((1, H, nblk, W, tq), lambda b, i: (b, 0, 0, 0, 0), pipeline_mode=pl.Buffered(1))],
        out_specs=pl.BlockSpec((1, tq, BRANCH_W), lambda b, i: (b, i, 0)),
        out_shape=jax.ShapeDtypeStruct((B, S, BRANCH_W), BF16),
        scratch_shapes=[pltpu.VMEM((H, 1, tq), F32), pltpu.VMEM((H, W, tq), F32)],
        compiler_params=_params("arbitrary", "arbitrary"),
        name="attention",
    )(q_t, k, v_t)


def _merge_kernel(x_ref, mod_ref, g_ref, wa_f32, wb_f32, wt_f32, wbr_ref, wout_ref, ya_ref, yb_ref, yc_ref, yd_ref,
                  fg_ref, rwt_ref, rb_ref,
                  o_ref, ls_ref, lst_ref, wt_ref, bc_ref, cr_ref, cnt_ref, wg_ref, count):
    @pl.when((pl.program_id(0) == 0) & (pl.program_id(1) == 0))
    def _():
        shift, wide = GATE_SHIFT, D_MODEL + LANES
        windows = (wa_f32[0, :, 0:wide],
                   jnp.concatenate([wa_f32[0, :, D_MODEL:], wb_f32[0, :, 0:LANES]], axis=1),
                   wb_f32[0, :, 0:wide],
                   jnp.concatenate([wb_f32[0, :, D_MODEL:], wt_f32[0]], axis=1))
        for n, win in enumerate(windows):
            wg_ref[:, n * D_MODEL:(n + 1) * D_MODEL] = pltpu.roll(win, wide - shift, 1)[:, :D_MODEL].astype(BF16)

    x = x_ref[0]
    mod = mod_ref[0]
    hb = _rms_mod(x, g_ref[...], mod[1:2, :], mod[0:1, :]).astype(BF16)
    merged = jnp.zeros(x.shape, F32)
    for n, y_ref in enumerate((ya_ref, yb_ref, yc_ref, yd_ref)):
        gate = _sigmoid(jnp.dot(hb, wg_ref[:, n * D_MODEL:(n + 1) * D_MODEL], preferred_element_type=F32))
        merged = merged + gate * jnp.dot(y_ref[0], wbr_ref[n], preferred_element_type=F32)
    out = jnp.dot(merged.astype(BF16), wout_ref[...], preferred_element_type=F32)
    x_new = x + mod[2:3, :] * out
    o_ref[0] = x_new
    _route_block(x_new, mod, fg_ref, rwt_ref, rb_ref, ls_ref, lst_ref, wt_ref, bc_ref, cr_ref, cnt_ref, count)


GATE_SHIFT = ATT_HEADS


def _merge(x, mod_l, norm_g, w_in, layer, wbranch, wout, ya, yb, yc, yd, ffn_norm_g, router_wt, router_b, *, tb):
    B, S, D = x.shape
    full = lambda a: pl.BlockSpec(a.shape, lambda b, j: (0,) * a.ndim)
    tok = lambda w: pl.BlockSpec((1, tb, w), lambda b, j: (b, j, 0))
    assert w_in.shape[2] == MIX_COLS + GATE_SHIFT + N_BRANCH * D
    route_specs, route_shapes = _route_out_specs(B, S, tb)
    return pl.pallas_call(
        _merge_kernel,
        grid=(B, S // tb),
        in_specs=[tok(D), pl.BlockSpec((1, N_MOD, D), lambda b, j: (b, 0, 0)), full(norm_g),
                  _w_in_cols(layer, MIX_COLS, 2 * D), _w_in_cols(layer, MIX_COLS + 2 * D, 2 * D),
                  _w_in_cols(layer, MIX_COLS + 4 * D, LANES),
                  full(wbranch), full(wout)] + [tok(BRANCH_W)] * 4
                 + [full(ffn_norm_g), full(router_wt), full(router_b)],
        out_specs=[tok(D)] + route_specs,
        out_shape=[jax.ShapeDtypeStruct((B, S, D), F32)] + route_shapes,
        scratch_shapes=[pltpu.VMEM((D, N_BRANCH * D), BF16), pltpu.VMEM((N_EXPERTS, 1), F32)],
        compiler_params=_params("arbitrary", "arbitrary"),
        name="merge",
    )(x, mod_l, norm_g, w_in, w_in, w_in, wbranch, wout, ya, yb, yc, yd, ffn_norm_g, router_wt, router_b)


def _route(scores_t, bias_t):
    E, n = scores_t.shape
    sel = scores_t + bias_t
    eidx = lax.broadcasted_iota(jnp.int32, (E, n), 0)
    neg = jnp.full((E, n), -jnp.inf, F32)
    gscore = []
    sub = lax.broadcasted_iota(jnp.int32, (GROUP_SIZE, n), 0)
    for g in range(N_GROUPS):
        blk = sel[g * GROUP_SIZE:(g + 1) * GROUP_SIZE, :]
        m1 = jnp.max(blk, axis=0, keepdims=True)
        first = jnp.min(jnp.where(blk == m1, sub, GROUP_SIZE), axis=0, keepdims=True)
        m2 = jnp.max(jnp.where(sub == first, -jnp.inf, blk), axis=0, keepdims=True)
        gscore.append(m1 + m2)
    emask = []
    for g in range(N_GROUPS):
        beaten = jnp.zeros((1, n), jnp.int32)
        for g2 in range(N_GROUPS):
            if g2 == g:
                continue
            wins = (gscore[g2] > gscore[g]) | ((gscore[g2] == gscore[g]) & (g2 < g))
            beaten = beaten + wins.astype(jnp.int32)
        emask.append(jnp.broadcast_to(beaten < TOPK_GROUPS, (GROUP_SIZE, n)))
    cur = jnp.where(jnp.concatenate(emask, axis=0), sel, neg)
    chosen = jnp.zeros((E, n), jnp.bool_)
    firsts = []
    for _ in range(TOP_K):
        m = jnp.max(cur, axis=0, keepdims=True)
        first = jnp.min(jnp.where(cur == m, eidx, E), axis=0, keepdims=True)
        hit = eidx == first
        chosen = chosen | hit
        cur = jnp.where(hit, neg, cur)
        firsts.append(first)
    w = jnp.where(chosen, scores_t, 0.0)
    return w / jnp.sum(w, axis=0, keepdims=True) * ROUTE_SCALE, chosen, firsts


def _route_block(x, mod, g_ref, rwt_ref, rb_ref, ls_ref, lst_ref, wt_ref, bc_ref, cr_ref, cnt_ref, count):
    first_step = (pl.program_id(0) == 0) & (pl.program_id(1) == 0)

    @pl.when(first_step)
    def _():
        count[...] = jnp.zeros_like(count)

    h = _rms_mod(x, g_ref[...], mod[4:5, :], mod[3:4, :])
    def split(a):
        hi = a.astype(BF16)
        return hi, (a - hi.astype(F32)).astype(BF16)

    nt_dot = lambda a, b: lax.dot_general(a, b, (((1,), (1,)), ((), ())), preferred_element_type=F32)
    (w_hi, w_lo), (h_hi, h_lo) = split(rwt_ref[...]), split(h)
    logits_t = nt_dot(w_hi, h_lo) + nt_dot(w_lo, h_hi) + nt_dot(w_hi, h_hi)
    gates_t, chosen, firsts = _route(_sigmoid(logits_t), rb_ref[...])
    E, tb = gates_t.shape
    ones = jnp.where(chosen, 1.0, 0.0)
    rr = lax.broadcasted_iota(jnp.int32, (tb, tb), 0)
    cc = lax.broadcasted_iota(jnp.int32, (tb, tb), 1)
    upper = jnp.where(rr <= cc, 1.0, 0.0).astype(BF16)
    incl = jnp.dot(ones.astype(BF16), upper, preferred_element_type=F32)
    block_count = jnp.sum(ones, axis=1, keepdims=True)
    block_count = block_count + (block_count.astype(jnp.int32) & 1).astype(F32)
    er = lax.broadcasted_iota(jnp.int32, (E, E), 0)
    ec = lax.broadcasted_iota(jnp.int32, (E, E), 1)
    before = jnp.where(ec < er, 1.0, 0.0).astype(F32)
    local_base = jnp.dot(before, jnp.broadcast_to(block_count, (E, LANES)), precision=HIGHEST,
                         preferred_element_type=F32)[:, 0:1]
    row_all = local_base + (incl - ones)
    eidx = lax.broadcasted_iota(jnp.int32, (E, tb), 0)
    ls_rows = [jnp.sum(jnp.where(eidx == f, row_all, 0.0), axis=0, keepdims=True) for f in firsts]
    w_rows = [jnp.sum(jnp.where(eidx == f, gates_t, 0.0), axis=0, keepdims=True) for f in firsts]
    ls_ref[...] = jnp.concatenate(ls_rows + [jnp.zeros((SUBLANES - TOP_K, tb), F32)], axis=0).astype(jnp.int32)
    zpad = jnp.zeros((LANES - TOP_K, tb), F32)
    lst_ref[0] = jnp.concatenate(ls_rows + [zpad], axis=0).T.astype(jnp.int32)
    wt_ref[0] = jnp.concatenate(w_rows + [zpad], axis=0).T

    def as_row(col):
        sel = lax.broadcasted_iota(jnp.int32, (E, LANES), 0) == lax.broadcasted_iota(jnp.int32, (E, LANES), 1)
        return jnp.sum(jnp.where(sel, col, 0.0), axis=0, keepdims=True).astype(jnp.int32)

    total_lane = lax.broadcasted_iota(jnp.int32, (1, LANES), 1) == N_EXPERTS
    bc_ref[0] = as_row(block_count) + jnp.where(total_lane, jnp.sum(block_count).astype(jnp.int32), 0)
    cr_ref[0] = as_row(count[...])
    count[...] = count[...] + block_count
    cnt_ref[...] = jnp.broadcast_to(count[...], cnt_ref.shape).astype(jnp.int32)


def _route_out_specs(B, S, tb):
    nj = S // tb
    tok = pl.BlockSpec((1, tb, LANES), lambda b, j: (b, j, 0))
    per_block = pl.BlockSpec((1, 1, LANES), lambda b, j: (b * nj + j, 0, 0))
    specs = [pl.BlockSpec((SUBLANES, tb), lambda b, j: (0, b * nj + j)), tok, tok, per_block, per_block,
             pl.BlockSpec((N_EXPERTS, LANES), lambda b, j: (0, 0))]
    shapes = [jax.ShapeDtypeStruct((SUBLANES, B * S), jnp.int32),
              jax.ShapeDtypeStruct((B, S, LANES), jnp.int32),
              jax.ShapeDtypeStruct((B, S, LANES), F32),
              jax.ShapeDtypeStruct((B * nj, 1, LANES), jnp.int32),
              jax.ShapeDtypeStruct((B * nj, 1, LANES), jnp.int32),
              jax.ShapeDtypeStruct((N_EXPERTS, LANES), jnp.int32)]
    return specs, shapes


def _route_fin_kernel(cnt_ref, meta_ref, base_ref, *, tm):
    cnt = cnt_ref[...]
    ntile = lax.shift_right_logical(cnt + (tm - 1), int(math.log2(tm)))
    er = lax.broadcasted_iota(jnp.int32, (N_EXPERTS, N_EXPERTS), 0)
    ec = lax.broadcasted_iota(jnp.int32, (N_EXPERTS, N_EXPERTS), 1)
    before = jnp.where(ec < er, 1.0, 0.0).astype(F32)
    start = jnp.dot(before, ntile.astype(F32), precision=HIGHEST, preferred_element_type=F32).astype(jnp.int32)
    base = start * tm
    base_ref[...] = base
    nlane = meta_ref.shape[1]
    lane = lax.broadcasted_iota(jnp.int32, (1, nlane), 1)
    end = (start + ntile)[:, 0:1]
    tile_expert = jnp.sum(jnp.where(end <= lane, 1, 0), axis=0, keepdims=True)
    tile_expert = jnp.minimum(tile_expert, N_EXPERTS - 1)
    meta_ref[...] = jnp.where(lane == nlane - 1, end[N_EXPERTS - 1:N_EXPERTS, :], tile_expert)


def _route_fin(cnt, *, tm, meta_lanes):
    full = lambda a: pl.BlockSpec(a.shape, lambda i: (0,) * a.ndim)
    outs = [jax.ShapeDtypeStruct((1, meta_lanes), jnp.int32), jax.ShapeDtypeStruct(cnt.shape, jnp.int32)]
    return pl.pallas_call(
        functools.partial(_route_fin_kernel, tm=tm),
        grid=(1,),
        in_specs=[full(cnt)],
        out_specs=[full(o) for o in outs],
        out_shape=outs,
        compiler_params=_params("arbitrary"),
        name="route_fin",
    )(cnt)


HALF = D_MODEL // 2
SLAB = HALF // LANES
HIGH_BITS = 0xFFFF0000


def _to_slabs(ref, value, is_bf16_exact=False):
    n = value.shape[0]
    if not is_bf16_exact:
        value = value.astype(BF16).astype(F32)
    bits = lax.bitcast_convert_type(value, jnp.uint32)
    packed = (bits[:, HALF:] & jnp.uint32(HIGH_BITS)) | (bits[:, :HALF] >> 16)
    for s in range(SLAB):
        ref[pl.ds(s, n, stride=SLAB), :] = packed[:, s * LANES:(s + 1) * LANES]


def _from_slabs(ref, n):
    packed = jnp.concatenate([ref[pl.ds(s, n, stride=SLAB), :] for s in range(SLAB)], axis=1)
    low = lax.bitcast_convert_type(packed << 16, F32).astype(BF16)
    high = lax.bitcast_convert_type(packed & jnp.uint32(HIGH_BITS), F32).astype(BF16)
    return low, high


def _for_expert_runs(base_sm, bc_sm, cr_sm, max_run, fn):
    def per_expert(e, local_row):
        n = bc_sm[0, 0, e]
        sorted_row = base_sm[e] + cr_sm[0, 0, e]

        @pl.when(n > 0)
        def _():
            fn(local_row, sorted_row, n)

        return local_row + n

    lax.fori_loop(0, N_EXPERTS, per_expert, 0)


def _for_pieces(n, max_piece, fn):
    done = 0
    bit = max_piece
    while bit >= 2:
        take = n & bit

        @pl.when(take != 0)
        def _(done=done, bit=bit):
            fn(done, bit)

        done = done + take
        bit //= 2


MAX_GROUP_ROWS = 512


def _local_rows(tb):
    return tb * TOP_K + N_EXPERTS


def _row_groups(tb):
    rows = _local_rows(tb)
    return next(g for g in range(1, rows) if rows % (g * 2 * SUBLANES) == 0 and rows // g <= MAX_GROUP_ROWS)


def _wait_all_runs(src, dst, sem, total_rows):
    nrows = pl.multiple_of(total_rows * SLAB, 2 * SLAB)
    pltpu.make_async_copy(src.at[pl.ds(0, nrows)], dst.at[pl.ds(0, nrows)], sem).wait()


def _queue(nrows):
    return int(math.log2(nrows)) % 2


def _slab_rows(ref, row, nrows):
    return ref.at[pl.ds(pl.multiple_of(row * SLAB, 2 * SLAB), nrows * SLAB)]


def _dispatch_kernel(cnt_sm, base_sm, bc_sm, cr_sm, bc_prev, cr_prev, ls_ref, x_ref, mod_ref, g_ref, xs_hbm,
                     stage, zeros, sem, zsem, *, tb, tm):
    step = pl.program_id(0) * pl.num_programs(1) + pl.program_id(1)
    last_step = pl.num_programs(0) * pl.num_programs(1) - 1
    cur = lax.rem(step, 2)
    mod = mod_ref[0]
    hb = _rms_mod(x_ref[0], g_ref[...], mod[4:5, :], mod[3:4, :]).astype(BF16)
    ls = ls_ref[...]
    gr = _local_rows(tb) // _row_groups(tb)
    for grp in range(_row_groups(tb)):
        row = grp * gr + lax.broadcasted_iota(jnp.int32, (gr, tb), 0)
        hit = ls[0:1, :] == row
        for k in range(1, TOP_K):
            hit = hit | (ls[k:k + 1, :] == row)
        perm = jnp.where(hit, 1.0, 0.0).astype(BF16)
        _to_slabs(stage.at[cur, pl.ds(grp * gr * SLAB, gr * SLAB)], jnp.dot(perm, hb, preferred_element_type=F32),
                  is_bf16_exact=True)

    def run_copy(buf):
        return lambda local_row, sorted_row, nrows: pltpu.make_async_copy(
            _slab_rows(stage.at[buf], local_row, nrows), _slab_rows(xs_hbm, sorted_row, nrows), sem.at[buf])

    _for_expert_runs(base_sm, bc_sm, cr_sm, tb, lambda *a: run_copy(cur)(*a).start())

    @pl.when(step == 0)
    def _():
        zeros[...] = jnp.zeros_like(zeros)

        def per_expert(e, carry):
            n = cnt_sm[e]
            npad = lax.rem(tm - lax.rem(n, tm), tm)
            zcopy = lambda done, size: pltpu.make_async_copy(
                _slab_rows(zeros, 0, size), _slab_rows(xs_hbm, base_sm[e] + n + done, size), zsem)
            _for_pieces(npad, tm // 2, lambda *a: zcopy(*a).start())
            _for_pieces(npad, tm // 2, lambda *a: zcopy(*a).wait())
            return carry

        lax.fori_loop(0, N_EXPERTS, per_expert, 0)

    @pl.when(step > 0)
    def _():
        _wait_all_runs(stage.at[1 - cur], xs_hbm, sem.at[1 - cur], bc_prev[0, 0, N_EXPERTS])

    @pl.when(step == last_step)
    def _():
        _wait_all_runs(stage.at[cur], xs_hbm, sem.at[cur], bc_sm[0, 0, N_EXPERTS])


def _dispatch(cnt, base, blk_cnt, blk_carry, ls, x, mod_l, norm_g, *, tb, tm, n_slots):
    B, S, D = x.shape
    nj = S // tb
    smem_block = lambda shift: pl.BlockSpec(
        (1, 1, LANES), lambda b, j, *_: (jnp.maximum(b * nj + j + shift, 0), 0, 0), memory_space=pltpu.SMEM)
    grid_spec = pltpu.PrefetchScalarGridSpec(
        num_scalar_prefetch=2,
        grid=(B, nj),
        in_specs=[smem_block(0), smem_block(0), smem_block(-1), smem_block(-1),
                  pl.BlockSpec((SUBLANES, tb), lambda b, j, *_: (0, b * nj + j)),
                  pl.BlockSpec((1, tb, D), lambda b, j, *_: (b, j, 0)),
                  pl.BlockSpec((1, N_MOD, D), lambda b, j, *_: (b, 0, 0)),
                  pl.BlockSpec(norm_g.shape, lambda b, j, *_: (0, 0))],
        out_specs=pl.BlockSpec(memory_space=pltpu.HBM),
        scratch_shapes=[pltpu.VMEM((2, _local_rows(tb) * SLAB, LANES), jnp.uint32),
                        pltpu.VMEM((tm // 2 * SLAB, LANES), jnp.uint32),
                        pltpu.SemaphoreType.DMA((2,)), pltpu.SemaphoreType.DMA])
    return pl.pallas_call(
        functools.partial(_dispatch_kernel, tb=tb, tm=tm),
        grid_spec=grid_spec,
        out_shape=jax.ShapeDtypeStruct((n_slots * SLAB, LANES), jnp.uint32),
        compiler_params=_params("arbitrary", "arbitrary"),
        name="dispatch",
    )(cnt, base, blk_cnt, blk_carry, blk_cnt, blk_carry, ls, x, mod_l, norm_g)


def _experts_kernel(te_sm, nt_sm, x_ref, w1_ref, w3_ref, w2_ref, y_ref, w1b, w3b, w2b, *, tm):
    i = pl.program_id(0)
    in_use = i < nt_sm[0]

    @pl.when(in_use)
    def _():
        @pl.when((i == 0) | (te_sm[i] != te_sm[jnp.maximum(i - 1, 0)]))
        def _():
            w1b[...] = w1_ref[0].astype(BF16)
            w3b[...] = w3_ref[0].astype(BF16)
            w2b[...] = w2_ref[0].astype(BF16)

        for c in range(tm // EXPERT_CHUNK):
            rows = pl.ds(c * EXPERT_CHUNK * SLAB, EXPERT_CHUNK * SLAB)
            x_lo, x_hi = _from_slabs(x_ref.at[rows], EXPERT_CHUNK)
            up = lambda w: (jnp.dot(x_lo, w[:HALF, :], preferred_element_type=F32)
                            + jnp.dot(x_hi, w[HALF:, :], preferred_element_type=F32))
            hid = _silu(up(w1b)) * up(w3b)
            _to_slabs(y_ref.at[rows], jnp.dot(hid.astype(BF16), w2b[...], preferred_element_type=F32))

    @pl.when(jnp.logical_not(in_use))
    def _():
        y_ref[...] = jnp.zeros_like(y_ref)


def _experts(tile_expert, n_tiles, xs, w1, w3, w2, layer, *, tm):
    _, E, D, F = w1.shape
    n_slots = xs.shape[0] // SLAB
    last = lambda i, te, nt: jnp.minimum(i, nt[0] - 1)
    grid_spec = pltpu.PrefetchScalarGridSpec(
        num_scalar_prefetch=2,
        grid=(n_slots // tm,),
        in_specs=[pl.BlockSpec((tm * SLAB, LANES), lambda i, te, nt: (last(i, te, nt), 0)),
                  pl.BlockSpec((None, 1, D, F), lambda i, te, nt: (layer, te[last(i, te, nt)], 0, 0)),
                  pl.BlockSpec((None, 1, D, F), lambda i, te, nt: (layer, te[last(i, te, nt)], 0, 0)),
                  pl.BlockSpec((None, 1, F, D), lambda i, te, nt: (layer, te[last(i, te, nt)], 0, 0))],
        out_specs=pl.BlockSpec((tm * SLAB, LANES), lambda i, te, nt: (i, 0)),
        scratch_shapes=[pltpu.VMEM((D, F), BF16), pltpu.VMEM((D, F), BF16), pltpu.VMEM((F, D), BF16)])
    return pl.pallas_call(
        functools.partial(_experts_kernel, tm=tm),
        grid_spec=grid_spec,
        out_shape=jax.ShapeDtypeStruct(xs.shape, jnp.uint32),
        compiler_params=_params("arbitrary"),
        name="experts",
    )(tile_expert, n_tiles, xs, w1, w3, w2)


def _combine_kernel(base_sm, bc_sm, cr_sm, bc_next, cr_next, x_ref, mod_ref, g_ref, lst_ref, wt_ref,
                    sw1_ref, sw3_ref, sw2_ref, fg_ref, ys_hbm, o_ref, stage, sem, *, tb, final_norm):
    step = pl.program_id(0) * pl.num_programs(1) + pl.program_id(1)
    last_step = pl.num_programs(0) * pl.num_programs(1) - 1
    cur = lax.rem(step, 2)

    def run_copy(buf):
        return lambda local_row, sorted_row, nrows: pltpu.make_async_copy(
            _slab_rows(ys_hbm, sorted_row, nrows), _slab_rows(stage.at[buf], local_row, nrows), sem.at[buf])

    @pl.when(step == 0)
    def _():
        stage[...] = jnp.zeros_like(stage)
        _for_expert_runs(base_sm, bc_sm, cr_sm, tb, lambda *a: run_copy(cur)(*a).start())

    @pl.when(step < last_step)
    def _():
        _for_expert_runs(base_sm, bc_next, cr_next, tb, lambda *a: run_copy(1 - cur)(*a).start())

    x = x_ref[0]
    mod = mod_ref[0]
    hb = _rms_mod(x, g_ref[...], mod[4:5, :], mod[3:4, :]).astype(BF16)
    a = jnp.dot(hb, sw1_ref[...], preferred_element_type=F32)
    b = jnp.dot(hb, sw3_ref[...], preferred_element_type=F32)
    acc = jnp.dot((_silu(a) * b).astype(BF16), sw2_ref[...], preferred_element_type=F32)
    wt = wt_ref[0]
    lst = lst_ref[0]
    _wait_all_runs(ys_hbm, stage.at[cur], sem.at[cur], bc_sm[0, 0, N_EXPERTS])
    gr = _local_rows(tb) // _row_groups(tb)
    routed = [jnp.zeros((tb, HALF), F32)] * 2
    for grp in range(_row_groups(tb)):
        halves = _from_slabs(stage.at[cur, pl.ds(grp * gr * SLAB, gr * SLAB)], gr)
        row = grp * gr + lax.broadcasted_iota(jnp.int32, (1, gr), 1)
        gate = jnp.zeros((tb, gr), F32)
        for k in range(TOP_K):
            gate = jnp.where(lst[:, k:k + 1] == row, wt[:, k:k + 1], gate)
        gate = gate.astype(BF16)
        routed = [r + jnp.dot(gate, yg, preferred_element_type=F32) for r, yg in zip(routed, halves)]
    acc = acc + jnp.concatenate(routed, axis=1)
    y = x + mod[5:6, :] * acc
    if final_norm:
        y = y * lax.rsqrt(jnp.mean(y * y, axis=-1, keepdims=True) + EPS) * fg_ref[...]
    o_ref[0] = y


def _combine(base, blk_cnt, blk_carry, x, mod_l, norm_g, lst, wt, sw1, sw3, sw2, final_g, ys, *, tb, final_norm):
    B, S, D = x.shape
    nj = S // tb
    full = lambda a: pl.BlockSpec(a.shape, lambda b, j, *_: (0,) * a.ndim)
    tok = lambda w: pl.BlockSpec((1, tb, w), lambda b, j, *_: (b, j, 0))
    smem_block = lambda shift: pl.BlockSpec(
        (1, 1, LANES), lambda b, j, *_: (jnp.minimum(b * nj + j + shift, B * nj - 1), 0, 0), memory_space=pltpu.SMEM)
    grid_spec = pltpu.PrefetchScalarGridSpec(
        num_scalar_prefetch=1,
        grid=(B, nj),
        in_specs=[smem_block(0), smem_block(0), smem_block(1), smem_block(1), tok(D),
                  pl.BlockSpec((1, N_MOD, D), lambda b, j, *_: (b, 0, 0)),
                  full(norm_g), tok(LANES), tok(LANES), full(sw1), full(sw3), full(sw2), full(final_g),
                  pl.BlockSpec(memory_space=pltpu.HBM)],
        out_specs=tok(D),
        scratch_shapes=[pltpu.VMEM((2, _local_rows(tb) * SLAB, LANES), jnp.uint32), pltpu.SemaphoreType.DMA((2,))])
    return pl.pallas_call(
        functools.partial(_combine_kernel, tb=tb, final_norm=final_norm),
        grid_spec=grid_spec,
        out_shape=jax.ShapeDtypeStruct((B, S, D), F32),
        compiler_params=_params("arbitrary", "arbitrary"),
        name="combine",
    )(base, blk_cnt, blk_carry, blk_cnt, blk_carry, x, mod_l, norm_g, lst, wt, sw1, sw3, sw2, final_g, ys)


def _block_diag(w):
    G, a, b = w.shape
    out = jnp.zeros((G * a, G * b), w.dtype)
    for g in range(G):
        out = out.at[g * a:(g + 1) * a, g * b:(g + 1) * b].set(w[g])
    return out


def kernel(x, c, w_in, b_f, pool_w, pool_scale, sg_ln_g, sg_ln_b, sg_w, sg_b, conv_w, conv_b, conv_ln_g,
           conv_ln_b, w_branch, w_out, mix_norm_g, ffn_norm_g, ada_w, ada_b, router_w, router_bias,
           exp_w1, exp_w3, exp_w2, shared_w1, shared_w3, shared_w2, final_norm_g):
    B, S, D = x.shape
    L = w_in.shape[0]
    tb = min(512, S)
    tq = tb
    tm = EXPERT_TILE
    tr = min(MOE_BLOCK, S)
    n_tiles = -(-(B * (S // tr) * _local_rows(tr)) // tm) + N_EXPERTS
    meta_lanes = -(-(n_tiles + 1) // LANES) * LANES
    row = lambda a: a.reshape(1, -1)

    mod = _modulation(c, ada_w, ada_b)
    for l in range(L):
        bfc = jnp.pad(b_f[l], (0, LANES - ATT_HEADS)).reshape(1, LANES)
        poolw = _block_diag(pool_w[l]).astype(BF16)
        sgb = jnp.repeat(sg_b[l].T, SG_HD, axis=1)

        ya, yb, yc, q, k, v = _mixer_in(
            x, mod[l], row(mix_norm_g[l]), w_in, l, bfc, poolw, row(pool_scale[l]),
            row(sg_ln_g[l]), row(sg_ln_b[l]), sg_w[l], sgb, conv_w[l], row(conv_b[l]),
            row(conv_ln_g[l]), row(conv_ln_b[l]), tb=tb)
        yd = _attention(q, k, v, tq=tq)
        fnorm = row(ffn_norm_g[l])
        x, ls, lst, wt, blk_cnt, blk_carry, cnt = _merge(
            x, mod[l], row(mix_norm_g[l]), w_in, l, w_branch[l].astype(BF16), w_out[l].astype(BF16),
            ya, yb, yc, yd, fnorm, router_w[l].T, router_bias[l].reshape(-1, 1), tb=tr)
        meta, base = _route_fin(cnt, tm=tm, meta_lanes=meta_lanes)
        xs = _dispatch(cnt[:, 0], base[:, 0], blk_cnt, blk_carry, ls, x, mod[l], fnorm,
                       tb=tr, tm=tm, n_slots=n_tiles * tm)
        ys = _experts(meta[0, :n_tiles], meta[0, meta_lanes - 1:], xs, exp_w1, exp_w3, exp_w2, l, tm=tm)
        x = _combine(base[:, 0], blk_cnt, blk_carry, x, mod[l], fnorm, lst, wt, shared_w1[l].astype(BF16),
                     shared_w3[l].astype(BF16), shared_w2[l].astype(BF16), row(final_norm_g), ys,
                     tb=tr, final_norm=(l == L - 1))
    return x
```

```python
import functools
import math

import jax
import jax.numpy as jnp
from jax import lax
from jax.experimental import pallas as pl
from jax.experimental.pallas import tpu as pltpu

F32 = jnp.float32
BF16 = jnp.bfloat16
HIGHEST = lax.Precision.HIGHEST

D_MODEL = 1024
BRANCH_W = 256
N_BRANCH = 4
POOL_WINDOWS = (2, 4, 8, 16)
POOL_GW = 64
POOL_TAIL = 16
SG_HEADS = 4
SG_CHUNK = 128
SG_HD = 64
CONV_WIDTH = 31
CONV_TAIL = 32
ATT_HEADS = 4
ATT_HD = 64
N_EXPERTS = 64
TOP_K = 6
N_GROUPS = 8
GROUP_SIZE = N_EXPERTS // N_GROUPS
TOPK_GROUPS = 4
EXPERT_FF = 256
SHARED_FF = 256
ROUTE_SCALE = 2.5
EPS = 1e-6
N_MOD = 6
MIX_COLS = 8 * BRANCH_W
LANES = 128
SUBLANES = 8
LOG2E = 1.4426950408889634
EXPERT_TILE = 512
EXPERT_CHUNK = 256
MOE_BLOCK = 512
VMEM_LIMIT = 56 * 1024 * 1024


def _sigmoid(x):
    return 1.0 / (1.0 + jnp.exp(-x))


def _silu(x):
    return x * _sigmoid(x)


def _rms_mod(x, g, scale, shift):
    y = x * lax.rsqrt(jnp.mean(x * x, axis=-1, keepdims=True) + EPS)
    return (y * g) * (1.0 + scale) + shift


def _layer_norm(x, g, b):
    mu = jnp.mean(x, axis=-1, keepdims=True)
    xc = x - mu
    var = jnp.mean(xc * xc, axis=-1, keepdims=True)
    return xc * lax.rsqrt(var + EPS) * g + b


def _params(*sem):
    return pltpu.CompilerParams(dimension_semantics=sem, vmem_limit_bytes=VMEM_LIMIT)


def _mod_kernel(c_ref, w_ref, b_ref, o_ref):
    c = c_ref[...]
    o_ref[0] = jnp.dot(_silu(c), w_ref[0], precision=HIGHEST, preferred_element_type=F32) + b_ref[0]


def _modulation(c, ada_w, ada_b):
    L, D, N = ada_w.shape
    B = c.shape[0]
    tn = 1536
    out = pl.pallas_call(
        _mod_kernel,
        grid=(L, N // tn),
        in_specs=[pl.BlockSpec((B, D), lambda l, j: (0, 0)),
                  pl.BlockSpec((1, D, tn), lambda l, j: (l, 0, j)),
                  pl.BlockSpec((1, 1, tn), lambda l, j: (l, 0, j))],
        out_specs=pl.BlockSpec((1, B, tn), lambda l, j: (l, 0, j)),
        out_shape=jax.ShapeDtypeStruct((L, B, N), F32),
        compiler_params=_params("arbitrary", "arbitrary"),
        name="modulation",
    )(c, ada_w, ada_b.reshape(L, 1, N))
    return out.reshape(L, B, N_MOD, D)


def _mixer_in_kernel(x_ref, mod_ref, g_ref, wmix_f32, wfc_f32, bfc_ref,
                     poolw_ref, pools_ref, sglg_ref, sglb_ref, sgw_ref, sgb_ref,
                     cw_ref, cb_ref, clg_ref, clb_ref,
                     ya_ref, yb_ref, yc_ref, q_ref, k_ref, v_ref,
                     pool_ext, conv_ext, cum_c, wmix_ref, wfc_ref, *, tb):
    j = pl.program_id(1)

    @pl.when((pl.program_id(0) == 0) & (j == 0))
    def _():
        wmix_ref[...] = wmix_f32[0].astype(BF16)
        wfc_ref[...] = wfc_f32[0].astype(BF16)

    @pl.when(j == 0)
    def _():
        pool_ext[0:POOL_TAIL, :] = jnp.zeros((POOL_TAIL, BRANCH_W), F32)
        conv_ext[0:CONV_TAIL, :] = jnp.zeros((CONV_TAIL, BRANCH_W), F32)
        cum_c[...] = jnp.zeros_like(cum_c)

    mod = mod_ref[0]
    h = _rms_mod(x_ref[0], g_ref[...], mod[1:2, :], mod[0:1, :])
    hb = h.astype(BF16)
    proj = jnp.dot(hb, wmix_ref[...], preferred_element_type=F32)

    lane = lax.broadcasted_iota(jnp.int32, (1, BRANCH_W), 1)
    row = lax.broadcasted_iota(jnp.int32, (tb, 1), 0)

    u = proj[:, 0:BRANCH_W]
    pool_ext[POOL_TAIL:POOL_TAIL + tb, :] = u
    ext = pool_ext[...]
    s2 = ext + pltpu.roll(ext, 1, 0)
    s4 = s2 + pltpu.roll(s2, 2, 0)
    s8 = s4 + pltpu.roll(s4, 4, 0)
    s16 = s8 + pltpu.roll(s8, 8, 0)
    grp = lane // POOL_GW
    wsum = jnp.where(grp == 0, s2, jnp.where(grp == 1, s4, jnp.where(grp == 2, s8, s16)))[POOL_TAIL:, :]
    win = jnp.where(grp == 0, 2.0, jnp.where(grp == 1, 4.0, jnp.where(grp == 2, 8.0, 16.0)))
    count = jnp.minimum((j * tb + row + 1).astype(F32), win)
    pooled = wsum / count - u
    ya = jnp.dot(pooled.astype(BF16), poolw_ref[...], preferred_element_type=F32) * pools_ref[...]
    ya_ref[0] = ya.astype(BF16)
    pool_ext[0:POOL_TAIL, :] = u[tb - POOL_TAIL:, :]

    z = proj[:, BRANCH_W:3 * BRANCH_W]
    z = 0.5 * z * (1.0 + jnp.tanh(math.sqrt(2.0 / math.pi) * (z + 0.044715 * (z * z * z))))
    su = z[:, 0:BRANCH_W]
    sv = _layer_norm(z[:, BRANCH_W:], sglg_ref[...], sglb_ref[...])
    r128 = lax.broadcasted_iota(jnp.int32, (SG_CHUNK, SG_CHUNK), 0)
    c128 = lax.broadcasted_iota(jnp.int32, (SG_CHUNK, SG_CHUNK), 1)
    wcat = jnp.concatenate(
        [jnp.where(r128 >= c128, sgw_ref[hh], 0.0) for hh in range(SG_HEADS)], axis=1).astype(BF16)
    head = lane // SG_HD
    for ci in range(tb // SG_CHUNK):
        rows = slice(ci * SG_CHUNK, (ci + 1) * SG_CHUNK)
        vch = sv[rows, :]
        vstack = jnp.concatenate(
            [jnp.where(head == hh, vch, 0.0) for hh in range(SG_HEADS)], axis=0).astype(BF16)
        s = jnp.dot(wcat, vstack, preferred_element_type=F32) + sgb_ref[...]
        yb_ref[0, rows, :] = (su[rows, :] * s).astype(BF16)

    glu = proj[:, 3 * BRANCH_W:4 * BRANCH_W] * _sigmoid(proj[:, 4 * BRANCH_W:5 * BRANCH_W])
    conv_ext[CONV_TAIL:CONV_TAIL + tb, :] = glu
    acc = jnp.zeros((tb, BRANCH_W), F32) + cb_ref[...]
    for kk in range(CONV_WIDTH):
        off = CONV_TAIL - (CONV_WIDTH - 1) + kk
        acc = acc + cw_ref[kk:kk + 1, :] * conv_ext[off:off + tb, :]
    yc_ref[0] = _silu(_layer_norm(acc, clg_ref[...], clb_ref[...])).astype(BF16)
    conv_ext[0:CONV_TAIL, :] = glu[tb - CONV_TAIL:, :]

    def log_sigmoid(t):
        return jnp.minimum(t, 0.0) - jnp.log(1.0 + jnp.exp(-jnp.abs(t)))

    rr = lax.broadcasted_iota(jnp.int32, (tb, tb), 0)
    cc = lax.broadcasted_iota(jnp.int32, (tb, tb), 1)
    lower = jnp.where(rr >= cc, 1.0, 0.0).astype(BF16)
    lf_c = log_sigmoid(jnp.dot(hb, wfc_ref[...], preferred_element_type=F32) + bfc_ref[...])
    lf_hi = lf_c.astype(BF16)
    lf_mid = (lf_c - lf_hi.astype(F32)).astype(BF16)
    lf_lo = ((lf_c - lf_hi.astype(F32)) - lf_mid.astype(F32)).astype(BF16)
    cs_c = cum_c[...] + sum(jnp.dot(lower, piece, preferred_element_type=F32) for piece in (lf_lo, lf_mid, lf_hi))
    cum_c[...] = cs_c[tb - 1:tb, :]
    cl2 = cs_c * LOG2E
    ln = lax.broadcasted_iota(jnp.int32, (1, LANES), 1)
    ones_q = jnp.where((ln >= ATT_HD + 3) & (ln < ATT_HD + 6), 1.0, 0.0)
    ones_k = jnp.where((ln >= ATT_HD) & (ln < ATT_HD + 3), 1.0, 0.0)
    ones_v = jnp.where(ln == ATT_HD, 1.0, 0.0)
    for hh in range(ATT_HEADS):
        pair = (hh // 2) * LANES
        qs = proj[:, 5 * BRANCH_W + pair:5 * BRANCH_W + pair + LANES] * (LOG2E / math.sqrt(ATT_HD))
        ks = proj[:, 6 * BRANCH_W + pair:6 * BRANCH_W + pair + LANES]
        vs = proj[:, 7 * BRANCH_W + pair:7 * BRANCH_W + pair + LANES]
        if hh % 2:
            qs, ks, vs = (pltpu.roll(a, ATT_HD, 1) for a in (qs, ks, vs))
        c = cl2[:, hh:hh + 1]
        hi = c.astype(BF16).astype(F32)
        mid = (c - hi).astype(BF16).astype(F32)
        lo = (c - hi) - mid
        q_extra = jnp.where(ln == ATT_HD, hi, jnp.where(ln == ATT_HD + 1, mid, jnp.where(ln == ATT_HD + 2, lo, ones_q)))
        k_extra = jnp.where(ln == ATT_HD + 3, -hi,
                            jnp.where(ln == ATT_HD + 4, -mid, jnp.where(ln == ATT_HD + 5, -lo, ones_k)))
        q_ref[0, hh, 0] = jnp.where(ln < ATT_HD, qs, q_extra).T.astype(BF16)
        k_ref[0, hh] = jnp.where(ln < ATT_HD, ks, k_extra).astype(BF16)
        v_ref[0, hh, 0] = jnp.where(ln < ATT_HD, vs, ones_v).T.astype(BF16)


def _w_in_cols(layer, first_col, ncols):
    return pl.BlockSpec((1, D_MODEL, ncols), lambda b, j: (layer, 0, first_col // ncols), pipeline_mode=pl.Buffered(1))


def _mixer_in(x, mod_l, norm_g, w_in, layer, bfc, poolw, pools, sglg, sglb, sgw, sgb, cw, cb, clg, clb, *, tb):
    B, S, D = x.shape
    full = lambda a: pl.BlockSpec(a.shape, lambda b, j: (0,) * a.ndim)
    tok = lambda w: pl.BlockSpec((1, tb, w), lambda b, j: (b, j, 0))
    head = pl.BlockSpec((1, ATT_HEADS, tb, LANES), lambda b, j: (b, 0, j, 0))
    head_t = pl.BlockSpec((1, ATT_HEADS, 1, LANES, tb), lambda b, j: (b, 0, j, 0, 0))
    consts = (bfc, poolw, pools, sglg, sglb, sgw, sgb, cw, cb, clg, clb)
    act = jax.ShapeDtypeStruct((B, S, BRANCH_W), BF16)
    att = jax.ShapeDtypeStruct((B, ATT_HEADS, S, LANES), BF16)
    att_t = jax.ShapeDtypeStruct((B, ATT_HEADS, S // tb, LANES, tb), BF16)
    return pl.pallas_call(
        functools.partial(_mixer_in_kernel, tb=tb),
        grid=(B, S // tb),
        in_specs=[tok(D), pl.BlockSpec((1, N_MOD, D), lambda b, j: (b, 0, 0)), full(norm_g),
                  _w_in_cols(layer, 0, MIX_COLS), _w_in_cols(layer, MIX_COLS, LANES)] + [full(a) for a in consts],
        out_specs=[tok(BRANCH_W)] * 3 + [head_t, head, head_t],
        out_shape=[act] * 3 + [att_t, att, att_t],
        scratch_shapes=[pltpu.VMEM((POOL_TAIL + tb, BRANCH_W), F32),
                        pltpu.VMEM((CONV_TAIL + tb, BRANCH_W), F32),
                        pltpu.VMEM((1, LANES), F32),
                        pltpu.VMEM((D, MIX_COLS), BF16), pltpu.VMEM((D, LANES), BF16)],
        compiler_params=_params("arbitrary", "arbitrary"),
        name="mixer_in",
    )(x, mod_l, norm_g, w_in, w_in, *consts)


def _attn_kernel(q_ref, k_ref, v_ref, o_ref, m_ref, acc_ref, *, tq):
    i = pl.program_id(1)
    key = lax.broadcasted_iota(jnp.int32, (tq, tq), 0)
    qry = lax.broadcasted_iota(jnp.int32, (tq, tq), 1)
    m_ref[...] = jnp.full(m_ref.shape, -jnp.inf, F32)
    acc_ref[...] = jnp.zeros(acc_ref.shape, F32)

    def block(kj, diagonal):
        ks = pl.multiple_of(kj * tq, tq)
        logits = [jnp.dot(k_ref[0, hh, pl.ds(ks, tq), :], q_ref[0, hh, 0], preferred_element_type=F32)
                  for hh in range(ATT_HEADS)]
        for hh in range(ATT_HEADS):
            s = logits[hh]
            if diagonal:
                s = jnp.where(key <= qry, s, -jnp.inf)
            m_old = m_ref[hh]
            m_new = jnp.maximum(m_old, jnp.max(s, axis=0, keepdims=True))
            p = jnp.exp2(s - m_new)
            pv = jnp.dot(v_ref[0, hh, kj], p.astype(BF16), preferred_element_type=F32)
            acc_ref[hh] = jnp.exp2(m_old - m_new) * acc_ref[hh] + pv
            m_ref[hh] = m_new

    def body(kj, carry):
        block(kj, False)
        return carry

    lax.fori_loop(0, i, body, 0)
    block(i, True)

    ln = lax.broadcasted_iota(jnp.int32, (1, LANES), 1)
    for pr in range(ATT_HEADS // 2):
        o = []
        for hh in (2 * pr, 2 * pr + 1):
            acc = acc_ref[hh]
            o.append((acc / acc[ATT_HD:ATT_HD + 1, :]).T)
        o_ref[0, :, pr * LANES:(pr + 1) * LANES] = jnp.where(ln < ATT_HD, o[0], pltpu.roll(o[1], ATT_HD, 1)).astype(BF16)


def _attention(q_t, k, v_t, *, tq):
    B, H, S, W = k.shape
    nblk = S // tq
    assert q_t.shape == (B, H, nblk, W, tq) and v_t.shape == q_t.shape
    return pl.pallas_call(
        functools.partial(_attn_kernel, tq=tq),
        grid=(B, nblk),
        in_specs=[pl.BlockSpec((1, H, 1, W, tq), lambda b, i: (b, 0, i, 0, 0)),
                  pl.BlockSpec((1, H, S, W), lambda b, i: (b, 0, 0, 0), pipeline_mode=pl.Buffered(1)),
                  pl.BlockSpec((1, H, nblk, W, tq), lambda b, i: (b, 0, 0, 0, 0), pipeline_mode=pl.Buffered(1))],
        out_specs=pl.BlockSpec((1, tq, BRANCH_W), lambda b, i: (b, i, 0)),
        out_shape=jax.ShapeDtypeStruct((B, S, BRANCH_W), BF16),
        scratch_shapes=[pltpu.VMEM((H, 1, tq), F32), pltpu.VMEM((H, W, tq), F32)],
        compiler_params=_params("arbitrary", "arbitrary"),
        name="attention",
    )(q_t, k, v_t)


def _merge_kernel(x_ref, mod_ref, g_ref, wa_f32, wb_f32, wt_f32, wbr_ref, wout_ref, ya_ref, yb_ref, yc_ref, yd_ref,
                  fg_ref, rwt_ref, rb_ref,
                  o_ref, ls_ref, lst_ref, wt_ref, bc_ref, cr_ref, cnt_ref, wg_ref, count):
    @pl.when((pl.program_id(0) == 0) & (pl.program_id(1) == 0))
    def _():
        shift, wide = GATE_SHIFT, D_MODEL + LANES
        windows = (wa_f32[0, :, 0:wide],
                   jnp.concatenate([wa_f32[0, :, D_MODEL:], wb_f32[0, :, 0:LANES]], axis=1),
                   wb_f32[0, :, 0:wide],
                   jnp.concatenate([wb_f32[0, :, D_MODEL:], wt_f32[0]], axis=1))
        for n, win in enumerate(windows):
            wg_ref[:, n * D_MODEL:(n + 1) * D_MODEL] = pltpu.roll(win, wide - shift, 1)[:, :D_MODEL].astype(BF16)

    x = x_ref[0]
    mod = mod_ref[0]
    hb = _rms_mod(x, g_ref[...], mod[1:2, :], mod[0:1, :]).astype(BF16)
    merged = jnp.zeros(x.shape, F32)
    for n, y_ref in enumerate((ya_ref, yb_ref, yc_ref, yd_ref)):
        gate = _sigmoid(jnp.dot(hb, wg_ref[:, n * D_MODEL:(n + 1) * D_MODEL], preferred_element_type=F32))
        merged = merged + gate * jnp.dot(y_ref[0], wbr_ref[n], preferred_element_type=F32)
    out = jnp.dot(merged.astype(BF16), wout_ref[...], preferred_element_type=F32)
    x_new = x + mod[2:3, :] * out
    o_ref[0] = x_new
    _route_block(x_new, mod, fg_ref, rwt_ref, rb_ref, ls_ref, lst_ref, wt_ref, bc_ref, cr_ref, cnt_ref, count)


GATE_SHIFT = ATT_HEADS


def _merge(x, mod_l, norm_g, w_in, layer, wbranch, wout, ya, yb, yc, yd, ffn_norm_g, router_wt, router_b, *, tb):
    B, S, D = x.shape
    full = lambda a: pl.BlockSpec(a.shape, lambda b, j: (0,) * a.ndim)
    tok = lambda w: pl.BlockSpec((1, tb, w), lambda b, j: (b, j, 0))
    assert w_in.shape[2] == MIX_COLS + GATE_SHIFT + N_BRANCH * D
    route_specs, route_shapes = _route_out_specs(B, S, tb)
    return pl.pallas_call(
        _merge_kernel,
        grid=(B, S // tb),
        in_specs=[tok(D), pl.BlockSpec((1, N_MOD, D), lambda b, j: (b, 0, 0)), full(norm_g),
                  _w_in_cols(layer, MIX_COLS, 2 * D), _w_in_cols(layer, MIX_COLS + 2 * D, 2 * D),
                  _w_in_cols(layer, MIX_COLS + 4 * D, LANES),
                  full(wbranch), full(wout)] + [tok(BRANCH_W)] * 4
                 + [full(ffn_norm_g), full(router_wt), full(router_b)],
        out_specs=[tok(D)] + route_specs,
        out_shape=[jax.ShapeDtypeStruct((B, S, D), F32)] + route_shapes,
        scratch_shapes=[pltpu.VMEM((D, N_BRANCH * D), BF16), pltpu.VMEM((N_EXPERTS, 1), F32)],
        compiler_params=_params("arbitrary", "arbitrary"),
        name="merge",
    )(x, mod_l, norm_g, w_in, w_in, w_in, wbranch, wout, ya, yb, yc, yd, ffn_norm_g, router_wt, router_b)


def _route(scores_t, bias_t):
    E, n = scores_t.shape
    sel = scores_t + bias_t
    eidx = lax.broadcasted_iota(jnp.int32, (E, n), 0)
    neg = jnp.full((E, n), -jnp.inf, F32)
    gscore = []
    sub = lax.broadcasted_iota(jnp.int32, (GROUP_SIZE, n), 0)
    for g in range(N_GROUPS):
        blk = sel[g * GROUP_SIZE:(g + 1) * GROUP_SIZE, :]
        m1 = jnp.max(blk, axis=0, keepdims=True)
        first = jnp.min(jnp.where(blk == m1, sub, GROUP_SIZE), axis=0, keepdims=True)
        m2 = jnp.max(jnp.where(sub == first, -jnp.inf, blk), axis=0, keepdims=True)
        gscore.append(m1 + m2)
    emask = []
    for g in range(N_GROUPS):
        beaten = jnp.zeros((1, n), jnp.int32)
        for g2 in range(N_GROUPS):
            if g2 == g:
                continue
            wins = (gscore[g2] > gscore[g]) | ((gscore[g2] == gscore[g]) & (g2 < g))
            beaten = beaten + wins.astype(jnp.int32)
        emask.append(jnp.broadcast_to(beaten < TOPK_GROUPS, (GROUP_SIZE, n)))
    cur = jnp.where(jnp.concatenate(emask, axis=0), sel, neg)
    chosen = jnp.zeros((E, n), jnp.bool_)
    firsts = []
    for _ in range(TOP_K):
        m = jnp.max(cur, axis=0, keepdims=True)
        first = jnp.min(jnp.where(cur == m, eidx, E), axis=0, keepdims=True)
        hit = eidx == first
        chosen = chosen | hit
        cur = jnp.where(hit, neg, cur)
        firsts.append(first)
    w = jnp.where(chosen, scores_t, 0.0)
    return w / jnp.sum(w, axis=0, keepdims=True) * ROUTE_SCALE, chosen, firsts


def _route_block(x, mod, g_ref, rwt_ref, rb_ref, ls_ref, lst_ref, wt_ref, bc_ref, cr_ref, cnt_ref, count):
    first_step = (pl.program_id(0) == 0) & (pl.program_id(1) == 0)

    @pl.when(first_step)
    def _():
        count[...] = jnp.zeros_like(count)

    h = _rms_mod(x, g_ref[...], mod[4:5, :], mod[3:4, :])
    def split(a):
        hi = a.astype(BF16)
        return hi, (a - hi.astype(F32)).astype(BF16)

    nt_dot = lambda a, b: lax.dot_general(a, b, (((1,), (1,)), ((), ())), preferred_element_type=F32)
    (w_hi, w_lo), (h_hi, h_lo) = split(rwt_ref[...]), split(h)
    logits_t = nt_dot(w_hi, h_lo) + nt_dot(w_lo, h_hi) + nt_dot(w_hi, h_hi)
    gates_t, chosen, firsts = _route(_sigmoid(logits_t), rb_ref[...])
    E, tb = gates_t.shape
    ones = jnp.where(chosen, 1.0, 0.0)
    rr = lax.broadcasted_iota(jnp.int32, (tb, tb), 0)
    cc = lax.broadcasted_iota(jnp.int32, (tb, tb), 1)
    upper = jnp.where(rr <= cc, 1.0, 0.0).astype(BF16)
    incl = jnp.dot(ones.astype(BF16), upper, preferred_element_type=F32)
    block_count = jnp.sum(ones, axis=1, keepdims=True)
    block_count = block_count + (block_count.astype(jnp.int32) & 1).astype(F32)
    er = lax.broadcasted_iota(jnp.int32, (E, E), 0)
    ec = lax.broadcasted_iota(jnp.int32, (E, E), 1)
    before = jnp.where(ec < er, 1.0, 0.0).astype(F32)
    local_base = jnp.dot(before, jnp.broadcast_to(block_count, (E, LANES)), precision=HIGHEST,
                         preferred_element_type=F32)[:, 0:1]
    row_all = local_base + (incl - ones)
    eidx = lax.broadcasted_iota(jnp.int32, (E, tb), 0)
    ls_rows = [jnp.sum(jnp.where(eidx == f, row_all, 0.0), axis=0, keepdims=True) for f in firsts]
    w_rows = [jnp.sum(jnp.where(eidx == f, gates_t, 0.0), axis=0, keepdims=True) for f in firsts]
    ls_ref[...] = jnp.concatenate(ls_rows + [jnp.zeros((SUBLANES - TOP_K, tb), F32)], axis=0).astype(jnp.int32)
    zpad = jnp.zeros((LANES - TOP_K, tb), F32)
    lst_ref[0] = jnp.concatenate(ls_rows + [zpad], axis=0).T.astype(jnp.int32)
    wt_ref[0] = jnp.concatenate(w_rows + [zpad], axis=0).T

    def as_row(col):
        sel = lax.broadcasted_iota(jnp.int32, (E, LANES), 0) == lax.broadcasted_iota(jnp.int32, (E, LANES), 1)
        return jnp.sum(jnp.where(sel, col, 0.0), axis=0, keepdims=True).astype(jnp.int32)

    total_lane = lax.broadcasted_iota(jnp.int32, (1, LANES), 1) == N_EXPERTS
    bc_ref[0] = as_row(block_count) + jnp.where(total_lane, jnp.sum(block_count).astype(jnp.int32), 0)
    cr_ref[0] = as_row(count[...])
    count[...] = count[...] + block_count
    cnt_ref[...] = jnp.broadcast_to(count[...], cnt_ref.shape).astype(jnp.int32)


def _route_out_specs(B, S, tb):
    nj = S // tb
    tok = pl.BlockSpec((1, tb, LANES), lambda b, j: (b, j, 0))
    per_block = pl.BlockSpec((1, 1, LANES), lambda b, j: (b * nj + j, 0, 0))
    specs = [pl.BlockSpec((SUBLANES, tb), lambda b, j: (0, b * nj + j)), tok, tok, per_block, per_block,
             pl.BlockSpec((N_EXPERTS, LANES), lambda b, j: (0, 0))]
    shapes = [jax.ShapeDtypeStruct((SUBLANES, B * S), jnp.int32),
              jax.ShapeDtypeStruct((B, S, LANES), jnp.int32),
              jax.ShapeDtypeStruct((B, S, LANES), F32),
              jax.ShapeDtypeStruct((B * nj, 1, LANES), jnp.int32),
              jax.ShapeDtypeStruct((B * nj, 1, LANES), jnp.int32),
              jax.ShapeDtypeStruct((N_EXPERTS, LANES), jnp.int32)]
    return specs, shapes


def _route_fin_kernel(cnt_ref, meta_ref, base_ref, *, tm):
    cnt = cnt_ref[...]
    ntile = lax.shift_right_logical(cnt + (tm - 1), int(math.log2(tm)))
    er = lax.broadcasted_iota(jnp.int32, (N_EXPERTS, N_EXPERTS), 0)
    ec = lax.broadcasted_iota(jnp.int32, (N_EXPERTS, N_EXPERTS), 1)
    before = jnp.where(ec < er, 1.0, 0.0).astype(F32)
    start = jnp.dot(before, ntile.astype(F32), precision=HIGHEST, preferred_element_type=F32).astype(jnp.int32)
    base = start * tm
    base_ref[...] = base
    nlane = meta_ref.shape[1]
    lane = lax.broadcasted_iota(jnp.int32, (1, nlane), 1)
    end = (start + ntile)[:, 0:1]
    tile_expert = jnp.sum(jnp.where(end <= lane, 1, 0), axis=0, keepdims=True)
    tile_expert = jnp.minimum(tile_expert, N_EXPERTS - 1)
    meta_ref[...] = jnp.where(lane == nlane - 1, end[N_EXPERTS - 1:N_EXPERTS, :], tile_expert)


def _route_fin(cnt, *, tm, meta_lanes):
    full = lambda a: pl.BlockSpec(a.shape, lambda i: (0,) * a.ndim)
    outs = [jax.ShapeDtypeStruct((1, meta_lanes), jnp.int32), jax.ShapeDtypeStruct(cnt.shape, jnp.int32)]
    return pl.pallas_call(
        functools.partial(_route_fin_kernel, tm=tm),
        grid=(1,),
        in_specs=[full(cnt)],
        out_specs=[full(o) for o in outs],
        out_shape=outs,
        compiler_params=_params("arbitrary"),
        name="route_fin",
    )(cnt)


SLAB = D_MODEL // LANES


def _to_slabs(ref, value):
    n = value.shape[0]
    for s in range(SLAB):
        ref[pl.ds(s, n, stride=SLAB), :] = value[:, s * LANES:(s + 1) * LANES]


def _from_slabs(ref, n):
    return jnp.concatenate([ref[pl.ds(s, n, stride=SLAB), :] for s in range(SLAB)], axis=1)


def _for_expert_runs(base_sm, bc_sm, cr_sm, fn):
    def per_expert(e, local_row):
        n = bc_sm[0, 0, e]
        sorted_row = base_sm[e] + cr_sm[0, 0, e]

        @pl.when(n > 0)
        def _():
            fn(local_row, sorted_row, n)

        return local_row + n

    lax.fori_loop(0, N_EXPERTS, per_expert, 0)


def _for_pieces(n, max_piece, fn):
    done = 0
    bit = max_piece
    while bit >= 2:
        take = n & bit

        @pl.when(take != 0)
        def _(done=done, bit=bit):
            fn(done, bit)

        done = done + take
        bit //= 2


MAX_GROUP_ROWS = 512


def _local_rows(tb):
    return tb * TOP_K + N_EXPERTS


def _row_groups(tb):
    rows = _local_rows(tb)
    return next(g for g in range(1, rows) if rows % (g * 2 * SUBLANES) == 0 and rows // g <= MAX_GROUP_ROWS)


def _wait_all_runs(src, dst, sem, total_rows):
    nrows = pl.multiple_of(total_rows * SLAB, 2 * SLAB)
    pltpu.make_async_copy(src.at[pl.ds(0, nrows)], dst.at[pl.ds(0, nrows)], sem).wait()


def _slab_rows(ref, row, nrows):
    return ref.at[pl.ds(pl.multiple_of(row * SLAB, 2 * SLAB), nrows * SLAB)]


def _dispatch_kernel(cnt_sm, base_sm, bc_sm, cr_sm, bc_prev, cr_prev, ls_ref, x_ref, mod_ref, g_ref, xs_hbm,
                     stage, zeros, sem, zsem, *, tb, tm):
    step = pl.program_id(0) * pl.num_programs(1) + pl.program_id(1)
    last_step = pl.num_programs(0) * pl.num_programs(1) - 1
    cur = lax.rem(step, 2)
    mod = mod_ref[0]
    hb = _rms_mod(x_ref[0], g_ref[...], mod[4:5, :], mod[3:4, :]).astype(BF16)
    ls = ls_ref[...]
    gr = _local_rows(tb) // _row_groups(tb)
    for grp in range(_row_groups(tb)):
        row = grp * gr + lax.broadcasted_iota(jnp.int32, (gr, tb), 0)
        hit = ls[0:1, :] == row
        for k in range(1, TOP_K):
            hit = hit | (ls[k:k + 1, :] == row)
        perm = jnp.where(hit, 1.0, 0.0).astype(BF16)
        _to_slabs(stage.at[cur, pl.ds(grp * gr * SLAB, gr * SLAB)], jnp.dot(perm, hb, preferred_element_type=F32))

    def run_copy(buf):
        return lambda local_row, sorted_row, nrows: pltpu.make_async_copy(
            _slab_rows(stage.at[buf], local_row, nrows), _slab_rows(xs_hbm, sorted_row, nrows), sem.at[buf])

    _for_expert_runs(base_sm, bc_sm, cr_sm, lambda *a: run_copy(cur)(*a).start())

    @pl.when(step == 0)
    def _():
        zeros[...] = jnp.zeros_like(zeros)

        def per_expert(e, carry):
            n = cnt_sm[e]
            npad = lax.rem(tm - lax.rem(n, tm), tm)
            zcopy = lambda done, size: pltpu.make_async_copy(
                _slab_rows(zeros, 0, size), _slab_rows(xs_hbm, base_sm[e] + n + done, size), zsem)
            _for_pieces(npad, tm // 2, lambda *a: zcopy(*a).start())
            _for_pieces(npad, tm // 2, lambda *a: zcopy(*a).wait())
            return carry

        lax.fori_loop(0, N_EXPERTS, per_expert, 0)

    @pl.when(step > 0)
    def _():
        _wait_all_runs(stage.at[1 - cur], xs_hbm, sem.at[1 - cur], bc_prev[0, 0, N_EXPERTS])

    @pl.when(step == last_step)
    def _():
        _wait_all_runs(stage.at[cur], xs_hbm, sem.at[cur], bc_sm[0, 0, N_EXPERTS])


def _dispatch(cnt, base, blk_cnt, blk_carry, ls, x, mod_l, norm_g, *, tb, tm, n_slots):
    B, S, D = x.shape
    nj = S // tb
    smem_block = lambda shift: pl.BlockSpec(
        (1, 1, LANES), lambda b, j, *_: (jnp.maximum(b * nj + j + shift, 0), 0, 0), memory_space=pltpu.SMEM)
    grid_spec = pltpu.PrefetchScalarGridSpec(
        num_scalar_prefetch=2,
        grid=(B, nj),
        in_specs=[smem_block(0), smem_block(0), smem_block(-1), smem_block(-1),
                  pl.BlockSpec((SUBLANES, tb), lambda b, j, *_: (0, b * nj + j)),
                  pl.BlockSpec((1, tb, D), lambda b, j, *_: (b, j, 0)),
                  pl.BlockSpec((1, N_MOD, D), lambda b, j, *_: (b, 0, 0)),
                  pl.BlockSpec(norm_g.shape, lambda b, j, *_: (0, 0))],
        out_specs=pl.BlockSpec(memory_space=pltpu.HBM),
        scratch_shapes=[pltpu.VMEM((2, _local_rows(tb) * SLAB, LANES), F32),
                        pltpu.VMEM((tm // 2 * SLAB, LANES), F32),
                        pltpu.SemaphoreType.DMA((2,)), pltpu.SemaphoreType.DMA])
    return pl.pallas_call(
        functools.partial(_dispatch_kernel, tb=tb, tm=tm),
        grid_spec=grid_spec,
        out_shape=jax.ShapeDtypeStruct((n_slots * SLAB, LANES), F32),
        compiler_params=_params("arbitrary", "arbitrary"),
        name="dispatch",
    )(cnt, base, blk_cnt, blk_carry, blk_cnt, blk_carry, ls, x, mod_l, norm_g)


def _experts_kernel(te_sm, nt_sm, x_ref, w1_ref, w3_ref, w2_ref, y_ref, w1b, w3b, w2b, *, tm):
    i = pl.program_id(0)
    in_use = i < nt_sm[0]

    @pl.when(in_use)
    def _():
        @pl.when((i == 0) | (te_sm[i] != te_sm[jnp.maximum(i - 1, 0)]))
        def _():
            w1b[...] = w1_ref[0].astype(BF16)
            w3b[...] = w3_ref[0].astype(BF16)
            w2b[...] = w2_ref[0].astype(BF16)

        for c in range(tm // EXPERT_CHUNK):
            rows = pl.ds(c * EXPERT_CHUNK * SLAB, EXPERT_CHUNK * SLAB)
            xb = _from_slabs(x_ref.at[rows], EXPERT_CHUNK).astype(BF16)
            up = lambda w: jnp.dot(xb, w[...], preferred_element_type=F32)
            hid = _silu(up(w1b)) * up(w3b)
            _to_slabs(y_ref.at[rows], jnp.dot(hid.astype(BF16), w2b[...], preferred_element_type=F32))

    @pl.when(jnp.logical_not(in_use))
    def _():
        y_ref[...] = jnp.zeros_like(y_ref)


def _experts(tile_expert, n_tiles, xs, w1, w3, w2, layer, *, tm):
    _, E, D, F = w1.shape
    n_slots = xs.shape[0] // SLAB
    last = lambda i, te, nt: jnp.minimum(i, nt[0] - 1)
    grid_spec = pltpu.PrefetchScalarGridSpec(
        num_scalar_prefetch=2,
        grid=(n_slots // tm,),
        in_specs=[pl.BlockSpec((tm * SLAB, LANES), lambda i, te, nt: (last(i, te, nt), 0)),
                  pl.BlockSpec((None, 1, D, F), lambda i, te, nt: (layer, te[last(i, te, nt)], 0, 0)),
                  pl.BlockSpec((None, 1, D, F), lambda i, te, nt: (layer, te[last(i, te, nt)], 0, 0)),
                  pl.BlockSpec((None, 1, F, D), lambda i, te, nt: (layer, te[last(i, te, nt)], 0, 0))],
        out_specs=pl.BlockSpec((tm * SLAB, LANES), lambda i, te, nt: (i, 0)),
        scratch_shapes=[pltpu.VMEM((D, F), BF16), pltpu.VMEM((D, F), BF16), pltpu.VMEM((F, D), BF16)])
    return pl.pallas_call(
        functools.partial(_experts_kernel, tm=tm),
        grid_spec=grid_spec,
        out_shape=jax.ShapeDtypeStruct(xs.shape, F32),
        compiler_params=_params("arbitrary"),
        name="experts",
    )(tile_expert, n_tiles, xs, w1, w3, w2)


def _combine_kernel(base_sm, bc_sm, cr_sm, bc_next, cr_next, x_ref, mod_ref, g_ref, lst_ref, wt_ref,
                    sw1_ref, sw3_ref, sw2_ref, fg_ref, ys_hbm, o_ref, stage, sem, *, tb, final_norm):
    step = pl.program_id(0) * pl.num_programs(1) + pl.program_id(1)
    last_step = pl.num_programs(0) * pl.num_programs(1) - 1
    cur = lax.rem(step, 2)

    def run_copy(buf):
        return lambda local_row, sorted_row, nrows: pltpu.make_async_copy(
            _slab_rows(ys_hbm, sorted_row, nrows), _slab_rows(stage.at[buf], local_row, nrows), sem.at[buf])

    @pl.when(step == 0)
    def _():
        stage[...] = jnp.zeros_like(stage)
        _for_expert_runs(base_sm, bc_sm, cr_sm, lambda *a: run_copy(cur)(*a).start())

    @pl.when(step < last_step)
    def _():
        _for_expert_runs(base_sm, bc_next, cr_next, lambda *a: run_copy(1 - cur)(*a).start())

    x = x_ref[0]
    mod = mod_ref[0]
    hb = _rms_mod(x, g_ref[...], mod[4:5, :], mod[3:4, :]).astype(BF16)
    a = jnp.dot(hb, sw1_ref[...], preferred_element_type=F32)
    b = jnp.dot(hb, sw3_ref[...], preferred_element_type=F32)
    acc = jnp.dot((_silu(a) * b).astype(BF16), sw2_ref[...], preferred_element_type=F32)
    wt = wt_ref[0]
    lst = lst_ref[0]
    _wait_all_runs(ys_hbm, stage.at[cur], sem.at[cur], bc_sm[0, 0, N_EXPERTS])
    gr = _local_rows(tb) // _row_groups(tb)
    for grp in range(_row_groups(tb)):
        yg = _from_slabs(stage.at[cur, pl.ds(grp * gr * SLAB, gr * SLAB)], gr).astype(BF16)
        row = grp * gr + lax.broadcasted_iota(jnp.int32, (1, gr), 1)
        gate = jnp.zeros((tb, gr), F32)
        for k in range(TOP_K):
            gate = jnp.where(lst[:, k:k + 1] == row, wt[:, k:k + 1], gate)
        gate = gate.astype(BF16)
        acc = acc + jnp.dot(gate, yg, preferred_element_type=F32)
    y = x + mod[5:6, :] * acc
    if final_norm:
        y = y * lax.rsqrt(jnp.mean(y * y, axis=-1, keepdims=True) + EPS) * fg_ref[...]
    o_ref[0] = y


def _combine(base, blk_cnt, blk_carry, x, mod_l, norm_g, lst, wt, sw1, sw3, sw2, final_g, ys, *, tb, final_norm):
    B, S, D = x.shape
    nj = S // tb
    full = lambda a: pl.BlockSpec(a.shape, lambda b, j, *_: (0,) * a.ndim)
    tok = lambda w: pl.BlockSpec((1, tb, w), lambda b, j, *_: (b, j, 0))
    smem_block = lambda shift: pl.BlockSpec(
        (1, 1, LANES), lambda b, j, *_: (jnp.minimum(b * nj + j + shift, B * nj - 1), 0, 0), memory_space=pltpu.SMEM)
    grid_spec = pltpu.PrefetchScalarGridSpec(
        num_scalar_prefetch=1,
        grid=(B, nj),
        in_specs=[smem_block(0), smem_block(0), smem_block(1), smem_block(1), tok(D),
                  pl.BlockSpec((1, N_MOD, D), lambda b, j, *_: (b, 0, 0)),
                  full(norm_g), tok(LANES), tok(LANES), full(sw1), full(sw3), full(sw2), full(final_g),
                  pl.BlockSpec(memory_space=pltpu.HBM)],
        out_specs=tok(D),
        scratch_shapes=[pltpu.VMEM((2, _local_rows(tb) * SLAB, LANES), F32), pltpu.SemaphoreType.DMA((2,))])
    return pl.pallas_call(
        functools.partial(_combine_kernel, tb=tb, final_norm=final_norm),
        grid_spec=grid_spec,
        out_shape=jax.ShapeDtypeStruct((B, S, D), F32),
        compiler_params=_params("arbitrary", "arbitrary"),
        name="combine",
    )(base, blk_cnt, blk_carry, blk_cnt, blk_carry, x, mod_l, norm_g, lst, wt, sw1, sw3, sw2, final_g, ys)


def _block_diag(w):
    G, a, b = w.shape
    out = jnp.zeros((G * a, G * b), w.dtype)
    for g in range(G):
        out = out.at[g * a:(g + 1) * a, g * b:(g + 1) * b].set(w[g])
    return out


def kernel(x, c, w_in, b_f, pool_w, pool_scale, sg_ln_g, sg_ln_b, sg_w, sg_b, conv_w, conv_b, conv_ln_g,
           conv_ln_b, w_branch, w_out, mix_norm_g, ffn_norm_g, ada_w, ada_b, router_w, router_bias,
           exp_w1, exp_w3, exp_w2, shared_w1, shared_w3, shared_w2, final_norm_g):
    B, S, D = x.shape
    L = w_in.shape[0]
    tb = min(512, S)
    tq = tb
    tm = EXPERT_TILE
    tr = min(MOE_BLOCK, S)
    n_tiles = -(-(B * (S // tr) * _local_rows(tr)) // tm) + N_EXPERTS
    meta_lanes = -(-(n_tiles + 1) // LANES) * LANES
    row = lambda a: a.reshape(1, -1)

    mod = _modulation(c, ada_w, ada_b)
    for l in range(L):
        bfc = jnp.pad(b_f[l], (0, LANES - ATT_HEADS)).reshape(1, LANES)
        poolw = _block_diag(pool_w[l]).astype(BF16)
        sgb = jnp.repeat(sg_b[l].T, SG_HD, axis=1)

        ya, yb, yc, q, k, v = _mixer_in(
            x, mod[l], row(mix_norm_g[l]), w_in, l, bfc, poolw, row(pool_scale[l]),
            row(sg_ln_g[l]), row(sg_ln_b[l]), sg_w[l], sgb, conv_w[l], row(conv_b[l]),
            row(conv_ln_g[l]), row(conv_ln_b[l]), tb=tb)
        yd = _attention(q, k, v, tq=tq)
        fnorm = row(ffn_norm_g[l])
        x, ls, lst, wt, blk_cnt, blk_carry, cnt = _merge(
            x, mod[l], row(mix_norm_g[l]), w_in, l, w_branch[l].astype(BF16), w_out[l].astype(BF16),
            ya, yb, yc, yd, fnorm, router_w[l].T, router_bias[l].reshape(-1, 1), tb=tr)
        meta, base = _route_fin(cnt, tm=tm, meta_lanes=meta_lanes)
        xs = _dispatch(cnt[:, 0], base[:, 0], blk_cnt, blk_carry, ls, x, mod[l], fnorm,
                       tb=tr, tm=tm, n_slots=n_tiles * tm)
        ys = _experts(meta[0, :n_tiles], meta[0, meta_lanes - 1:], xs, exp_w1, exp_w3, exp_w2, l, tm=tm)
        x = _combine(base[:, 0], blk_cnt, blk_carry, x, mod[l], fnorm, lst, wt, shared_w1[l].astype(BF16),
                     shared_w3[l].astype(BF16), shared_w2[l].astype(BF16), row(final_norm_g), ys,
                     tb=tr, final_norm=(l == L - 1))
    return x
```
